```python
import math
import jax
import jax.numpy as jnp
from jax import lax
import numpy as np

D_MODEL = 4096
BATCH = 4
SEQ = 2048
DEPTH = 1

N_META = 16
RMS_EPS = 1e-6
L2_EPS = 1e-6

DN_HEADS = 16
DN_DK = 128
DN_DV = 128
DN_CONV = 4
DN_CHUNK = 64

DA_HEADS = 8
DA_DK = 128
DA_DV = 2 * DA_DK
Q_BLOCK = 128

N_EXPERTS = 32
TOP_K = 4
D_EXPERT = 1536
SWIGLU_LIMIT = 7.0
SWIGLU_ALPHA = 1.702
MOE_BLOCK = 256

DN_QK = DN_HEADS * DN_DK
DN_VW = DN_HEADS * DN_DV
DA_QK = DA_HEADS * 2 * DA_DK
DA_VW = DA_HEADS * DA_DV
IN_SIZES = (DN_QK, DN_QK, DN_VW, DN_VW, DN_HEADS, DN_HEADS, DA_QK, DA_QK, DA_VW, D_MODEL, D_MODEL)
IN_WIDTH = sum(IN_SIZES)
IN_SPLITS = tuple(int(s) for s in np.cumsum(IN_SIZES)[:-1])

kernel_name = 'hybrid_gdn_diffattn_moe'


def rms_norm(x, w):
    xf = x.astype(jnp.float32)
    y = xf * lax.rsqrt(jnp.mean(xf * xf, axis=-1, keepdims=True) + RMS_EPS)
    return (y * w.astype(jnp.float32)).astype(x.dtype)


def l2_norm(x):
    xf = x.astype(jnp.float32)
    return xf * lax.rsqrt(jnp.sum(xf * xf, axis=-1, keepdims=True) + L2_EPS)


def causal_dwconv(x, w):
    kw, c = w.shape
    return lax.conv_general_dilated(
        x, w[:, None, :].astype(x.dtype), window_strides=(1,), padding=[(kw - 1, 0)],
        dimension_numbers=('NWC', 'WIO', 'NWC'), feature_group_count=c)


def chunk_gated_delta_rule(q, k, v, beta, g):
    f32 = jnp.float32
    bsz, L, H, dk = q.shape
    dv = v.shape[-1]
    C = DN_CHUNK
    n = L // C

    def to_chunks(t):
        t = t.astype(f32).reshape((bsz, n, C, H) + t.shape[3:])
        return jnp.moveaxis(t, (1, 3), (0, 2))

    q = to_chunks(q) * (dk ** -0.5)
    k = to_chunks(k)
    v = to_chunks(v)
    beta = to_chunks(beta)
    g = jnp.cumsum(to_chunks(g), axis=-1)
    causal = jnp.tril(jnp.ones((C, C), dtype=bool))
    strict = jnp.tril(jnp.ones((C, C), dtype=bool), -1)
    decay = jnp.exp(jnp.where(causal, g[..., :, None] - g[..., None, :], -jnp.inf))

    kb = k * beta[..., None]
    lower = jnp.where(strict, jnp.einsum('nbhcd,nbhsd->nbhcs', kb, k) * decay, 0.0)
    rhs = jnp.concatenate([v * beta[..., None], kb * jnp.exp(g)[..., None]], axis=-1)
    sol = lax.linalg.triangular_solve(jnp.eye(C, dtype=f32) + lower, rhs,
                                      left_side=True, lower=True, unit_diagonal=True)
    u, w = sol[..., :dv], sol[..., dv:]
    qk = jnp.where(causal, jnp.einsum('nbhcd,nbhsd->nbhcs', q, k) * decay, 0.0)

    def step(S, xs):
        qc, kc, uc, wc, qkc, gc = xs
        v_new = uc - jnp.einsum('bhcd,bhde->bhce', wc, S)
        o = (jnp.einsum('bhcd,bhde->bhce', qc * jnp.exp(gc)[..., None], S)
             + jnp.einsum('bhcs,bhse->bhce', qkc, v_new))
        g_last = gc[..., -1:]
        S = (S * jnp.exp(g_last)[..., None]
             + jnp.einsum('bhcd,bhce->bhde', kc * jnp.exp(g_last - gc)[..., None], v_new))
        return S, o

    S0 = jnp.zeros((bsz, H, dk, dv), f32)
    _, o = lax.scan(step, S0, (q, k, u, w, qk, g))
    return jnp.moveaxis(o, (0, 2), (1, 3)).reshape(bsz, L, H, dv)


def gated_deltanet(q, k, v, z, b, a, conv_w, a_log, dt_bias, norm_w):
    bsz, L, _ = q.shape
    f32 = jnp.float32
    qkv = jax.nn.silu(causal_dwconv(jnp.concatenate([q, k, v], axis=-1), conv_w))
    q, k, v = jnp.split(qkv, [DN_QK, 2 * DN_QK], axis=-1)
    q = l2_norm(q.reshape(bsz, L, DN_HEADS, DN_DK))
    k = l2_norm(k.reshape(bsz, L, DN_HEADS, DN_DK))
    v = v.reshape(bsz, L, DN_HEADS, DN_DV)
    beta = jax.nn.sigmoid(b.astype(f32))
    g = -jnp.exp(a_log.astype(f32)) * jax.nn.softplus(a.astype(f32) + dt_bias.astype(f32))
    pad = (-L) % DN_CHUNK

    def padf(t):
        return jnp.pad(t, [(0, 0), (pad, 0)] + [(0, 0)] * (t.ndim - 2))

    o = chunk_gated_delta_rule(padf(q), padf(k), padf(v), padf(beta), padf(g))[:, pad:]
    o = rms_norm(o, norm_w) * jax.nn.silu(z.astype(f32).reshape(bsz, L, DN_HEADS, DN_DV))
    return o.reshape(bsz, L, DN_VW).astype(z.dtype)


def diff_attention(q, k, v, q_norm, k_norm, lam_q1, lam_k1, lam_q2, lam_k2, sub_norm, lambda_init):
    f32 = jnp.float32
    bsz, L, _ = q.shape
    q = rms_norm(q.reshape(bsz, L, DA_HEADS, 2, DA_DK), q_norm)
    k = rms_norm(k.reshape(bsz, L, DA_HEADS, 2, DA_DK), k_norm)
    v = v.reshape(bsz, L, DA_HEADS, DA_DV)
    lam = (jnp.exp(jnp.sum(lam_q1.astype(f32) * lam_k1.astype(f32)))
           - jnp.exp(jnp.sum(lam_q2.astype(f32) * lam_k2.astype(f32))) + lambda_init)
    n_blk = -(-L // Q_BLOCK)
    Lp = n_blk * Q_BLOCK
    pad = Lp - L
    q = jnp.pad(q, [(0, 0), (0, pad), (0, 0), (0, 0), (0, 0)])
    k = jnp.pad(k, [(0, 0), (0, pad), (0, 0), (0, 0), (0, 0)])
    v = jnp.pad(v, [(0, 0), (0, pad), (0, 0), (0, 0)])
    q_blocks = jnp.moveaxis(q.reshape(bsz, n_blk, Q_BLOCK, DA_HEADS, 2, DA_DK), 1, 0)
    key_pos = jnp.arange(Lp)
    scale = DA_DK ** -0.5

    def block(args):
        qb, start = args
        s = jnp.einsum('bqhmd,bkhmd->bhmqk', qb, k).astype(f32) * scale
        q_pos = start + jnp.arange(Q_BLOCK)
        s = jnp.where(key_pos[None, :] <= q_pos[:, None], s, -jnp.inf)
        p = jax.nn.softmax(s, axis=-1)
        attn = p[:, :, 0] - lam * p[:, :, 1]
        return jnp.einsum('bhqk,bkhe->bqhe', attn.astype(v.dtype), v)

    starts = jnp.arange(n_blk) * Q_BLOCK
    o = lax.map(block, (q_blocks, starts))
    o = jnp.moveaxis(o, 0, 1).reshape(bsz, Lp, DA_HEADS, DA_DV)[:, :L]
    o = rms_norm(o, sub_norm) * (1.0 - lambda_init)
    return o.reshape(bsz, L, DA_VW)


def moe_ffn(h, router_w, router_b, w_gate_up, b_gate_up, w_down, b_down):
    N, D = h.shape
    logits = (h @ router_w).astype(jnp.float32) + router_b.astype(jnp.float32)
    top_val, top_idx = lax.top_k(logits, TOP_K)
    gates = jax.nn.softmax(top_val, axis=-1)
    A = N * TOP_K
    e_flat = top_idx.reshape(-1).astype(jnp.int32)
    t_flat = jnp.repeat(jnp.arange(N, dtype=jnp.int32), TOP_K)
    g_flat = gates.reshape(-1)
    e_s, t_s, g_s = lax.sort((e_flat, t_flat, g_flat), num_keys=1, is_stable=True)
    counts = jax.ops.segment_sum(jnp.ones((A,), jnp.int32), e_flat, num_segments=N_EXPERTS)
    starts = jnp.cumsum(counts) - counts
    padded = (counts + MOE_BLOCK - 1) // MOE_BLOCK * MOE_BLOCK
    pad_ends = jnp.cumsum(padded)
    pad_starts = pad_ends - padded
    dest = pad_starts[e_s] + (jnp.arange(A, dtype=jnp.int32) - starts[e_s])
    n_blocks = -(-(A + N_EXPERTS * (MOE_BLOCK - 1)) // MOE_BLOCK)
    P = n_blocks * MOE_BLOCK
    row_tok = jnp.zeros((P,), jnp.int32).at[dest].set(t_s)
    row_gate = jnp.zeros((P,), h.dtype).at[dest].set(g_s.astype(h.dtype))
    block_expert = jnp.minimum(
        jnp.searchsorted(pad_ends, jnp.arange(n_blocks, dtype=jnp.int32) * MOE_BLOCK, side='right'),
        N_EXPERTS - 1)

    def expert_block(args):
        tok, gate, e = args
        xb = h[tok]
        gu = xb @ w_gate_up[e] + b_gate_up[e]
        gt = jnp.minimum(gu[:, :D_EXPERT], SWIGLU_LIMIT)
        up = jnp.clip(gu[:, D_EXPERT:], -SWIGLU_LIMIT, SWIGLU_LIMIT)
        act = gt * jax.nn.sigmoid(SWIGLU_ALPHA * gt) * (up + 1.0)
        y = act @ w_down[e] + b_down[e]
        return y * gate[:, None]

    y = lax.map(expert_block, (row_tok.reshape(n_blocks, MOE_BLOCK),
                               row_gate.reshape(n_blocks, MOE_BLOCK), block_expert))
    return jnp.zeros_like(h).at[row_tok].add(y.reshape(P, D))


def setup_inputs(seed: int = 0) -> dict:
    key = jax.random.key(seed)
    ks = jax.random.split(key, 25)
    f32 = jnp.float32

    def nrm(k, shape, scale):
        return jax.random.normal(k, shape, f32) * scale

    def gain(k, shape):
        return 1.0 + 0.02 * jax.random.normal(k, shape, f32)

    dt = jnp.exp(jax.random.uniform(ks[5], (DEPTH, DN_HEADS), f32, math.log(1e-3), math.log(1e-1)))
    return {
        'x': nrm(ks[0], (BATCH, SEQ, D_MODEL), 1.0),
        'meta_tokens': nrm(ks[1], (N_META, D_MODEL), 1.0),
        'mix_norm': gain(ks[2], (DEPTH, D_MODEL)),
        'w_in': nrm(ks[3], (DEPTH, D_MODEL, IN_WIDTH), D_MODEL ** -0.5),
        'dn_conv': nrm(ks[4], (DEPTH, DN_CONV, 2 * DN_QK + DN_VW), DN_CONV ** -0.5),
        'dn_a_log': jnp.log(jax.random.uniform(ks[6], (DEPTH, DN_HEADS), f32, 1.0, 16.0)),
        'dn_dt_bias': dt + jnp.log(-jnp.expm1(-dt)),
        'dn_out_norm': gain(ks[7], (DEPTH, DN_DV)),
        'da_q_norm': gain(ks[8], (DEPTH, DA_DK)),
        'da_k_norm': gain(ks[9], (DEPTH, DA_DK)),
        'da_lam_q1': nrm(ks[10], (DEPTH, DA_DK), 0.1),
        'da_lam_k1': nrm(ks[11], (DEPTH, DA_DK), 0.1),
        'da_lam_q2': nrm(ks[12], (DEPTH, DA_DK), 0.1),
        'da_lam_k2': nrm(ks[13], (DEPTH, DA_DK), 0.1),
        'da_sub_norm': gain(ks[14], (DEPTH, DA_DV)),
        'w_branch_a': nrm(ks[15], (DEPTH, DN_VW, D_MODEL), DN_VW ** -0.5),
        'w_branch_b': nrm(ks[16], (DEPTH, DA_VW, D_MODEL), DA_VW ** -0.5),
        'w_out': nrm(ks[17], (DEPTH, D_MODEL, D_MODEL), D_MODEL ** -0.5),
        'ffn_norm': gain(ks[18], (DEPTH, D_MODEL)),
        'router_w': nrm(ks[19], (DEPTH, D_MODEL, N_EXPERTS), D_MODEL ** -0.5),
        'router_b': nrm(ks[20], (DEPTH, N_EXPERTS), 0.01),
        'w_gate_up': nrm(ks[21], (DEPTH, N_EXPERTS, D_MODEL, 2 * D_EXPERT), D_MODEL ** -0.5),
        'b_gate_up': nrm(ks[22], (DEPTH, N_EXPERTS, 2 * D_EXPERT), 0.01),
        'w_down': nrm(ks[23], (DEPTH, N_EXPERTS, D_EXPERT, D_MODEL), D_EXPERT ** -0.5),
        'b_down': nrm(ks[24], (DEPTH, N_EXPERTS, D_MODEL), 0.01),
    }


def reference(x, meta_tokens, mix_norm, w_in, dn_conv, dn_a_log, dn_dt_bias, dn_out_norm,
              da_q_norm, da_k_norm, da_lam_q1, da_lam_k1, da_lam_q2, da_lam_k2, da_sub_norm,
              w_branch_a, w_branch_b, w_out, ffn_norm, router_w, router_b,
              w_gate_up, b_gate_up, w_down, b_down):
    bsz = x.shape[0]
    meta = jnp.broadcast_to(meta_tokens[None].astype(x.dtype), (bsz, N_META, D_MODEL))
    h = jnp.concatenate([meta, x], axis=1)
    for layer in range(DEPTH):
        u = rms_norm(h, mix_norm[layer])
        proj = u @ w_in[layer]
        dq, dk, dv, dz, db, da, aq, ak, av, ga, gb = jnp.split(proj, IN_SPLITS, axis=-1)
        y_a = gated_deltanet(dq, dk, dv, dz, db, da, dn_conv[layer], dn_a_log[layer],
                             dn_dt_bias[layer], dn_out_norm[layer])
        lambda_init = 0.8 - 0.6 * math.exp(-0.3 * layer)
        y_b = diff_attention(aq, ak, av, da_q_norm[layer], da_k_norm[layer], da_lam_q1[layer],
                             da_lam_k1[layer], da_lam_q2[layer], da_lam_k2[layer],
                             da_sub_norm[layer], lambda_init)
        merged = (jax.nn.sigmoid(ga) * (y_a @ w_branch_a[layer])
                  + jax.nn.sigmoid(gb) * (y_b @ w_branch_b[layer]))
        h = h + merged @ w_out[layer]
        if layer == DEPTH - 1:
            h = h[:, N_META:]
        u = rms_norm(h, ffn_norm[layer])
        bh, lh, _ = u.shape
        h = h + moe_ffn(u.reshape(bh * lh, D_MODEL), router_w[layer], router_b[layer],
                        w_gate_up[layer], b_gate_up[layer], w_down[layer],
                        b_down[layer]).reshape(bh, lh, D_MODEL)
    return h
```

```python
import functools
import math

import jax
import jax.numpy as jnp
from jax import lax
from jax.experimental import pallas as pl
from jax.experimental.pallas import tpu as pltpu

F32 = jnp.float32
BF16 = jnp.bfloat16

N_META = 16
RMS_EPS = 1e-6
L2_EPS = 1e-6

DN_HEADS = 16
DN_DK = 128
DN_DV = 128
DN_CONV = 4
DN_CHUNK = 64
DN_W = DN_HEADS * DN_DK

DA_HEADS = 8
DA_DK = 128
DA_DV = 256
LAMBDA_INIT = 0.8 - 0.6 * math.exp(-0.3 * 0)

N_EXPERTS = 32
TOP_K = 4
D_EXPERT = 1536
SWIGLU_LIMIT = 7.0
SWIGLU_ALPHA = 1.702

LANES = 128
MOE_ROWS = 256
MOE_ITEM_BLOCKS = 6
NEG_BIG = -1e30
VMEM_LIMIT = 56 * 1024 * 1024

COL_DZ = 3 * DN_W
COL_AQ = 4 * DN_W
COL_AK = COL_AQ + DA_HEADS * 2 * DA_DK
COL_AV = COL_AK + DA_HEADS * 2 * DA_DK
COL_GA = COL_AV + DA_HEADS * DA_DV


def _params(*sem, vmem=None):
    return pltpu.CompilerParams(dimension_semantics=sem, vmem_limit_bytes=vmem)


def _rms_cast_kernel(x_ref, w_ref, o_ref):
    x = x_ref[...]
    y = x * lax.rsqrt(jnp.mean(x * x, axis=-1, keepdims=True) + RMS_EPS) * w_ref[...]
    o_ref[...] = y.astype(o_ref.dtype)


def rms_cast(x, w, tm):
    m, d = x.shape
    return pl.pallas_call(
        _rms_cast_kernel,
        grid=(m // tm,),
        in_specs=[pl.BlockSpec((tm, d), lambda i: (i, 0)),
                  pl.BlockSpec((1, d), lambda i: (0, 0))],
        out_specs=pl.BlockSpec((tm, d), lambda i: (i, 0)),
        out_shape=jax.ShapeDtypeStruct((m, d), BF16),
        compiler_params=_params("parallel"),
        name="rms_cast",
    )(x, w.reshape(1, d))


def _mm_kernel(a_ref, b_ref, *rest, nk, has_res):
    if has_res:
        r_ref, o_ref, acc_ref = rest
    else:
        o_ref, acc_ref = rest
    k = pl.program_id(2)

    @pl.when(k == 0)
    def _():
        acc_ref[...] = jnp.zeros_like(acc_ref)

    acc_ref[...] += jnp.dot(a_ref[...], b_ref[...].astype(BF16), preferred_element_type=F32)

    @pl.when(k == nk - 1)
    def _():
        acc = acc_ref[...]
        if has_res:
            acc = acc + r_ref[...]
        o_ref[...] = acc.astype(o_ref.dtype)


def matmul(a, b, out_dtype, tm, tn, tk, res=None, name="matmul"):
    m, kd = a.shape
    _, n = b.shape
    nk = kd // tk
    in_specs = [pl.BlockSpec((tm, tk), lambda i, j, k: (i, k)),
                pl.BlockSpec((tk, tn), lambda i, j, k: (k, j))]
    args = [a, b]
    if res is not None:
        in_specs.append(pl.BlockSpec((tm, tn), lambda i, j, k: (i, j)))
        args.append(res)
    return pl.pallas_call(
        functools.partial(_mm_kernel, nk=nk, has_res=res is not None),
        grid=(m // tm, n // tn, nk),
        in_specs=in_specs,
        out_specs=pl.BlockSpec((tm, tn), lambda i, j, k: (i, j)),
        out_shape=jax.ShapeDtypeStruct((m, n), out_dtype),
        scratch_shapes=[pltpu.VMEM((tm, tn), F32)],
        compiler_params=_params("parallel", "parallel", "arbitrary", vmem=VMEM_LIMIT),
        name=name,
    )(*args)


def _deltanet_kernel(qkv_ref, z_ref, ba_ref, qkvm_ref, bam_ref, cw_ref, hp_ref, nw_ref,
                     o_ref, full_ref, act_ref, ba_s, s_ref):
    t = pl.program_id(1)
    C = DN_CHUNK
    n_pad = C - N_META

    @pl.when(t == 0)
    def _():
        s_ref[...] = jnp.zeros_like(s_ref)
        full_ref[0:8 + n_pad, :] = jnp.zeros((8 + n_pad, 3 * DN_W), F32)
        full_ref[8 + n_pad:8 + C, :] = qkvm_ref[...].astype(F32)
        ba_s[0:n_pad, :] = jnp.zeros((n_pad, LANES), F32)
        ba_s[n_pad:C, :] = bam_ref[...]

    @pl.when(t > 0)
    def _():
        full_ref[8:8 + C, :] = qkv_ref[...].astype(F32)
        ba_s[...] = ba_ref[...]

    conv = cw_ref[0:1, :] * full_ref[5:5 + C, :]
    for j in range(1, DN_CONV):
        conv = conv + cw_ref[j:j + 1, :] * full_ref[5 + j:5 + j + C, :]
    act_ref[...] = conv * jax.nn.sigmoid(conv)
    full_ref[0:8, :] = full_ref[C:C + 8, :]

    row1 = lax.broadcasted_iota(jnp.int32, (C, 1), 0)
    valid = jnp.where((t > 0) | (row1 >= n_pad), 1.0, 0.0).astype(F32)
    ba = ba_s[...]
    beta_all = jax.nn.sigmoid(ba) * valid
    xg = ba + hp_ref[1:2, :]
    softplus = jnp.maximum(xg, 0.0) + jnp.log1p(jnp.exp(-jnp.abs(xg)))
    g_all = -jnp.exp(hp_ref[0:1, :]) * softplus * valid

    row = lax.broadcasted_iota(jnp.int32, (C, C), 0)
    col = lax.broadcasted_iota(jnp.int32, (C, C), 1)
    causal = row >= col
    strict = row > col
    eye = jnp.where(row == col, 1.0, 0.0).astype(F32)
    tril = jnp.where(causal, 1.0, 0.0).astype(BF16)

    g_hi = g_all.astype(BF16)
    r1 = g_all - g_hi.astype(F32)
    g_mid = r1.astype(BF16)
    g_lo = (r1 - g_mid.astype(F32)).astype(BF16)
    g3 = jnp.dot(tril, jnp.concatenate([g_hi, g_mid, g_lo], axis=1), preferred_element_type=F32)
    gcum = g3[:, 0:LANES] + g3[:, LANES:2 * LANES] + g3[:, 2 * LANES:3 * LANES]
    gcum_t = gcum.T

    scale = DN_DK ** -0.5
    nw = nw_ref[...]
    for h in range(DN_HEADS):
        sl = slice(h * DN_DK, (h + 1) * DN_DK)
        qh = act_ref[:, h * DN_DK:(h + 1) * DN_DK]
        kh = act_ref[:, DN_W + h * DN_DK:DN_W + (h + 1) * DN_DK]
        vh = act_ref[:, 2 * DN_W + h * DN_DV:2 * DN_W + (h + 1) * DN_DV]
        qn = qh * (lax.rsqrt(jnp.sum(qh * qh, axis=-1, keepdims=True) + L2_EPS) * scale)
        kn = kh * lax.rsqrt(jnp.sum(kh * kh, axis=-1, keepdims=True) + L2_EPS)
        beta = beta_all[:, h:h + 1]
        gc = gcum[:, DN_HEADS + h:DN_HEADS + h + 1]
        gr = gcum_t[DN_HEADS + h:DN_HEADS + h + 1, :]
        g_last = gc[C - 1:C, :]
        e_g = jnp.exp(gc)
        decay = jnp.exp(jnp.where(causal, gc - gr, NEG_BIG))

        kb = kn * beta
        kn_b = kn.astype(BF16)
        kk = lax.dot_general(kb.astype(BF16), kn_b, (((1,), (1,)), ((), ())),
                             preferred_element_type=F32)
        a_mat = jnp.where(strict, kk * decay, 0.0)
        bm = -a_mat
        inv = eye + bm
        for _ in range(5):
            bb = bm.astype(BF16)
            bm = jnp.dot(bb, bb, preferred_element_type=F32)
            inv = inv + jnp.dot(inv.astype(BF16), bm.astype(BF16), preferred_element_type=F32)
        rhs = jnp.concatenate([vh * beta, kb * e_g], axis=1).astype(BF16)
        sol = jnp.dot(inv.astype(BF16), rhs, preferred_element_type=F32)
        u = sol[:, 0:DN_DV]
        w = sol[:, DN_DV:2 * DN_DV]
        qk = lax.dot_general(qn.astype(BF16), kn_b, (((1,), (1,)), ((), ())),
                             preferred_element_type=F32)
        qk = jnp.where(causal, qk * decay, 0.0)

        s = s_ref[h]
        s_b = s.astype(BF16)
        v_new = u - jnp.dot(w.astype(BF16), s_b, preferred_element_type=F32)
        v_new_b = v_new.astype(BF16)
        o = (jnp.dot((qn * e_g).astype(BF16), s_b, preferred_element_type=F32)
             + jnp.dot(qk.astype(BF16), v_new_b, preferred_element_type=F32))
        k_dec = (kn * jnp.exp(g_last - gc)).astype(BF16)
        s_ref[h] = s * jnp.exp(g_last) + lax.dot_general(
            k_dec, v_new_b, (((0,), (0,)), ((), ())), preferred_element_type=F32)

        zh = z_ref[:, sl].astype(F32)
        o = o * lax.rsqrt(jnp.mean(o * o, axis=-1, keepdims=True) + RMS_EPS) * nw
        o_ref[:, sl] = (o * (zh * jax.nn.sigmoid(zh))).astype(o_ref.dtype)


def gated_deltanet(proj_x, ba_x, proj_m, ba_m, conv_w, a_log, dt_bias, norm_w, bsz, seq):
    C = DN_CHUNK
    n_chunks = seq // C
    w3 = 3 * DN_W
    hp = jnp.zeros((2, LANES), F32)
    hp = hp.at[0, DN_HEADS:2 * DN_HEADS].set(a_log.astype(F32))
    hp = hp.at[1, DN_HEADS:2 * DN_HEADS].set(dt_bias.astype(F32))

    def xrow(b, t):
        return b * n_chunks + jnp.maximum(t - 1, 0)

    return pl.pallas_call(
        _deltanet_kernel,
        grid=(bsz, n_chunks + 1),
        in_specs=[
            pl.BlockSpec((C, w3), lambda b, t: (xrow(b, t), 0)),
            pl.BlockSpec((C, DN_W), lambda b, t: (xrow(b, t), COL_DZ // DN_W)),
            pl.BlockSpec((C, LANES), lambda b, t: (xrow(b, t), 0)),
            pl.BlockSpec((N_META, w3), lambda b, t: (0, 0)),
            pl.BlockSpec((N_META, LANES), lambda b, t: (0, 0)),
            pl.BlockSpec((DN_CONV, w3), lambda b, t: (0, 0)),
            pl.BlockSpec((2, LANES), lambda b, t: (0, 0)),
            pl.BlockSpec((1, DN_DV), lambda b, t: (0, 0)),
        ],
        out_specs=pl.BlockSpec((C, DN_W), lambda b, t: (xrow(b, t), 0)),
        out_shape=jax.ShapeDtypeStruct((bsz * seq, DN_W), BF16),
        scratch_shapes=[
            pltpu.VMEM((C + 8, w3), F32),
            pltpu.VMEM((C, w3), F32),
            pltpu.VMEM((C, LANES), F32),
            pltpu.VMEM((DN_HEADS, DN_DK, DN_DV), F32),
        ],
        compiler_params=_params("parallel", "arbitrary", vmem=VMEM_LIMIT),
        name="gated_deltanet",
    )(proj_x, proj_x, ba_x, proj_m, ba_m, conv_w.astype(F32), hp, norm_w.reshape(1, DN_DV).astype(F32))


def _diff_attn_kernel(q_ref, k_ref, v_ref, km_ref, vm_ref, qn_ref, kn_ref, lam_ref, sn_ref,
                      o_ref, kx_s, kmeta_s, *, tq, seq):
    qi = pl.program_id(2)

    def rms(x, w):
        return x * lax.rsqrt(jnp.mean(x * x, axis=-1, keepdims=True) + RMS_EPS) * w

    @pl.when(qi == 0)
    def _():
        for m in range(2):
            kx = k_ref[:, m * DA_DK:(m + 1) * DA_DK].astype(F32)
            kx_s[m] = rms(kx, kn_ref[...]).astype(BF16)
            kmx = km_ref[:, m * DA_DK:(m + 1) * DA_DK].astype(F32)
            kmeta_s[m] = rms(kmx, kn_ref[...]).astype(BF16)

    lam = (jnp.exp(jnp.sum(lam_ref[0:1, :] * lam_ref[1:2, :], axis=-1, keepdims=True))
           - jnp.exp(jnp.sum(lam_ref[2:3, :] * lam_ref[3:4, :], axis=-1, keepdims=True))
           + LAMBDA_INIT)
    row = qi * tq + lax.broadcasted_iota(jnp.int32, (tq, seq), 0)
    col = lax.broadcasted_iota(jnp.int32, (tq, seq), 1)
    visible = col <= row
    v_x = v_ref[...]
    v_m = vm_ref[...]
    scale = DA_DK ** -0.5
    outs = []
    for m in range(2):
        qx = q_ref[:, m * DA_DK:(m + 1) * DA_DK].astype(F32)
        qn = (rms(qx, qn_ref[...]) * scale).astype(BF16)
        sx = lax.dot_general(qn, kx_s[m], (((1,), (1,)), ((), ())), preferred_element_type=F32)
        sx = jnp.where(visible, sx, NEG_BIG)
        sm = lax.dot_general(qn, kmeta_s[m], (((1,), (1,)), ((), ())), preferred_element_type=F32)
        mx = jnp.maximum(jnp.max(sx, axis=-1, keepdims=True), jnp.max(sm, axis=-1, keepdims=True))
        px = jnp.exp(sx - mx)
        pm = jnp.exp(sm - mx)
        den = jnp.sum(px, axis=-1, keepdims=True) + jnp.sum(pm, axis=-1, keepdims=True)
        acc = (jnp.dot(px.astype(BF16), v_x, preferred_element_type=F32)
               + jnp.dot(pm.astype(BF16), v_m, preferred_element_type=F32))
        outs.append(acc / den)
    out = outs[0] - lam * outs[1]
    out = rms(out, sn_ref[...]) * (1.0 - LAMBDA_INIT)
    o_ref[...] = out.astype(o_ref.dtype)


def diff_attention(proj_x, proj_m, q_norm, k_norm, lam4, sub_norm, bsz, seq, tq=256):
    nq = seq // tq
    cq = COL_AQ // DA_DV
    ck = COL_AK // DA_DV
    cv = COL_AV // DA_DV
    return pl.pallas_call(
        functools.partial(_diff_attn_kernel, tq=tq, seq=seq),
        grid=(bsz, DA_HEADS, nq),
        in_specs=[
            pl.BlockSpec((tq, DA_DV), lambda b, h, i: (b * nq + i, cq + h)),
            pl.BlockSpec((seq, DA_DV), lambda b, h, i: (b, ck + h)),
            pl.BlockSpec((seq, DA_DV), lambda b, h, i: (b, cv + h)),
            pl.BlockSpec((N_META, DA_DV), lambda b, h, i: (0, ck + h)),
            pl.BlockSpec((N_META, DA_DV), lambda b, h, i: (0, cv + h)),
            pl.BlockSpec((1, DA_DK), lambda b, h, i: (0, 0)),
            pl.BlockSpec((1, DA_DK), lambda b, h, i: (0, 0)),
            pl.BlockSpec((4, DA_DK), lambda b, h, i: (0, 0)),
            pl.BlockSpec((1, DA_DV), lambda b, h, i: (0, 0)),
        ],
        out_specs=pl.BlockSpec((tq, DA_DV), lambda b, h, i: (b * nq + i, h)),
        out_shape=jax.ShapeDtypeStruct((bsz * seq, DA_HEADS * DA_DV), BF16),
        scratch_shapes=[pltpu.VMEM((2, seq, DA_DK), BF16),
                        pltpu.VMEM((2, N_META, DA_DK), BF16)],
        compiler_params=_params("parallel", "parallel", "arbitrary", vmem=VMEM_LIMIT),
        name="diff_attention",
    )(proj_x, proj_x, proj_x, proj_m, proj_m,
      q_norm.reshape(1, DA_DK).astype(F32), k_norm.reshape(1, DA_DK).astype(F32),
      lam4.astype(F32), sub_norm.reshape(1, DA_DV).astype(F32))


def _merge_kernel(ya_ref, yb_ref, wa_ref, wb_ref, ga_ref, gb_ref, o_ref):
    pa = jnp.dot(ya_ref[...], wa_ref[...].astype(BF16), preferred_element_type=F32)
    pb = jnp.dot(yb_ref[...], wb_ref[...].astype(BF16), preferred_element_type=F32)
    ga = jax.nn.sigmoid(ga_ref[...].astype(F32))
    gb = jax.nn.sigmoid(gb_ref[...].astype(F32))
    o_ref[...] = (ga * pa + gb * pb).astype(o_ref.dtype)


def branch_merge(y_a, y_b, w_a, w_b, proj_x, d_model, tm=1024, tn=512):
    m, ka = y_a.shape
    kb = y_b.shape[1]
    cga = COL_GA // tn
    cgb = (COL_GA + d_model) // tn
    return pl.pallas_call(
        _merge_kernel,
        grid=(m // tm, d_model // tn),
        in_specs=[
            pl.BlockSpec((tm, ka), lambda i, j: (i, 0)),
            pl.BlockSpec((tm, kb), lambda i, j: (i, 0)),
            pl.BlockSpec((ka, tn), lambda i, j: (0, j)),
            pl.BlockSpec((kb, tn), lambda i, j: (0, j)),
            pl.BlockSpec((tm, tn), lambda i, j: (i, cga + j)),
            pl.BlockSpec((tm, tn), lambda i, j: (i, cgb + j)),
        ],
        out_specs=pl.BlockSpec((tm, tn), lambda i, j: (i, j)),
        out_shape=jax.ShapeDtypeStruct((m, d_model), BF16),
        compiler_params=_params("parallel", "parallel", vmem=VMEM_LIMIT),
        name="branch_merge",
    )(y_a, y_b, w_a, w_b, proj_x, proj_x)


def _router_kernel(h_ref, nw_ref, whi_ref, wlo_ref, b_ref, idx_ref, gate_ref):
    x = h_ref[...]
    u = x * lax.rsqrt(jnp.mean(x * x, axis=-1, keepdims=True) + RMS_EPS) * nw_ref[...]
    u_hi = u.astype(BF16)
    u_lo = (u - u_hi.astype(F32)).astype(BF16)
    logits = (jnp.dot(u_hi, whi_ref[...], preferred_element_type=F32)
              + jnp.dot(u_lo, whi_ref[...], preferred_element_type=F32)
              + jnp.dot(u_hi, wlo_ref[...], preferred_element_type=F32)
              + b_ref[...])
    lane = lax.broadcasted_iota(jnp.int32, logits.shape, 1)
    lane_f = lane.astype(F32)
    vals = logits
    tops, idxs = [], []
    for _ in range(TOP_K):
        mx = jnp.max(vals, axis=-1, keepdims=True)
        ix = jnp.min(jnp.where(vals == mx, lane_f, float(LANES)), axis=-1, keepdims=True).astype(jnp.int32)
        tops.append(mx)
        idxs.append(ix)
        vals = jnp.where(lane == ix, -3.0e38, vals)
    exps = [jnp.exp(tv - tops[0]) for tv in tops]
    den = exps[0] + exps[1] + exps[2] + exps[3]
    idx_out = jnp.zeros(logits.shape, jnp.int32)
    gate_out = jnp.zeros(logits.shape, F32)
    for k in range(TOP_K):
        idx_out = jnp.where(lane == k, idxs[k], idx_out)
        gate_out = jnp.where(lane == k, exps[k] / den, gate_out)
    idx_ref[...] = idx_out
    gate_ref[...] = gate_out


def router(h2, ffn_norm, router_w, router_b, tm=256):
    n, d = h2.shape
    wpad = jnp.zeros((d, LANES), F32).at[:, :N_EXPERTS].set(router_w.astype(F32))
    w_hi = wpad.astype(BF16)
    w_lo = (wpad - w_hi.astype(F32)).astype(BF16)
    bias = jnp.full((1, LANES), NEG_BIG, F32).at[0, :N_EXPERTS].set(router_b.astype(F32))
    return pl.pallas_call(
        _router_kernel,
        grid=(n // tm,),
        in_specs=[
            pl.BlockSpec((tm, d), lambda i: (i, 0)),
            pl.BlockSpec((1, d), lambda i: (0, 0)),
            pl.BlockSpec((d, LANES), lambda i: (0, 0)),
            pl.BlockSpec((d, LANES), lambda i: (0, 0)),
            pl.BlockSpec((1, LANES), lambda i: (0, 0)),
        ],
        out_specs=[pl.BlockSpec((tm, LANES), lambda i: (i, 0)),
                   pl.BlockSpec((tm, LANES), lambda i: (i, 0))],
        out_shape=[jax.ShapeDtypeStruct((n, LANES), jnp.int32),
                   jax.ShapeDtypeStruct((n, LANES), F32)],
        compiler_params=_params("parallel", vmem=VMEM_LIMIT),
        name="router",
    )(h2, ffn_norm.reshape(1, d).astype(F32), w_hi, w_lo, bias)


def _gather_norm_kernel(tok_ref, h_hbm, nw_ref, o_ref, buf, sem, *, rows):
    i = pl.program_id(0)

    def row_copy(r):
        tok = tok_ref[i * rows + r]
        return pltpu.make_async_copy(h_hbm.at[pl.ds(tok, 1), :], buf.at[pl.ds(r, 1), :], sem)

    def start(r, c):
        row_copy(r).start()
        return c

    def wait(r, c):
        row_copy(r).wait()
        return c

    lax.fori_loop(0, rows, start, 0)
    lax.fori_loop(0, rows, wait, 0)
    x = buf[...]
    y = x * lax.rsqrt(jnp.mean(x * x, axis=-1, keepdims=True) + RMS_EPS) * nw_ref[...]
    o_ref[...] = y.astype(o_ref.dtype)


def gather_norm(h2, row_tok, ffn_norm, rows=MOE_ROWS):
    n, d = h2.shape
    p = row_tok.shape[0]
    return pl.pallas_call(
        functools.partial(_gather_norm_kernel, rows=rows),
        grid_spec=pltpu.PrefetchScalarGridSpec(
            num_scalar_prefetch=1,
            grid=(p // rows,),
            in_specs=[pl.BlockSpec(memory_space=pl.ANY),
                      pl.BlockSpec((1, d), lambda i, tok: (0, 0))],
            out_specs=pl.BlockSpec((rows, d), lambda i, tok: (i, 0)),
            scratch_shapes=[pltpu.VMEM((rows, d), F32), pltpu.SemaphoreType.DMA(())],
        ),
        out_shape=jax.ShapeDtypeStruct((p, d), BF16),
        compiler_params=_params("arbitrary", vmem=VMEM_LIMIT),
        name="gather_norm",
    )(row_tok, h2, ffn_norm.reshape(1, d).astype(F32))


def _expert_kernel(ie_ref, isb_ref, inb_ref, xs_hbm, wg_ref, wu_ref, bg_ref, bu_ref, wd_ref, bd_ref,
                   ys_hbm, xbuf, act, ybuf, sem_x, sem_y, *, j1, j2, tn1, tn2):
    i = pl.program_id(0)
    j = pl.program_id(1)
    nb = inb_ref[i]
    sb = isb_ref[i]
    R = MOE_ROWS

    def x_copy(r):
        return pltpu.make_async_copy(xs_hbm.at[pl.ds((sb + r) * R, R), :],
                                     xbuf.at[pl.ds(r * R, R), :], sem_x)

    def y_copy(slot, r, jd):
        return pltpu.make_async_copy(
            ybuf.at[slot, pl.ds(r * R, R), :],
            ys_hbm.at[pl.ds((sb + r) * R, R), pl.ds(pl.multiple_of(jd * tn2, tn2), tn2)],
            sem_y.at[slot])

    def for_blocks(fn):
        for r in range(MOE_ITEM_BLOCKS):
            @pl.when(r < nb)
            def _():
                fn(r)

    @pl.when((nb > 0) & (j == 0))
    def _():
        for_blocks(lambda r: x_copy(r).start())
        for_blocks(lambda r: x_copy(r).wait())

    @pl.when((nb > 0) & (j < j1))
    def _():
        wg = wg_ref[...].astype(BF16)
        wu = wu_ref[...].astype(BF16)

        def body(r, c):
            r0 = pl.multiple_of(r * R, R)
            x = xbuf[pl.ds(r0, R), :]
            g = jnp.dot(x, wg, preferred_element_type=F32) + bg_ref[...]
            u = jnp.dot(x, wu, preferred_element_type=F32) + bu_ref[...]
            gt = jnp.minimum(g, SWIGLU_LIMIT)
            up = jnp.clip(u, -SWIGLU_LIMIT, SWIGLU_LIMIT)
            a = gt * jax.nn.sigmoid(SWIGLU_ALPHA * gt) * (up + 1.0)
            act[j, pl.ds(r0, R), :] = a.astype(BF16)
            return c

        lax.fori_loop(0, nb, body, 0)

    @pl.when((nb > 0) & (j >= j1))
    def _():
        jd = j - j1
        slot = jd % 2

        @pl.when(jd >= 2)
        def _():
            for_blocks(lambda r: y_copy(slot, r, jd).wait())

        wd = wd_ref[...].astype(BF16)

        def body(r, c):
            r0 = pl.multiple_of(r * R, R)
            y = bd_ref[...] + jnp.dot(act[0, pl.ds(r0, R), :], wd[0:tn1, :], preferred_element_type=F32)
            for c in range(1, j1):
                y = y + jnp.dot(act[c, pl.ds(r0, R), :], wd[c * tn1:(c + 1) * tn1, :],
                                preferred_element_type=F32)
            ybuf[slot, pl.ds(r0, R), :] = y
            return c

        lax.fori_loop(0, nb, body, 0)
        for_blocks(lambda r: y_copy(slot, r, jd).start())

        @pl.when(jd == j2 - 1)
        def _():
            for_blocks(lambda r: y_copy(1 - slot, r, jd).wait())
            for_blocks(lambda r: y_copy(slot, r, jd).wait())


def expert_mlp(xs, item_e, item_sb, item_nb, w_gate_up, b_gate_up, w_down, b_down, tn1=256, tn2=512):
    p, d = xs.shape
    n_items = item_e.shape[0]
    de = w_down.shape[1]
    j1 = de // tn1
    j2 = d // tn2
    rmax = MOE_ITEM_BLOCKS * MOE_ROWS

    def jj(i, j, nb):
        return jnp.where(nb[i] > 0, j, j1 + j2 - 1)

    def gate_map(i, j, ie, isb, inb):
        return (ie[i], 0, jnp.minimum(jj(i, j, inb), j1 - 1))

    def up_map(i, j, ie, isb, inb):
        return (ie[i], 0, j1 + jnp.minimum(jj(i, j, inb), j1 - 1))

    def down_map(i, j, ie, isb, inb):
        return (ie[i], 0, jnp.maximum(jj(i, j, inb) - j1, 0))

    return pl.pallas_call(
        functools.partial(_expert_kernel, j1=j1, j2=j2, tn1=tn1, tn2=tn2),
        grid_spec=pltpu.PrefetchScalarGridSpec(
            num_scalar_prefetch=3,
            grid=(n_items, j1 + j2),
            in_specs=[
                pl.BlockSpec(memory_space=pl.ANY),
                pl.BlockSpec((None, d, tn1), gate_map),
                pl.BlockSpec((None, d, tn1), up_map),
                pl.BlockSpec((None, 1, tn1), gate_map),
                pl.BlockSpec((None, 1, tn1), up_map),
                pl.BlockSpec((None, de, tn2), down_map),
                pl.BlockSpec((None, 1, tn2), down_map),
            ],
            out_specs=pl.BlockSpec(memory_space=pl.ANY),
            scratch_shapes=[
                pltpu.VMEM((rmax, d), BF16),
                pltpu.VMEM((j1, rmax, tn1), BF16),
                pltpu.VMEM((2, rmax, tn2), F32),
                pltpu.SemaphoreType.DMA(()),
                pltpu.SemaphoreType.DMA((2,)),
            ],
        ),
        out_shape=jax.ShapeDtypeStruct((p, d), F32),
        compiler_params=_params("arbitrary", "arbitrary", vmem=VMEM_LIMIT),
        name="expert_mlp",
    )(item_e, item_sb, item_nb, xs, w_gate_up, w_gate_up,
      b_gate_up.reshape(N_EXPERTS, 1, -1), b_gate_up.reshape(N_EXPERTS, 1, -1),
      w_down, b_down.reshape(N_EXPERTS, 1, -1))


def _combine_kernel(dest_ref, h_ref, g_ref, ys_hbm, o_ref, buf, sem, *, tt):
    i = pl.program_id(0)

    def row_copy(n):
        t = n // TOP_K
        k = n % TOP_K
        src = dest_ref[i * tt * TOP_K + n]
        return pltpu.make_async_copy(ys_hbm.at[pl.ds(src, 1), :], buf.at[k, pl.ds(t, 1), :], sem)

    def start(n, c):
        row_copy(n).start()
        return c

    def wait(n, c):
        row_copy(n).wait()
        return c

    lax.fori_loop(0, tt * TOP_K, start, 0)
    lax.fori_loop(0, tt * TOP_K, wait, 0)
    acc = h_ref[...]
    g = g_ref[...]
    for k in range(TOP_K):
        acc = acc + g[:, k:k + 1] * buf[k]
    o_ref[...] = acc


def combine(h2, gates, ys, dest, tt=64):
    n, d = h2.shape
    return pl.pallas_call(
        functools.partial(_combine_kernel, tt=tt),
        grid_spec=pltpu.PrefetchScalarGridSpec(
            num_scalar_prefetch=1,
            grid=(n // tt,),
            in_specs=[pl.BlockSpec((tt, d), lambda i, dst: (i, 0)),
                      pl.BlockSpec((tt, LANES), lambda i, dst: (i, 0)),
                      pl.BlockSpec(memory_space=pl.ANY)],
            out_specs=pl.BlockSpec((tt, d), lambda i, dst: (i, 0)),
            scratch_shapes=[pltpu.VMEM((TOP_K, tt, d), F32), pltpu.SemaphoreType.DMA(())],
        ),
        out_shape=jax.ShapeDtypeStruct((n, d), F32),
        compiler_params=_params("arbitrary", vmem=VMEM_LIMIT),
        name="moe_combine",
    )(dest, h2, gates, ys)


def routing_tables(top_idx, n_tok):
    a = n_tok * TOP_K
    e_flat = top_idx.reshape(-1)
    onehot = (e_flat[:, None] == jnp.arange(N_EXPERTS, dtype=jnp.int32)[None, :]).astype(jnp.int32)
    csum = jnp.cumsum(onehot, axis=0)
    rank = jnp.sum(csum * onehot, axis=1) - 1
    counts = csum[-1]
    nblk = (counts + MOE_ROWS - 1) // MOE_ROWS
    blk_end = jnp.cumsum(nblk)
    blk_start = blk_end - nblk
    dest = blk_start[e_flat] * MOE_ROWS + rank
    n_blocks = -(-(a + N_EXPERTS * (MOE_ROWS - 1)) // MOE_ROWS)
    p = n_blocks * MOE_ROWS
    t_flat = jnp.arange(a, dtype=jnp.int32) // TOP_K
    row_tok = jnp.zeros((p,), jnp.int32).at[dest].set(t_flat)

    n_items = N_EXPERTS + n_blocks // MOE_ITEM_BLOCKS
    items_per_e = (nblk + MOE_ITEM_BLOCKS - 1) // MOE_ITEM_BLOCKS
    item_end = jnp.cumsum(items_per_e)
    item_start = item_end - items_per_e
    total_items = item_end[-1]
    ii = jnp.arange(n_items, dtype=jnp.int32)
    ic = jnp.minimum(ii, total_items - 1)
    e_i = jnp.minimum(jnp.searchsorted(item_end, ic, side='right'), N_EXPERTS - 1).astype(jnp.int32)
    local = ic - item_start[e_i]
    item_sb = (blk_start[e_i] + local * MOE_ITEM_BLOCKS).astype(jnp.int32)
    item_nb = jnp.clip(nblk[e_i] - local * MOE_ITEM_BLOCKS, 0, MOE_ITEM_BLOCKS)
    item_nb = jnp.where(ii < total_items, item_nb, 0).astype(jnp.int32)
    return dest.astype(jnp.int32), row_tok, e_i, item_sb, item_nb


def kernel(x, meta_tokens, mix_norm, w_in, dn_conv, dn_a_log, dn_dt_bias, dn_out_norm, da_q_norm, da_k_norm, da_lam_q1, da_lam_k1, da_lam_q2, da_lam_k2, da_sub_norm, w_branch_a, w_branch_b, w_out, ffn_norm, router_w, router_b, w_gate_up, b_gate_up, w_down, b_down):
    bsz, seq, d = x.shape
    n_tok = bsz * seq
    x2 = x.reshape(n_tok, d)
    layer = 0

    w = w_in[layer]
    c_ba = 4 * DN_W
    w_main = jnp.concatenate([w[:, :c_ba], w[:, c_ba + 2 * DN_HEADS:]], axis=1).astype(BF16)
    w_ba = jnp.pad(w[:, c_ba:c_ba + 2 * DN_HEADS], ((0, 0), (0, LANES - 2 * DN_HEADS))).astype(BF16)

    u_x = rms_cast(x2, mix_norm[layer], 256)
    u_m = rms_cast(meta_tokens.astype(F32), mix_norm[layer], N_META)
    proj_x = matmul(u_x, w_main, BF16, 1024, 1024, 1024, name="in_proj")
    proj_m = matmul(u_m, w_main, BF16, N_META, 1024, 1024, name="in_proj_meta")
    ba_x = matmul(u_x, w_ba, F32, 1024, LANES, 1024, name="in_proj_ba")
    ba_m = matmul(u_m, w_ba, F32, N_META, LANES, 1024, name="in_proj_ba_meta")

    y_a = gated_deltanet(proj_x, ba_x, proj_m, ba_m, dn_conv[layer], dn_a_log[layer],
                         dn_dt_bias[layer], dn_out_norm[layer], bsz, seq)
    lam4 = jnp.stack([da_lam_q1[layer], da_lam_k1[layer], da_lam_q2[layer], da_lam_k2[layer]])
    y_b = diff_attention(proj_x, proj_m, da_q_norm[layer], da_k_norm[layer], lam4,
                         da_sub_norm[layer], bsz, seq)

    merged = branch_merge(y_a, y_b, w_branch_a[layer], w_branch_b[layer], proj_x, d)
    h2 = matmul(merged, w_out[layer], F32, 1024, 1024, 1024, res=x2, name="out_proj")

    top_idx, gates = router(h2, ffn_norm[layer], router_w[layer], router_b[layer])
    dest, row_tok, item_e, item_sb, item_nb = routing_tables(top_idx[:, :TOP_K], n_tok)
    xs = gather_norm(h2, row_tok, ffn_norm[layer])
    ys = expert_mlp(xs, item_e, item_sb, item_nb, w_gate_up[layer], b_gate_up[layer],
                    w_down[layer], b_down[layer])
    out = combine(h2, gates, ys, dest)
    return out.reshape(bsz, seq, d)
```

```python
import functools
import math

import jax
import jax.numpy as jnp
from jax import lax
from jax.experimental import pallas as pl
from jax.experimental.pallas import tpu as pltpu

F32 = jnp.float32
BF16 = jnp.bfloat16

N_META = 16
RMS_EPS = 1e-6
L2_EPS = 1e-6

DN_HEADS = 16
DN_DK = 128
DN_DV = 128
DN_CONV = 4
DN_CHUNK = 64
DN_W = DN_HEADS * DN_DK
DN_HIST = 16
DN_CONV_COLS = 1024

DA_HEADS = 8
DA_DK = 128
DA_DV = 256
LAMBDA_INIT = 0.8 - 0.6 * math.exp(-0.3 * 0)

N_EXPERTS = 32
TOP_K = 4
D_EXPERT = 1536
SWIGLU_LIMIT = 7.0
SWIGLU_ALPHA = 1.702

LANES = 128
MOE_ROWS = 256
MOE_ITEM_BLOCKS = 6
NEG_BIG = -1e30
VMEM_LIMIT = 56 * 1024 * 1024

COL_DZ = 3 * DN_W
COL_AQ = 4 * DN_W
COL_AK = COL_AQ + DA_HEADS * 2 * DA_DK
COL_AV = COL_AK + DA_HEADS * 2 * DA_DK
COL_GA = COL_AV + DA_HEADS * DA_DV


def _params(*sem, vmem=None):
    return pltpu.CompilerParams(dimension_semantics=sem, vmem_limit_bytes=vmem)


def _rms_cast_kernel(x_ref, w_ref, o_ref):
    x = x_ref[...]
    y = x * lax.rsqrt(jnp.mean(x * x, axis=-1, keepdims=True) + RMS_EPS) * w_ref[...]
    o_ref[...] = y.astype(o_ref.dtype)


def rms_cast(x, w, tm):
    m, d = x.shape
    return pl.pallas_call(
        _rms_cast_kernel,
        grid=(m // tm,),
        in_specs=[pl.BlockSpec((tm, d), lambda i: (i, 0)),
                  pl.BlockSpec((1, d), lambda i: (0, 0))],
        out_specs=pl.BlockSpec((tm, d), lambda i: (i, 0)),
        out_shape=jax.ShapeDtypeStruct((m, d), BF16),
        compiler_params=_params("parallel"),
        name="rms_cast",
    )(x, w.reshape(1, d))


def _mm_fullk_kernel(a_ref, b_ref, *rest, has_res):
    acc = jnp.dot(a_ref[...], b_ref[...].astype(BF16), preferred_element_type=F32)
    if has_res:
        r_ref, o_ref = rest
        acc = acc + r_ref[...]
    else:
        o_ref, = rest
    o_ref[...] = acc.astype(o_ref.dtype)


def matmul_fullk(a, b, out_dtype, tm, tn, res=None, name="matmul"):
    m, kd = a.shape
    _, n = b.shape
    in_specs = [pl.BlockSpec((tm, kd), lambda i, j: (i, 0)),
                pl.BlockSpec((kd, tn), lambda i, j: (0, j))]
    args = [a, b]
    if res is not None:
        in_specs.append(pl.BlockSpec((tm, tn), lambda i, j: (i, j)))
        args.append(res)
    return pl.pallas_call(
        functools.partial(_mm_fullk_kernel, has_res=res is not None),
        grid=(m // tm, n // tn),
        in_specs=in_specs,
        out_specs=pl.BlockSpec((tm, tn), lambda i, j: (i, j)),
        out_shape=jax.ShapeDtypeStruct((m, n), out_dtype),
        compiler_params=_params("parallel", "parallel", vmem=VMEM_LIMIT),
        name=name,
    )(*args)


def _mm_kernel(a_ref, b_ref, *rest, nk, has_res):
    if has_res:
        r_ref, o_ref, acc_ref = rest
    else:
        o_ref, acc_ref = rest
    k = pl.program_id(2)

    @pl.when(k == 0)
    def _():
        acc_ref[...] = jnp.zeros_like(acc_ref)

    acc_ref[...] += jnp.dot(a_ref[...], b_ref[...].astype(BF16), preferred_element_type=F32)

    @pl.when(k == nk - 1)
    def _():
        acc = acc_ref[...]
        if has_res:
            acc = acc + r_ref[...]
        o_ref[...] = acc.astype(o_ref.dtype)


def matmul(a, b, out_dtype, tm, tn, tk, res=None, name="matmul"):
    m, kd = a.shape
    _, n = b.shape
    nk = kd // tk
    in_specs = [pl.BlockSpec((tm, tk), lambda i, j, k: (i, k)),
                pl.BlockSpec((tk, tn), lambda i, j, k: (k, j))]
    args = [a, b]
    if res is not None:
        in_specs.append(pl.BlockSpec((tm, tn), lambda i, j, k: (i, j)))
        args.append(res)
    return pl.pallas_call(
        functools.partial(_mm_kernel, nk=nk, has_res=res is not None),
        grid=(m // tm, n // tn, nk),
        in_specs=in_specs,
        out_specs=pl.BlockSpec((tm, tn), lambda i, j, k: (i, j)),
        out_shape=jax.ShapeDtypeStruct((m, n), out_dtype),
        scratch_shapes=[pltpu.VMEM((tm, tn), F32)],
        compiler_params=_params("parallel", "parallel", "arbitrary", vmem=VMEM_LIMIT),
        name=name,
    )(*args)


def _deltanet_kernel(qkv_ref, z_ref, ba_ref, qkvm_ref, bam_ref, cw_ref, hp_ref, nw_ref,
                     o_ref, full_ref, act_ref, ba_s, s_ref):
    t = pl.program_id(1)
    C = DN_CHUNK
    HIST = DN_HIST
    n_pad = C - N_META

    @pl.when(t == 0)
    def _():
        s_ref[...] = jnp.zeros_like(s_ref)
        full_ref[0:HIST + n_pad, :] = jnp.zeros((HIST + n_pad, 3 * DN_W), BF16)
        full_ref[HIST + n_pad:HIST + C, :] = qkvm_ref[...]
        ba_s[0:n_pad, :] = jnp.zeros((n_pad, LANES), F32)
        ba_s[n_pad:C, :] = bam_ref[...]

    @pl.when(t > 0)
    def _():
        full_ref[HIST:HIST + C, :] = qkv_ref[...]
        ba_s[...] = ba_ref[...]

    n_shift = DN_CONV - 1
    sr = lax.broadcasted_iota(jnp.int32, (n_shift * C, HIST + C), 0)
    sc = lax.broadcasted_iota(jnp.int32, (n_shift * C, HIST + C), 1)
    shift = jnp.where(sc == HIST - n_shift + (sr % C) + (sr // C), 1.0, 0.0).astype(BF16)
    for c0 in range(0, 3 * DN_W, DN_CONV_COLS):
        cs = slice(c0, c0 + DN_CONV_COLS)
        delayed = jnp.dot(shift, full_ref[:, cs], preferred_element_type=F32)
        conv = cw_ref[n_shift:n_shift + 1, cs] * full_ref[HIST:HIST + C, cs].astype(F32)
        for j in range(n_shift):
            conv = conv + cw_ref[j:j + 1, cs] * delayed[j * C:(j + 1) * C, :]
        act_ref[:, cs] = conv * jax.nn.sigmoid(conv)
    full_ref[0:HIST, :] = full_ref[C:C + HIST, :]

    row1 = lax.broadcasted_iota(jnp.int32, (C, 1), 0)
    valid = jnp.where((t > 0) | (row1 >= n_pad), 1.0, 0.0).astype(F32)
    ba = ba_s[...]
    beta_all = jax.nn.sigmoid(ba) * valid
    xg = ba + hp_ref[1:2, :]
    softplus = jnp.maximum(xg, 0.0) + jnp.log1p(jnp.exp(-jnp.abs(xg)))
    g_all = -jnp.exp(hp_ref[0:1, :]) * softplus * valid

    row = lax.broadcasted_iota(jnp.int32, (C, C), 0)
    col = lax.broadcasted_iota(jnp.int32, (C, C), 1)
    causal = row >= col
    strict = row > col
    eye = jnp.where(row == col, 1.0, 0.0).astype(F32)
    tril = jnp.where(causal, 1.0, 0.0).astype(BF16)

    g_hi = g_all.astype(BF16)
    r1 = g_all - g_hi.astype(F32)
    g_mid = r1.astype(BF16)
    g_lo = (r1 - g_mid.astype(F32)).astype(BF16)
    g3 = jnp.dot(tril, jnp.concatenate([g_hi, g_mid, g_lo], axis=1), preferred_element_type=F32)
    gcum = g3[:, 0:LANES] + g3[:, LANES:2 * LANES] + g3[:, 2 * LANES:3 * LANES]
    gcum_t = gcum.T

    scale = DN_DK ** -0.5
    nw = nw_ref[...]
    heads = range(DN_HEADS)

    def mm(a, b):
        return jnp.dot(a.astype(BF16), b.astype(BF16), preferred_element_type=F32)

    kn_b, lhs1, decay, rhs, qe_b, kdec_t, e_last = [], [], [], [], [], [], []
    for h in heads:
        qh = act_ref[:, h * DN_DK:(h + 1) * DN_DK]
        kh = act_ref[:, DN_W + h * DN_DK:DN_W + (h + 1) * DN_DK]
        vh = act_ref[:, 2 * DN_W + h * DN_DV:2 * DN_W + (h + 1) * DN_DV]
        qn = qh * (lax.rsqrt(jnp.sum(qh * qh, axis=-1, keepdims=True) + L2_EPS) * scale)
        kn = kh * lax.rsqrt(jnp.sum(kh * kh, axis=-1, keepdims=True) + L2_EPS)
        beta = beta_all[:, h:h + 1]
        gc = gcum[:, DN_HEADS + h:DN_HEADS + h + 1]
        gr = gcum_t[DN_HEADS + h:DN_HEADS + h + 1, :]
        g_last = gc[C - 1:C, :]
        e_g = jnp.exp(gc)
        kb = kn * beta
        kn_b.append(kn.astype(BF16))
        lhs1.append(jnp.concatenate([kb, qn], axis=0).astype(BF16))
        decay.append(jnp.exp(jnp.where(causal, gc - gr, NEG_BIG)))
        rhs.append(jnp.concatenate([vh * beta, kb * e_g], axis=1).astype(BF16))
        qe_b.append((qn * e_g).astype(BF16))
        kdec_t.append((kn * jnp.exp(g_last - gc)).T.astype(BF16))
        e_last.append(jnp.exp(g_last))

    kq = [lax.dot_general(lhs1[h], kn_b[h], (((1,), (1,)), ((), ())), preferred_element_type=F32)
          for h in heads]
    qk_b = [jnp.where(causal, kq[h][C:2 * C, :] * decay[h], 0.0).astype(BF16) for h in heads]
    b1 = [-jnp.where(strict, kq[h][0:C, :] * decay[h], 0.0) for h in heads]
    b2 = [mm(b1[h], b1[h]) for h in heads]
    b4 = [mm(b2[h], b2[h]) for h in heads]
    p1 = [(eye + b1[h]) + mm(eye + b1[h], b2[h]) for h in heads]
    b8 = [mm(b4[h], b4[h]) for h in heads]
    b16 = [mm(b8[h], b8[h]) for h in heads]
    p2 = [(eye + b4[h]) + mm(eye + b4[h], b8[h]) for h in heads]
    b32 = [mm(b16[h], b16[h]) for h in heads]
    p12 = [mm(p1[h], p2[h]) for h in heads]
    p3 = [(eye + b16[h]) + mm(eye + b16[h], b32[h]) for h in heads]
    inv = [mm(p12[h], p3[h]) for h in heads]
    sol = [mm(inv[h], rhs[h]) for h in heads]

    s_old = [s_ref[h] for h in heads]
    ws_lhs = [jnp.concatenate([sol[h][:, DN_DV:2 * DN_DV].astype(BF16), qe_b[h]], axis=0) for h in heads]
    ws = [mm(ws_lhs[h], s_old[h]) for h in heads]
    v_new = [(sol[h][:, 0:DN_DV] - ws[h][0:C, :]).astype(BF16) for h in heads]
    ov_lhs = [jnp.concatenate([qk_b[h], kdec_t[h]], axis=0) for h in heads]
    ov = [mm(ov_lhs[h], v_new[h]) for h in heads]
    for h in heads:
        s_ref[h] = s_old[h] * e_last[h] + ov[h][C:C + DN_DK, :]
        o = ws[h][C:2 * C, :] + ov[h][0:C, :]
        zh = z_ref[:, h * DN_DV:(h + 1) * DN_DV].astype(F32)
        o = o * lax.rsqrt(jnp.mean(o * o, axis=-1, keepdims=True) + RMS_EPS) * nw
        o_ref[:, h * DN_DV:(h + 1) * DN_DV] = (o * (zh * jax.nn.sigmoid(zh))).astype(o_ref.dtype)


def gated_deltanet(proj_x, ba_x, proj_m, ba_m, conv_w, a_log, dt_bias, norm_w, bsz, seq):
    C = DN_CHUNK
    n_chunks = seq // C
    w3 = 3 * DN_W
    hp = jnp.zeros((2, LANES), F32)
    hp = hp.at[0, DN_HEADS:2 * DN_HEADS].set(a_log.astype(F32))
    hp = hp.at[1, DN_HEADS:2 * DN_HEADS].set(dt_bias.astype(F32))

    def xrow(b, t):
        return b * n_chunks + jnp.maximum(t - 1, 0)

    return pl.pallas_call(
        _deltanet_kernel,
        grid=(bsz, n_chunks + 1),
        in_specs=[
            pl.BlockSpec((C, w3), lambda b, t: (xrow(b, t), 0)),
            pl.BlockSpec((C, DN_W), lambda b, t: (xrow(b, t), COL_DZ // DN_W)),
            pl.BlockSpec((C, LANES), lambda b, t: (xrow(b, t), 0)),
            pl.BlockSpec((N_META, w3), lambda b, t: (0, 0)),
            pl.BlockSpec((N_META, LANES), lambda b, t: (0, 0)),
            pl.BlockSpec((DN_CONV, w3), lambda b, t: (0, 0)),
            pl.BlockSpec((2, LANES), lambda b, t: (0, 0)),
            pl.BlockSpec((1, DN_DV), lambda b, t: (0, 0)),
        ],
        out_specs=pl.BlockSpec((C, DN_W), lambda b, t: (xrow(b, t), 0)),
        out_shape=jax.ShapeDtypeStruct((bsz * seq, DN_W), BF16),
        scratch_shapes=[
            pltpu.VMEM((DN_HIST + C, w3), BF16),
            pltpu.VMEM((C, w3), F32),
            pltpu.VMEM((C, LANES), F32),
            pltpu.VMEM((DN_HEADS, DN_DK, DN_DV), F32),
        ],
        compiler_params=_params("parallel", "arbitrary", vmem=VMEM_LIMIT),
        name="gated_deltanet",
    )(proj_x, proj_x, ba_x, proj_m, ba_m, conv_w.astype(F32), hp, norm_w.reshape(1, DN_DV).astype(F32))


def _diff_attn_kernel(q_ref, k_ref, v_ref, km_ref, vm_ref, qn_ref, kn_ref, lam_ref, sn_ref,
                      o_ref, kx_s, kmeta_s, *, tq, seq):
    qi = pl.program_id(2)

    def rms(x, w):
        return x * lax.rsqrt(jnp.mean(x * x, axis=-1, keepdims=True) + RMS_EPS) * w

    @pl.when(qi == 0)
    def _():
        for m in range(2):
            kx = k_ref[:, m * DA_DK:(m + 1) * DA_DK].astype(F32)
            kx_s[m] = rms(kx, kn_ref[...]).astype(BF16)
            kmx = km_ref[:, m * DA_DK:(m + 1) * DA_DK].astype(F32)
            kmeta_s[m] = rms(kmx, kn_ref[...]).astype(BF16)

    lam = (jnp.exp(jnp.sum(lam_ref[0:1, :] * lam_ref[1:2, :], axis=-1, keepdims=True))
           - jnp.exp(jnp.sum(lam_ref[2:3, :] * lam_ref[3:4, :], axis=-1, keepdims=True))
           + LAMBDA_INIT)
    row = lax.broadcasted_iota(jnp.int32, (tq, tq), 0)
    col = lax.broadcasted_iota(jnp.int32, (tq, tq), 1)
    diag_visible = col <= row
    v_m = vm_ref[...]
    scale = DA_DK ** -0.5 * math.log2(math.e)
    nt = (((1,), (1,)), ((), ()))

    def attend(n_prev):
        outs = []
        for m in range(2):
            qx = q_ref[:, m * DA_DK:(m + 1) * DA_DK].astype(F32)
            qn = (rms(qx, qn_ref[...]) * scale).astype(BF16)
            sd = lax.dot_general(qn, kx_s[m, n_prev:n_prev + tq, :], nt, preferred_element_type=F32)
            sd = jnp.where(diag_visible, sd, NEG_BIG)
            sm = lax.dot_general(qn, kmeta_s[m], nt, preferred_element_type=F32)
            mx = jnp.maximum(jnp.max(sd, axis=-1, keepdims=True), jnp.max(sm, axis=-1, keepdims=True))
            if n_prev:
                sp = lax.dot_general(qn, kx_s[m, 0:n_prev, :], nt, preferred_element_type=F32)
                mx = jnp.maximum(mx, jnp.max(sp, axis=-1, keepdims=True))
            pd = jnp.exp2(sd - mx)
            pm = jnp.exp2(sm - mx)
            den = jnp.sum(pd, axis=-1, keepdims=True) + jnp.sum(pm, axis=-1, keepdims=True)
            acc = (jnp.dot(pd.astype(BF16), v_ref[n_prev:n_prev + tq, :], preferred_element_type=F32)
                   + jnp.dot(pm.astype(BF16), v_m, preferred_element_type=F32))
            if n_prev:
                pp = jnp.exp2(sp - mx)
                den = den + jnp.sum(pp, axis=-1, keepdims=True)
                acc = acc + jnp.dot(pp.astype(BF16), v_ref[0:n_prev, :], preferred_element_type=F32)
            outs.append(acc / den)
        out = outs[0] - lam * outs[1]
        out = rms(out, sn_ref[...]) * (1.0 - LAMBDA_INIT)
        o_ref[...] = out.astype(o_ref.dtype)

    for blk in range(seq // tq):
        @pl.when(qi == blk)
        def _():
            attend(blk * tq)


def diff_attention(proj_x, proj_m, q_norm, k_norm, lam4, sub_norm, bsz, seq, tq=256):
    nq = seq // tq
    cq = COL_AQ // DA_DV
    ck = COL_AK // DA_DV
    cv = COL_AV // DA_DV
    return pl.pallas_call(
        functools.partial(_diff_attn_kernel, tq=tq, seq=seq),
        grid=(bsz, DA_HEADS, nq),
        in_specs=[
            pl.BlockSpec((tq, DA_DV), lambda b, h, i: (b * nq + i, cq + h)),
            pl.BlockSpec((seq, DA_DV), lambda b, h, i: (b, ck + h)),
            pl.BlockSpec((seq, DA_DV), lambda b, h, i: (b, cv + h)),
            pl.BlockSpec((N_META, DA_DV), lambda b, h, i: (0, ck + h)),
            pl.BlockSpec((N_META, DA_DV), lambda b, h, i: (0, cv + h)),
            pl.BlockSpec((1, DA_DK), lambda b, h, i: (0, 0)),
            pl.BlockSpec((1, DA_DK), lambda b, h, i: (0, 0)),
            pl.BlockSpec((4, DA_DK), lambda b, h, i: (0, 0)),
            pl.BlockSpec((1, DA_DV), lambda b, h, i: (0, 0)),
        ],
        out_specs=pl.BlockSpec((tq, DA_DV), lambda b, h, i: (b * nq + i, h)),
        out_shape=jax.ShapeDtypeStruct((bsz * seq, DA_HEADS * DA_DV), BF16),
        scratch_shapes=[pltpu.VMEM((2, seq, DA_DK), BF16),
                        pltpu.VMEM((2, N_META, DA_DK), BF16)],
        compiler_params=_params("parallel", "parallel", "arbitrary", vmem=VMEM_LIMIT),
        name="diff_attention",
    )(proj_x, proj_x, proj_x, proj_m, proj_m,
      q_norm.reshape(1, DA_DK).astype(F32), k_norm.reshape(1, DA_DK).astype(F32),
      lam4.astype(F32), sub_norm.reshape(1, DA_DV).astype(F32))


def _merge_kernel(ya_ref, yb_ref, wa_ref, wb_ref, ga_ref, gb_ref, o_ref):
    pa = jnp.dot(ya_ref[...], wa_ref[...].astype(BF16), preferred_element_type=F32)
    pb = jnp.dot(yb_ref[...], wb_ref[...].astype(BF16), preferred_element_type=F32)
    ga = jax.nn.sigmoid(ga_ref[...].astype(F32))
    gb = jax.nn.sigmoid(gb_ref[...].astype(F32))
    o_ref[...] = (ga * pa + gb * pb).astype(o_ref.dtype)


def branch_merge(y_a, y_b, w_a, w_b, proj_x, d_model, tm=1024, tn=512):
    m, ka = y_a.shape
    kb = y_b.shape[1]
    cga = COL_GA // tn
    cgb = (COL_GA + d_model) // tn
    return pl.pallas_call(
        _merge_kernel,
        grid=(m // tm, d_model // tn),
        in_specs=[
            pl.BlockSpec((tm, ka), lambda i, j: (i, 0)),
            pl.BlockSpec((tm, kb), lambda i, j: (i, 0)),
            pl.BlockSpec((ka, tn), lambda i, j: (0, j)),
            pl.BlockSpec((kb, tn), lambda i, j: (0, j)),
            pl.BlockSpec((tm, tn), lambda i, j: (i, cga + j)),
            pl.BlockSpec((tm, tn), lambda i, j: (i, cgb + j)),
        ],
        out_specs=pl.BlockSpec((tm, tn), lambda i, j: (i, j)),
        out_shape=jax.ShapeDtypeStruct((m, d_model), BF16),
        compiler_params=_params("parallel", "parallel", vmem=VMEM_LIMIT),
        name="branch_merge",
    )(y_a, y_b, w_a, w_b, proj_x, proj_x)


def _router_kernel(h_ref, nw_ref, whi_ref, wlo_ref, b_ref, idx_ref, gate_ref):
    x = h_ref[...]
    u = x * lax.rsqrt(jnp.mean(x * x, axis=-1, keepdims=True) + RMS_EPS) * nw_ref[...]
    u_hi = u.astype(BF16)
    u_lo = (u - u_hi.astype(F32)).astype(BF16)
    logits = (jnp.dot(u_hi, whi_ref[...], preferred_element_type=F32)
              + jnp.dot(u_lo, whi_ref[...], preferred_element_type=F32)
              + jnp.dot(u_hi, wlo_ref[...], preferred_element_type=F32)
              + b_ref[...])
    lane = lax.broadcasted_iota(jnp.int32, logits.shape, 1)
    lane_f = lane.astype(F32)
    vals = logits
    tops, idxs = [], []
    for _ in range(TOP_K):
        mx = jnp.max(vals, axis=-1, keepdims=True)
        ix = jnp.min(jnp.where(vals == mx, lane_f, float(LANES)), axis=-1, keepdims=True).astype(jnp.int32)
        tops.append(mx)
        idxs.append(ix)
        vals = jnp.where(lane == ix, -3.0e38, vals)
    exps = [jnp.exp(tv - tops[0]) for tv in tops]
    den = exps[0] + exps[1] + exps[2] + exps[3]
    idx_out = jnp.zeros(logits.shape, jnp.int32)
    gate_out = jnp.zeros(logits.shape, F32)
    for k in range(TOP_K):
        idx_out = jnp.where(lane == k, idxs[k], idx_out)
        gate_out = jnp.where(lane == k, exps[k] / den, gate_out)
    idx_ref[...] = idx_out
    gate_ref[...] = gate_out


def router(h2, ffn_norm, router_w, router_b, tm=256):
    n, d = h2.shape
    wpad = jnp.zeros((d, LANES), F32).at[:, :N_EXPERTS].set(router_w.astype(F32))
    w_hi = wpad.astype(BF16)
    w_lo = (wpad - w_hi.astype(F32)).astype(BF16)
    bias = jnp.full((1, LANES), NEG_BIG, F32).at[0, :N_EXPERTS].set(router_b.astype(F32))
    return pl.pallas_call(
        _router_kernel,
        grid=(n // tm,),
        in_specs=[
            pl.BlockSpec((tm, d), lambda i: (i, 0)),
            pl.BlockSpec((1, d), lambda i: (0, 0)),
            pl.BlockSpec((d, LANES), lambda i: (0, 0)),
            pl.BlockSpec((d, LANES), lambda i: (0, 0)),
            pl.BlockSpec((1, LANES), lambda i: (0, 0)),
        ],
        out_specs=[pl.BlockSpec((tm, LANES), lambda i: (i, 0)),
                   pl.BlockSpec((tm, LANES), lambda i: (i, 0))],
        out_shape=[jax.ShapeDtypeStruct((n, LANES), jnp.int32),
                   jax.ShapeDtypeStruct((n, LANES), F32)],
        compiler_params=_params("parallel", vmem=VMEM_LIMIT),
        name="router",
    )(h2, ffn_norm.reshape(1, d).astype(F32), w_hi, w_lo, bias)


def _gather_norm_kernel(tok_ref, h_hbm, nw_ref, o_ref, buf, sem, *, rows, n_steps):
    i = pl.program_id(0)

    def issue(blk, slot):
        def start(r, c):
            tok = tok_ref[blk * rows + r]
            pltpu.make_async_copy(h_hbm.at[pl.ds(tok, 1), :], buf.at[slot, pl.ds(r, 1), :],
                                  sem.at[slot]).start()
            return c
        lax.fori_loop(0, rows, start, 0, unroll=8)

    @pl.when(i == 0)
    def _():
        issue(0, 0)

    slot = i % 2

    @pl.when(i + 1 < n_steps)
    def _():
        issue(i + 1, 1 - slot)

    pltpu.make_async_copy(h_hbm.at[pl.ds(0, rows), :], buf.at[slot], sem.at[slot]).wait()
    x = buf[slot]
    y = x * lax.rsqrt(jnp.mean(x * x, axis=-1, keepdims=True) + RMS_EPS) * nw_ref[...]
    o_ref[...] = y.astype(o_ref.dtype)


def gather_norm(h2, row_tok, ffn_norm, rows=MOE_ROWS):
    n, d = h2.shape
    p = row_tok.shape[0]
    return pl.pallas_call(
        functools.partial(_gather_norm_kernel, rows=rows, n_steps=p // rows),
        grid_spec=pltpu.PrefetchScalarGridSpec(
            num_scalar_prefetch=1,
            grid=(p // rows,),
            in_specs=[pl.BlockSpec(memory_space=pl.ANY),
                      pl.BlockSpec((1, d), lambda i, tok: (0, 0))],
            out_specs=pl.BlockSpec((rows, d), lambda i, tok: (i, 0)),
            scratch_shapes=[pltpu.VMEM((2, rows, d), F32), pltpu.SemaphoreType.DMA((2,))],
        ),
        out_shape=jax.ShapeDtypeStruct((p, d), BF16),
        compiler_params=_params("arbitrary", vmem=VMEM_LIMIT),
        name="gather_norm",
    )(row_tok, h2, ffn_norm.reshape(1, d).astype(F32))


def _expert_kernel(ie_ref, isb_ref, inb_ref, xs_hbm, wg_ref, wu_ref, bg_ref, bu_ref, wd_ref, bd_ref,
                   ys_hbm, xbuf, act, ybuf, sem_x, sem_y, *, j1, j2, tn1, tn2):
    i = pl.program_id(0)
    j = pl.program_id(1)
    nb = inb_ref[i]
    sb = isb_ref[i]
    R = MOE_ROWS

    def x_copy(r):
        return pltpu.make_async_copy(xs_hbm.at[pl.ds((sb + r) * R, R), :],
                                     xbuf.at[pl.ds(r * R, R), :], sem_x)

    def y_copy(slot, r, jd):
        return pltpu.make_async_copy(
            ybuf.at[slot, pl.ds(r * R, R), :],
            ys_hbm.at[pl.ds((sb + r) * R, R), pl.ds(pl.multiple_of(jd * tn2, tn2), tn2)],
            sem_y.at[slot])

    def for_blocks(fn):
        for r in range(MOE_ITEM_BLOCKS):
            @pl.when(r < nb)
            def _():
                fn(r)

    @pl.when((nb > 0) & (j == 0))
    def _():
        for_blocks(lambda r: x_copy(r).start())
        for_blocks(lambda r: x_copy(r).wait())

    @pl.when((nb > 0) & (j < j1))
    def _():
        wg = wg_ref[...].astype(BF16)
        wu = wu_ref[...].astype(BF16)

        def body(r, c):
            r0 = pl.multiple_of(r * R, R)
            x = xbuf[pl.ds(r0, R), :]
            g = jnp.dot(x, wg, preferred_element_type=F32) + bg_ref[...]
            u = jnp.dot(x, wu, preferred_element_type=F32) + bu_ref[...]
            gt = jnp.minimum(g, SWIGLU_LIMIT)
            up = jnp.clip(u, -SWIGLU_LIMIT, SWIGLU_LIMIT)
            a = gt * jax.nn.sigmoid(SWIGLU_ALPHA * gt) * (up + 1.0)
            act[j, pl.ds(r0, R), :] = a.astype(BF16)
            return c

        lax.fori_loop(0, nb, body, 0)

    @pl.when((nb > 0) & (j >= j1))
    def _():
        jd = j - j1
        slot = jd % 2

        @pl.when(jd >= 2)
        def _():
            for_blocks(lambda r: y_copy(slot, r, jd).wait())

        wd = wd_ref[...].astype(BF16)

        def body(r, c):
            r0 = pl.multiple_of(r * R, R)
            y = bd_ref[...] + jnp.dot(act[0, pl.ds(r0, R), :], wd[0:tn1, :], preferred_element_type=F32)
            for c in range(1, j1):
                y = y + jnp.dot(act[c, pl.ds(r0, R), :], wd[c * tn1:(c + 1) * tn1, :],
                                preferred_element_type=F32)
            ybuf[slot, pl.ds(r0, R), :] = y
            return c

        lax.fori_loop(0, nb, body, 0)
        for_blocks(lambda r: y_copy(slot, r, jd).start())

        @pl.when(jd == j2 - 1)
        def _():
            for_blocks(lambda r: y_copy(1 - slot, r, jd).wait())
            for_blocks(lambda r: y_copy(slot, r, jd).wait())


def expert_mlp(xs, item_e, item_sb, item_nb, w_gate_up, b_gate_up, w_down, b_down, tn1=256, tn2=512):
    p, d = xs.shape
    n_items = item_e.shape[0]
    de = w_down.shape[1]
    j1 = de // tn1
    j2 = d // tn2
    rmax = MOE_ITEM_BLOCKS * MOE_ROWS

    def jj(i, j, nb):
        return jnp.where(nb[i] > 0, j, j1 + j2 - 1)

    def gate_map(i, j, ie, isb, inb):
        return (ie[i], 0, jnp.minimum(jj(i, j, inb), j1 - 1))

    def up_map(i, j, ie, isb, inb):
        return (ie[i], 0, j1 + jnp.minimum(jj(i, j, inb), j1 - 1))

    def down_map(i, j, ie, isb, inb):
        return (ie[i], 0, jnp.maximum(jj(i, j, inb) - j1, 0))

    return pl.pallas_call(
        functools.partial(_expert_kernel, j1=j1, j2=j2, tn1=tn1, tn2=tn2),
        grid_spec=pltpu.PrefetchScalarGridSpec(
            num_scalar_prefetch=3,
            grid=(n_items, j1 + j2),
            in_specs=[
                pl.BlockSpec(memory_space=pl.ANY),
                pl.BlockSpec((None, d, tn1), gate_map),
                pl.BlockSpec((None, d, tn1), up_map),
                pl.BlockSpec((None, 1, tn1), gate_map),
                pl.BlockSpec((None, 1, tn1), up_map),
                pl.BlockSpec((None, de, tn2), down_map),
                pl.BlockSpec((None, 1, tn2), down_map),
            ],
            out_specs=pl.BlockSpec(memory_space=pl.ANY),
            scratch_shapes=[
                pltpu.VMEM((rmax, d), BF16),
                pltpu.VMEM((j1, rmax, tn1), BF16),
                pltpu.VMEM((2, rmax, tn2), F32),
                pltpu.SemaphoreType.DMA(()),
                pltpu.SemaphoreType.DMA((2,)),
            ],
        ),
        out_shape=jax.ShapeDtypeStruct((p, d), F32),
        compiler_params=_params("arbitrary", "arbitrary", vmem=VMEM_LIMIT),
        name="expert_mlp",
    )(item_e, item_sb, item_nb, xs, w_gate_up, w_gate_up,
      b_gate_up.reshape(N_EXPERTS, 1, -1), b_gate_up.reshape(N_EXPERTS, 1, -1),
      w_down, b_down.reshape(N_EXPERTS, 1, -1))


def _combine_kernel(dest_ref, h_ref, g_ref, ys_hbm, o_ref, buf, sem, *, tt, n_tok, n_steps):
    i = pl.program_id(0)

    def issue(blk, slot):
        for k in range(TOP_K):
            def start(t, c):
                src = dest_ref[k * n_tok + blk * tt + t]
                pltpu.make_async_copy(ys_hbm.at[pl.ds(src, 1), :], buf.at[slot, k, pl.ds(t, 1), :],
                                      sem.at[slot]).start()
                return c
            lax.fori_loop(0, tt, start, 0, unroll=8)

    @pl.when(i == 0)
    def _():
        issue(0, 0)

    slot = i % 2

    @pl.when(i + 1 < n_steps)
    def _():
        issue(i + 1, 1 - slot)

    for k in range(TOP_K):
        pltpu.make_async_copy(ys_hbm.at[pl.ds(0, tt), :], buf.at[slot, k], sem.at[slot]).wait()
    acc = h_ref[...]
    g = g_ref[...]
    for k in range(TOP_K):
        acc = acc + g[:, k:k + 1] * buf[slot, k]
    o_ref[...] = acc


def combine(h2, gates, ys, dest_km, tt=64):
    n, d = h2.shape
    return pl.pallas_call(
        functools.partial(_combine_kernel, tt=tt, n_tok=n, n_steps=n // tt),
        grid_spec=pltpu.PrefetchScalarGridSpec(
            num_scalar_prefetch=1,
            grid=(n // tt,),
            in_specs=[pl.BlockSpec((tt, d), lambda i, dst: (i, 0)),
                      pl.BlockSpec((tt, LANES), lambda i, dst: (i, 0)),
                      pl.BlockSpec(memory_space=pl.ANY)],
            out_specs=pl.BlockSpec((tt, d), lambda i, dst: (i, 0)),
            scratch_shapes=[pltpu.VMEM((2, TOP_K, tt, d), F32), pltpu.SemaphoreType.DMA((2,))],
        ),
        out_shape=jax.ShapeDtypeStruct((n, d), F32),
        compiler_params=_params("arbitrary", vmem=VMEM_LIMIT),
        name="moe_combine",
    )(dest_km, h2, gates, ys)


def routing_tables(top_idx, n_tok):
    a = n_tok * TOP_K
    e_flat = top_idx.reshape(-1)
    onehot = (e_flat[:, None] == jnp.arange(N_EXPERTS, dtype=jnp.int32)[None, :]).astype(jnp.int32)
    csum = jnp.cumsum(onehot, axis=0)
    rank = jnp.sum(csum * onehot, axis=1) - 1
    counts = csum[-1]
    nblk = (counts + MOE_ROWS - 1) // MOE_ROWS
    blk_end = jnp.cumsum(nblk)
    blk_start = blk_end - nblk
    dest = blk_start[e_flat] * MOE_ROWS + rank
    n_blocks = -(-(a + N_EXPERTS * (MOE_ROWS - 1)) // MOE_ROWS)
    p = n_blocks * MOE_ROWS
    t_flat = jnp.arange(a, dtype=jnp.int32) // TOP_K
    row_tok = jnp.zeros((p,), jnp.int32).at[dest].set(t_flat)

    n_items = N_EXPERTS + n_blocks // MOE_ITEM_BLOCKS
    items_per_e = (nblk + MOE_ITEM_BLOCKS - 1) // MOE_ITEM_BLOCKS
    item_end = jnp.cumsum(items_per_e)
    item_start = item_end - items_per_e
    total_items = item_end[-1]
    ii = jnp.arange(n_items, dtype=jnp.int32)
    ic = jnp.minimum(ii, total_items - 1)
    e_i = jnp.minimum(jnp.searchsorted(item_end, ic, side='right'), N_EXPERTS - 1).astype(jnp.int32)
    local = ic - item_start[e_i]
    item_sb = (blk_start[e_i] + local * MOE_ITEM_BLOCKS).astype(jnp.int32)
    item_nb = jnp.clip(nblk[e_i] - local * MOE_ITEM_BLOCKS, 0, MOE_ITEM_BLOCKS)
    item_nb = jnp.where(ii < total_items, item_nb, 0).astype(jnp.int32)
    dest_km = dest.astype(jnp.int32).reshape(n_tok, TOP_K).T.reshape(-1)
    return dest_km, row_tok, e_i, item_sb, item_nb


def kernel(x, meta_tokens, mix_norm, w_in, dn_conv, dn_a_log, dn_dt_bias, dn_out_norm, da_q_norm, da_k_norm, da_lam_q1, da_lam_k1, da_lam_q2, da_lam_k2, da_sub_norm, w_branch_a, w_branch_b, w_out, ffn_norm, router_w, router_b, w_gate_up, b_gate_up, w_down, b_down):
    bsz, seq, d = x.shape
    n_tok = bsz * seq
    x2 = x.reshape(n_tok, d)
    layer = 0

    w = w_in[layer]
    c_ba = 4 * DN_W
    w_main = jnp.concatenate([w[:, :c_ba], w[:, c_ba + 2 * DN_HEADS:]], axis=1).astype(BF16)
    w_ba = jnp.pad(w[:, c_ba:c_ba + 2 * DN_HEADS], ((0, 0), (0, LANES - 2 * DN_HEADS))).astype(BF16)

    u_x = rms_cast(x2, mix_norm[layer], 256)
    u_m = rms_cast(meta_tokens.astype(F32), mix_norm[layer], N_META)
    proj_x = matmul_fullk(u_x, w_main, BF16, 1024, 1024, name="in_proj")
    proj_m = matmul(u_m, w_main, BF16, N_META, 1024, 1024, name="in_proj_meta")
    ba_x = matmul(u_x, w_ba, F32, 1024, LANES, 1024, name="in_proj_ba")
    ba_m = matmul(u_m, w_ba, F32, N_META, LANES, 1024, name="in_proj_ba_meta")

    y_a = gated_deltanet(proj_x, ba_x, proj_m, ba_m, dn_conv[layer], dn_a_log[layer],
                         dn_dt_bias[layer], dn_out_norm[layer], bsz, seq)
    lam4 = jnp.stack([da_lam_q1[layer], da_lam_k1[layer], da_lam_q2[layer], da_lam_k2[layer]])
    y_b = diff_attention(proj_x, proj_m, da_q_norm[layer], da_k_norm[layer], lam4,
                         da_sub_norm[layer], bsz, seq)

    merged = branch_merge(y_a, y_b, w_branch_a[layer], w_branch_b[layer], proj_x, d)
    h2 = matmul_fullk(merged, w_out[layer], F32, 1024, 512, res=x2, name="out_proj")

    top_idx, gates = router(h2, ffn_norm[layer], router_w[layer], router_b[layer])
    dest, row_tok, item_e, item_sb, item_nb = routing_tables(top_idx[:, :TOP_K], n_tok)
    xs = gather_norm(h2, row_tok, ffn_norm[layer])
    ys = expert_mlp(xs, item_e, item_sb, item_nb, w_gate_up[layer], b_gate_up[layer],
                    w_down[layer], b_down[layer])
    out = combine(h2, gates, ys, dest)
    return out.reshape(bsz, seq, d)
```

```python
import functools
import math

import jax
import jax.numpy as jnp
from jax import lax
from jax.experimental import pallas as pl
from jax.experimental.pallas import tpu as pltpu

F32 = jnp.float32
BF16 = jnp.bfloat16

N_META = 16
RMS_EPS = 1e-6
L2_EPS = 1e-6

DN_HEADS = 16
DN_DK = 128
DN_DV = 128
DN_CONV = 4
DN_CHUNK = 64
DN_W = DN_HEADS * DN_DK
DN_HIST = 16
DN_CONV_COLS = 1024

DA_HEADS = 8
DA_DK = 128
DA_DV = 256
LAMBDA_INIT = 0.8 - 0.6 * math.exp(-0.3 * 0)

N_EXPERTS = 32
TOP_K = 4
D_EXPERT = 1536
SWIGLU_LIMIT = 7.0
SWIGLU_ALPHA = 1.702

LANES = 128
MOE_ROWS = 256
MOE_ITEM_BLOCKS = 5
MOE_GATHER_GROUP = 32
NEG_BIG = -1e30
VMEM_LIMIT = 56 * 1024 * 1024

COL_DZ = 3 * DN_W
COL_BA = 4 * DN_W
COL_AQ = 0
COL_AK = COL_AQ + DA_HEADS * 2 * DA_DK
COL_AV = COL_AK + DA_HEADS * 2 * DA_DK
COL_GA = COL_AV + DA_HEADS * DA_DV


def _params(*sem, vmem=None):
    return pltpu.CompilerParams(dimension_semantics=sem, vmem_limit_bytes=vmem)


def _rms_cast_kernel(x_ref, w_ref, o_ref):
    x = x_ref[...]
    y = x * lax.rsqrt(jnp.mean(x * x, axis=-1, keepdims=True) + RMS_EPS) * w_ref[...]
    o_ref[...] = y.astype(o_ref.dtype)


def rms_cast(x, w, tm):
    m, d = x.shape
    return pl.pallas_call(
        _rms_cast_kernel,
        grid=(m // tm,),
        in_specs=[pl.BlockSpec((tm, d), lambda i: (i, 0)),
                  pl.BlockSpec((1, d), lambda i: (0, 0))],
        out_specs=pl.BlockSpec((tm, d), lambda i: (i, 0)),
        out_shape=jax.ShapeDtypeStruct((m, d), BF16),
        compiler_params=_params("parallel"),
        name="rms_cast",
    )(x, w.reshape(1, d))


def _mm_fullk_kernel(a_ref, b_ref, *rest, has_res):
    acc = jnp.dot(a_ref[...], b_ref[...].astype(BF16), preferred_element_type=F32)
    if has_res:
        r_ref, o_ref = rest
        acc = acc + r_ref[...]
    else:
        o_ref, = rest
    o_ref[...] = acc.astype(o_ref.dtype)


def matmul_fullk(a, b, out_dtype, tm, tn, res=None, name="matmul", n_offset=0, n_blocks=None):
    m, kd = a.shape
    if n_blocks is None:
        n_blocks = b.shape[1] // tn
    in_specs = [pl.BlockSpec((tm, kd), lambda i, j: (i, 0)),
                pl.BlockSpec((kd, tn), lambda i, j: (0, n_offset + j))]
    args = [a, b]
    if res is not None:
        in_specs.append(pl.BlockSpec((tm, tn), lambda i, j: (i, j)))
        args.append(res)
    return pl.pallas_call(
        functools.partial(_mm_fullk_kernel, has_res=res is not None),
        grid=(m // tm, n_blocks),
        in_specs=in_specs,
        out_specs=pl.BlockSpec((tm, tn), lambda i, j: (i, j)),
        out_shape=jax.ShapeDtypeStruct((m, n_blocks * tn), out_dtype),
        compiler_params=_params("parallel", "parallel", vmem=VMEM_LIMIT),
        name=name,
    )(*args)


def _deltanet_kernel(qkv_ref, z_ref, ba_ref, qkvm_ref, bam_ref, cw_ref, hp_ref, nw_ref,
                     o_ref, full_ref, act_ref, ba_s, s_ref):
    t = pl.program_id(1)
    C = DN_CHUNK
    HIST = DN_HIST
    n_pad = C - N_META

    @pl.when(t == 0)
    def _():
        s_ref[...] = jnp.zeros_like(s_ref)
        full_ref[0:HIST + n_pad, :] = jnp.zeros((HIST + n_pad, 3 * DN_W), BF16)
        full_ref[HIST + n_pad:HIST + C, :] = qkvm_ref[...]
        ba_s[0:n_pad, :] = jnp.zeros((n_pad, LANES), F32)
        ba_s[n_pad:C, :] = bam_ref[...]

    @pl.when(t > 0)
    def _():
        full_ref[HIST:HIST + C, :] = qkv_ref[...]
        ba_s[...] = ba_ref[...]

    n_shift = DN_CONV - 1
    sr = lax.broadcasted_iota(jnp.int32, (n_shift * C, HIST + C), 0)
    sc = lax.broadcasted_iota(jnp.int32, (n_shift * C, HIST + C), 1)
    shift = jnp.where(sc == HIST - n_shift + (sr % C) + (sr // C), 1.0, 0.0).astype(BF16)
    for c0 in range(0, 3 * DN_W, DN_CONV_COLS):
        cs = slice(c0, c0 + DN_CONV_COLS)
        delayed = jnp.dot(shift, full_ref[:, cs], preferred_element_type=F32)
        conv = cw_ref[n_shift:n_shift + 1, cs] * full_ref[HIST:HIST + C, cs].astype(F32)
        for j in range(n_shift):
            conv = conv + cw_ref[j:j + 1, cs] * delayed[j * C:(j + 1) * C, :]
        act_ref[:, cs] = conv * jax.nn.sigmoid(conv)
    full_ref[0:HIST, :] = full_ref[C:C + HIST, :]

    row1 = lax.broadcasted_iota(jnp.int32, (C, 1), 0)
    valid = jnp.where((t > 0) | (row1 >= n_pad), 1.0, 0.0).astype(F32)
    ba = ba_s[...]
    beta_all = jax.nn.sigmoid(ba) * valid
    xg = ba + hp_ref[1:2, :]
    softplus = jnp.maximum(xg, 0.0) + jnp.log1p(jnp.exp(-jnp.abs(xg)))
    g_all = -jnp.exp(hp_ref[0:1, :]) * softplus * valid

    row = lax.broadcasted_iota(jnp.int32, (C, C), 0)
    col = lax.broadcasted_iota(jnp.int32, (C, C), 1)
    causal = row >= col
    strict = row > col
    eye = jnp.where(row == col, 1.0, 0.0).astype(F32)
    tril = jnp.where(causal, 1.0, 0.0).astype(BF16)

    g_hi = g_all.astype(BF16)
    r1 = g_all - g_hi.astype(F32)
    g_mid = r1.astype(BF16)
    g_lo = (r1 - g_mid.astype(F32)).astype(BF16)
    g3 = jnp.dot(tril, jnp.concatenate([g_hi, g_mid, g_lo], axis=1), preferred_element_type=F32)
    gcum = g3[:, 0:LANES] + g3[:, LANES:2 * LANES] + g3[:, 2 * LANES:3 * LANES]
    gcum_t = gcum.T

    scale = DN_DK ** -0.5
    nw = nw_ref[...]
    heads = range(DN_HEADS)

    def mm(a, b):
        return jnp.dot(a.astype(BF16), b.astype(BF16), preferred_element_type=F32)

    kn_b, lhs1, decay, rhs, qe_b, kdec_t, e_last = [], [], [], [], [], [], []
    for h in heads:
        qh = act_ref[:, h * DN_DK:(h + 1) * DN_DK]
        kh = act_ref[:, DN_W + h * DN_DK:DN_W + (h + 1) * DN_DK]
        vh = act_ref[:, 2 * DN_W + h * DN_DV:2 * DN_W + (h + 1) * DN_DV]
        qn = qh * (lax.rsqrt(jnp.sum(qh * qh, axis=-1, keepdims=True) + L2_EPS) * scale)
        kn = kh * lax.rsqrt(jnp.sum(kh * kh, axis=-1, keepdims=True) + L2_EPS)
        beta = beta_all[:, h:h + 1]
        gc = gcum[:, DN_HEADS + h:DN_HEADS + h + 1]
        gr = gcum_t[DN_HEADS + h:DN_HEADS + h + 1, :]
        g_last = gc[C - 1:C, :]
        e_g = jnp.exp(gc)
        kb = kn * beta
        kn_b.append(kn.astype(BF16))
        lhs1.append(jnp.concatenate([kb, qn], axis=0).astype(BF16))
        decay.append(jnp.exp(jnp.where(causal, gc - gr, NEG_BIG)))
        rhs.append(jnp.concatenate([vh * beta, kb * e_g], axis=1).astype(BF16))
        qe_b.append((qn * e_g).astype(BF16))
        kdec_t.append((kn * jnp.exp(g_last - gc)).T.astype(BF16))
        e_last.append(jnp.exp(g_last))

    kq = [lax.dot_general(lhs1[h], kn_b[h], (((1,), (1,)), ((), ())), preferred_element_type=F32)
          for h in heads]
    qk_b = [jnp.where(causal, kq[h][C:2 * C, :] * decay[h], 0.0).astype(BF16) for h in heads]
    b1 = [-jnp.where(strict, kq[h][0:C, :] * decay[h], 0.0) for h in heads]
    b2 = [mm(b1[h], b1[h]) for h in heads]
    b4 = [mm(b2[h], b2[h]) for h in heads]
    p1 = [(eye + b1[h]) + mm(eye + b1[h], b2[h]) for h in heads]
    b8 = [mm(b4[h], b4[h]) for h in heads]
    b16 = [mm(b8[h], b8[h]) for h in heads]
    p2 = [(eye + b4[h]) + mm(eye + b4[h], b8[h]) for h in heads]
    b32 = [mm(b16[h], b16[h]) for h in heads]
    p12 = [mm(p1[h], p2[h]) for h in heads]
    p3 = [(eye + b16[h]) + mm(eye + b16[h], b32[h]) for h in heads]
    inv = [mm(p12[h], p3[h]) for h in heads]
    sol = [mm(inv[h], rhs[h]) for h in heads]

    s_old = [s_ref[h] for h in heads]
    ws_lhs = [jnp.concatenate([sol[h][:, DN_DV:2 * DN_DV].astype(BF16), qe_b[h]], axis=0) for h in heads]
    ws = [mm(ws_lhs[h], s_old[h]) for h in heads]
    v_new = [(sol[h][:, 0:DN_DV] - ws[h][0:C, :]).astype(BF16) for h in heads]
    ov_lhs = [jnp.concatenate([qk_b[h], kdec_t[h]], axis=0) for h in heads]
    ov = [mm(ov_lhs[h], v_new[h]) for h in heads]
    for h in heads:
        s_ref[h] = s_old[h] * e_last[h] + ov[h][C:C + DN_DK, :]
        o = ws[h][C:2 * C, :] + ov[h][0:C, :]
        zh = z_ref[:, h * DN_DV:(h + 1) * DN_DV].astype(F32)
        o = o * lax.rsqrt(jnp.mean(o * o, axis=-1, keepdims=True) + RMS_EPS) * nw
        o_ref[:, h * DN_DV:(h + 1) * DN_DV] = (o * (zh * jax.nn.sigmoid(zh))).astype(o_ref.dtype)


def gated_deltanet(proj_x, ba_x, proj_m, ba_m, conv_w, a_log, dt_bias, norm_w, bsz, seq):
    C = DN_CHUNK
    n_chunks = seq // C
    w3 = 3 * DN_W
    hp = jnp.zeros((2, LANES), F32)
    hp = hp.at[0, DN_HEADS:2 * DN_HEADS].set(a_log.astype(F32))
    hp = hp.at[1, DN_HEADS:2 * DN_HEADS].set(dt_bias.astype(F32))

    def xrow(b, t):
        return b * n_chunks + jnp.maximum(t - 1, 0)

    return pl.pallas_call(
        _deltanet_kernel,
        grid=(bsz, n_chunks + 1),
        in_specs=[
            pl.BlockSpec((C, w3), lambda b, t: (xrow(b, t), 0)),
            pl.BlockSpec((C, DN_W), lambda b, t: (xrow(b, t), COL_DZ // DN_W)),
            pl.BlockSpec((C, LANES), lambda b, t: (xrow(b, t), 0)),
            pl.BlockSpec((N_META, w3), lambda b, t: (0, 0)),
            pl.BlockSpec((N_META, LANES), lambda b, t: (0, 0)),
            pl.BlockSpec((DN_CONV, w3), lambda b, t: (0, 0)),
            pl.BlockSpec((2, LANES), lambda b, t: (0, 0)),
            pl.BlockSpec((1, DN_DV), lambda b, t: (0, 0)),
        ],
        out_specs=pl.BlockSpec((C, DN_W), lambda b, t: (xrow(b, t), 0)),
        out_shape=jax.ShapeDtypeStruct((bsz * seq, DN_W), BF16),
        scratch_shapes=[
            pltpu.VMEM((DN_HIST + C, w3), BF16),
            pltpu.VMEM((C, w3), F32),
            pltpu.VMEM((C, LANES), F32),
            pltpu.VMEM((DN_HEADS, DN_DK, DN_DV), F32),
        ],
        compiler_params=_params("parallel", "arbitrary", vmem=VMEM_LIMIT),
        name="gated_deltanet",
    )(proj_x, proj_x, ba_x, proj_m, ba_m, conv_w.astype(F32), hp, norm_w.reshape(1, DN_DV).astype(F32))


def _diff_attn_kernel(q_ref, k_ref, v_ref, km_ref, vm_ref, qn_ref, kn_ref, lam_ref, sn_ref,
                      o_ref, kx_s, kmeta_s, *, tq, seq):
    qi = pl.program_id(2)

    def rms(x, w):
        return x * lax.rsqrt(jnp.mean(x * x, axis=-1, keepdims=True) + RMS_EPS) * w

    @pl.when(qi == 0)
    def _():
        for m in range(2):
            kx = k_ref[:, m * DA_DK:(m + 1) * DA_DK].astype(F32)
            kx_s[m] = rms(kx, kn_ref[...]).astype(BF16)
            kmx = km_ref[:, m * DA_DK:(m + 1) * DA_DK].astype(F32)
            kmeta_s[m] = rms(kmx, kn_ref[...]).astype(BF16)

    lam = (jnp.exp(jnp.sum(lam_ref[0:1, :] * lam_ref[1:2, :], axis=-1, keepdims=True))
           - jnp.exp(jnp.sum(lam_ref[2:3, :] * lam_ref[3:4, :], axis=-1, keepdims=True))
           + LAMBDA_INIT)
    row = lax.broadcasted_iota(jnp.int32, (tq, tq), 0)
    col = lax.broadcasted_iota(jnp.int32, (tq, tq), 1)
    diag_visible = col <= row
    v_m = vm_ref[...]
    scale = DA_DK ** -0.5 * math.log2(math.e)
    nt = (((1,), (1,)), ((), ()))

    def attend(n_prev):
        outs = []
        for m in range(2):
            qx = q_ref[:, m * DA_DK:(m + 1) * DA_DK].astype(F32)
            qn = (rms(qx, qn_ref[...]) * scale).astype(BF16)
            sd = lax.dot_general(qn, kx_s[m, n_prev:n_prev + tq, :], nt, preferred_element_type=F32)
            sd = jnp.where(diag_visible, sd, NEG_BIG)
            sm = lax.dot_general(qn, kmeta_s[m], nt, preferred_element_type=F32)
            mx = jnp.maximum(jnp.max(sd, axis=-1, keepdims=True), jnp.max(sm, axis=-1, keepdims=True))
            if n_prev:
                sp = lax.dot_general(qn, kx_s[m, 0:n_prev, :], nt, preferred_element_type=F32)
                mx = jnp.maximum(mx, jnp.max(sp, axis=-1, keepdims=True))
            pd = jnp.exp2(sd - mx)
            pm = jnp.exp2(sm - mx)
            den = jnp.sum(pd, axis=-1, keepdims=True) + jnp.sum(pm, axis=-1, keepdims=True)
            acc = (jnp.dot(pd.astype(BF16), v_ref[n_prev:n_prev + tq, :], preferred_element_type=F32)
                   + jnp.dot(pm.astype(BF16), v_m, preferred_element_type=F32))
            if n_prev:
                pp = jnp.exp2(sp - mx)
                den = den + jnp.sum(pp, axis=-1, keepdims=True)
                acc = acc + jnp.dot(pp.astype(BF16), v_ref[0:n_prev, :], preferred_element_type=F32)
            outs.append(acc / den)
        out = outs[0] - lam * outs[1]
        out = rms(out, sn_ref[...]) * (1.0 - LAMBDA_INIT)
        o_ref[...] = out.astype(o_ref.dtype)

    for blk in range(seq // tq):
        @pl.when(qi == blk)
        def _():
            attend(blk * tq)


def diff_attention(proj_x, proj_m, q_norm, k_norm, lam4, sub_norm, bsz, seq, tq=256):
    nq = seq // tq
    cq = COL_AQ // DA_DV
    ck = COL_AK // DA_DV
    cv = COL_AV // DA_DV
    return pl.pallas_call(
        functools.partial(_diff_attn_kernel, tq=tq, seq=seq),
        grid=(bsz, DA_HEADS, nq),
        in_specs=[
            pl.BlockSpec((tq, DA_DV), lambda b, h, i: (b * nq + i, cq + h)),
            pl.BlockSpec((seq, DA_DV), lambda b, h, i: (b, ck + h)),
            pl.BlockSpec((seq, DA_DV), lambda b, h, i: (b, cv + h)),
            pl.BlockSpec((N_META, DA_DV), lambda b, h, i: (0, h)),
            pl.BlockSpec((N_META, DA_DV), lambda b, h, i: (0, DA_HEADS + h)),
            pl.BlockSpec((1, DA_DK), lambda b, h, i: (0, 0)),
            pl.BlockSpec((1, DA_DK), lambda b, h, i: (0, 0)),
            pl.BlockSpec((4, DA_DK), lambda b, h, i: (0, 0)),
            pl.BlockSpec((1, DA_DV), lambda b, h, i: (0, 0)),
        ],
        out_specs=pl.BlockSpec((tq, DA_DV), lambda b, h, i: (b * nq + i, h)),
        out_shape=jax.ShapeDtypeStruct((bsz * seq, DA_HEADS * DA_DV), BF16),
        scratch_shapes=[pltpu.VMEM((2, seq, DA_DK), BF16),
                        pltpu.VMEM((2, N_META, DA_DK), BF16)],
        compiler_params=_params("parallel", "parallel", "arbitrary", vmem=VMEM_LIMIT),
        name="diff_attention",
    )(proj_x, proj_x, proj_x, proj_m, proj_m,
      q_norm.reshape(1, DA_DK).astype(F32), k_norm.reshape(1, DA_DK).astype(F32),
      lam4.astype(F32), sub_norm.reshape(1, DA_DV).astype(F32))


def _merge_kernel(ya_ref, yb_ref, wa_ref, wb_ref, ga_ref, gb_ref, o_ref):
    pa = jnp.dot(ya_ref[...], wa_ref[...].astype(BF16), preferred_element_type=F32)
    pb = jnp.dot(yb_ref[...], wb_ref[...].astype(BF16), preferred_element_type=F32)
    ga = jax.nn.sigmoid(ga_ref[...].astype(F32))
    gb = jax.nn.sigmoid(gb_ref[...].astype(F32))
    o_ref[...] = (ga * pa + gb * pb).astype(o_ref.dtype)


def branch_merge(y_a, y_b, w_a, w_b, proj_x, d_model, tm=1024, tn=512):
    m, ka = y_a.shape
    kb = y_b.shape[1]
    cga = COL_GA // tn
    cgb = (COL_GA + d_model) // tn
    return pl.pallas_call(
        _merge_kernel,
        grid=(m // tm, d_model // tn),
        in_specs=[
            pl.BlockSpec((tm, ka), lambda i, j: (i, 0)),
            pl.BlockSpec((tm, kb), lambda i, j: (i, 0)),
            pl.BlockSpec((ka, tn), lambda i, j: (0, j)),
            pl.BlockSpec((kb, tn), lambda i, j: (0, j)),
            pl.BlockSpec((tm, tn), lambda i, j: (i, cga + j)),
            pl.BlockSpec((tm, tn), lambda i, j: (i, cgb + j)),
        ],
        out_specs=pl.BlockSpec((tm, tn), lambda i, j: (i, j)),
        out_shape=jax.ShapeDtypeStruct((m, d_model), BF16),
        compiler_params=_params("parallel", "parallel", vmem=VMEM_LIMIT),
        name="branch_merge",
    )(y_a, y_b, w_a, w_b, proj_x, proj_x)


def _router_kernel(h_ref, nw_ref, whi_ref, wlo_ref, b_ref, idx_ref, gate_ref, up_ref):
    x = h_ref[...]
    u = x * lax.rsqrt(jnp.mean(x * x, axis=-1, keepdims=True) + RMS_EPS) * nw_ref[...]
    u_hi = u.astype(BF16)
    u_lo = (u - u_hi.astype(F32)).astype(BF16)
    half = u.shape[1] // 2
    bits = pltpu.bitcast(u_hi.astype(F32), jnp.uint32)
    up_ref[...] = bits[:, half:] | (bits[:, :half] >> 16)
    logits = (jnp.dot(u_hi, whi_ref[...], preferred_element_type=F32)
              + jnp.dot(u_lo, whi_ref[...], preferred_element_type=F32)
              + jnp.dot(u_hi, wlo_ref[...], preferred_element_type=F32)
              + b_ref[...])
    lane = lax.broadcasted_iota(jnp.int32, logits.shape, 1)
    lane_f = lane.astype(F32)
    vals = logits
    tops, idxs = [], []
    for _ in range(TOP_K):
        mx = jnp.max(vals, axis=-1, keepdims=True)
        ix = jnp.min(jnp.where(vals == mx, lane_f, float(LANES)), axis=-1, keepdims=True).astype(jnp.int32)
        tops.append(mx)
        idxs.append(ix)
        vals = jnp.where(lane == ix, -3.0e38, vals)
    exps = [jnp.exp(tv - tops[0]) for tv in tops]
    den = exps[0] + exps[1] + exps[2] + exps[3]
    idx_out = jnp.zeros(logits.shape, jnp.int32)
    gate_out = jnp.zeros(logits.shape, F32)
    for k in range(TOP_K):
        idx_out = jnp.where(lane == k, idxs[k], idx_out)
        gate_out = jnp.where(lane == k, exps[k] / den, gate_out)
    idx_ref[...] = idx_out
    gate_ref[...] = gate_out


def router(h2, ffn_norm, router_w, router_b, tm=256):
    n, d = h2.shape
    wpad = jnp.zeros((d, LANES), F32).at[:, :N_EXPERTS].set(router_w.astype(F32))
    w_hi = wpad.astype(BF16)
    w_lo = (wpad - w_hi.astype(F32)).astype(BF16)
    bias = jnp.full((1, LANES), NEG_BIG, F32).at[0, :N_EXPERTS].set(router_b.astype(F32))
    return pl.pallas_call(
        _router_kernel,
        grid=(n // tm,),
        in_specs=[
            pl.BlockSpec((tm, d), lambda i: (i, 0)),
            pl.BlockSpec((1, d), lambda i: (0, 0)),
            pl.BlockSpec((d, LANES), lambda i: (0, 0)),
            pl.BlockSpec((d, LANES), lambda i: (0, 0)),
            pl.BlockSpec((1, LANES), lambda i: (0, 0)),
        ],
        out_specs=[pl.BlockSpec((tm, LANES), lambda i: (i, 0)),
                   pl.BlockSpec((tm, LANES), lambda i: (i, 0)),
                   pl.BlockSpec((tm, d // 2), lambda i: (i, 0))],
        out_shape=[jax.ShapeDtypeStruct((n, LANES), jnp.int32),
                   jax.ShapeDtypeStruct((n, LANES), F32),
                   jax.ShapeDtypeStruct((n, d // 2), jnp.uint32)],
        compiler_params=_params("parallel", vmem=VMEM_LIMIT),
        name="router",
    )(h2, ffn_norm.reshape(1, d).astype(F32), w_hi, w_lo, bias)


def _expert_kernel(ie_ref, isb_ref, inb_ref, tok_ref, up_hbm, wg_ref, wu_ref, bg_ref, bu_ref,
                   wd_ref, bd_ref, ys_hbm, xbuf, act, ybuf, wg_s, wu_s, wd_s, sem_x, sem_y,
                   *, j1, j2, tn1, tn2, n_items):
    i = pl.program_id(0)
    j = pl.program_id(1)
    nb = inb_ref[i]
    sb = isb_ref[i]
    R = MOE_ROWS
    G = MOE_GATHER_GROUP
    half = xbuf.shape[1]

    def gather_groups(item, g_lo, n_groups):
        n_rows = inb_ref[item] * R
        base = isb_ref[item] * R
        for gi in range(n_groups):
            g0 = (g_lo + gi) * G

            @pl.when(g0 < n_rows)
            def _():
                def start(rr, c):
                    tok = tok_ref[base + g0 + rr]
                    pltpu.make_async_copy(up_hbm.at[pl.ds(tok, 1), :],
                                          xbuf.at[pl.ds(g0 + rr, 1), :], sem_x).start()
                    return c
                lax.fori_loop(0, G, start, 0, unroll=8)

    def x_block_wait(r):
        pltpu.make_async_copy(up_hbm.at[pl.ds(0, R), :], xbuf.at[pl.ds(r * R, R), :], sem_x).wait()

    def y_copy(slot, r, jd):
        return pltpu.make_async_copy(
            ybuf.at[slot, pl.ds(r * R, R), :],
            ys_hbm.at[pl.ds((sb + r) * R, R), pl.ds(pl.multiple_of(jd * tn2, tn2), tn2)],
            sem_y.at[slot])

    def for_blocks(fn):
        for r in range(MOE_ITEM_BLOCKS):
            @pl.when(r < nb)
            def _():
                fn(r)

    groups_total = MOE_ITEM_BLOCKS * R // G
    groups_per_step = groups_total // j2

    @pl.when((i == 0) & (j == 0))
    def _():
        gather_groups(0, 0, groups_total)

    @pl.when((nb > 0) & (j == 0))
    def _():
        for_blocks(x_block_wait)

    def gate_up_block(r0, wg_lo, wg_hi, wu_lo, wu_hi):
        x32 = xbuf[pl.ds(r0, R), :]
        x_lo = pltpu.bitcast(x32 << 16, F32).astype(BF16)
        x_hi = pltpu.bitcast(x32 & jnp.uint32(0xFFFF0000), F32).astype(BF16)
        g = (jnp.dot(x_lo, wg_lo, preferred_element_type=F32)
             + jnp.dot(x_hi, wg_hi, preferred_element_type=F32) + bg_ref[...])
        u = (jnp.dot(x_lo, wu_lo, preferred_element_type=F32)
             + jnp.dot(x_hi, wu_hi, preferred_element_type=F32) + bu_ref[...])
        gt = jnp.minimum(g, SWIGLU_LIMIT)
        up = jnp.clip(u, -SWIGLU_LIMIT, SWIGLU_LIMIT)
        a = gt * jax.nn.sigmoid(SWIGLU_ALPHA * gt) * (up + 1.0)
        act[j, pl.ds(r0, R), :] = a.astype(BF16)

    @pl.when((nb > 0) & (j < j1))
    def _():
        wg = wg_ref[...].astype(BF16)
        wu = wu_ref[...].astype(BF16)
        wg_s[...] = wg
        wu_s[...] = wu
        gate_up_block(0, wg[0:half, :], wg[half:, :], wu[0:half, :], wu[half:, :])

        def body(r, c):
            gate_up_block(pl.multiple_of(r * R, R), wg_s[0:half, :], wg_s[half:, :],
                          wu_s[0:half, :], wu_s[half:, :])
            return c

        lax.fori_loop(1, nb, body, 0)

    def down_block(slot, r0, wd_chunk):
        y = bd_ref[...] + jnp.dot(act[0, pl.ds(r0, R), :], wd_chunk(0), preferred_element_type=F32)
        for c in range(1, j1):
            y = y + jnp.dot(act[c, pl.ds(r0, R), :], wd_chunk(c), preferred_element_type=F32)
        ybuf[slot, pl.ds(r0, R), :] = y

    @pl.when((nb > 0) & (j >= j1))
    def _():
        jd = j - j1
        slot = jd % 2

        @pl.when(i + 1 < n_items)
        def _():
            gather_groups(i + 1, jd * groups_per_step, groups_per_step)

        @pl.when(jd >= 2)
        def _():
            for_blocks(lambda r: y_copy(slot, r, jd).wait())

        wd = wd_ref[...].astype(BF16)
        wd_s[...] = wd
        down_block(slot, 0, lambda c: wd[c * tn1:(c + 1) * tn1, :])

        def body(r, c):
            down_block(slot, pl.multiple_of(r * R, R), lambda c: wd_s[c * tn1:(c + 1) * tn1, :])
            return c

        lax.fori_loop(1, nb, body, 0)
        for_blocks(lambda r: y_copy(slot, r, jd).start())

        @pl.when(jd == j2 - 1)
        def _():
            for_blocks(lambda r: y_copy(1 - slot, r, jd).wait())
            for_blocks(lambda r: y_copy(slot, r, jd).wait())


def expert_mlp(u_packed, row_tok, item_e, item_sb, item_nb, w_gate_up, b_gate_up, w_down, b_down,
               tn1=256, tn2=512):
    p = row_tok.shape[0]
    d = 2 * u_packed.shape[1]
    n_items = item_e.shape[0]
    de = w_down.shape[1]
    j1 = de // tn1
    j2 = d // tn2
    rmax = MOE_ITEM_BLOCKS * MOE_ROWS

    assert (MOE_ITEM_BLOCKS * MOE_ROWS) % (MOE_GATHER_GROUP * j2) == 0

    def jj(i, j, nb):
        return jnp.where(nb[i] > 0, j, j1 + j2 - 1)

    def gate_map(i, j, ie, isb, inb, tok):
        return (ie[i], 0, jnp.minimum(jj(i, j, inb), j1 - 1))

    def up_map(i, j, ie, isb, inb, tok):
        return (ie[i], 0, j1 + jnp.minimum(jj(i, j, inb), j1 - 1))

    def down_map(i, j, ie, isb, inb, tok):
        return (ie[i], 0, jnp.maximum(jj(i, j, inb) - j1, 0))

    return pl.pallas_call(
        functools.partial(_expert_kernel, j1=j1, j2=j2, tn1=tn1, tn2=tn2, n_items=n_items),
        grid_spec=pltpu.PrefetchScalarGridSpec(
            num_scalar_prefetch=4,
            grid=(n_items, j1 + j2),
            in_specs=[
                pl.BlockSpec(memory_space=pl.ANY),
                pl.BlockSpec((None, d, tn1), gate_map),
                pl.BlockSpec((None, d, tn1), up_map),
                pl.BlockSpec((None, 1, tn1), gate_map),
                pl.BlockSpec((None, 1, tn1), up_map),
                pl.BlockSpec((None, de, tn2), down_map),
                pl.BlockSpec((None, 1, tn2), down_map),
            ],
            out_specs=pl.BlockSpec(memory_space=pl.ANY),
            scratch_shapes=[
                pltpu.VMEM((rmax, d // 2), jnp.uint32),
                pltpu.VMEM((j1, rmax, tn1), BF16),
                pltpu.VMEM((2, rmax, tn2), F32),
                pltpu.VMEM((d, tn1), BF16),
                pltpu.VMEM((d, tn1), BF16),
                pltpu.VMEM((de, tn2), BF16),
                pltpu.SemaphoreType.DMA(()),
                pltpu.SemaphoreType.DMA((2,)),
            ],
        ),
        out_shape=jax.ShapeDtypeStruct((p, d), F32),
        compiler_params=_params("arbitrary", "arbitrary", vmem=VMEM_LIMIT),
        name="expert_mlp",
    )(item_e, item_sb, item_nb, row_tok, u_packed, w_gate_up, w_gate_up,
      b_gate_up.reshape(N_EXPERTS, 1, -1), b_gate_up.reshape(N_EXPERTS, 1, -1),
      w_down, b_down.reshape(N_EXPERTS, 1, -1))


def _combine_kernel(dest_ref, h_ref, g_ref, ys_hbm, o_ref, buf, sem, *, tt, n_tok, n_steps):
    i = pl.program_id(0)

    def issue(blk, slot):
        for k in range(TOP_K):
            def start(t, c):
                src = dest_ref[k * n_tok + blk * tt + t]
                pltpu.make_async_copy(ys_hbm.at[pl.ds(src, 1), :], buf.at[slot, k, pl.ds(t, 1), :],
                                      sem.at[slot]).start()
                return c
            lax.fori_loop(0, tt, start, 0, unroll=8)

    @pl.when(i == 0)
    def _():
        issue(0, 0)

    slot = i % 2

    @pl.when(i + 1 < n_steps)
    def _():
        issue(i + 1, 1 - slot)

    for k in range(TOP_K):
        pltpu.make_async_copy(ys_hbm.at[pl.ds(0, tt), :], buf.at[slot, k], sem.at[slot]).wait()
    acc = h_ref[...]
    g = g_ref[...]
    for k in range(TOP_K):
        acc = acc + g[:, k:k + 1] * buf[slot, k]
    o_ref[...] = acc


def combine(h2, gates, ys, dest_km, tt=64):
    n, d = h2.shape
    return pl.pallas_call(
        functools.partial(_combine_kernel, tt=tt, n_tok=n, n_steps=n // tt),
        grid_spec=pltpu.PrefetchScalarGridSpec(
            num_scalar_prefetch=1,
            grid=(n // tt,),
            in_specs=[pl.BlockSpec((tt, d), lambda i, dst: (i, 0)),
                      pl.BlockSpec((tt, LANES), lambda i, dst: (i, 0)),
                      pl.BlockSpec(memory_space=pl.ANY)],
            out_specs=pl.BlockSpec((tt, d), lambda i, dst: (i, 0)),
            scratch_shapes=[pltpu.VMEM((2, TOP_K, tt, d), F32), pltpu.SemaphoreType.DMA((2,))],
        ),
        out_shape=jax.ShapeDtypeStruct((n, d), F32),
        compiler_params=_params("arbitrary", vmem=VMEM_LIMIT),
        name="moe_combine",
    )(dest_km, h2, gates, ys)


def routing_tables(top_idx, n_tok):
    a = n_tok * TOP_K
    e_flat = top_idx.reshape(-1)
    onehot = (e_flat[:, None] == jnp.arange(N_EXPERTS, dtype=jnp.int32)[None, :]).astype(jnp.int32)
    csum = jnp.cumsum(onehot, axis=0)
    rank = jnp.sum(csum * onehot, axis=1) - 1
    counts = csum[-1]
    nblk = (counts + MOE_ROWS - 1) // MOE_ROWS
    blk_end = jnp.cumsum(nblk)
    blk_start = blk_end - nblk
    dest = blk_start[e_flat] * MOE_ROWS + rank
    n_blocks = -(-(a + N_EXPERTS * (MOE_ROWS - 1)) // MOE_ROWS)
    p = n_blocks * MOE_ROWS
    t_flat = jnp.arange(a, dtype=jnp.int32) // TOP_K
    row_tok = jnp.zeros((p,), jnp.int32).at[dest].set(t_flat)

    n_items = N_EXPERTS + n_blocks // MOE_ITEM_BLOCKS
    items_per_e = (nblk + MOE_ITEM_BLOCKS - 1) // MOE_ITEM_BLOCKS
    item_end = jnp.cumsum(items_per_e)
    item_start = item_end - items_per_e
    total_items = item_end[-1]
    ii = jnp.arange(n_items, dtype=jnp.int32)
    ic = jnp.minimum(ii, total_items - 1)
    e_i = jnp.minimum(jnp.searchsorted(item_end, ic, side='right'), N_EXPERTS - 1).astype(jnp.int32)
    local = ic - item_start[e_i]
    item_sb = (blk_start[e_i] + local * MOE_ITEM_BLOCKS).astype(jnp.int32)
    item_nb = jnp.clip(nblk[e_i] - local * MOE_ITEM_BLOCKS, 0, MOE_ITEM_BLOCKS)
    item_nb = jnp.where(ii < total_items, item_nb, 0).astype(jnp.int32)
    dest_km = dest.astype(jnp.int32).reshape(n_tok, TOP_K).T.reshape(-1)
    return dest_km, row_tok, e_i, item_sb, item_nb


def kernel(x, meta_tokens, mix_norm, w_in, dn_conv, dn_a_log, dn_dt_bias, dn_out_norm, da_q_norm, da_k_norm, da_lam_q1, da_lam_k1, da_lam_q2, da_lam_k2, da_sub_norm, w_branch_a, w_branch_b, w_out, ffn_norm, router_w, router_b, w_gate_up, b_gate_up, w_down, b_down):
    bsz, seq, d = x.shape
    n_tok = bsz * seq
    x2 = x.reshape(n_tok, d)
    layer = 0

    w_bf = w_in[layer].astype(BF16)
    w_da = w_bf[:, COL_BA + 2 * DN_HEADS:]
    tn = 1024

    u_x = rms_cast(x2, mix_norm[layer], 256)
    u_m = rms_cast(meta_tokens.astype(F32), mix_norm[layer], N_META)
    proj_dn = matmul_fullk(u_x, w_bf, BF16, 1024, tn, name="in_proj_dn", n_blocks=COL_BA // tn)
    proj_da = matmul_fullk(u_x, w_da, BF16, 1024, tn, name="in_proj_da")
    ba_x = matmul_fullk(u_x, w_bf, F32, 1024, LANES, name="in_proj_ba",
                        n_offset=COL_BA // LANES, n_blocks=1)
    proj_dn_m = matmul_fullk(u_m, w_bf, BF16, N_META, tn, name="in_proj_dn_meta", n_blocks=COL_DZ // tn)
    proj_kv_m = matmul_fullk(u_m, w_da, BF16, N_META, tn, name="in_proj_kv_meta",
                             n_offset=COL_AK // tn, n_blocks=(COL_GA - COL_AK) // tn)
    ba_m = matmul_fullk(u_m, w_bf, F32, N_META, LANES, name="in_proj_ba_meta",
                        n_offset=COL_BA // LANES, n_blocks=1)

    y_a = gated_deltanet(proj_dn, ba_x, proj_dn_m, ba_m, dn_conv[layer], dn_a_log[layer],
                         dn_dt_bias[layer], dn_out_norm[layer], bsz, seq)
    lam4 = jnp.stack([da_lam_q1[layer], da_lam_k1[layer], da_lam_q2[layer], da_lam_k2[layer]])
    y_b = diff_attention(proj_da, proj_kv_m, da_q_norm[layer], da_k_norm[layer], lam4,
                         da_sub_norm[layer], bsz, seq)

    merged = branch_merge(y_a, y_b, w_branch_a[layer], w_branch_b[layer], proj_da, d)
    h2 = matmul_fullk(merged, w_out[layer], F32, 1024, 512, res=x2, name="out_proj")

    top_idx, gates, u_packed = router(h2, ffn_norm[layer], router_w[layer], router_b[layer])
    dest, row_tok, item_e, item_sb, item_nb = routing_tables(top_idx[:, :TOP_K], n_tok)
    ys = expert_mlp(u_packed, row_tok, item_e, item_sb, item_nb, w_gate_up[layer], b_gate_up[layer],
                    w_down[layer], b_down[layer])
    out = combine(h2, gates, ys, dest)
    return out.reshape(bsz, seq, d)
```

```python
import functools
import math

import jax
import jax.numpy as jnp
from jax import lax
from jax.experimental import pallas as pl
from jax.experimental.pallas import tpu as pltpu

F32 = jnp.float32
BF16 = jnp.bfloat16

N_META = 16
RMS_EPS = 1e-6
L2_EPS = 1e-6

DN_HEADS = 16
DN_DK = 128
DN_DV = 128
DN_CONV = 4
DN_CHUNK = 64
DN_W = DN_HEADS * DN_DK
DN_HIST = 16
DN_CONV_COLS = 1024

DA_HEADS = 8
DA_DK = 128
DA_DV = 256
LAMBDA_INIT = 0.8 - 0.6 * math.exp(-0.3 * 0)

N_EXPERTS = 32
TOP_K = 4
D_EXPERT = 1536
SWIGLU_LIMIT = 7.0
SWIGLU_ALPHA = 1.702

LANES = 128
MOE_ROWS = 256
MOE_ITEM_BLOCKS = 5
MOE_GATHER_GROUP = 32
NEG_BIG = -1e30
VMEM_LIMIT = 56 * 1024 * 1024

COL_DZ = 3 * DN_W
COL_BA = 4 * DN_W
COL_AQ = 0
COL_AK = COL_AQ + DA_HEADS * 2 * DA_DK
COL_AV = COL_AK + DA_HEADS * 2 * DA_DK
COL_GA = COL_AV + DA_HEADS * DA_DV


def _params(*sem, vmem=None):
    return pltpu.CompilerParams(dimension_semantics=sem, vmem_limit_bytes=vmem)


def _rms_cast_kernel(x_ref, w_ref, o_ref):
    x = x_ref[...]
    y = x * lax.rsqrt(jnp.mean(x * x, axis=-1, keepdims=True) + RMS_EPS) * w_ref[...]
    o_ref[...] = y.astype(o_ref.dtype)


def rms_cast(x, w, tm):
    m, d = x.shape
    return pl.pallas_call(
        _rms_cast_kernel,
        grid=(m // tm,),
        in_specs=[pl.BlockSpec((tm, d), lambda i: (i, 0)),
                  pl.BlockSpec((1, d), lambda i: (0, 0))],
        out_specs=pl.BlockSpec((tm, d), lambda i: (i, 0)),
        out_shape=jax.ShapeDtypeStruct((m, d), BF16),
        compiler_params=_params("parallel"),
        name="rms_cast",
    )(x, w.reshape(1, d))


def _mm_fullk_kernel(a_ref, b_ref, *rest, has_res):
    acc = jnp.dot(a_ref[...], b_ref[...].astype(BF16), preferred_element_type=F32)
    if has_res:
        r_ref, o_ref = rest
        acc = acc + r_ref[...]
    else:
        o_ref, = rest
    o_ref[...] = acc.astype(o_ref.dtype)


def matmul_fullk(a, b, out_dtype, tm, tn, res=None, name="matmul", n_offset=0, n_blocks=None):
    m, kd = a.shape
    if n_blocks is None:
        n_blocks = b.shape[1] // tn
    in_specs = [pl.BlockSpec((tm, kd), lambda i, j: (i, 0)),
                pl.BlockSpec((kd, tn), lambda i, j: (0, n_offset + j))]
    args = [a, b]
    if res is not None:
        in_specs.append(pl.BlockSpec((tm, tn), lambda i, j: (i, j)))
        args.append(res)
    return pl.pallas_call(
        functools.partial(_mm_fullk_kernel, has_res=res is not None),
        grid=(m // tm, n_blocks),
        in_specs=in_specs,
        out_specs=pl.BlockSpec((tm, tn), lambda i, j: (i, j)),
        out_shape=jax.ShapeDtypeStruct((m, n_blocks * tn), out_dtype),
        compiler_params=_params("parallel", "parallel", vmem=VMEM_LIMIT),
        name=name,
    )(*args)


def _deltanet_kernel(qkv_ref, z_ref, ba_ref, qkvm_ref, bam_ref, cw_ref, hp_ref, nw_ref,
                     o_ref, full_ref, act_ref, ba_s, s_ref):
    t = pl.program_id(1)
    C = DN_CHUNK
    HIST = DN_HIST
    n_pad = C - N_META

    @pl.when(t == 0)
    def _():
        s_ref[...] = jnp.zeros_like(s_ref)
        full_ref[0:HIST + n_pad, :] = jnp.zeros((HIST + n_pad, 3 * DN_W), BF16)
        full_ref[HIST + n_pad:HIST + C, :] = qkvm_ref[...]
        ba_s[0:n_pad, :] = jnp.zeros((n_pad, LANES), F32)
        ba_s[n_pad:C, :] = bam_ref[...]

    @pl.when(t > 0)
    def _():
        full_ref[HIST:HIST + C, :] = qkv_ref[...]
        ba_s[...] = ba_ref[...]

    n_shift = DN_CONV - 1
    sr = lax.broadcasted_iota(jnp.int32, (n_shift * C, HIST + C), 0)
    sc = lax.broadcasted_iota(jnp.int32, (n_shift * C, HIST + C), 1)
    shift = jnp.where(sc == HIST - n_shift + (sr % C) + (sr // C), 1.0, 0.0).astype(BF16)
    for c0 in range(0, 3 * DN_W, DN_CONV_COLS):
        cs = slice(c0, c0 + DN_CONV_COLS)
        delayed = jnp.dot(shift, full_ref[:, cs], preferred_element_type=F32)
        conv = cw_ref[n_shift:n_shift + 1, cs] * full_ref[HIST:HIST + C, cs].astype(F32)
        for j in range(n_shift):
            conv = conv + cw_ref[j:j + 1, cs] * delayed[j * C:(j + 1) * C, :]
        act_ref[:, cs] = conv * jax.nn.sigmoid(conv)
    full_ref[0:HIST, :] = full_ref[C:C + HIST, :]

    row1 = lax.broadcasted_iota(jnp.int32, (C, 1), 0)
    valid = jnp.where((t > 0) | (row1 >= n_pad), 1.0, 0.0).astype(F32)
    ba = ba_s[...]
    beta_all = jax.nn.sigmoid(ba) * valid
    xg = ba + hp_ref[1:2, :]
    softplus = jnp.maximum(xg, 0.0) + jnp.log1p(jnp.exp(-jnp.abs(xg)))
    g_all = -jnp.exp(hp_ref[0:1, :]) * softplus * valid

    row = lax.broadcasted_iota(jnp.int32, (C, C), 0)
    col = lax.broadcasted_iota(jnp.int32, (C, C), 1)
    causal = row >= col
    strict = row > col
    eye = jnp.where(row == col, 1.0, 0.0).astype(F32)
    tril = jnp.where(causal, 1.0, 0.0).astype(BF16)

    g_hi = g_all.astype(BF16)
    r1 = g_all - g_hi.astype(F32)
    g_mid = r1.astype(BF16)
    g_lo = (r1 - g_mid.astype(F32)).astype(BF16)
    g3 = jnp.dot(tril, jnp.concatenate([g_hi, g_mid, g_lo], axis=1), preferred_element_type=F32)
    gcum = g3[:, 0:LANES] + g3[:, LANES:2 * LANES] + g3[:, 2 * LANES:3 * LANES]
    gcum_t = gcum.T

    scale = DN_DK ** -0.5
    nw = nw_ref[...]
    heads = range(DN_HEADS)

    def mm(a, b):
        return jnp.dot(a.astype(BF16), b.astype(BF16), preferred_element_type=F32)

    kn_b, lhs1, decay, rhs, qe_b, kdec_t, e_last = [], [], [], [], [], [], []
    for h in heads:
        qh = act_ref[:, h * DN_DK:(h + 1) * DN_DK]
        kh = act_ref[:, DN_W + h * DN_DK:DN_W + (h + 1) * DN_DK]
        vh = act_ref[:, 2 * DN_W + h * DN_DV:2 * DN_W + (h + 1) * DN_DV]
        qn = qh * (lax.rsqrt(jnp.sum(qh * qh, axis=-1, keepdims=True) + L2_EPS) * scale)
        kn = kh * lax.rsqrt(jnp.sum(kh * kh, axis=-1, keepdims=True) + L2_EPS)
        beta = beta_all[:, h:h + 1]
        gc = gcum[:, DN_HEADS + h:DN_HEADS + h + 1]
        gr = gcum_t[DN_HEADS + h:DN_HEADS + h + 1, :]
        g_last = gc[C - 1:C, :]
        e_g = jnp.exp(gc)
        kb = kn * beta
        kn_b.append(kn.astype(BF16))
        lhs1.append(jnp.concatenate([kb, qn], axis=0).astype(BF16))
        decay.append(jnp.exp(jnp.where(causal, gc - gr, NEG_BIG)))
        rhs.append(jnp.concatenate([vh * beta, kb * e_g], axis=1).astype(BF16))
        qe_b.append((qn * e_g).astype(BF16))
        kdec_t.append((kn * jnp.exp(g_last - gc)).T.astype(BF16))
        e_last.append(jnp.exp(g_last))

    kq = [lax.dot_general(lhs1[h], kn_b[h], (((1,), (1,)), ((), ())), preferred_element_type=F32)
          for h in heads]
    qk_b = [jnp.where(causal, kq[h][C:2 * C, :] * decay[h], 0.0).astype(BF16) for h in heads]
    b1 = [-jnp.where(strict, kq[h][0:C, :] * decay[h], 0.0) for h in heads]
    b2 = [mm(b1[h], b1[h]) for h in heads]
    b4 = [mm(b2[h], b2[h]) for h in heads]
    p1 = [(eye + b1[h]) + mm(eye + b1[h], b2[h]) for h in heads]
    b8 = [mm(b4[h], b4[h]) for h in heads]
    b16 = [mm(b8[h], b8[h]) for h in heads]
    p2 = [(eye + b4[h]) + mm(eye + b4[h], b8[h]) for h in heads]
    b32 = [mm(b16[h], b16[h]) for h in heads]
    p12 = [mm(p1[h], p2[h]) for h in heads]
    p3 = [(eye + b16[h]) + mm(eye + b16[h], b32[h]) for h in heads]
    inv = [mm(p12[h], p3[h]) for h in heads]
    sol = [mm(inv[h], rhs[h]) for h in heads]

    s_old = [s_ref[h] for h in heads]
    ws_lhs = [jnp.concatenate([sol[h][:, DN_DV:2 * DN_DV].astype(BF16), qe_b[h]], axis=0) for h in heads]
    ws = [mm(ws_lhs[h], s_old[h]) for h in heads]
    v_new = [(sol[h][:, 0:DN_DV] - ws[h][0:C, :]).astype(BF16) for h in heads]
    ov_lhs = [jnp.concatenate([qk_b[h], kdec_t[h]], axis=0) for h in heads]
    ov = [mm(ov_lhs[h], v_new[h]) for h in heads]
    for h in heads:
        s_ref[h] = s_old[h] * e_last[h] + ov[h][C:C + DN_DK, :]
        o = ws[h][C:2 * C, :] + ov[h][0:C, :]
        zh = z_ref[:, h * DN_DV:(h + 1) * DN_DV].astype(F32)
        o = o * lax.rsqrt(jnp.mean(o * o, axis=-1, keepdims=True) + RMS_EPS) * nw
        o_ref[:, h * DN_DV:(h + 1) * DN_DV] = (o * (zh * jax.nn.sigmoid(zh))).astype(o_ref.dtype)


def gated_deltanet(proj_x, ba_x, proj_m, ba_m, conv_w, a_log, dt_bias, norm_w, bsz, seq):
    C = DN_CHUNK
    n_chunks = seq // C
    w3 = 3 * DN_W
    hp = jnp.zeros((2, LANES), F32)
    hp = hp.at[0, DN_HEADS:2 * DN_HEADS].set(a_log.astype(F32))
    hp = hp.at[1, DN_HEADS:2 * DN_HEADS].set(dt_bias.astype(F32))

    def xrow(b, t):
        return b * n_chunks + jnp.maximum(t - 1, 0)

    return pl.pallas_call(
        _deltanet_kernel,
        grid=(bsz, n_chunks + 1),
        in_specs=[
            pl.BlockSpec((C, w3), lambda b, t: (xrow(b, t), 0)),
            pl.BlockSpec((C, DN_W), lambda b, t: (xrow(b, t), COL_DZ // DN_W)),
            pl.BlockSpec((C, LANES), lambda b, t: (xrow(b, t), 0)),
            pl.BlockSpec((N_META, w3), lambda b, t: (0, 0)),
            pl.BlockSpec((N_META, LANES), lambda b, t: (0, 0)),
            pl.BlockSpec((DN_CONV, w3), lambda b, t: (0, 0)),
            pl.BlockSpec((2, LANES), lambda b, t: (0, 0)),
            pl.BlockSpec((1, DN_DV), lambda b, t: (0, 0)),
        ],
        out_specs=pl.BlockSpec((C, DN_W), lambda b, t: (xrow(b, t), 0)),
        out_shape=jax.ShapeDtypeStruct((bsz * seq, DN_W), BF16),
        scratch_shapes=[
            pltpu.VMEM((DN_HIST + C, w3), BF16),
            pltpu.VMEM((C, w3), F32),
            pltpu.VMEM((C, LANES), F32),
            pltpu.VMEM((DN_HEADS, DN_DK, DN_DV), F32),
        ],
        compiler_params=_params("parallel", "arbitrary", vmem=VMEM_LIMIT),
        name="gated_deltanet",
    )(proj_x, proj_x, ba_x, proj_m, ba_m, conv_w.astype(F32), hp, norm_w.reshape(1, DN_DV).astype(F32))


def _diff_attn_kernel(q_ref, k_ref, v_ref, km_ref, vm_ref, qn_ref, kn_ref, lam_ref, sn_ref,
                      o_ref, kx_s, kmeta_s, *, tq, seq):
    qi = pl.program_id(2)

    def rms(x, w):
        return x * lax.rsqrt(jnp.mean(x * x, axis=-1, keepdims=True) + RMS_EPS) * w

    @pl.when(qi == 0)
    def _():
        for m in range(2):
            kx = k_ref[:, m * DA_DK:(m + 1) * DA_DK].astype(F32)
            kx_s[m] = rms(kx, kn_ref[...]).astype(BF16)
            kmx = km_ref[:, m * DA_DK:(m + 1) * DA_DK].astype(F32)
            kmeta_s[m] = rms(kmx, kn_ref[...]).astype(BF16)

    lam = (jnp.exp(jnp.sum(lam_ref[0:1, :] * lam_ref[1:2, :], axis=-1, keepdims=True))
           - jnp.exp(jnp.sum(lam_ref[2:3, :] * lam_ref[3:4, :], axis=-1, keepdims=True))
           + LAMBDA_INIT)
    row = lax.broadcasted_iota(jnp.int32, (tq, tq), 0)
    col = lax.broadcasted_iota(jnp.int32, (tq, tq), 1)
    diag_visible = col <= row
    v_m = vm_ref[...]
    scale = DA_DK ** -0.5 * math.log2(math.e)
    nt = (((1,), (1,)), ((), ()))

    def attend(n_prev):
        outs = []
        for m in range(2):
            qx = q_ref[:, m * DA_DK:(m + 1) * DA_DK].astype(F32)
            qn = (rms(qx, qn_ref[...]) * scale).astype(BF16)
            sd = lax.dot_general(qn, kx_s[m, n_prev:n_prev + tq, :], nt, preferred_element_type=F32)
            sd = jnp.where(diag_visible, sd, NEG_BIG)
            sm = lax.dot_general(qn, kmeta_s[m], nt, preferred_element_type=F32)
            mx = jnp.maximum(jnp.max(sd, axis=-1, keepdims=True), jnp.max(sm, axis=-1, keepdims=True))
            if n_prev:
                sp = lax.dot_general(qn, kx_s[m, 0:n_prev, :], nt, preferred_element_type=F32)
                mx = jnp.maximum(mx, jnp.max(sp, axis=-1, keepdims=True))
            pd = jnp.exp2(sd - mx)
            pm = jnp.exp2(sm - mx)
            den = jnp.sum(pd, axis=-1, keepdims=True) + jnp.sum(pm, axis=-1, keepdims=True)
            acc = (jnp.dot(pd.astype(BF16), v_ref[n_prev:n_prev + tq, :], preferred_element_type=F32)
                   + jnp.dot(pm.astype(BF16), v_m, preferred_element_type=F32))
            if n_prev:
                pp = jnp.exp2(sp - mx)
                den = den + jnp.sum(pp, axis=-1, keepdims=True)
                acc = acc + jnp.dot(pp.astype(BF16), v_ref[0:n_prev, :], preferred_element_type=F32)
            outs.append(acc / den)
        out = outs[0] - lam * outs[1]
        out = rms(out, sn_ref[...]) * (1.0 - LAMBDA_INIT)
        o_ref[...] = out.astype(o_ref.dtype)

    for blk in range(seq // tq):
        @pl.when(qi == blk)
        def _():
            attend(blk * tq)


def diff_attention(proj_x, proj_m, q_norm, k_norm, lam4, sub_norm, bsz, seq, tq=256):
    nq = seq // tq
    cq = COL_AQ // DA_DV
    ck = COL_AK // DA_DV
    cv = COL_AV // DA_DV
    return pl.pallas_call(
        functools.partial(_diff_attn_kernel, tq=tq, seq=seq),
        grid=(bsz, DA_HEADS, nq),
        in_specs=[
            pl.BlockSpec((tq, DA_DV), lambda b, h, i: (b * nq + i, cq + h)),
            pl.BlockSpec((seq, DA_DV), lambda b, h, i: (b, ck + h)),
            pl.BlockSpec((seq, DA_DV), lambda b, h, i: (b, cv + h)),
            pl.BlockSpec((N_META, DA_DV), lambda b, h, i: (0, h)),
            pl.BlockSpec((N_META, DA_DV), lambda b, h, i: (0, DA_HEADS + h)),
            pl.BlockSpec((1, DA_DK), lambda b, h, i: (0, 0)),
            pl.BlockSpec((1, DA_DK), lambda b, h, i: (0, 0)),
            pl.BlockSpec((4, DA_DK), lambda b, h, i: (0, 0)),
            pl.BlockSpec((1, DA_DV), lambda b, h, i: (0, 0)),
        ],
        out_specs=pl.BlockSpec((tq, DA_DV), lambda b, h, i: (b * nq + i, h)),
        out_shape=jax.ShapeDtypeStruct((bsz * seq, DA_HEADS * DA_DV), BF16),
        scratch_shapes=[pltpu.VMEM((2, seq, DA_DK), BF16),
                        pltpu.VMEM((2, N_META, DA_DK), BF16)],
        compiler_params=_params("parallel", "parallel", "arbitrary", vmem=VMEM_LIMIT),
        name="diff_attention",
    )(proj_x, proj_x, proj_x, proj_m, proj_m,
      q_norm.reshape(1, DA_DK).astype(F32), k_norm.reshape(1, DA_DK).astype(F32),
      lam4.astype(F32), sub_norm.reshape(1, DA_DV).astype(F32))


def _merge_kernel(ya_ref, yb_ref, wa_ref, wb_ref, ga_ref, gb_ref, o_ref):
    pa = jnp.dot(ya_ref[...], wa_ref[...].astype(BF16), preferred_element_type=F32)
    pb = jnp.dot(yb_ref[...], wb_ref[...].astype(BF16), preferred_element_type=F32)
    ga = jax.nn.sigmoid(ga_ref[...].astype(F32))
    gb = jax.nn.sigmoid(gb_ref[...].astype(F32))
    o_ref[...] = (ga * pa + gb * pb).astype(o_ref.dtype)


def branch_merge(y_a, y_b, w_a, w_b, proj_x, d_model, tm=1024, tn=512):
    m, ka = y_a.shape
    kb = y_b.shape[1]
    cga = COL_GA // tn
    cgb = (COL_GA + d_model) // tn
    return pl.pallas_call(
        _merge_kernel,
        grid=(m // tm, d_model // tn),
        in_specs=[
            pl.BlockSpec((tm, ka), lambda i, j: (i, 0)),
            pl.BlockSpec((tm, kb), lambda i, j: (i, 0)),
            pl.BlockSpec((ka, tn), lambda i, j: (0, j)),
            pl.BlockSpec((kb, tn), lambda i, j: (0, j)),
            pl.BlockSpec((tm, tn), lambda i, j: (i, cga + j)),
            pl.BlockSpec((tm, tn), lambda i, j: (i, cgb + j)),
        ],
        out_specs=pl.BlockSpec((tm, tn), lambda i, j: (i, j)),
        out_shape=jax.ShapeDtypeStruct((m, d_model), BF16),
        compiler_params=_params("parallel", "parallel", vmem=VMEM_LIMIT),
        name="branch_merge",
    )(y_a, y_b, w_a, w_b, proj_x, proj_x)


def _router_kernel(h_ref, nw_ref, whi_ref, wlo_ref, b_ref, idx_ref, gate_ref, up_ref):
    x = h_ref[...]
    u = x * lax.rsqrt(jnp.mean(x * x, axis=-1, keepdims=True) + RMS_EPS) * nw_ref[...]
    u_hi = u.astype(BF16)
    u_lo = (u - u_hi.astype(F32)).astype(BF16)
    half = u.shape[1] // 2
    bits = pltpu.bitcast(u_hi.astype(F32), jnp.uint32)
    up_ref[...] = bits[:, half:] | (bits[:, :half] >> 16)
    logits = (jnp.dot(u_hi, whi_ref[...], preferred_element_type=F32)
              + jnp.dot(u_lo, whi_ref[...], preferred_element_type=F32)
              + jnp.dot(u_hi, wlo_ref[...], preferred_element_type=F32)
              + b_ref[...])
    lane = lax.broadcasted_iota(jnp.int32, logits.shape, 1)
    lane_f = lane.astype(F32)
    vals = logits
    tops, idxs = [], []
    for _ in range(TOP_K):
        mx = jnp.max(vals, axis=-1, keepdims=True)
        ix = jnp.min(jnp.where(vals == mx, lane_f, float(LANES)), axis=-1, keepdims=True).astype(jnp.int32)
        tops.append(mx)
        idxs.append(ix)
        vals = jnp.where(lane == ix, -3.0e38, vals)
    exps = [jnp.exp(tv - tops[0]) for tv in tops]
    den = exps[0] + exps[1] + exps[2] + exps[3]
    idx_out = jnp.zeros(logits.shape, jnp.int32)
    gate_out = jnp.zeros(logits.shape, F32)
    for k in range(TOP_K):
        idx_out = jnp.where(lane == k, idxs[k], idx_out)
        gate_out = jnp.where(lane == k, exps[k] / den, gate_out)
    idx_ref[...] = idx_out
    gate_ref[...] = gate_out


def router(h2, ffn_norm, router_w, router_b, tm=256):
    n, d = h2.shape
    wpad = jnp.zeros((d, LANES), F32).at[:, :N_EXPERTS].set(router_w.astype(F32))
    w_hi = wpad.astype(BF16)
    w_lo = (wpad - w_hi.astype(F32)).astype(BF16)
    bias = jnp.full((1, LANES), NEG_BIG, F32).at[0, :N_EXPERTS].set(router_b.astype(F32))
    return pl.pallas_call(
        _router_kernel,
        grid=(n // tm,),
        in_specs=[
            pl.BlockSpec((tm, d), lambda i: (i, 0)),
            pl.BlockSpec((1, d), lambda i: (0, 0)),
            pl.BlockSpec((d, LANES), lambda i: (0, 0)),
            pl.BlockSpec((d, LANES), lambda i: (0, 0)),
            pl.BlockSpec((1, LANES), lambda i: (0, 0)),
        ],
        out_specs=[pl.BlockSpec((tm, LANES), lambda i: (i, 0)),
                   pl.BlockSpec((tm, LANES), lambda i: (i, 0)),
                   pl.BlockSpec((tm, d // 2), lambda i: (i, 0))],
        out_shape=[jax.ShapeDtypeStruct((n, LANES), jnp.int32),
                   jax.ShapeDtypeStruct((n, LANES), F32),
                   jax.ShapeDtypeStruct((n, d // 2), jnp.uint32)],
        compiler_params=_params("parallel", vmem=VMEM_LIMIT),
        name="router",
    )(h2, ffn_norm.reshape(1, d).astype(F32), w_hi, w_lo, bias)


def _expert_kernel(ie_ref, isb_ref, inb_ref, tok_ref, up_hbm, wgu_hbm, wd_hbm, bgu_ref, bd_ref,
                   ys_hbm, xbuf, act, ybuf, wgu_buf, wd_buf, wg_s, wu_s, wd_s,
                   sem_x, sem_y, sem_gu, sem_d, *, j1, j2, tn1, tn2, n_items):
    i = pl.program_id(0)
    nb = inb_ref[i]
    sb = isb_ref[i]
    ex = ie_ref[i]
    R = MOE_ROWS
    G = MOE_GATHER_GROUP

    def gu_copies(expert, jt, slot):
        return [pltpu.make_async_copy(
            wgu_hbm.at[expert, :, pl.ds(pl.multiple_of(which * j1 * tn1 + jt * tn1, tn1), tn1)],
            wgu_buf.at[slot, which], sem_gu.at[slot]) for which in range(2)]

    def d_copy(expert, jd, slot):
        return pltpu.make_async_copy(
            wd_hbm.at[expert, :, pl.ds(pl.multiple_of(jd * tn2, tn2), tn2)],
            wd_buf.at[slot], sem_d.at[slot])

    def gather_groups(item, g_lo, n_groups):
        n_rows = inb_ref[item] * R
        base = isb_ref[item] * R
        for gi in range(n_groups):
            g0 = (g_lo + gi) * G

            @pl.when(g0 < n_rows)
            def _():
                def start(rr, c):
                    tok = tok_ref[base + g0 + rr]
                    pltpu.make_async_copy(up_hbm.at[pl.ds(tok, 1), :],
                                          xbuf.at[pl.ds(g0 + rr, 1), :], sem_x).start()
                    return c
                lax.fori_loop(0, G, start, 0, unroll=8)

    def x_block_wait(r):
        pltpu.make_async_copy(up_hbm.at[pl.ds(0, R), :], xbuf.at[pl.ds(r * R, R), :], sem_x).wait()

    def y_copy(slot, r, jd):
        return pltpu.make_async_copy(
            ybuf.at[slot, pl.ds(r * R, R), :],
            ys_hbm.at[pl.ds((sb + r) * R, R), pl.ds(pl.multiple_of(jd * tn2, tn2), tn2)],
            sem_y.at[slot])

    def for_blocks(fn):
        for r in range(MOE_ITEM_BLOCKS):
            @pl.when(r < nb)
            def _():
                fn(r)

    groups_total = MOE_ITEM_BLOCKS * R // G
    groups_per_step = groups_total // j2

    @pl.when(i == 0)
    def _():
        gather_groups(0, 0, groups_total)
        for cp in gu_copies(ex, 0, 0):
            cp.start()

    def gate_up_block(jt, r0, wg, wu):
        x32 = xbuf[pl.ds(r0, R), :]
        x_lo = pltpu.bitcast(x32 << 16, F32).astype(BF16)
        x_hi = pltpu.bitcast(x32 & jnp.uint32(0xFFFF0000), F32).astype(BF16)
        x = jnp.concatenate([x_lo, x_hi], axis=1)
        g = jnp.dot(x, wg, preferred_element_type=F32) + bgu_ref[pl.ds(jt, 1), :]
        u = jnp.dot(x, wu, preferred_element_type=F32) + bgu_ref[pl.ds(j1 + jt, 1), :]
        gt = jnp.minimum(g, SWIGLU_LIMIT)
        up = jnp.clip(u, -SWIGLU_LIMIT, SWIGLU_LIMIT)
        a = gt * jax.nn.sigmoid(SWIGLU_ALPHA * gt) * (up + 1.0)
        act[jt, pl.ds(r0, R), :] = a.astype(BF16)

    def down_block(jd, slot, r0, wd):
        lhs = jnp.concatenate([act[c, pl.ds(r0, R), :] for c in range(j1)], axis=1)
        ybuf[slot, pl.ds(r0, R), :] = (jnp.dot(lhs, wd, preferred_element_type=F32)
                                       + bd_ref[pl.ds(jd, 1), :])

    @pl.when(nb > 0)
    def _():
        for_blocks(x_block_wait)
        d_copy(ex, 0, 0).start()

        def gate_up_tile(jt, carry):
            slot = jt % 2

            @pl.when(jt + 1 < j1)
            def _():
                for cp in gu_copies(ex, jt + 1, 1 - slot):
                    cp.start()

            for cp in gu_copies(ex, jt, slot):
                cp.wait()
            wg = wgu_buf[slot, 0].astype(BF16)
            wu = wgu_buf[slot, 1].astype(BF16)
            wg_s[...] = wg
            wu_s[...] = wu
            gate_up_block(jt, 0, wg, wu)

            def body(r, c):
                gate_up_block(jt, pl.multiple_of(r * R, R), wg_s[...], wu_s[...])
                return c

            lax.fori_loop(1, nb, body, 0)
            return carry

        lax.fori_loop(0, j1, gate_up_tile, 0)

        @pl.when(i + 1 < n_items)
        def _():
            @pl.when(inb_ref[i + 1] > 0)
            def _():
                for cp in gu_copies(ie_ref[i + 1], 0, 0):
                    cp.start()

        def down_tile(jd, carry):
            slot = jd % 2

            @pl.when(jd + 1 < j2)
            def _():
                d_copy(ex, jd + 1, 1 - slot).start()

            @pl.when(i + 1 < n_items)
            def _():
                gather_groups(i + 1, jd * groups_per_step, groups_per_step)

            @pl.when(jd >= 2)
            def _():
                for_blocks(lambda r: y_copy(slot, r, jd).wait())

            d_copy(ex, jd, slot).wait()
            wd = wd_buf[slot].astype(BF16)
            wd_s[...] = wd
            down_block(jd, slot, 0, wd)

            def body(r, c):
                down_block(jd, slot, pl.multiple_of(r * R, R), wd_s[...])
                return c

            lax.fori_loop(1, nb, body, 0)
            for_blocks(lambda r: y_copy(slot, r, jd).start())
            return carry

        lax.fori_loop(0, j2, down_tile, 0)
        for_blocks(lambda r: y_copy(0, r, 0).wait())
        for_blocks(lambda r: y_copy(1, r, 0).wait())


def expert_mlp(u_packed, row_tok, item_e, item_sb, item_nb, w_gate_up, b_gate_up, w_down, b_down,
               tn1=256, tn2=512):
    p = row_tok.shape[0]
    d = 2 * u_packed.shape[1]
    n_items = item_e.shape[0]
    de = w_down.shape[1]
    j1 = de // tn1
    j2 = d // tn2
    rmax = MOE_ITEM_BLOCKS * MOE_ROWS

    assert (MOE_ITEM_BLOCKS * MOE_ROWS) % (MOE_GATHER_GROUP * j2) == 0

    def bias_map(i, ie, isb, inb, tok):
        return (ie[i], 0, 0)

    return pl.pallas_call(
        functools.partial(_expert_kernel, j1=j1, j2=j2, tn1=tn1, tn2=tn2, n_items=n_items),
        grid_spec=pltpu.PrefetchScalarGridSpec(
            num_scalar_prefetch=4,
            grid=(n_items,),
            in_specs=[
                pl.BlockSpec(memory_space=pl.ANY),
                pl.BlockSpec(memory_space=pl.ANY),
                pl.BlockSpec(memory_space=pl.ANY),
                pl.BlockSpec((None, 2 * j1, tn1), bias_map),
                pl.BlockSpec((None, j2, tn2), bias_map),
            ],
            out_specs=pl.BlockSpec(memory_space=pl.ANY),
            scratch_shapes=[
                pltpu.VMEM((rmax, d // 2), jnp.uint32),
                pltpu.VMEM((j1, rmax, tn1), BF16),
                pltpu.VMEM((2, rmax, tn2), F32),
                pltpu.VMEM((2, 2, d, tn1), F32),
                pltpu.VMEM((2, de, tn2), F32),
                pltpu.VMEM((d, tn1), BF16),
                pltpu.VMEM((d, tn1), BF16),
                pltpu.VMEM((de, tn2), BF16),
                pltpu.SemaphoreType.DMA(()),
                pltpu.SemaphoreType.DMA((2,)),
                pltpu.SemaphoreType.DMA((2,)),
                pltpu.SemaphoreType.DMA((2,)),
            ],
        ),
        out_shape=jax.ShapeDtypeStruct((p, d), F32),
        compiler_params=_params("arbitrary", vmem=VMEM_LIMIT),
        name="expert_mlp",
    )(item_e, item_sb, item_nb, row_tok, u_packed, w_gate_up, w_down,
      b_gate_up.reshape(N_EXPERTS, 2 * j1, tn1), b_down.reshape(N_EXPERTS, j2, tn2))


def _combine_kernel(dest_ref, h_ref, g_ref, ys_hbm, o_ref, buf, sem, *, tt, n_tok, n_steps):
    i = pl.program_id(0)

    def issue(blk, slot):
        for k in range(TOP_K):
            def start(t, c):
                src = dest_ref[k * n_tok + blk * tt + t]
                pltpu.make_async_copy(ys_hbm.at[pl.ds(src, 1), :], buf.at[slot, k, pl.ds(t, 1), :],
                                      sem.at[slot]).start()
                return c
            lax.fori_loop(0, tt, start, 0, unroll=8)

    @pl.when(i == 0)
    def _():
        issue(0, 0)

    slot = i % 2

    @pl.when(i + 1 < n_steps)
    def _():
        issue(i + 1, 1 - slot)

    for k in range(TOP_K):
        pltpu.make_async_copy(ys_hbm.at[pl.ds(0, tt), :], buf.at[slot, k], sem.at[slot]).wait()
    acc = h_ref[...]
    g = g_ref[...]
    for k in range(TOP_K):
        acc = acc + g[:, k:k + 1] * buf[slot, k]
    o_ref[...] = acc


def combine(h2, gates, ys, dest_km, tt=64):
    n, d = h2.shape
    return pl.pallas_call(
        functools.partial(_combine_kernel, tt=tt, n_tok=n, n_steps=n // tt),
        grid_spec=pltpu.PrefetchScalarGridSpec(
            num_scalar_prefetch=1,
            grid=(n // tt,),
            in_specs=[pl.BlockSpec((tt, d), lambda i, dst: (i, 0)),
                      pl.BlockSpec((tt, LANES), lambda i, dst: (i, 0)),
                      pl.BlockSpec(memory_space=pl.ANY)],
            out_specs=pl.BlockSpec((tt, d), lambda i, dst: (i, 0)),
            scratch_shapes=[pltpu.VMEM((2, TOP_K, tt, d), F32), pltpu.SemaphoreType.DMA((2,))],
        ),
        out_shape=jax.ShapeDtypeStruct((n, d), F32),
        compiler_params=_params("arbitrary", vmem=VMEM_LIMIT),
        name="moe_combine",
    )(dest_km, h2, gates, ys)


def routing_tables(top_idx, n_tok):
    a = n_tok * TOP_K
    e_flat = top_idx.reshape(-1)
    onehot = (e_flat[:, None] == jnp.arange(N_EXPERTS, dtype=jnp.int32)[None, :]).astype(jnp.int32)
    csum = jnp.cumsum(onehot, axis=0)
    rank = jnp.sum(csum * onehot, axis=1) - 1
    counts = csum[-1]
    nblk = (counts + MOE_ROWS - 1) // MOE_ROWS
    blk_end = jnp.cumsum(nblk)
    blk_start = blk_end - nblk
    dest = blk_start[e_flat] * MOE_ROWS + rank
    n_blocks = -(-(a + N_EXPERTS * (MOE_ROWS - 1)) // MOE_ROWS)
    p = n_blocks * MOE_ROWS
    t_flat = jnp.arange(a, dtype=jnp.int32) // TOP_K
    row_tok = jnp.zeros((p,), jnp.int32).at[dest].set(t_flat)

    n_items = N_EXPERTS + n_blocks // MOE_ITEM_BLOCKS
    items_per_e = (nblk + MOE_ITEM_BLOCKS - 1) // MOE_ITEM_BLOCKS
    item_end = jnp.cumsum(items_per_e)
    item_start = item_end - items_per_e
    total_items = item_end[-1]
    ii = jnp.arange(n_items, dtype=jnp.int32)
    ic = jnp.minimum(ii, total_items - 1)
    e_i = jnp.minimum(jnp.searchsorted(item_end, ic, side='right'), N_EXPERTS - 1).astype(jnp.int32)
    local = ic - item_start[e_i]
    item_sb = (blk_start[e_i] + local * MOE_ITEM_BLOCKS).astype(jnp.int32)
    item_nb = jnp.clip(nblk[e_i] - local * MOE_ITEM_BLOCKS, 0, MOE_ITEM_BLOCKS)
    item_nb = jnp.where(ii < total_items, item_nb, 0).astype(jnp.int32)
    dest_km = dest.astype(jnp.int32).reshape(n_tok, TOP_K).T.reshape(-1)
    return dest_km, row_tok, e_i, item_sb, item_nb


def kernel(x, meta_tokens, mix_norm, w_in, dn_conv, dn_a_log, dn_dt_bias, dn_out_norm, da_q_norm, da_k_norm, da_lam_q1, da_lam_k1, da_lam_q2, da_lam_k2, da_sub_norm, w_branch_a, w_branch_b, w_out, ffn_norm, router_w, router_b, w_gate_up, b_gate_up, w_down, b_down):
    bsz, seq, d = x.shape
    n_tok = bsz * seq
    x2 = x.reshape(n_tok, d)
    layer = 0

    w_bf = w_in[layer].astype(BF16)
    w_da = w_bf[:, COL_BA + 2 * DN_HEADS:]
    tn = 1024

    u_x = rms_cast(x2, mix_norm[layer], 256)
    u_m = rms_cast(meta_tokens.astype(F32), mix_norm[layer], N_META)
    proj_dn = matmul_fullk(u_x, w_bf, BF16, 1024, tn, name="in_proj_dn", n_blocks=COL_BA // tn)
    proj_da = matmul_fullk(u_x, w_da, BF16, 1024, tn, name="in_proj_da")
    ba_x = matmul_fullk(u_x, w_bf, F32, 1024, LANES, name="in_proj_ba",
                        n_offset=COL_BA // LANES, n_blocks=1)
    proj_dn_m = matmul_fullk(u_m, w_bf, BF16, N_META, tn, name="in_proj_dn_meta", n_blocks=COL_DZ // tn)
    proj_kv_m = matmul_fullk(u_m, w_da, BF16, N_META, tn, name="in_proj_kv_meta",
                             n_offset=COL_AK // tn, n_blocks=(COL_GA - COL_AK) // tn)
    ba_m = matmul_fullk(u_m, w_bf, F32, N_META, LANES, name="in_proj_ba_meta",
                        n_offset=COL_BA // LANES, n_blocks=1)

    y_a = gated_deltanet(proj_dn, ba_x, proj_dn_m, ba_m, dn_conv[layer], dn_a_log[layer],
                         dn_dt_bias[layer], dn_out_norm[layer], bsz, seq)
    lam4 = jnp.stack([da_lam_q1[layer], da_lam_k1[layer], da_lam_q2[layer], da_lam_k2[layer]])
    y_b = diff_attention(proj_da, proj_kv_m, da_q_norm[layer], da_k_norm[layer], lam4,
                         da_sub_norm[layer], bsz, seq)

    merged = branch_merge(y_a, y_b, w_branch_a[layer], w_branch_b[layer], proj_da, d)
    h2 = matmul_fullk(merged, w_out[layer], F32, 1024, 512, res=x2, name="out_proj")

    top_idx, gates, u_packed = router(h2, ffn_norm[layer], router_w[layer], router_b[layer])
    dest, row_tok, item_e, item_sb, item_nb = routing_tables(top_idx[:, :TOP_K], n_tok)
    ys = expert_mlp(u_packed, row_tok, item_e, item_sb, item_nb, w_gate_up[layer], b_gate_up[layer],
                    w_down[layer], b_down[layer])
    out = combine(h2, gates, ys, dest)
    return out.reshape(bsz, seq, d)
```

```python
import functools
import math

import jax
import jax.numpy as jnp
from jax import lax
from jax.experimental import pallas as pl
from jax.experimental.pallas import tpu as pltpu

F32 = jnp.float32
BF16 = jnp.bfloat16

N_META = 16
RMS_EPS = 1e-6
L2_EPS = 1e-6

DN_HEADS = 16
DN_DK = 128
DN_DV = 128
DN_CONV = 4
DN_CHUNK = 64
DN_W = DN_HEADS * DN_DK
DN_HIST = 16
DN_CONV_COLS = 1024

DA_HEADS = 8
DA_DK = 128
DA_DV = 256
LAMBDA_INIT = 0.8 - 0.6 * math.exp(-0.3 * 0)

N_EXPERTS = 32
TOP_K = 4
D_EXPERT = 1536
SWIGLU_LIMIT = 7.0
SWIGLU_ALPHA = 1.702

LANES = 128
MOE_ROWS = 256
MOE_ITEM_BLOCKS = 5
NEG_BIG = -1e30
VMEM_LIMIT = 56 * 1024 * 1024

COL_DZ = 3 * DN_W
COL_BA = 4 * DN_W
COL_AQ = 0
COL_AK = COL_AQ + DA_HEADS * 2 * DA_DK
COL_AV = COL_AK + DA_HEADS * 2 * DA_DK
COL_GA = COL_AV + DA_HEADS * DA_DV


def _params(*sem, vmem=None):
    return pltpu.CompilerParams(dimension_semantics=sem, vmem_limit_bytes=vmem)


def _rms_cast_kernel(x_ref, w_ref, o_ref):
    x = x_ref[...]
    y = x * lax.rsqrt(jnp.mean(x * x, axis=-1, keepdims=True) + RMS_EPS) * w_ref[...]
    o_ref[...] = y.astype(o_ref.dtype)


def rms_cast(x, w, tm):
    m, d = x.shape
    return pl.pallas_call(
        _rms_cast_kernel,
        grid=(m // tm,),
        in_specs=[pl.BlockSpec((tm, d), lambda i: (i, 0)),
                  pl.BlockSpec((1, d), lambda i: (0, 0))],
        out_specs=pl.BlockSpec((tm, d), lambda i: (i, 0)),
        out_shape=jax.ShapeDtypeStruct((m, d), BF16),
        compiler_params=_params("parallel"),
        name="rms_cast",
    )(x, w.reshape(1, d))


def _repack_kernel(w_ref, dn_ref, da_ref, *, dn_cols, da_start):
    dn_ref[...] = w_ref[:, 0:dn_cols].astype(BF16)
    da_ref[...] = w_ref[:, da_start:].astype(BF16)


def repack_w_in(w, dn_cols, da_start, tr=128):
    rows, width = w.shape
    return pl.pallas_call(
        functools.partial(_repack_kernel, dn_cols=dn_cols, da_start=da_start),
        grid=(rows // tr,),
        in_specs=[pl.BlockSpec((tr, width), lambda i: (i, 0))],
        out_specs=[pl.BlockSpec((tr, dn_cols), lambda i: (i, 0)),
                   pl.BlockSpec((tr, width - da_start), lambda i: (i, 0))],
        out_shape=[jax.ShapeDtypeStruct((rows, dn_cols), BF16),
                   jax.ShapeDtypeStruct((rows, width - da_start), BF16)],
        compiler_params=_params("parallel", vmem=VMEM_LIMIT),
        name="repack_w_in",
    )(w)


def _mm_fullk_kernel(a_ref, b_ref, *rest, has_res):
    acc = jnp.dot(a_ref[...], b_ref[...].astype(BF16), preferred_element_type=F32)
    if has_res:
        r_ref, o_ref = rest
        acc = acc + r_ref[...]
    else:
        o_ref, = rest
    o_ref[...] = acc.astype(o_ref.dtype)


def matmul_fullk(a, b, out_dtype, tm, tn, res=None, name="matmul", n_offset=0, n_blocks=None):
    m, kd = a.shape
    if n_blocks is None:
        n_blocks = b.shape[1] // tn
    in_specs = [pl.BlockSpec((tm, kd), lambda i, j: (i, 0)),
                pl.BlockSpec((kd, tn), lambda i, j: (0, n_offset + j))]
    args = [a, b]
    if res is not None:
        in_specs.append(pl.BlockSpec((tm, tn), lambda i, j: (i, j)))
        args.append(res)
    return pl.pallas_call(
        functools.partial(_mm_fullk_kernel, has_res=res is not None),
        grid=(m // tm, n_blocks),
        in_specs=in_specs,
        out_specs=pl.BlockSpec((tm, tn), lambda i, j: (i, j)),
        out_shape=jax.ShapeDtypeStruct((m, n_blocks * tn), out_dtype),
        compiler_params=_params("parallel", "parallel", vmem=VMEM_LIMIT),
        name=name,
    )(*args)


def _deltanet_kernel(qkv_ref, z_ref, ba_ref, qkvm_ref, bam_ref, cw_ref, hp_ref, nw_ref,
                     o_ref, full_ref, act_ref, ba_s, s_ref):
    t = pl.program_id(1)
    C = DN_CHUNK
    HIST = DN_HIST
    n_pad = C - N_META

    @pl.when(t == 0)
    def _():
        s_ref[...] = jnp.zeros_like(s_ref)
        full_ref[0:HIST + n_pad, :] = jnp.zeros((HIST + n_pad, 3 * DN_W), BF16)
        full_ref[HIST + n_pad:HIST + C, :] = qkvm_ref[...]
        ba_s[0:n_pad, :] = jnp.zeros((n_pad, LANES), F32)
        ba_s[n_pad:C, :] = bam_ref[...]

    @pl.when(t > 0)
    def _():
        full_ref[HIST:HIST + C, :] = qkv_ref[...]
        ba_s[...] = ba_ref[...]

    n_shift = DN_CONV - 1
    sr = lax.broadcasted_iota(jnp.int32, (n_shift * C, HIST + C), 0)
    sc = lax.broadcasted_iota(jnp.int32, (n_shift * C, HIST + C), 1)
    shift = jnp.where(sc == HIST - n_shift + (sr % C) + (sr // C), 1.0, 0.0).astype(BF16)
    for c0 in range(0, 3 * DN_W, DN_CONV_COLS):
        cs = slice(c0, c0 + DN_CONV_COLS)
        delayed = jnp.dot(shift, full_ref[:, cs], preferred_element_type=F32)
        conv = cw_ref[n_shift:n_shift + 1, cs] * full_ref[HIST:HIST + C, cs].astype(F32)
        for j in range(n_shift):
            conv = conv + cw_ref[j:j + 1, cs] * delayed[j * C:(j + 1) * C, :]
        act_ref[:, cs] = conv * jax.nn.sigmoid(conv)
    full_ref[0:HIST, :] = full_ref[C:C + HIST, :]

    row1 = lax.broadcasted_iota(jnp.int32, (C, 1), 0)
    valid = jnp.where((t > 0) | (row1 >= n_pad), 1.0, 0.0).astype(F32)
    ba = ba_s[...]
    beta_all = jax.nn.sigmoid(ba) * valid
    xg = ba + hp_ref[1:2, :]
    softplus = jnp.maximum(xg, 0.0) + jnp.log1p(jnp.exp(-jnp.abs(xg)))
    g_all = -jnp.exp(hp_ref[0:1, :]) * softplus * valid

    row = lax.broadcasted_iota(jnp.int32, (C, C), 0)
    col = lax.broadcasted_iota(jnp.int32, (C, C), 1)
    causal = row >= col
    strict = row > col
    eye = jnp.where(row == col, 1.0, 0.0).astype(F32)
    tril = jnp.where(causal, 1.0, 0.0).astype(BF16)

    g_hi = g_all.astype(BF16)
    r1 = g_all - g_hi.astype(F32)
    g_mid = r1.astype(BF16)
    g_lo = (r1 - g_mid.astype(F32)).astype(BF16)
    g3 = jnp.dot(tril, jnp.concatenate([g_hi, g_mid, g_lo], axis=1), preferred_element_type=F32)
    gcum = g3[:, 0:LANES] + g3[:, LANES:2 * LANES] + g3[:, 2 * LANES:3 * LANES]
    gcum_t = gcum.T

    scale = DN_DK ** -0.5
    nw = nw_ref[...]
    heads = range(DN_HEADS)

    def mm(a, b):
        return jnp.dot(a.astype(BF16), b.astype(BF16), preferred_element_type=F32)

    kn_b, lhs1, decay, rhs, qe_b, kdec_t, e_last = [], [], [], [], [], [], []
    for h in heads:
        qh = act_ref[:, h * DN_DK:(h + 1) * DN_DK]
        kh = act_ref[:, DN_W + h * DN_DK:DN_W + (h + 1) * DN_DK]
        vh = act_ref[:, 2 * DN_W + h * DN_DV:2 * DN_W + (h + 1) * DN_DV]
        qn = qh * (lax.rsqrt(jnp.sum(qh * qh, axis=-1, keepdims=True) + L2_EPS) * scale)
        kn = kh * lax.rsqrt(jnp.sum(kh * kh, axis=-1, keepdims=True) + L2_EPS)
        beta = beta_all[:, h:h + 1]
        gc = gcum[:, DN_HEADS + h:DN_HEADS + h + 1]
        gr = gcum_t[DN_HEADS + h:DN_HEADS + h + 1, :]
        g_last = gc[C - 1:C, :]
        e_g = jnp.exp(gc)
        kb = kn * beta
        kn_b.append(kn.astype(BF16))
        lhs1.append(jnp.concatenate([kb, qn], axis=0).astype(BF16))
        decay.append(jnp.exp(jnp.where(causal, gc - gr, NEG_BIG)))
        rhs.append(jnp.concatenate([vh * beta, kb * e_g], axis=1).astype(BF16))
        qe_b.append((qn * e_g).astype(BF16))
        kdec_t.append((kn * jnp.exp(g_last - gc)).T.astype(BF16))
        e_last.append(jnp.exp(g_last))

    kq = [lax.dot_general(lhs1[h], kn_b[h], (((1,), (1,)), ((), ())), preferred_element_type=F32)
          for h in heads]
    qk_b = [jnp.where(causal, kq[h][C:2 * C, :] * decay[h], 0.0).astype(BF16) for h in heads]
    b1 = [-jnp.where(strict, kq[h][0:C, :] * decay[h], 0.0) for h in heads]
    b2 = [mm(b1[h], b1[h]) for h in heads]
    b4 = [mm(b2[h], b2[h]) for h in heads]
    p1 = [(eye + b1[h]) + mm(eye + b1[h], b2[h]) for h in heads]
    b8 = [mm(b4[h], b4[h]) for h in heads]
    b16 = [mm(b8[h], b8[h]) for h in heads]
    p2 = [(eye + b4[h]) + mm(eye + b4[h], b8[h]) for h in heads]
    b32 = [mm(b16[h], b16[h]) for h in heads]
    p12 = [mm(p1[h], p2[h]) for h in heads]
    p3 = [(eye + b16[h]) + mm(eye + b16[h], b32[h]) for h in heads]
    inv = [mm(p12[h], p3[h]) for h in heads]
    sol = [mm(inv[h], rhs[h]) for h in heads]

    s_old = [s_ref[h] for h in heads]
    ws_lhs = [jnp.concatenate([sol[h][:, DN_DV:2 * DN_DV].astype(BF16), qe_b[h]], axis=0) for h in heads]
    ws = [mm(ws_lhs[h], s_old[h]) for h in heads]
    v_new = [(sol[h][:, 0:DN_DV] - ws[h][0:C, :]).astype(BF16) for h in heads]
    ov_lhs = [jnp.concatenate([qk_b[h], kdec_t[h]], axis=0) for h in heads]
    ov = [mm(ov_lhs[h], v_new[h]) for h in heads]
    for h in heads:
        s_ref[h] = s_old[h] * e_last[h] + ov[h][C:C + DN_DK, :]
        o = ws[h][C:2 * C, :] + ov[h][0:C, :]
        zh = z_ref[:, h * DN_DV:(h + 1) * DN_DV].astype(F32)
        o = o * lax.rsqrt(jnp.mean(o * o, axis=-1, keepdims=True) + RMS_EPS) * nw
        o_ref[:, h * DN_DV:(h + 1) * DN_DV] = (o * (zh * jax.nn.sigmoid(zh))).astype(o_ref.dtype)


def gated_deltanet(proj_x, ba_x, proj_m, ba_m, conv_w, a_log, dt_bias, norm_w, bsz, seq):
    C = DN_CHUNK
    n_chunks = seq // C
    w3 = 3 * DN_W
    hp = jnp.zeros((2, LANES), F32)
    hp = hp.at[0, DN_HEADS:2 * DN_HEADS].set(a_log.astype(F32))
    hp = hp.at[1, DN_HEADS:2 * DN_HEADS].set(dt_bias.astype(F32))

    def xrow(b, t):
        return b * n_chunks + jnp.maximum(t - 1, 0)

    return pl.pallas_call(
        _deltanet_kernel,
        grid=(bsz, n_chunks + 1),
        in_specs=[
            pl.BlockSpec((C, w3), lambda b, t: (xrow(b, t), 0)),
            pl.BlockSpec((C, DN_W), lambda b, t: (xrow(b, t), COL_DZ // DN_W)),
            pl.BlockSpec((C, LANES), lambda b, t: (xrow(b, t), 0)),
            pl.BlockSpec((N_META, w3), lambda b, t: (0, 0)),
            pl.BlockSpec((N_META, LANES), lambda b, t: (0, 0)),
            pl.BlockSpec((DN_CONV, w3), lambda b, t: (0, 0)),
            pl.BlockSpec((2, LANES), lambda b, t: (0, 0)),
            pl.BlockSpec((1, DN_DV), lambda b, t: (0, 0)),
        ],
        out_specs=pl.BlockSpec((C, DN_W), lambda b, t: (xrow(b, t), 0)),
        out_shape=jax.ShapeDtypeStruct((bsz * seq, DN_W), BF16),
        scratch_shapes=[
            pltpu.VMEM((DN_HIST + C, w3), BF16),
            pltpu.VMEM((C, w3), F32),
            pltpu.VMEM((C, LANES), F32),
            pltpu.VMEM((DN_HEADS, DN_DK, DN_DV), F32),
        ],
        compiler_params=_params("parallel", "arbitrary", vmem=VMEM_LIMIT),
        name="gated_deltanet",
    )(proj_x, proj_x, ba_x, proj_m, ba_m, conv_w.astype(F32), hp, norm_w.reshape(1, DN_DV).astype(F32))


def _diff_attn_kernel(q_ref, k_ref, v_ref, km_ref, vm_ref, qn_ref, kn_ref, lam_ref, sn_ref,
                      o_ref, kx_s, kmeta_s, *, tq, seq):
    qi = pl.program_id(2)

    def rms(x, w):
        return x * lax.rsqrt(jnp.mean(x * x, axis=-1, keepdims=True) + RMS_EPS) * w

    @pl.when(qi == 0)
    def _():
        for m in range(2):
            kx = k_ref[:, m * DA_DK:(m + 1) * DA_DK].astype(F32)
            kx_s[m] = rms(kx, kn_ref[...]).astype(BF16)
            kmx = km_ref[:, m * DA_DK:(m + 1) * DA_DK].astype(F32)
            kmeta_s[m] = rms(kmx, kn_ref[...]).astype(BF16)

    lam = (jnp.exp(jnp.sum(lam_ref[0:1, :] * lam_ref[1:2, :], axis=-1, keepdims=True))
           - jnp.exp(jnp.sum(lam_ref[2:3, :] * lam_ref[3:4, :], axis=-1, keepdims=True))
           + LAMBDA_INIT)
    row = lax.broadcasted_iota(jnp.int32, (tq, tq), 0)
    col = lax.broadcasted_iota(jnp.int32, (tq, tq), 1)
    diag_visible = col <= row
    v_m = vm_ref[...]
    scale = DA_DK ** -0.5 * math.log2(math.e)
    nt = (((1,), (1,)), ((), ()))

    def attend(n_prev):
        outs = []
        for m in range(2):
            qx = q_ref[:, m * DA_DK:(m + 1) * DA_DK].astype(F32)
            qn = (rms(qx, qn_ref[...]) * scale).astype(BF16)
            sd = lax.dot_general(qn, kx_s[m, n_prev:n_prev + tq, :], nt, preferred_element_type=F32)
            sd = jnp.where(diag_visible, sd, NEG_BIG)
            sm = lax.dot_general(qn, kmeta_s[m], nt, preferred_element_type=F32)
            mx = jnp.maximum(jnp.max(sd, axis=-1, keepdims=True), jnp.max(sm, axis=-1, keepdims=True))
            if n_prev:
                sp = lax.dot_general(qn, kx_s[m, 0:n_prev, :], nt, preferred_element_type=F32)
                mx = jnp.maximum(mx, jnp.max(sp, axis=-1, keepdims=True))
            pd = jnp.exp2(sd - mx)
            pm = jnp.exp2(sm - mx)
            den = jnp.sum(pd, axis=-1, keepdims=True) + jnp.sum(pm, axis=-1, keepdims=True)
            acc = (jnp.dot(pd.astype(BF16), v_ref[n_prev:n_prev + tq, :], preferred_element_type=F32)
                   + jnp.dot(pm.astype(BF16), v_m, preferred_element_type=F32))
            if n_prev:
                pp = jnp.exp2(sp - mx)
                den = den + jnp.sum(pp, axis=-1, keepdims=True)
                acc = acc + jnp.dot(pp.astype(BF16), v_ref[0:n_prev, :], preferred_element_type=F32)
            outs.append(acc / den)
        out = outs[0] - lam * outs[1]
        out = rms(out, sn_ref[...]) * (1.0 - LAMBDA_INIT)
        o_ref[...] = out.astype(o_ref.dtype)

    for blk in range(seq // tq):
        @pl.when(qi == blk)
        def _():
            attend(blk * tq)


def diff_attention(proj_x, proj_m, q_norm, k_norm, lam4, sub_norm, bsz, seq, tq=256):
    nq = seq // tq
    cq = COL_AQ // DA_DV
    ck = COL_AK // DA_DV
    cv = COL_AV // DA_DV
    return pl.pallas_call(
        functools.partial(_diff_attn_kernel, tq=tq, seq=seq),
        grid=(bsz, DA_HEADS, nq),
        in_specs=[
            pl.BlockSpec((tq, DA_DV), lambda b, h, i: (b * nq + i, cq + h)),
            pl.BlockSpec((seq, DA_DV), lambda b, h, i: (b, ck + h)),
            pl.BlockSpec((seq, DA_DV), lambda b, h, i: (b, cv + h)),
            pl.BlockSpec((N_META, DA_DV), lambda b, h, i: (0, h)),
            pl.BlockSpec((N_META, DA_DV), lambda b, h, i: (0, DA_HEADS + h)),
            pl.BlockSpec((1, DA_DK), lambda b, h, i: (0, 0)),
            pl.BlockSpec((1, DA_DK), lambda b, h, i: (0, 0)),
            pl.BlockSpec((4, DA_DK), lambda b, h, i: (0, 0)),
            pl.BlockSpec((1, DA_DV), lambda b, h, i: (0, 0)),
        ],
        out_specs=pl.BlockSpec((tq, DA_DV), lambda b, h, i: (b * nq + i, h)),
        out_shape=jax.ShapeDtypeStruct((bsz * seq, DA_HEADS * DA_DV), BF16),
        scratch_shapes=[pltpu.VMEM((2, seq, DA_DK), BF16),
                        pltpu.VMEM((2, N_META, DA_DK), BF16)],
        compiler_params=_params("parallel", "parallel", "arbitrary", vmem=VMEM_LIMIT),
        name="diff_attention",
    )(proj_x, proj_x, proj_x, proj_m, proj_m,
      q_norm.reshape(1, DA_DK).astype(F32), k_norm.reshape(1, DA_DK).astype(F32),
      lam4.astype(F32), sub_norm.reshape(1, DA_DV).astype(F32))


def _merge_kernel(ya_ref, yb_ref, wa_ref, wb_ref, ga_ref, gb_ref, o_ref):
    pa = jnp.dot(ya_ref[...], wa_ref[...].astype(BF16), preferred_element_type=F32)
    pb = jnp.dot(yb_ref[...], wb_ref[...].astype(BF16), preferred_element_type=F32)
    ga = jax.nn.sigmoid(ga_ref[...].astype(F32))
    gb = jax.nn.sigmoid(gb_ref[...].astype(F32))
    o_ref[...] = (ga * pa + gb * pb).astype(o_ref.dtype)


def branch_merge(y_a, y_b, w_a, w_b, proj_x, d_model, tm=1024, tn=512):
    m, ka = y_a.shape
    kb = y_b.shape[1]
    cga = COL_GA // tn
    cgb = (COL_GA + d_model) // tn
    return pl.pallas_call(
        _merge_kernel,
        grid=(m // tm, d_model // tn),
        in_specs=[
            pl.BlockSpec((tm, ka), lambda i, j: (i, 0)),
            pl.BlockSpec((tm, kb), lambda i, j: (i, 0)),
            pl.BlockSpec((ka, tn), lambda i, j: (0, j)),
            pl.BlockSpec((kb, tn), lambda i, j: (0, j)),
            pl.BlockSpec((tm, tn), lambda i, j: (i, cga + j)),
            pl.BlockSpec((tm, tn), lambda i, j: (i, cgb + j)),
        ],
        out_specs=pl.BlockSpec((tm, tn), lambda i, j: (i, j)),
        out_shape=jax.ShapeDtypeStruct((m, d_model), BF16),
        compiler_params=_params("parallel", "parallel", vmem=VMEM_LIMIT),
        name="branch_merge",
    )(y_a, y_b, w_a, w_b, proj_x, proj_x)


def _router_kernel(h_ref, nw_ref, whi_ref, wlo_ref, b_ref, idx_ref, gate_ref, up_ref):
    x = h_ref[...]
    u = x * lax.rsqrt(jnp.mean(x * x, axis=-1, keepdims=True) + RMS_EPS) * nw_ref[...]
    u_hi = u.astype(BF16)
    u_lo = (u - u_hi.astype(F32)).astype(BF16)
    half = u.shape[1] // 2
    bits = pltpu.bitcast(u_hi.astype(F32), jnp.uint32)
    up_ref[...] = bits[:, half:] | (bits[:, :half] >> 16)
    logits = (jnp.dot(u_hi, whi_ref[...], preferred_element_type=F32)
              + jnp.dot(u_lo, whi_ref[...], preferred_element_type=F32)
              + jnp.dot(u_hi, wlo_ref[...], preferred_element_type=F32)
              + b_ref[...])
    lane = lax.broadcasted_iota(jnp.int32, logits.shape, 1)
    lane_f = lane.astype(F32)
    vals = logits
    tops, idxs = [], []
    for _ in range(TOP_K):
        mx = jnp.max(vals, axis=-1, keepdims=True)
        ix = jnp.min(jnp.where(vals == mx, lane_f, float(LANES)), axis=-1, keepdims=True).astype(jnp.int32)
        tops.append(mx)
        idxs.append(ix)
        vals = jnp.where(lane == ix, -3.0e38, vals)
    exps = [jnp.exp(tv - tops[0]) for tv in tops]
    den = exps[0] + exps[1] + exps[2] + exps[3]
    idx_out = jnp.zeros(logits.shape, jnp.int32)
    gate_out = jnp.zeros(logits.shape, F32)
    for k in range(TOP_K):
        idx_out = jnp.where(lane == k, idxs[k], idx_out)
        gate_out = jnp.where(lane == k, exps[k] / den, gate_out)
    idx_ref[...] = idx_out
    gate_ref[...] = gate_out


def router(h2, ffn_norm, router_w, router_b, tm=256):
    n, d = h2.shape
    wpad = jnp.zeros((d, LANES), F32).at[:, :N_EXPERTS].set(router_w.astype(F32))
    w_hi = wpad.astype(BF16)
    w_lo = (wpad - w_hi.astype(F32)).astype(BF16)
    bias = jnp.full((1, LANES), NEG_BIG, F32).at[0, :N_EXPERTS].set(router_b.astype(F32))
    return pl.pallas_call(
        _router_kernel,
        grid=(n // tm,),
        in_specs=[
            pl.BlockSpec((tm, d), lambda i: (i, 0)),
            pl.BlockSpec((1, d), lambda i: (0, 0)),
            pl.BlockSpec((d, LANES), lambda i: (0, 0)),
            pl.BlockSpec((d, LANES), lambda i: (0, 0)),
            pl.BlockSpec((1, LANES), lambda i: (0, 0)),
        ],
        out_specs=[pl.BlockSpec((tm, LANES), lambda i: (i, 0)),
                   pl.BlockSpec((tm, LANES), lambda i: (i, 0)),
                   pl.BlockSpec((tm, d // 2), lambda i: (i, 0))],
        out_shape=[jax.ShapeDtypeStruct((n, LANES), jnp.int32),
                   jax.ShapeDtypeStruct((n, LANES), F32),
                   jax.ShapeDtypeStruct((n, d // 2), jnp.uint32)],
        compiler_params=_params("parallel", vmem=VMEM_LIMIT),
        name="router",
    )(h2, ffn_norm.reshape(1, d).astype(F32), w_hi, w_lo, bias)


def _expert_kernel(ie_ref, isb_ref, inb_ref, tok_ref, up_hbm, wgu_hbm, wd_hbm, bgu_ref, bd_ref,
                   ys_hbm, xbuf, act, ybuf, wgu_buf, wd_buf, wg_s, wu_s, wd_s,
                   sem_x, sem_y, sem_gu, sem_d, *, j1, j2, tn1, tn2, n_items):
    i = pl.program_id(0)
    nb = inb_ref[i]
    sb = isb_ref[i]
    ex = ie_ref[i]
    R = MOE_ROWS

    def gu_copies(expert, jt, slot):
        return [pltpu.make_async_copy(
            wgu_hbm.at[expert, :, pl.ds(pl.multiple_of(which * j1 * tn1 + jt * tn1, tn1), tn1)],
            wgu_buf.at[slot, which], sem_gu.at[slot]) for which in range(2)]

    def d_copy(expert, jd, slot):
        return pltpu.make_async_copy(
            wd_hbm.at[expert, :, pl.ds(pl.multiple_of(jd * tn2, tn2), tn2)],
            wd_buf.at[slot], sem_d.at[slot])

    rmax = xbuf.shape[0]
    rows_per_tile = rmax // j2
    last_tok = tok_ref.shape[0] - 1

    def row_gather(base, r):
        tok = tok_ref[jnp.minimum(base + r, last_tok)]
        return pltpu.make_async_copy(up_hbm.at[pl.ds(tok, 1), :], xbuf.at[pl.ds(r, 1), :], sem_x)

    def x_wait_all():
        for r in range(MOE_ITEM_BLOCKS):
            pltpu.make_async_copy(up_hbm.at[pl.ds(0, R), :], xbuf.at[pl.ds(r * R, R), :],
                                  sem_x).wait()

    def y_copy(slot, r, jd):
        return pltpu.make_async_copy(
            ybuf.at[slot, pl.ds(r * R, R), :],
            ys_hbm.at[pl.ds((sb + r) * R, R), pl.ds(pl.multiple_of(jd * tn2, tn2), tn2)],
            sem_y.at[slot])

    def for_blocks(fn):
        for r in range(MOE_ITEM_BLOCKS):
            @pl.when(r < nb)
            def _():
                fn(r)

    @pl.when(i == 0)
    def _():
        def start(r, c):
            row_gather(sb * R, r).start()
            return c
        lax.fori_loop(0, rmax, start, 0, unroll=8)
        for cp in gu_copies(ex, 0, 0):
            cp.start()

    prev_active = inb_ref[jnp.maximum(i - 1, 0)] > 0

    @pl.when((i == 0) | prev_active)
    def _():
        x_wait_all()

    def gate_up_block(jt, r0, rows, wg, wu):
        x32 = xbuf[pl.ds(r0, rows), :]
        x_lo = pltpu.bitcast(x32 << 16, F32).astype(BF16)
        x_hi = pltpu.bitcast(x32 & jnp.uint32(0xFFFF0000), F32).astype(BF16)
        x = jnp.concatenate([x_lo, x_hi], axis=1)
        g = jnp.dot(x, wg, preferred_element_type=F32) + bgu_ref[pl.ds(jt, 1), :]
        u = jnp.dot(x, wu, preferred_element_type=F32) + bgu_ref[pl.ds(j1 + jt, 1), :]
        gt = jnp.minimum(g, SWIGLU_LIMIT)
        up = jnp.clip(u, -SWIGLU_LIMIT, SWIGLU_LIMIT)
        a = gt * jax.nn.sigmoid(SWIGLU_ALPHA * gt) * (up + 1.0)
        act[jt, pl.ds(r0, rows), :] = a.astype(BF16)

    def down_block(jd, slot, r0, rows, wd):
        lhs = jnp.concatenate([act[c, pl.ds(r0, rows), :] for c in range(j1)], axis=1)
        ybuf[slot, pl.ds(r0, rows), :] = (jnp.dot(lhs, wd, preferred_element_type=F32)
                                          + bd_ref[pl.ds(jd, 1), :])

    def rest_blocks(block_fn):
        rest = nb - 1

        def pair(p, c):
            block_fn(pl.multiple_of(R + p * 2 * R, R), 2 * R)
            return c

        lax.fori_loop(0, rest // 2, pair, 0)

        @pl.when(rest % 2 == 1)
        def _():
            block_fn(pl.multiple_of((nb - 1) * R, R), R)

    nxt = jnp.minimum(i + 1, n_items - 1)
    next_base = isb_ref[nxt] * R

    @pl.when(nb > 0)
    def _():
        d_copy(ex, 0, 0).start()

        def gate_up_tile(jt, carry):
            slot = jt % 2

            @pl.when(jt + 1 < j1)
            def _():
                for cp in gu_copies(ex, jt + 1, 1 - slot):
                    cp.start()

            for cp in gu_copies(ex, jt, slot):
                cp.wait()
            wg = wgu_buf[slot, 0].astype(BF16)
            wu = wgu_buf[slot, 1].astype(BF16)
            wg_s[...] = wg
            wu_s[...] = wu
            gate_up_block(jt, 0, R, wg, wu)
            rest_blocks(lambda r0, rows: gate_up_block(jt, r0, rows, wg_s[...], wu_s[...]))
            return carry

        lax.fori_loop(0, j1, gate_up_tile, 0)

        @pl.when(i + 1 < n_items)
        def _():
            @pl.when(inb_ref[i + 1] > 0)
            def _():
                for cp in gu_copies(ie_ref[i + 1], 0, 0):
                    cp.start()

        def down_tile(jd, carry):
            slot = jd % 2

            @pl.when(jd + 1 < j2)
            def _():
                d_copy(ex, jd + 1, 1 - slot).start()

            @pl.when(jd >= 2)
            def _():
                for_blocks(lambda r: y_copy(slot, r, jd).wait())

            d_copy(ex, jd, slot).wait()
            wd = wd_buf[slot].astype(BF16)
            wd_s[...] = wd
            down_block(jd, slot, 0, R, wd)
            for rr in range(rows_per_tile):
                row_gather(next_base, jd * rows_per_tile + rr).start()
            rest_blocks(lambda r0, rows: down_block(jd, slot, r0, rows, wd_s[...]))
            for_blocks(lambda r: y_copy(slot, r, jd).start())
            return carry

        lax.fori_loop(0, j2, down_tile, 0)
        for_blocks(lambda r: y_copy(0, r, 0).wait())
        for_blocks(lambda r: y_copy(1, r, 0).wait())

        @pl.when(i == n_items - 1)
        def _():
            x_wait_all()


def expert_mlp(u_packed, row_tok, item_e, item_sb, item_nb, w_gate_up, b_gate_up, w_down, b_down,
               tn1=256, tn2=512):
    p = row_tok.shape[0]
    d = 2 * u_packed.shape[1]
    n_items = item_e.shape[0]
    de = w_down.shape[1]
    j1 = de // tn1
    j2 = d // tn2
    rmax = MOE_ITEM_BLOCKS * MOE_ROWS

    assert rmax % j2 == 0

    def bias_map(i, ie, isb, inb, tok):
        return (ie[i], 0, 0)

    return pl.pallas_call(
        functools.partial(_expert_kernel, j1=j1, j2=j2, tn1=tn1, tn2=tn2, n_items=n_items),
        grid_spec=pltpu.PrefetchScalarGridSpec(
            num_scalar_prefetch=4,
            grid=(n_items,),
            in_specs=[
                pl.BlockSpec(memory_space=pl.ANY),
                pl.BlockSpec(memory_space=pl.ANY),
                pl.BlockSpec(memory_space=pl.ANY),
                pl.BlockSpec((None, 2 * j1, tn1), bias_map),
                pl.BlockSpec((None, j2, tn2), bias_map),
            ],
            out_specs=pl.BlockSpec(memory_space=pl.ANY),
            scratch_shapes=[
                pltpu.VMEM((rmax, d // 2), jnp.uint32),
                pltpu.VMEM((j1, rmax, tn1), BF16),
                pltpu.VMEM((2, rmax, tn2), F32),
                pltpu.VMEM((2, 2, d, tn1), F32),
                pltpu.VMEM((2, de, tn2), F32),
                pltpu.VMEM((d, tn1), BF16),
                pltpu.VMEM((d, tn1), BF16),
                pltpu.VMEM((de, tn2), BF16),
                pltpu.SemaphoreType.DMA(()),
                pltpu.SemaphoreType.DMA((2,)),
                pltpu.SemaphoreType.DMA((2,)),
                pltpu.SemaphoreType.DMA((2,)),
            ],
        ),
        out_shape=jax.ShapeDtypeStruct((p, d), F32),
        compiler_params=_params("arbitrary", vmem=VMEM_LIMIT),
        name="expert_mlp",
    )(item_e, item_sb, item_nb, row_tok, u_packed, w_gate_up, w_down,
      b_gate_up.reshape(N_EXPERTS, 2 * j1, tn1), b_down.reshape(N_EXPERTS, j2, tn2))


def _combine_kernel(dest_ref, h_ref, g_ref, ys_hbm, o_ref, buf, sem, *, tt, n_tok, n_steps):
    i = pl.program_id(0)

    def issue(blk, slot):
        for k in range(TOP_K):
            def start(t, c):
                src = dest_ref[k * n_tok + blk * tt + t]
                pltpu.make_async_copy(ys_hbm.at[pl.ds(src, 1), :], buf.at[slot, k, pl.ds(t, 1), :],
                                      sem.at[slot]).start()
                return c
            lax.fori_loop(0, tt, start, 0, unroll=8)

    @pl.when(i == 0)
    def _():
        issue(0, 0)

    slot = i % 2

    @pl.when(i + 1 < n_steps)
    def _():
        issue(i + 1, 1 - slot)

    for k in range(TOP_K):
        pltpu.make_async_copy(ys_hbm.at[pl.ds(0, tt), :], buf.at[slot, k], sem.at[slot]).wait()
    acc = h_ref[...]
    g = g_ref[...]
    for k in range(TOP_K):
        acc = acc + g[:, k:k + 1] * buf[slot, k]
    o_ref[...] = acc


def combine(h2, gates, ys, dest_km, tt=64):
    n, d = h2.shape
    return pl.pallas_call(
        functools.partial(_combine_kernel, tt=tt, n_tok=n, n_steps=n // tt),
        grid_spec=pltpu.PrefetchScalarGridSpec(
            num_scalar_prefetch=1,
            grid=(n // tt,),
            in_specs=[pl.BlockSpec((tt, d), lambda i, dst: (i, 0)),
                      pl.BlockSpec((tt, LANES), lambda i, dst: (i, 0)),
                      pl.BlockSpec(memory_space=pl.ANY)],
            out_specs=pl.BlockSpec((tt, d), lambda i, dst: (i, 0)),
            scratch_shapes=[pltpu.VMEM((2, TOP_K, tt, d), F32), pltpu.SemaphoreType.DMA((2,))],
        ),
        out_shape=jax.ShapeDtypeStruct((n, d), F32),
        compiler_params=_params("arbitrary", vmem=VMEM_LIMIT),
        name="moe_combine",
    )(dest_km, h2, gates, ys)


def routing_tables(top_idx, n_tok):
    a = n_tok * TOP_K
    e_flat = top_idx.reshape(-1)
    onehot = (e_flat[:, None] == jnp.arange(N_EXPERTS, dtype=jnp.int32)[None, :]).astype(jnp.int32)
    csum = jnp.cumsum(onehot, axis=0)
    rank = jnp.sum(csum * onehot, axis=1) - 1
    counts = csum[-1]
    nblk = (counts + MOE_ROWS - 1) // MOE_ROWS
    blk_end = jnp.cumsum(nblk)
    blk_start = blk_end - nblk
    dest = blk_start[e_flat] * MOE_ROWS + rank
    n_blocks = -(-(a + N_EXPERTS * (MOE_ROWS - 1)) // MOE_ROWS)
    p = n_blocks * MOE_ROWS
    t_flat = jnp.arange(a, dtype=jnp.int32) // TOP_K
    row_tok = jnp.zeros((p,), jnp.int32).at[dest].set(t_flat)

    n_items = N_EXPERTS + n_blocks // MOE_ITEM_BLOCKS
    items_per_e = (nblk + MOE_ITEM_BLOCKS - 1) // MOE_ITEM_BLOCKS
    item_end = jnp.cumsum(items_per_e)
    item_start = item_end - items_per_e
    total_items = item_end[-1]
    ii = jnp.arange(n_items, dtype=jnp.int32)
    ic = jnp.minimum(ii, total_items - 1)
    e_i = jnp.minimum(jnp.searchsorted(item_end, ic, side='right'), N_EXPERTS - 1).astype(jnp.int32)
    local = ic - item_start[e_i]
    item_sb = (blk_start[e_i] + local * MOE_ITEM_BLOCKS).astype(jnp.int32)
    item_nb = jnp.clip(nblk[e_i] - local * MOE_ITEM_BLOCKS, 0, MOE_ITEM_BLOCKS)
    item_nb = jnp.where(ii < total_items, item_nb, 0).astype(jnp.int32)
    dest_km = dest.astype(jnp.int32).reshape(n_tok, TOP_K).T.reshape(-1)
    return dest_km, row_tok, e_i, item_sb, item_nb


def kernel(x, meta_tokens, mix_norm, w_in, dn_conv, dn_a_log, dn_dt_bias, dn_out_norm, da_q_norm, da_k_norm, da_lam_q1, da_lam_k1, da_lam_q2, da_lam_k2, da_sub_norm, w_branch_a, w_branch_b, w_out, ffn_norm, router_w, router_b, w_gate_up, b_gate_up, w_down, b_down):
    bsz, seq, d = x.shape
    n_tok = bsz * seq
    x2 = x.reshape(n_tok, d)
    layer = 0

    w_bf, w_da = repack_w_in(w_in[layer], COL_BA + LANES, COL_BA + 2 * DN_HEADS)
    tn = 1024

    u_x = rms_cast(x2, mix_norm[layer], 256)
    u_m = rms_cast(meta_tokens.astype(F32), mix_norm[layer], N_META)
    proj_dn = matmul_fullk(u_x, w_bf, BF16, 1024, tn, name="in_proj_dn", n_blocks=COL_BA // tn)
    proj_da = matmul_fullk(u_x, w_da, BF16, 1024, tn, name="in_proj_da")
    ba_x = matmul_fullk(u_x, w_bf, F32, 1024, LANES, name="in_proj_ba",
                        n_offset=COL_BA // LANES, n_blocks=1)
    proj_dn_m = matmul_fullk(u_m, w_bf, BF16, N_META, tn, name="in_proj_dn_meta", n_blocks=COL_DZ // tn)
    proj_kv_m = matmul_fullk(u_m, w_da, BF16, N_META, tn, name="in_proj_kv_meta",
                             n_offset=COL_AK // tn, n_blocks=(COL_GA - COL_AK) // tn)
    ba_m = matmul_fullk(u_m, w_bf, F32, N_META, LANES, name="in_proj_ba_meta",
                        n_offset=COL_BA // LANES, n_blocks=1)

    y_a = gated_deltanet(proj_dn, ba_x, proj_dn_m, ba_m, dn_conv[layer], dn_a_log[layer],
                         dn_dt_bias[layer], dn_out_norm[layer], bsz, seq)
    lam4 = jnp.stack([da_lam_q1[layer], da_lam_k1[layer], da_lam_q2[layer], da_lam_k2[layer]])
    y_b = diff_attention(proj_da, proj_kv_m, da_q_norm[layer], da_k_norm[layer], lam4,
                         da_sub_norm[layer], bsz, seq)

    merged = branch_merge(y_a, y_b, w_branch_a[layer], w_branch_b[layer], proj_da, d)
    h2 = matmul_fullk(merged, w_out[layer], F32, 1024, 512, res=x2, name="out_proj")

    top_idx, gates, u_packed = router(h2, ffn_norm[layer], router_w[layer], router_b[layer])
    dest, row_tok, item_e, item_sb, item_nb = routing_tables(top_idx[:, :TOP_K], n_tok)
    ys = expert_mlp(u_packed, row_tok, item_e, item_sb, item_nb, w_gate_up[layer], b_gate_up[layer],
                    w_down[layer], b_down[layer])
    out = combine(h2, gates, ys, dest)
    return out.reshape(bsz, seq, d)
```

```python
import functools
import math

import jax
import jax.numpy as jnp
from jax import lax
from jax.experimental import pallas as pl
from jax.experimental.pallas import tpu as pltpu

F32 = jnp.float32
BF16 = jnp.bfloat16

N_META = 16
RMS_EPS = 1e-6
L2_EPS = 1e-6

DN_HEADS = 16
DN_DK = 128
DN_DV = 128
DN_CONV = 4
DN_CHUNK = 64
DN_W = DN_HEADS * DN_DK
DN_HIST = 16
DN_CONV_COLS = 1024

DA_HEADS = 8
DA_DK = 128
DA_DV = 256
LAMBDA_INIT = 0.8 - 0.6 * math.exp(-0.3 * 0)

N_EXPERTS = 32
TOP_K = 4
D_EXPERT = 1536
SWIGLU_LIMIT = 7.0
SWIGLU_ALPHA = 1.702

LANES = 128
MOE_ROWS = 256
MOE_ITEM_BLOCKS = 5
NEG_BIG = -1e30
VMEM_LIMIT = 56 * 1024 * 1024

COL_DZ = 3 * DN_W
COL_BA = 4 * DN_W
COL_AQ = 0
COL_AK = COL_AQ + DA_HEADS * 2 * DA_DK
COL_AV = COL_AK + DA_HEADS * 2 * DA_DK
COL_GA = COL_AV + DA_HEADS * DA_DV


def _params(*sem, vmem=None):
    return pltpu.CompilerParams(dimension_semantics=sem, vmem_limit_bytes=vmem)


def _rms_cast_kernel(x_ref, w_ref, o_ref):
    x = x_ref[...]
    y = x * lax.rsqrt(jnp.mean(x * x, axis=-1, keepdims=True) + RMS_EPS) * w_ref[...]
    o_ref[...] = y.astype(o_ref.dtype)


def rms_cast(x, w, tm):
    m, d = x.shape
    return pl.pallas_call(
        _rms_cast_kernel,
        grid=(m // tm,),
        in_specs=[pl.BlockSpec((tm, d), lambda i: (i, 0)),
                  pl.BlockSpec((1, d), lambda i: (0, 0))],
        out_specs=pl.BlockSpec((tm, d), lambda i: (i, 0)),
        out_shape=jax.ShapeDtypeStruct((m, d), BF16),
        compiler_params=_params("parallel"),
        name="rms_cast",
    )(x, w.reshape(1, d))


def _repack_kernel(w_ref, dn_ref, da_ref, *, dn_cols, da_start):
    dn_ref[...] = w_ref[:, 0:dn_cols].astype(BF16)
    da_ref[...] = w_ref[:, da_start:].astype(BF16)


def repack_w_in(w_all, layer, dn_cols, da_start, tr=128):
    _, rows, width = w_all.shape
    w = w_all
    return pl.pallas_call(
        functools.partial(_repack_kernel, dn_cols=dn_cols, da_start=da_start),
        grid=(rows // tr,),
        in_specs=[pl.BlockSpec((None, tr, width), lambda i: (layer, i, 0))],
        out_specs=[pl.BlockSpec((tr, dn_cols), lambda i: (i, 0)),
                   pl.BlockSpec((tr, width - da_start), lambda i: (i, 0))],
        out_shape=[jax.ShapeDtypeStruct((rows, dn_cols), BF16),
                   jax.ShapeDtypeStruct((rows, width - da_start), BF16)],
        compiler_params=_params("parallel", vmem=VMEM_LIMIT),
        name="repack_w_in",
    )(w)


def _mm_fullk_kernel(a_ref, b_ref, *rest, has_res):
    acc = jnp.dot(a_ref[...], b_ref[...].astype(BF16), preferred_element_type=F32)
    if has_res:
        r_ref, o_ref = rest
        acc = acc + r_ref[...]
    else:
        o_ref, = rest
    o_ref[...] = acc.astype(o_ref.dtype)


def matmul_fullk(a, b, out_dtype, tm, tn, res=None, name="matmul", n_offset=0, n_blocks=None):
    m, kd = a.shape
    if n_blocks is None:
        n_blocks = b.shape[1] // tn
    in_specs = [pl.BlockSpec((tm, kd), lambda i, j: (i, 0)),
                pl.BlockSpec((kd, tn), lambda i, j: (0, n_offset + j))]
    args = [a, b]
    if res is not None:
        in_specs.append(pl.BlockSpec((tm, tn), lambda i, j: (i, j)))
        args.append(res)
    return pl.pallas_call(
        functools.partial(_mm_fullk_kernel, has_res=res is not None),
        grid=(m // tm, n_blocks),
        in_specs=in_specs,
        out_specs=pl.BlockSpec((tm, tn), lambda i, j: (i, j)),
        out_shape=jax.ShapeDtypeStruct((m, n_blocks * tn), out_dtype),
        compiler_params=_params("parallel", "parallel", vmem=VMEM_LIMIT),
        name=name,
    )(*args)


def _deltanet_kernel(qkv_ref, z_ref, ba_ref, qkvm_ref, bam_ref, cw_ref, hp_ref, nw_ref,
                     o_ref, full_ref, act_ref, ba_s, s_ref):
    t = pl.program_id(1)
    C = DN_CHUNK
    HIST = DN_HIST
    n_pad = C - N_META

    @pl.when(t == 0)
    def _():
        s_ref[...] = jnp.zeros_like(s_ref)
        full_ref[0:HIST + n_pad, :] = jnp.zeros((HIST + n_pad, 3 * DN_W), BF16)
        full_ref[HIST + n_pad:HIST + C, :] = qkvm_ref[...]
        ba_s[0:n_pad, :] = jnp.zeros((n_pad, LANES), F32)
        ba_s[n_pad:C, :] = bam_ref[...]

    @pl.when(t > 0)
    def _():
        full_ref[HIST:HIST + C, :] = qkv_ref[...]
        ba_s[...] = ba_ref[...]

    n_shift = DN_CONV - 1
    sr = lax.broadcasted_iota(jnp.int32, (n_shift * C, HIST + C), 0)
    sc = lax.broadcasted_iota(jnp.int32, (n_shift * C, HIST + C), 1)
    shift = jnp.where(sc == HIST - n_shift + (sr % C) + (sr // C), 1.0, 0.0).astype(BF16)
    for c0 in range(0, 3 * DN_W, DN_CONV_COLS):
        cs = slice(c0, c0 + DN_CONV_COLS)
        delayed = jnp.dot(shift, full_ref[:, cs], preferred_element_type=F32)
        conv = cw_ref[n_shift:n_shift + 1, cs] * full_ref[HIST:HIST + C, cs].astype(F32)
        for j in range(n_shift):
            conv = conv + cw_ref[j:j + 1, cs] * delayed[j * C:(j + 1) * C, :]
        act_ref[:, cs] = conv * jax.nn.sigmoid(conv)
    full_ref[0:HIST, :] = full_ref[C:C + HIST, :]

    row1 = lax.broadcasted_iota(jnp.int32, (C, 1), 0)
    valid = jnp.where((t > 0) | (row1 >= n_pad), 1.0, 0.0).astype(F32)
    ba = ba_s[...]
    beta_all = jax.nn.sigmoid(ba) * valid
    xg = ba + hp_ref[1:2, :]
    softplus = jnp.maximum(xg, 0.0) + jnp.log1p(jnp.exp(-jnp.abs(xg)))
    g_all = -jnp.exp(hp_ref[0:1, :]) * softplus * valid

    row = lax.broadcasted_iota(jnp.int32, (C, C), 0)
    col = lax.broadcasted_iota(jnp.int32, (C, C), 1)
    causal = row >= col
    strict = row > col
    eye = jnp.where(row == col, 1.0, 0.0).astype(F32)
    tril = jnp.where(causal, 1.0, 0.0).astype(BF16)

    g_hi = g_all.astype(BF16)
    r1 = g_all - g_hi.astype(F32)
    g_mid = r1.astype(BF16)
    g_lo = (r1 - g_mid.astype(F32)).astype(BF16)
    g3 = jnp.dot(tril, jnp.concatenate([g_hi, g_mid, g_lo], axis=1), preferred_element_type=F32)
    gcum = g3[:, 0:LANES] + g3[:, LANES:2 * LANES] + g3[:, 2 * LANES:3 * LANES]
    gcum_t = gcum.T

    scale = DN_DK ** -0.5
    nw = nw_ref[...]
    heads = range(DN_HEADS)

    def mm(a, b):
        return jnp.dot(a.astype(BF16), b.astype(BF16), preferred_element_type=F32)

    kn_b, lhs1, decay, rhs, qe_b, kdec_t, e_last = [], [], [], [], [], [], []
    for h in heads:
        qh = act_ref[:, h * DN_DK:(h + 1) * DN_DK]
        kh = act_ref[:, DN_W + h * DN_DK:DN_W + (h + 1) * DN_DK]
        vh = act_ref[:, 2 * DN_W + h * DN_DV:2 * DN_W + (h + 1) * DN_DV]
        qn = qh * (lax.rsqrt(jnp.sum(qh * qh, axis=-1, keepdims=True) + L2_EPS) * scale)
        kn = kh * lax.rsqrt(jnp.sum(kh * kh, axis=-1, keepdims=True) + L2_EPS)
        beta = beta_all[:, h:h + 1]
        gc = gcum[:, DN_HEADS + h:DN_HEADS + h + 1]
        gr = gcum_t[DN_HEADS + h:DN_HEADS + h + 1, :]
        g_last = gc[C - 1:C, :]
        e_g = jnp.exp(gc)
        kb = kn * beta
        kn_b.append(kn.astype(BF16))
        lhs1.append(jnp.concatenate([kb, qn], axis=0).astype(BF16))
        decay.append(jnp.exp(jnp.where(causal, gc - gr, NEG_BIG)))
        rhs.append(jnp.concatenate([vh * beta, kb * e_g], axis=1).astype(BF16))
        qe_b.append((qn * e_g).astype(BF16))
        kdec_t.append((kn * jnp.exp(g_last - gc)).T.astype(BF16))
        e_last.append(jnp.exp(g_last))

    kq = [lax.dot_general(lhs1[h], kn_b[h], (((1,), (1,)), ((), ())), preferred_element_type=F32)
          for h in heads]
    qk_b = [jnp.where(causal, kq[h][C:2 * C, :] * decay[h], 0.0).astype(BF16) for h in heads]
    b1 = [-jnp.where(strict, kq[h][0:C, :] * decay[h], 0.0) for h in heads]
    b2 = [mm(b1[h], b1[h]) for h in heads]
    b4 = [mm(b2[h], b2[h]) for h in heads]
    p1 = [(eye + b1[h]) + mm(eye + b1[h], b2[h]) for h in heads]
    b8 = [mm(b4[h], b4[h]) for h in heads]
    b16 = [mm(b8[h], b8[h]) for h in heads]
    p2 = [(eye + b4[h]) + mm(eye + b4[h], b8[h]) for h in heads]
    b32 = [mm(b16[h], b16[h]) for h in heads]
    p12 = [mm(p1[h], p2[h]) for h in heads]
    p3 = [(eye + b16[h]) + mm(eye + b16[h], b32[h]) for h in heads]
    inv = [mm(p12[h], p3[h]) for h in heads]
    sol = [mm(inv[h], rhs[h]) for h in heads]

    s_old = [s_ref[h] for h in heads]
    ws_lhs = [jnp.concatenate([sol[h][:, DN_DV:2 * DN_DV].astype(BF16), qe_b[h]], axis=0) for h in heads]
    ws = [mm(ws_lhs[h], s_old[h]) for h in heads]
    v_new = [(sol[h][:, 0:DN_DV] - ws[h][0:C, :]).astype(BF16) for h in heads]
    ov_lhs = [jnp.concatenate([qk_b[h], kdec_t[h]], axis=0) for h in heads]
    ov = [mm(ov_lhs[h], v_new[h]) for h in heads]
    for h in heads:
        s_ref[h] = s_old[h] * e_last[h] + ov[h][C:C + DN_DK, :]
        o = ws[h][C:2 * C, :] + ov[h][0:C, :]
        zh = z_ref[:, h * DN_DV:(h + 1) * DN_DV].astype(F32)
        o = o * lax.rsqrt(jnp.mean(o * o, axis=-1, keepdims=True) + RMS_EPS) * nw
        o_ref[:, h * DN_DV:(h + 1) * DN_DV] = (o * (zh * jax.nn.sigmoid(zh))).astype(o_ref.dtype)


def gated_deltanet(proj_x, ba_x, proj_m, ba_m, conv_w, a_log, dt_bias, norm_w, bsz, seq):
    C = DN_CHUNK
    n_chunks = seq // C
    w3 = 3 * DN_W
    hp = jnp.zeros((2, LANES), F32)
    hp = hp.at[0, DN_HEADS:2 * DN_HEADS].set(a_log.astype(F32))
    hp = hp.at[1, DN_HEADS:2 * DN_HEADS].set(dt_bias.astype(F32))

    def xrow(b, t):
        return b * n_chunks + jnp.maximum(t - 1, 0)

    return pl.pallas_call(
        _deltanet_kernel,
        grid=(bsz, n_chunks + 1),
        in_specs=[
            pl.BlockSpec((C, w3), lambda b, t: (xrow(b, t), 0)),
            pl.BlockSpec((C, DN_W), lambda b, t: (xrow(b, t), COL_DZ // DN_W)),
            pl.BlockSpec((C, LANES), lambda b, t: (xrow(b, t), 0)),
            pl.BlockSpec((N_META, w3), lambda b, t: (0, 0)),
            pl.BlockSpec((N_META, LANES), lambda b, t: (0, 0)),
            pl.BlockSpec((DN_CONV, w3), lambda b, t: (0, 0)),
            pl.BlockSpec((2, LANES), lambda b, t: (0, 0)),
            pl.BlockSpec((1, DN_DV), lambda b, t: (0, 0)),
        ],
        out_specs=pl.BlockSpec((C, DN_W), lambda b, t: (xrow(b, t), 0)),
        out_shape=jax.ShapeDtypeStruct((bsz * seq, DN_W), BF16),
        scratch_shapes=[
            pltpu.VMEM((DN_HIST + C, w3), BF16),
            pltpu.VMEM((C, w3), F32),
            pltpu.VMEM((C, LANES), F32),
            pltpu.VMEM((DN_HEADS, DN_DK, DN_DV), F32),
        ],
        compiler_params=_params("parallel", "arbitrary", vmem=VMEM_LIMIT),
        name="gated_deltanet",
    )(proj_x, proj_x, ba_x, proj_m, ba_m, conv_w.astype(F32), hp, norm_w.reshape(1, DN_DV).astype(F32))


def _diff_attn_kernel(q_ref, k_ref, v_ref, km_ref, vm_ref, qn_ref, kn_ref, lam_ref, sn_ref,
                      o_ref, kx_s, kmeta_s, *, tq, seq):
    qi = pl.program_id(2)

    def rms(x, w):
        return x * lax.rsqrt(jnp.mean(x * x, axis=-1, keepdims=True) + RMS_EPS) * w

    @pl.when(qi == 0)
    def _():
        for m in range(2):
            kx = k_ref[:, m * DA_DK:(m + 1) * DA_DK].astype(F32)
            kx_s[m] = rms(kx, kn_ref[...]).astype(BF16)
            kmx = km_ref[:, m * DA_DK:(m + 1) * DA_DK].astype(F32)
            kmeta_s[m] = rms(kmx, kn_ref[...]).astype(BF16)

    lam = (jnp.exp(jnp.sum(lam_ref[0:1, :] * lam_ref[1:2, :], axis=-1, keepdims=True))
           - jnp.exp(jnp.sum(lam_ref[2:3, :] * lam_ref[3:4, :], axis=-1, keepdims=True))
           + LAMBDA_INIT)
    row = lax.broadcasted_iota(jnp.int32, (tq, tq), 0)
    col = lax.broadcasted_iota(jnp.int32, (tq, tq), 1)
    diag_visible = col <= row
    v_m = vm_ref[...]
    scale = DA_DK ** -0.5 * math.log2(math.e)
    nt = (((1,), (1,)), ((), ()))

    def attend(n_prev):
        maps = range(2)
        qn = [(rms(q_ref[:, m * DA_DK:(m + 1) * DA_DK].astype(F32), qn_ref[...]) * scale).astype(BF16)
              for m in maps]
        sd = [jnp.where(diag_visible,
                        lax.dot_general(qn[m], kx_s[m, n_prev:n_prev + tq, :], nt,
                                        preferred_element_type=F32), NEG_BIG) for m in maps]
        sm = [lax.dot_general(qn[m], kmeta_s[m], nt, preferred_element_type=F32) for m in maps]
        mx = [jnp.maximum(jnp.max(sd[m], axis=-1, keepdims=True),
                          jnp.max(sm[m], axis=-1, keepdims=True)) for m in maps]
        if n_prev:
            sp = [lax.dot_general(qn[m], kx_s[m, 0:n_prev, :], nt, preferred_element_type=F32)
                  for m in maps]
            mx = [jnp.maximum(mx[m], jnp.max(sp[m], axis=-1, keepdims=True)) for m in maps]
        pd = [jnp.exp2(sd[m] - mx[m]) for m in maps]
        pm = [jnp.exp2(sm[m] - mx[m]) for m in maps]
        den = [jnp.sum(pd[m], axis=-1, keepdims=True) + jnp.sum(pm[m], axis=-1, keepdims=True)
               for m in maps]
        acc = [jnp.dot(pd[m].astype(BF16), v_ref[n_prev:n_prev + tq, :], preferred_element_type=F32)
               + jnp.dot(pm[m].astype(BF16), v_m, preferred_element_type=F32) for m in maps]
        if n_prev:
            pp = [jnp.exp2(sp[m] - mx[m]) for m in maps]
            den = [den[m] + jnp.sum(pp[m], axis=-1, keepdims=True) for m in maps]
            acc = [acc[m] + jnp.dot(pp[m].astype(BF16), v_ref[0:n_prev, :], preferred_element_type=F32)
                   for m in maps]
        out = acc[0] / den[0] - lam * (acc[1] / den[1])
        out = rms(out, sn_ref[...]) * (1.0 - LAMBDA_INIT)
        o_ref[...] = out.astype(o_ref.dtype)

    for blk in range(seq // tq):
        @pl.when(qi == blk)
        def _():
            attend(blk * tq)


def diff_attention(proj_x, proj_m, q_norm, k_norm, lam4, sub_norm, bsz, seq, tq=256):
    nq = seq // tq
    cq = COL_AQ // DA_DV
    ck = COL_AK // DA_DV
    cv = COL_AV // DA_DV
    return pl.pallas_call(
        functools.partial(_diff_attn_kernel, tq=tq, seq=seq),
        grid=(bsz, DA_HEADS, nq),
        in_specs=[
            pl.BlockSpec((tq, DA_DV), lambda b, h, i: (b * nq + i, cq + h)),
            pl.BlockSpec((seq, DA_DV), lambda b, h, i: (b, ck + h)),
            pl.BlockSpec((seq, DA_DV), lambda b, h, i: (b, cv + h)),
            pl.BlockSpec((N_META, DA_DV), lambda b, h, i: (0, h)),
            pl.BlockSpec((N_META, DA_DV), lambda b, h, i: (0, DA_HEADS + h)),
            pl.BlockSpec((1, DA_DK), lambda b, h, i: (0, 0)),
            pl.BlockSpec((1, DA_DK), lambda b, h, i: (0, 0)),
            pl.BlockSpec((4, DA_DK), lambda b, h, i: (0, 0)),
            pl.BlockSpec((1, DA_DV), lambda b, h, i: (0, 0)),
        ],
        out_specs=pl.BlockSpec((tq, DA_DV), lambda b, h, i: (b * nq + i, h)),
        out_shape=jax.ShapeDtypeStruct((bsz * seq, DA_HEADS * DA_DV), BF16),
        scratch_shapes=[pltpu.VMEM((2, seq, DA_DK), BF16),
                        pltpu.VMEM((2, N_META, DA_DK), BF16)],
        compiler_params=_params("parallel", "parallel", "arbitrary", vmem=VMEM_LIMIT),
        name="diff_attention",
    )(proj_x, proj_x, proj_x, proj_m, proj_m,
      q_norm.reshape(1, DA_DK).astype(F32), k_norm.reshape(1, DA_DK).astype(F32),
      lam4.astype(F32), sub_norm.reshape(1, DA_DV).astype(F32))


def _merge_kernel(ya_ref, yb_ref, wa_ref, wb_ref, ga_ref, gb_ref, o_ref):
    pa = jnp.dot(ya_ref[...], wa_ref[...].astype(BF16), preferred_element_type=F32)
    pb = jnp.dot(yb_ref[...], wb_ref[...].astype(BF16), preferred_element_type=F32)
    ga = jax.nn.sigmoid(ga_ref[...].astype(F32))
    gb = jax.nn.sigmoid(gb_ref[...].astype(F32))
    o_ref[...] = (ga * pa + gb * pb).astype(o_ref.dtype)


def branch_merge(y_a, y_b, w_a, w_b, proj_x, d_model, tm=1024, tn=512):
    m, ka = y_a.shape
    kb = y_b.shape[1]
    cga = COL_GA // tn
    cgb = (COL_GA + d_model) // tn
    return pl.pallas_call(
        _merge_kernel,
        grid=(m // tm, d_model // tn),
        in_specs=[
            pl.BlockSpec((tm, ka), lambda i, j: (i, 0)),
            pl.BlockSpec((tm, kb), lambda i, j: (i, 0)),
            pl.BlockSpec((ka, tn), lambda i, j: (0, j)),
            pl.BlockSpec((kb, tn), lambda i, j: (0, j)),
            pl.BlockSpec((tm, tn), lambda i, j: (i, cga + j)),
            pl.BlockSpec((tm, tn), lambda i, j: (i, cgb + j)),
        ],
        out_specs=pl.BlockSpec((tm, tn), lambda i, j: (i, j)),
        out_shape=jax.ShapeDtypeStruct((m, d_model), BF16),
        compiler_params=_params("parallel", "parallel", vmem=VMEM_LIMIT),
        name="branch_merge",
    )(y_a, y_b, w_a, w_b, proj_x, proj_x)


def _router_kernel(h_ref, nw_ref, whi_ref, wlo_ref, b_ref, idx_ref, gate_ref, up_ref):
    x = h_ref[...]
    u = x * lax.rsqrt(jnp.mean(x * x, axis=-1, keepdims=True) + RMS_EPS) * nw_ref[...]
    u_hi = u.astype(BF16)
    u_lo = (u - u_hi.astype(F32)).astype(BF16)
    half = u.shape[1] // 2
    bits = pltpu.bitcast(u_hi.astype(F32), jnp.uint32)
    up_ref[...] = bits[:, half:] | (bits[:, :half] >> 16)
    logits = (jnp.dot(u_hi, whi_ref[...], preferred_element_type=F32)
              + jnp.dot(u_lo, whi_ref[...], preferred_element_type=F32)
              + jnp.dot(u_hi, wlo_ref[...], preferred_element_type=F32)
              + b_ref[...])
    lane = lax.broadcasted_iota(jnp.int32, logits.shape, 1)
    lane_f = lane.astype(F32)
    vals = logits
    tops, idxs = [], []
    for _ in range(TOP_K):
        mx = jnp.max(vals, axis=-1, keepdims=True)
        ix = jnp.min(jnp.where(vals == mx, lane_f, float(LANES)), axis=-1, keepdims=True).astype(jnp.int32)
        tops.append(mx)
        idxs.append(ix)
        vals = jnp.where(lane == ix, -3.0e38, vals)
    exps = [jnp.exp(tv - tops[0]) for tv in tops]
    den = exps[0] + exps[1] + exps[2] + exps[3]
    idx_out = jnp.zeros(logits.shape, jnp.int32)
    gate_out = jnp.zeros(logits.shape, F32)
    for k in range(TOP_K):
        idx_out = jnp.where(lane == k, idxs[k], idx_out)
        gate_out = jnp.where(lane == k, exps[k] / den, gate_out)
    idx_ref[...] = idx_out
    gate_ref[...] = gate_out


def router(h2, ffn_norm, router_w, router_b, tm=256):
    n, d = h2.shape
    wpad = jnp.zeros((d, LANES), F32).at[:, :N_EXPERTS].set(router_w.astype(F32))
    w_hi = wpad.astype(BF16)
    w_lo = (wpad - w_hi.astype(F32)).astype(BF16)
    bias = jnp.full((1, LANES), NEG_BIG, F32).at[0, :N_EXPERTS].set(router_b.astype(F32))
    return pl.pallas_call(
        _router_kernel,
        grid=(n // tm,),
        in_specs=[
            pl.BlockSpec((tm, d), lambda i: (i, 0)),
            pl.BlockSpec((1, d), lambda i: (0, 0)),
            pl.BlockSpec((d, LANES), lambda i: (0, 0)),
            pl.BlockSpec((d, LANES), lambda i: (0, 0)),
            pl.BlockSpec((1, LANES), lambda i: (0, 0)),
        ],
        out_specs=[pl.BlockSpec((tm, LANES), lambda i: (i, 0)),
                   pl.BlockSpec((tm, LANES), lambda i: (i, 0)),
                   pl.BlockSpec((tm, d // 2), lambda i: (i, 0))],
        out_shape=[jax.ShapeDtypeStruct((n, LANES), jnp.int32),
                   jax.ShapeDtypeStruct((n, LANES), F32),
                   jax.ShapeDtypeStruct((n, d // 2), jnp.uint32)],
        compiler_params=_params("parallel", vmem=VMEM_LIMIT),
        name="router",
    )(h2, ffn_norm.reshape(1, d).astype(F32), w_hi, w_lo, bias)


def _expert_kernel(ie_ref, isb_ref, inb_ref, tok_ref, up_hbm, wgu_hbm, wd_hbm, bgu_ref, bd_ref,
                   ys_hbm, xbuf, act, ybuf, wgu_buf, wd_buf, wg_s, wu_s, wd_s,
                   sem_x, sem_y, sem_gu, sem_d, *, j1, j2, tn1, tn2, n_items):
    i = pl.program_id(0)
    nb = inb_ref[i]
    sb = isb_ref[i]
    ex = ie_ref[i]
    R = MOE_ROWS

    def gu_copies(expert, jt, slot):
        return [pltpu.make_async_copy(
            wgu_hbm.at[expert, :, pl.ds(pl.multiple_of(which * j1 * tn1 + jt * tn1, tn1), tn1)],
            wgu_buf.at[slot, which], sem_gu.at[slot]) for which in range(2)]

    def d_copy(expert, jd, slot):
        return pltpu.make_async_copy(
            wd_hbm.at[expert, :, pl.ds(pl.multiple_of(jd * tn2, tn2), tn2)],
            wd_buf.at[slot], sem_d.at[slot])

    rmax = xbuf.shape[0]
    rows_per_tile = rmax // j2
    last_tok = tok_ref.shape[0] - 1

    def row_gather(base, r):
        tok = tok_ref[jnp.minimum(base + r, last_tok)]
        return pltpu.make_async_copy(up_hbm.at[pl.ds(tok, 1), :], xbuf.at[pl.ds(r, 1), :], sem_x)

    def x_wait_all():
        for r in range(MOE_ITEM_BLOCKS):
            pltpu.make_async_copy(up_hbm.at[pl.ds(0, R), :], xbuf.at[pl.ds(r * R, R), :],
                                  sem_x).wait()

    def y_copy(slot, r, jd):
        return pltpu.make_async_copy(
            ybuf.at[slot, pl.ds(r * R, R), :],
            ys_hbm.at[pl.ds((sb + r) * R, R), pl.ds(pl.multiple_of(jd * tn2, tn2), tn2)],
            sem_y.at[slot])

    def for_blocks(fn):
        for r in range(MOE_ITEM_BLOCKS):
            @pl.when(r < nb)
            def _():
                fn(r)

    @pl.when(i == 0)
    def _():
        def start(r, c):
            row_gather(sb * R, r).start()
            return c
        lax.fori_loop(0, rmax, start, 0, unroll=8)
        for cp in gu_copies(ex, 0, 0):
            cp.start()

    prev_active = inb_ref[jnp.maximum(i - 1, 0)] > 0

    @pl.when((i == 0) | prev_active)
    def _():
        x_wait_all()

    def gate_up_block(jt, r0, rows, wg, wu):
        x32 = xbuf[pl.ds(r0, rows), :]
        x_lo = pltpu.bitcast(x32 << 16, F32).astype(BF16)
        x_hi = pltpu.bitcast(x32 & jnp.uint32(0xFFFF0000), F32).astype(BF16)
        x = jnp.concatenate([x_lo, x_hi], axis=1)
        g = jnp.dot(x, wg, preferred_element_type=F32) + bgu_ref[pl.ds(jt, 1), :]
        u = jnp.dot(x, wu, preferred_element_type=F32) + bgu_ref[pl.ds(j1 + jt, 1), :]
        gt = jnp.minimum(g, SWIGLU_LIMIT)
        up = jnp.clip(u, -SWIGLU_LIMIT, SWIGLU_LIMIT)
        a = gt * jax.nn.sigmoid(SWIGLU_ALPHA * gt) * (up + 1.0)
        act[jt, pl.ds(r0, rows), :] = a.astype(BF16)

    def down_block(jd, slot, r0, rows, wd):
        lhs = jnp.concatenate([act[c, pl.ds(r0, rows), :] for c in range(j1)], axis=1)
        ybuf[slot, pl.ds(r0, rows), :] = (jnp.dot(lhs, wd, preferred_element_type=F32)
                                          + bd_ref[pl.ds(jd, 1), :])

    def rest_blocks(block_fn):
        rest = nb - 1

        def pair(p, c):
            block_fn(pl.multiple_of(R + p * 2 * R, R), 2 * R)
            return c

        lax.fori_loop(0, rest // 2, pair, 0)

        @pl.when(rest % 2 == 1)
        def _():
            block_fn(pl.multiple_of((nb - 1) * R, R), R)

    nxt = jnp.minimum(i + 1, n_items - 1)
    next_base = isb_ref[nxt] * R

    @pl.when(nb > 0)
    def _():
        d_copy(ex, 0, 0).start()

        def gate_up_tile(jt, carry):
            slot = jt % 2

            @pl.when(jt + 1 < j1)
            def _():
                for cp in gu_copies(ex, jt + 1, 1 - slot):
                    cp.start()

            for cp in gu_copies(ex, jt, slot):
                cp.wait()
            wg = wgu_buf[slot, 0].astype(BF16)
            wu = wgu_buf[slot, 1].astype(BF16)
            wg_s[...] = wg
            wu_s[...] = wu
            gate_up_block(jt, 0, R, wg, wu)
            rest_blocks(lambda r0, rows: gate_up_block(jt, r0, rows, wg_s[...], wu_s[...]))
            return carry

        lax.fori_loop(0, j1, gate_up_tile, 0)

        @pl.when(i + 1 < n_items)
        def _():
            @pl.when(inb_ref[i + 1] > 0)
            def _():
                for cp in gu_copies(ie_ref[i + 1], 0, 0):
                    cp.start()

        def down_tile(jd, carry):
            slot = jd % 2

            @pl.when(jd + 1 < j2)
            def _():
                d_copy(ex, jd + 1, 1 - slot).start()

            @pl.when(jd >= 2)
            def _():
                for_blocks(lambda r: y_copy(slot, r, jd).wait())

            d_copy(ex, jd, slot).wait()
            wd = wd_buf[slot].astype(BF16)
            wd_s[...] = wd
            down_block(jd, slot, 0, R, wd)
            for rr in range(rows_per_tile):
                row_gather(next_base, jd * rows_per_tile + rr).start()
            rest_blocks(lambda r0, rows: down_block(jd, slot, r0, rows, wd_s[...]))
            for_blocks(lambda r: y_copy(slot, r, jd).start())
            return carry

        lax.fori_loop(0, j2, down_tile, 0)
        for_blocks(lambda r: y_copy(0, r, 0).wait())
        for_blocks(lambda r: y_copy(1, r, 0).wait())

        @pl.when(i == n_items - 1)
        def _():
            x_wait_all()


def expert_mlp(u_packed, row_tok, item_e, item_sb, item_nb, w_gate_up, b_gate_up, w_down, b_down,
               tn1=256, tn2=512):
    p = row_tok.shape[0]
    d = 2 * u_packed.shape[1]
    n_items = item_e.shape[0]
    de = w_down.shape[1]
    j1 = de // tn1
    j2 = d // tn2
    rmax = MOE_ITEM_BLOCKS * MOE_ROWS

    assert rmax % j2 == 0

    def bias_map(i, ie, isb, inb, tok):
        return (ie[i], 0, 0)

    return pl.pallas_call(
        functools.partial(_expert_kernel, j1=j1, j2=j2, tn1=tn1, tn2=tn2, n_items=n_items),
        grid_spec=pltpu.PrefetchScalarGridSpec(
            num_scalar_prefetch=4,
            grid=(n_items,),
            in_specs=[
                pl.BlockSpec(memory_space=pl.ANY),
                pl.BlockSpec(memory_space=pl.ANY),
                pl.BlockSpec(memory_space=pl.ANY),
                pl.BlockSpec((None, 2 * j1, tn1), bias_map),
                pl.BlockSpec((None, j2, tn2), bias_map),
            ],
            out_specs=pl.BlockSpec(memory_space=pl.ANY),
            scratch_shapes=[
                pltpu.VMEM((rmax, d // 2), jnp.uint32),
                pltpu.VMEM((j1, rmax, tn1), BF16),
                pltpu.VMEM((2, rmax, tn2), F32),
                pltpu.VMEM((2, 2, d, tn1), F32),
                pltpu.VMEM((2, de, tn2), F32),
                pltpu.VMEM((d, tn1), BF16),
                pltpu.VMEM((d, tn1), BF16),
                pltpu.VMEM((de, tn2), BF16),
                pltpu.SemaphoreType.DMA(()),
                pltpu.SemaphoreType.DMA((2,)),
                pltpu.SemaphoreType.DMA((2,)),
                pltpu.SemaphoreType.DMA((2,)),
            ],
        ),
        out_shape=jax.ShapeDtypeStruct((p, d), F32),
        compiler_params=_params("arbitrary", vmem=VMEM_LIMIT),
        name="expert_mlp",
    )(item_e, item_sb, item_nb, row_tok, u_packed, w_gate_up, w_down,
      b_gate_up.reshape(N_EXPERTS, 2 * j1, tn1), b_down.reshape(N_EXPERTS, j2, tn2))


def _combine_kernel(dest_ref, h_ref, g_ref, ys_hbm, o_ref, buf, sem, *, tt, n_tok, n_steps):
    i = pl.program_id(0)

    def issue(blk, slot):
        for k in range(TOP_K):
            def start(t, c):
                src = dest_ref[k * n_tok + blk * tt + t]
                pltpu.make_async_copy(ys_hbm.at[pl.ds(src, 1), :], buf.at[slot, k, pl.ds(t, 1), :],
                                      sem.at[slot]).start()
                return c
            lax.fori_loop(0, tt, start, 0, unroll=8)

    @pl.when(i == 0)
    def _():
        issue(0, 0)

    slot = i % 2

    @pl.when(i + 1 < n_steps)
    def _():
        issue(i + 1, 1 - slot)

    for k in range(TOP_K):
        pltpu.make_async_copy(ys_hbm.at[pl.ds(0, tt), :], buf.at[slot, k], sem.at[slot]).wait()
    acc = h_ref[...]
    g = g_ref[...]
    for k in range(TOP_K):
        acc = acc + g[:, k:k + 1] * buf[slot, k]
    o_ref[...] = acc


def combine(h2, gates, ys, dest_km, tt=64):
    n, d = h2.shape
    return pl.pallas_call(
        functools.partial(_combine_kernel, tt=tt, n_tok=n, n_steps=n // tt),
        grid_spec=pltpu.PrefetchScalarGridSpec(
            num_scalar_prefetch=1,
            grid=(n // tt,),
            in_specs=[pl.BlockSpec((tt, d), lambda i, dst: (i, 0)),
                      pl.BlockSpec((tt, LANES), lambda i, dst: (i, 0)),
                      pl.BlockSpec(memory_space=pl.ANY)],
            out_specs=pl.BlockSpec((tt, d), lambda i, dst: (i, 0)),
            scratch_shapes=[pltpu.VMEM((2, TOP_K, tt, d), F32), pltpu.SemaphoreType.DMA((2,))],
        ),
        out_shape=jax.ShapeDtypeStruct((n, d), F32),
        compiler_params=_params("arbitrary", vmem=VMEM_LIMIT),
        name="moe_combine",
    )(dest_km, h2, gates, ys)


def routing_tables(top_idx, n_tok):
    a = n_tok * TOP_K
    e_flat = top_idx.reshape(-1)
    onehot = (e_flat[:, None] == jnp.arange(N_EXPERTS, dtype=jnp.int32)[None, :]).astype(jnp.int32)
    csum = jnp.cumsum(onehot, axis=0)
    rank = jnp.sum(csum * onehot, axis=1) - 1
    counts = csum[-1]
    nblk = (counts + MOE_ROWS - 1) // MOE_ROWS
    blk_end = jnp.cumsum(nblk)
    blk_start = blk_end - nblk
    dest = blk_start[e_flat] * MOE_ROWS + rank
    n_blocks = -(-(a + N_EXPERTS * (MOE_ROWS - 1)) // MOE_ROWS)
    p = n_blocks * MOE_ROWS
    t_flat = jnp.arange(a, dtype=jnp.int32) // TOP_K
    row_tok = jnp.zeros((p,), jnp.int32).at[dest].set(t_flat)

    n_items = N_EXPERTS + n_blocks // MOE_ITEM_BLOCKS
    items_per_e = (nblk + MOE_ITEM_BLOCKS - 1) // MOE_ITEM_BLOCKS
    item_end = jnp.cumsum(items_per_e)
    item_start = item_end - items_per_e
    total_items = item_end[-1]
    ii = jnp.arange(n_items, dtype=jnp.int32)
    ic = jnp.minimum(ii, total_items - 1)
    e_i = jnp.minimum(jnp.searchsorted(item_end, ic, side='right'), N_EXPERTS - 1).astype(jnp.int32)
    local = ic - item_start[e_i]
    item_sb = (blk_start[e_i] + local * MOE_ITEM_BLOCKS).astype(jnp.int32)
    item_nb = jnp.clip(nblk[e_i] - local * MOE_ITEM_BLOCKS, 0, MOE_ITEM_BLOCKS)
    item_nb = jnp.where(ii < total_items, item_nb, 0).astype(jnp.int32)
    dest_km = dest.astype(jnp.int32).reshape(n_tok, TOP_K).T.reshape(-1)
    return dest_km, row_tok, e_i, item_sb, item_nb


def kernel(x, meta_tokens, mix_norm, w_in, dn_conv, dn_a_log, dn_dt_bias, dn_out_norm, da_q_norm, da_k_norm, da_lam_q1, da_lam_k1, da_lam_q2, da_lam_k2, da_sub_norm, w_branch_a, w_branch_b, w_out, ffn_norm, router_w, router_b, w_gate_up, b_gate_up, w_down, b_down):
    bsz, seq, d = x.shape
    n_tok = bsz * seq
    x2 = x.reshape(n_tok, d)
    layer = 0

    w_bf, w_da = repack_w_in(w_in, layer, COL_BA + LANES, COL_BA + 2 * DN_HEADS)
    tn = 1024

    u_x = rms_cast(x2, mix_norm[layer], 256)
    u_m = rms_cast(meta_tokens.astype(F32), mix_norm[layer], N_META)
    proj_dn = matmul_fullk(u_x, w_bf, BF16, 1024, tn, name="in_proj_dn", n_blocks=COL_BA // tn)
    proj_da = matmul_fullk(u_x, w_da, BF16, 1024, tn, name="in_proj_da")
    ba_x = matmul_fullk(u_x, w_bf, F32, 1024, LANES, name="in_proj_ba",
                        n_offset=COL_BA // LANES, n_blocks=1)
    proj_dn_m = matmul_fullk(u_m, w_bf, BF16, N_META, tn, name="in_proj_dn_meta", n_blocks=COL_DZ // tn)
    proj_kv_m = matmul_fullk(u_m, w_da, BF16, N_META, tn, name="in_proj_kv_meta",
                             n_offset=COL_AK // tn, n_blocks=(COL_GA - COL_AK) // tn)
    ba_m = matmul_fullk(u_m, w_bf, F32, N_META, LANES, name="in_proj_ba_meta",
                        n_offset=COL_BA // LANES, n_blocks=1)

    y_a = gated_deltanet(proj_dn, ba_x, proj_dn_m, ba_m, dn_conv[layer], dn_a_log[layer],
                         dn_dt_bias[layer], dn_out_norm[layer], bsz, seq)
    lam4 = jnp.stack([da_lam_q1[layer], da_lam_k1[layer], da_lam_q2[layer], da_lam_k2[layer]])
    y_b = diff_attention(proj_da, proj_kv_m, da_q_norm[layer], da_k_norm[layer], lam4,
                         da_sub_norm[layer], bsz, seq)

    merged = branch_merge(y_a, y_b, w_branch_a[layer], w_branch_b[layer], proj_da, d)
    h2 = matmul_fullk(merged, w_out[layer], F32, 1024, 512, res=x2, name="out_proj")

    top_idx, gates, u_packed = router(h2, ffn_norm[layer], router_w[layer], router_b[layer])
    dest, row_tok, item_e, item_sb, item_nb = routing_tables(top_idx[:, :TOP_K], n_tok)
    ys = expert_mlp(u_packed, row_tok, item_e, item_sb, item_nb, w_gate_up[layer], b_gate_up[layer],
                    w_down[layer], b_down[layer])
    out = combine(h2, gates, ys, dest)
    return out.reshape(bsz, seq, d)
```

```python
import functools
import math

import jax
import jax.numpy as jnp
from jax import lax
from jax.experimental import pallas as pl
from jax.experimental.pallas import tpu as pltpu

F32 = jnp.float32
BF16 = jnp.bfloat16

N_META = 16
RMS_EPS = 1e-6
L2_EPS = 1e-6

DN_HEADS = 16
DN_DK = 128
DN_DV = 128
DN_CONV = 4
DN_CHUNK = 64
DN_W = DN_HEADS * DN_DK
DN_HIST = 16
DN_CONV_COLS = 1024

DA_HEADS = 8
DA_DK = 128
DA_DV = 256
LAMBDA_INIT = 0.8 - 0.6 * math.exp(-0.3 * 0)

N_EXPERTS = 32
TOP_K = 4
D_EXPERT = 1536
SWIGLU_LIMIT = 7.0
SWIGLU_ALPHA = 1.702

LANES = 128
MOE_ROWS = 256
MOE_ITEM_BLOCKS = 5
NEG_BIG = -1e30
VMEM_LIMIT = 56 * 1024 * 1024

COL_DZ = 3 * DN_W
COL_BA = 4 * DN_W
COL_AQ = 0
COL_AK = COL_AQ + DA_HEADS * 2 * DA_DK
COL_AV = COL_AK + DA_HEADS * 2 * DA_DK
COL_GA = COL_AV + DA_HEADS * DA_DV


def _params(*sem, vmem=None):
    return pltpu.CompilerParams(dimension_semantics=sem, vmem_limit_bytes=vmem)


def _rms_cast_kernel(x_ref, w_ref, o_ref):
    x = x_ref[...]
    y = x * lax.rsqrt(jnp.mean(x * x, axis=-1, keepdims=True) + RMS_EPS) * w_ref[...]
    o_ref[...] = y.astype(o_ref.dtype)


def rms_cast(x, w, tm):
    m, d = x.shape
    return pl.pallas_call(
        _rms_cast_kernel,
        grid=(m // tm,),
        in_specs=[pl.BlockSpec((tm, d), lambda i: (i, 0)),
                  pl.BlockSpec((1, d), lambda i: (0, 0))],
        out_specs=pl.BlockSpec((tm, d), lambda i: (i, 0)),
        out_shape=jax.ShapeDtypeStruct((m, d), BF16),
        compiler_params=_params("parallel"),
        name="rms_cast",
    )(x, w.reshape(1, d))


def _mm_fullk_kernel(a_ref, b_ref, *rest, has_res):
    acc = jnp.dot(a_ref[...], b_ref[...].astype(BF16), preferred_element_type=F32)
    if has_res:
        r_ref, o_ref = rest
        acc = acc + r_ref[...]
    else:
        o_ref, = rest
    o_ref[...] = acc.astype(o_ref.dtype)


def matmul_fullk(a, b, out_dtype, tm, tn, res=None, name="matmul", n_offset=0, n_blocks=None):
    m, kd = a.shape
    if n_blocks is None:
        n_blocks = b.shape[1] // tn
    in_specs = [pl.BlockSpec((tm, kd), lambda i, j: (i, 0)),
                pl.BlockSpec((kd, tn), lambda i, j: (0, n_offset + j))]
    args = [a, b]
    if res is not None:
        in_specs.append(pl.BlockSpec((tm, tn), lambda i, j: (i, j)))
        args.append(res)
    return pl.pallas_call(
        functools.partial(_mm_fullk_kernel, has_res=res is not None),
        grid=(m // tm, n_blocks),
        in_specs=in_specs,
        out_specs=pl.BlockSpec((tm, tn), lambda i, j: (i, j)),
        out_shape=jax.ShapeDtypeStruct((m, n_blocks * tn), out_dtype),
        compiler_params=_params("parallel", "parallel", vmem=VMEM_LIMIT),
        name=name,
    )(*args)


def _deltanet_kernel(qkv_ref, z_ref, ba_ref, qkvm_ref, bam_ref, cw_ref, hp_ref, nw_ref,
                     o_ref, full_ref, act_ref, ba_s, s_ref):
    t = pl.program_id(1)
    C = DN_CHUNK
    HIST = DN_HIST
    n_pad = C - N_META

    @pl.when(t == 0)
    def _():
        s_ref[...] = jnp.zeros_like(s_ref)
        full_ref[0:HIST + n_pad, :] = jnp.zeros((HIST + n_pad, 3 * DN_W), BF16)
        full_ref[HIST + n_pad:HIST + C, :] = qkvm_ref[...]
        ba_s[0:n_pad, :] = jnp.zeros((n_pad, LANES), F32)
        ba_s[n_pad:C, :] = bam_ref[...]

    @pl.when(t > 0)
    def _():
        full_ref[HIST:HIST + C, :] = qkv_ref[...]
        ba_s[...] = ba_ref[...]

    n_shift = DN_CONV - 1
    sr = lax.broadcasted_iota(jnp.int32, (n_shift * C, HIST + C), 0)
    sc = lax.broadcasted_iota(jnp.int32, (n_shift * C, HIST + C), 1)
    shift = jnp.where(sc == HIST - n_shift + (sr % C) + (sr // C), 1.0, 0.0).astype(BF16)
    for c0 in range(0, 3 * DN_W, DN_CONV_COLS):
        cs = slice(c0, c0 + DN_CONV_COLS)
        delayed = jnp.dot(shift, full_ref[:, cs], preferred_element_type=F32)
        conv = cw_ref[n_shift:n_shift + 1, cs] * full_ref[HIST:HIST + C, cs].astype(F32)
        for j in range(n_shift):
            conv = conv + cw_ref[j:j + 1, cs] * delayed[j * C:(j + 1) * C, :]
        act_ref[:, cs] = conv * jax.nn.sigmoid(conv)
    full_ref[0:HIST, :] = full_ref[C:C + HIST, :]

    row1 = lax.broadcasted_iota(jnp.int32, (C, 1), 0)
    valid = jnp.where((t > 0) | (row1 >= n_pad), 1.0, 0.0).astype(F32)
    ba = ba_s[...]
    beta_all = jax.nn.sigmoid(ba) * valid
    xg = ba + hp_ref[1:2, :]
    softplus = jnp.maximum(xg, 0.0) + jnp.log1p(jnp.exp(-jnp.abs(xg)))
    g_all = -jnp.exp(hp_ref[0:1, :]) * softplus * valid

    row = lax.broadcasted_iota(jnp.int32, (C, C), 0)
    col = lax.broadcasted_iota(jnp.int32, (C, C), 1)
    causal = row >= col
    strict = row > col
    eye = jnp.where(row == col, 1.0, 0.0).astype(F32)
    tril = jnp.where(causal, 1.0, 0.0).astype(BF16)

    g_hi = g_all.astype(BF16)
    r1 = g_all - g_hi.astype(F32)
    g_mid = r1.astype(BF16)
    g_lo = (r1 - g_mid.astype(F32)).astype(BF16)
    g3 = jnp.dot(tril, jnp.concatenate([g_hi, g_mid, g_lo], axis=1), preferred_element_type=F32)
    gcum = g3[:, 0:LANES] + g3[:, LANES:2 * LANES] + g3[:, 2 * LANES:3 * LANES]
    gcum_t = gcum.T

    scale = DN_DK ** -0.5
    nw = nw_ref[...]
    heads = range(DN_HEADS)

    def mm(a, b):
        return jnp.dot(a.astype(BF16), b.astype(BF16), preferred_element_type=F32)

    kn_b, lhs1, decay, rhs, qe_b, kdec_t, e_last = [], [], [], [], [], [], []
    for h in heads:
        qh = act_ref[:, h * DN_DK:(h + 1) * DN_DK]
        kh = act_ref[:, DN_W + h * DN_DK:DN_W + (h + 1) * DN_DK]
        vh = act_ref[:, 2 * DN_W + h * DN_DV:2 * DN_W + (h + 1) * DN_DV]
        qn = qh * (lax.rsqrt(jnp.sum(qh * qh, axis=-1, keepdims=True) + L2_EPS) * scale)
        kn = kh * lax.rsqrt(jnp.sum(kh * kh, axis=-1, keepdims=True) + L2_EPS)
        beta = beta_all[:, h:h + 1]
        gc = gcum[:, DN_HEADS + h:DN_HEADS + h + 1]
        gr = gcum_t[DN_HEADS + h:DN_HEADS + h + 1, :]
        g_last = gc[C - 1:C, :]
        e_g = jnp.exp(gc)
        kb = kn * beta
        kn_b.append(kn.astype(BF16))
        lhs1.append(jnp.concatenate([kb, qn], axis=0).astype(BF16))
        decay.append(jnp.exp(jnp.where(causal, gc - gr, NEG_BIG)))
        rhs.append(jnp.concatenate([vh * beta, kb * e_g], axis=1).astype(BF16))
        qe_b.append((qn * e_g).astype(BF16))
        kdec_t.append((kn * jnp.exp(g_last - gc)).T.astype(BF16))
        e_last.append(jnp.exp(g_last))

    kq = [lax.dot_general(lhs1[h], kn_b[h], (((1,), (1,)), ((), ())), preferred_element_type=F32)
          for h in heads]
    qk_b = [jnp.where(causal, kq[h][C:2 * C, :] * decay[h], 0.0).astype(BF16) for h in heads]
    b1 = [-jnp.where(strict, kq[h][0:C, :] * decay[h], 0.0) for h in heads]
    b2 = [mm(b1[h], b1[h]) for h in heads]
    b4 = [mm(b2[h], b2[h]) for h in heads]
    p1 = [(eye + b1[h]) + mm(eye + b1[h], b2[h]) for h in heads]
    b8 = [mm(b4[h], b4[h]) for h in heads]
    b16 = [mm(b8[h], b8[h]) for h in heads]
    p2 = [(eye + b4[h]) + mm(eye + b4[h], b8[h]) for h in heads]
    b32 = [mm(b16[h], b16[h]) for h in heads]
    p12 = [mm(p1[h], p2[h]) for h in heads]
    p3 = [(eye + b16[h]) + mm(eye + b16[h], b32[h]) for h in heads]
    inv = [mm(p12[h], p3[h]) for h in heads]
    sol = [mm(inv[h], rhs[h]) for h in heads]

    s_old = [s_ref[h] for h in heads]
    ws_lhs = [jnp.concatenate([sol[h][:, DN_DV:2 * DN_DV].astype(BF16), qe_b[h]], axis=0) for h in heads]
    ws = [mm(ws_lhs[h], s_old[h]) for h in heads]
    v_new = [(sol[h][:, 0:DN_DV] - ws[h][0:C, :]).astype(BF16) for h in heads]
    ov_lhs = [jnp.concatenate([qk_b[h], kdec_t[h]], axis=0) for h in heads]
    ov = [mm(ov_lhs[h], v_new[h]) for h in heads]
    for h in heads:
        s_ref[h] = s_old[h] * e_last[h] + ov[h][C:C + DN_DK, :]
        o = ws[h][C:2 * C, :] + ov[h][0:C, :]
        zh = z_ref[:, h * DN_DV:(h + 1) * DN_DV].astype(F32)
        o = o * lax.rsqrt(jnp.mean(o * o, axis=-1, keepdims=True) + RMS_EPS) * nw
        o_ref[:, h * DN_DV:(h + 1) * DN_DV] = (o * (zh * jax.nn.sigmoid(zh))).astype(o_ref.dtype)


def gated_deltanet(proj_x, ba_x, proj_m, ba_m, conv_w, a_log, dt_bias, norm_w, bsz, seq):
    C = DN_CHUNK
    n_chunks = seq // C
    w3 = 3 * DN_W
    hp = jnp.zeros((2, LANES), F32)
    hp = hp.at[0, DN_HEADS:2 * DN_HEADS].set(a_log.astype(F32))
    hp = hp.at[1, DN_HEADS:2 * DN_HEADS].set(dt_bias.astype(F32))

    def xrow(b, t):
        return b * n_chunks + jnp.maximum(t - 1, 0)

    return pl.pallas_call(
        _deltanet_kernel,
        grid=(bsz, n_chunks + 1),
        in_specs=[
            pl.BlockSpec((C, w3), lambda b, t: (xrow(b, t), 0)),
            pl.BlockSpec((C, DN_W), lambda b, t: (xrow(b, t), COL_DZ // DN_W)),
            pl.BlockSpec((C, LANES), lambda b, t: (xrow(b, t), 0)),
            pl.BlockSpec((N_META, w3), lambda b, t: (0, 0)),
            pl.BlockSpec((N_META, LANES), lambda b, t: (0, 0)),
            pl.BlockSpec((DN_CONV, w3), lambda b, t: (0, 0)),
            pl.BlockSpec((2, LANES), lambda b, t: (0, 0)),
            pl.BlockSpec((1, DN_DV), lambda b, t: (0, 0)),
        ],
        out_specs=pl.BlockSpec((C, DN_W), lambda b, t: (xrow(b, t), 0)),
        out_shape=jax.ShapeDtypeStruct((bsz * seq, DN_W), BF16),
        scratch_shapes=[
            pltpu.VMEM((DN_HIST + C, w3), BF16),
            pltpu.VMEM((C, w3), F32),
            pltpu.VMEM((C, LANES), F32),
            pltpu.VMEM((DN_HEADS, DN_DK, DN_DV), F32),
        ],
        compiler_params=_params("parallel", "arbitrary", vmem=VMEM_LIMIT),
        name="gated_deltanet",
    )(proj_x, proj_x, ba_x, proj_m, ba_m, conv_w.astype(F32), hp, norm_w.reshape(1, DN_DV).astype(F32))


def _diff_attn_kernel(q_ref, k_ref, v_ref, km_ref, vm_ref, qn_ref, kn_ref, lam_ref, sn_ref,
                      o_ref, kx_s, kmeta_s, *, tq, seq):
    qi = pl.program_id(2)

    def rms(x, w):
        return x * lax.rsqrt(jnp.mean(x * x, axis=-1, keepdims=True) + RMS_EPS) * w

    @pl.when(qi == 0)
    def _():
        for m in range(2):
            kx = k_ref[:, m * DA_DK:(m + 1) * DA_DK].astype(F32)
            kx_s[m] = rms(kx, kn_ref[...]).astype(BF16)
            kmx = km_ref[:, m * DA_DK:(m + 1) * DA_DK].astype(F32)
            kmeta_s[m] = rms(kmx, kn_ref[...]).astype(BF16)

    lam = (jnp.exp(jnp.sum(lam_ref[0:1, :] * lam_ref[1:2, :], axis=-1, keepdims=True))
           - jnp.exp(jnp.sum(lam_ref[2:3, :] * lam_ref[3:4, :], axis=-1, keepdims=True))
           + LAMBDA_INIT)
    row = lax.broadcasted_iota(jnp.int32, (tq, tq), 0)
    col = lax.broadcasted_iota(jnp.int32, (tq, tq), 1)
    diag_visible = col <= row
    v_m = vm_ref[...]
    scale = DA_DK ** -0.5 * math.log2(math.e)
    nt = (((1,), (1,)), ((), ()))

    def attend(n_prev):
        maps = range(2)
        qn = [(rms(q_ref[:, m * DA_DK:(m + 1) * DA_DK].astype(F32), qn_ref[...]) * scale).astype(BF16)
              for m in maps]
        sd = [jnp.where(diag_visible,
                        lax.dot_general(qn[m], kx_s[m, n_prev:n_prev + tq, :], nt,
                                        preferred_element_type=F32), NEG_BIG) for m in maps]
        sm = [lax.dot_general(qn[m], kmeta_s[m], nt, preferred_element_type=F32) for m in maps]
        mx = [jnp.maximum(jnp.max(sd[m], axis=-1, keepdims=True),
                          jnp.max(sm[m], axis=-1, keepdims=True)) for m in maps]
        if n_prev:
            sp = [lax.dot_general(qn[m], kx_s[m, 0:n_prev, :], nt, preferred_element_type=F32)
                  for m in maps]
            mx = [jnp.maximum(mx[m], jnp.max(sp[m], axis=-1, keepdims=True)) for m in maps]
        pd = [jnp.exp2(sd[m] - mx[m]) for m in maps]
        pm = [jnp.exp2(sm[m] - mx[m]) for m in maps]
        den = [jnp.sum(pd[m], axis=-1, keepdims=True) + jnp.sum(pm[m], axis=-1, keepdims=True)
               for m in maps]
        acc = [jnp.dot(pd[m].astype(BF16), v_ref[n_prev:n_prev + tq, :], preferred_element_type=F32)
               + jnp.dot(pm[m].astype(BF16), v_m, preferred_element_type=F32) for m in maps]
        if n_prev:
            pp = [jnp.exp2(sp[m] - mx[m]) for m in maps]
            den = [den[m] + jnp.sum(pp[m], axis=-1, keepdims=True) for m in maps]
            acc = [acc[m] + jnp.dot(pp[m].astype(BF16), v_ref[0:n_prev, :], preferred_element_type=F32)
                   for m in maps]
        out = acc[0] / den[0] - lam * (acc[1] / den[1])
        out = rms(out, sn_ref[...]) * (1.0 - LAMBDA_INIT)
        o_ref[...] = out.astype(o_ref.dtype)

    for blk in range(seq // tq):
        @pl.when(qi == blk)
        def _():
            attend(blk * tq)


def diff_attention(proj_x, proj_m, q_norm, k_norm, lam4, sub_norm, bsz, seq, tq=256):
    nq = seq // tq
    cq = COL_AQ // DA_DV
    ck = COL_AK // DA_DV
    cv = COL_AV // DA_DV
    return pl.pallas_call(
        functools.partial(_diff_attn_kernel, tq=tq, seq=seq),
        grid=(bsz, DA_HEADS, nq),
        in_specs=[
            pl.BlockSpec((tq, DA_DV), lambda b, h, i: (b * nq + i, cq + h)),
            pl.BlockSpec((seq, DA_DV), lambda b, h, i: (b, ck + h)),
            pl.BlockSpec((seq, DA_DV), lambda b, h, i: (b, cv + h)),
            pl.BlockSpec((N_META, DA_DV), lambda b, h, i: (0, h)),
            pl.BlockSpec((N_META, DA_DV), lambda b, h, i: (0, DA_HEADS + h)),
            pl.BlockSpec((1, DA_DK), lambda b, h, i: (0, 0)),
            pl.BlockSpec((1, DA_DK), lambda b, h, i: (0, 0)),
            pl.BlockSpec((4, DA_DK), lambda b, h, i: (0, 0)),
            pl.BlockSpec((1, DA_DV), lambda b, h, i: (0, 0)),
        ],
        out_specs=pl.BlockSpec((tq, DA_DV), lambda b, h, i: (b * nq + i, h)),
        out_shape=jax.ShapeDtypeStruct((bsz * seq, DA_HEADS * DA_DV), BF16),
        scratch_shapes=[pltpu.VMEM((2, seq, DA_DK), BF16),
                        pltpu.VMEM((2, N_META, DA_DK), BF16)],
        compiler_params=_params("parallel", "parallel", "arbitrary", vmem=VMEM_LIMIT),
        name="diff_attention",
    )(proj_x, proj_x, proj_x, proj_m, proj_m,
      q_norm.reshape(1, DA_DK).astype(F32), k_norm.reshape(1, DA_DK).astype(F32),
      lam4.astype(F32), sub_norm.reshape(1, DA_DV).astype(F32))


def _merge_kernel(ya_ref, yb_ref, wa_ref, wb_ref, ga_ref, gb_ref, o_ref):
    pa = jnp.dot(ya_ref[...], wa_ref[...].astype(BF16), preferred_element_type=F32)
    pb = jnp.dot(yb_ref[...], wb_ref[...].astype(BF16), preferred_element_type=F32)
    ga = jax.nn.sigmoid(ga_ref[...].astype(F32))
    gb = jax.nn.sigmoid(gb_ref[...].astype(F32))
    o_ref[...] = (ga * pa + gb * pb).astype(o_ref.dtype)


def branch_merge(y_a, y_b, w_a, w_b, proj_x, d_model, tm=1024, tn=512):
    m, ka = y_a.shape
    kb = y_b.shape[1]
    cga = COL_GA // tn
    cgb = (COL_GA + d_model) // tn
    return pl.pallas_call(
        _merge_kernel,
        grid=(m // tm, d_model // tn),
        in_specs=[
            pl.BlockSpec((tm, ka), lambda i, j: (i, 0)),
            pl.BlockSpec((tm, kb), lambda i, j: (i, 0)),
            pl.BlockSpec((ka, tn), lambda i, j: (0, j)),
            pl.BlockSpec((kb, tn), lambda i, j: (0, j)),
            pl.BlockSpec((tm, tn), lambda i, j: (i, cga + j)),
            pl.BlockSpec((tm, tn), lambda i, j: (i, cgb + j)),
        ],
        out_specs=pl.BlockSpec((tm, tn), lambda i, j: (i, j)),
        out_shape=jax.ShapeDtypeStruct((m, d_model), BF16),
        compiler_params=_params("parallel", "parallel", vmem=VMEM_LIMIT),
        name="branch_merge",
    )(y_a, y_b, w_a, w_b, proj_x, proj_x)


def _router_kernel(h_ref, nw_ref, whi_ref, wlo_ref, b_ref, idx_ref, gate_ref, up_ref):
    x = h_ref[...]
    u = x * lax.rsqrt(jnp.mean(x * x, axis=-1, keepdims=True) + RMS_EPS) * nw_ref[...]
    u_hi = u.astype(BF16)
    u_lo = (u - u_hi.astype(F32)).astype(BF16)
    half = u.shape[1] // 2
    bits = pltpu.bitcast(u_hi.astype(F32), jnp.uint32)
    up_ref[...] = bits[:, half:] | (bits[:, :half] >> 16)
    logits = (jnp.dot(u_hi, whi_ref[...], preferred_element_type=F32)
              + jnp.dot(u_lo, whi_ref[...], preferred_element_type=F32)
              + jnp.dot(u_hi, wlo_ref[...], preferred_element_type=F32)
              + b_ref[...])
    lane = lax.broadcasted_iota(jnp.int32, logits.shape, 1)
    lane_f = lane.astype(F32)
    vals = logits
    tops, idxs = [], []
    for _ in range(TOP_K):
        mx = jnp.max(vals, axis=-1, keepdims=True)
        ix = jnp.min(jnp.where(vals == mx, lane_f, float(LANES)), axis=-1, keepdims=True).astype(jnp.int32)
        tops.append(mx)
        idxs.append(ix)
        vals = jnp.where(lane == ix, -3.0e38, vals)
    exps = [jnp.exp(tv - tops[0]) for tv in tops]
    den = exps[0] + exps[1] + exps[2] + exps[3]
    idx_out = jnp.zeros(logits.shape, jnp.int32)
    gate_out = jnp.zeros(logits.shape, F32)
    for k in range(TOP_K):
        idx_out = jnp.where(lane == k, idxs[k], idx_out)
        gate_out = jnp.where(lane == k, exps[k] / den, gate_out)
    idx_ref[...] = idx_out
    gate_ref[...] = gate_out


def router(h2, ffn_norm, router_w, router_b, tm=256):
    n, d = h2.shape
    wpad = jnp.zeros((d, LANES), F32).at[:, :N_EXPERTS].set(router_w.astype(F32))
    w_hi = wpad.astype(BF16)
    w_lo = (wpad - w_hi.astype(F32)).astype(BF16)
    bias = jnp.full((1, LANES), NEG_BIG, F32).at[0, :N_EXPERTS].set(router_b.astype(F32))
    return pl.pallas_call(
        _router_kernel,
        grid=(n // tm,),
        in_specs=[
            pl.BlockSpec((tm, d), lambda i: (i, 0)),
            pl.BlockSpec((1, d), lambda i: (0, 0)),
            pl.BlockSpec((d, LANES), lambda i: (0, 0)),
            pl.BlockSpec((d, LANES), lambda i: (0, 0)),
            pl.BlockSpec((1, LANES), lambda i: (0, 0)),
        ],
        out_specs=[pl.BlockSpec((tm, LANES), lambda i: (i, 0)),
                   pl.BlockSpec((tm, LANES), lambda i: (i, 0)),
                   pl.BlockSpec((tm, d // 2), lambda i: (i, 0))],
        out_shape=[jax.ShapeDtypeStruct((n, LANES), jnp.int32),
                   jax.ShapeDtypeStruct((n, LANES), F32),
                   jax.ShapeDtypeStruct((n, d // 2), jnp.uint32)],
        compiler_params=_params("parallel", vmem=VMEM_LIMIT),
        name="router",
    )(h2, ffn_norm.reshape(1, d).astype(F32), w_hi, w_lo, bias)


def _expert_kernel(ie_ref, isb_ref, inb_ref, tok_ref, up_hbm, wgu_hbm, wd_hbm, bgu_ref, bd_ref,
                   ys_hbm, xbuf, act, ybuf, wgu_buf, wd_buf, wg_s, wu_s, wd_s,
                   sem_x, sem_y, sem_gu, sem_d, *, j1, j2, tn1, tn2, n_items):
    i = pl.program_id(0)
    nb = inb_ref[i]
    sb = isb_ref[i]
    ex = ie_ref[i]
    R = MOE_ROWS

    def gu_copies(expert, jt, slot):
        return [pltpu.make_async_copy(
            wgu_hbm.at[expert, :, pl.ds(pl.multiple_of(which * j1 * tn1 + jt * tn1, tn1), tn1)],
            wgu_buf.at[slot, which], sem_gu.at[slot]) for which in range(2)]

    def d_copy(expert, jd, slot):
        return pltpu.make_async_copy(
            wd_hbm.at[expert, :, pl.ds(pl.multiple_of(jd * tn2, tn2), tn2)],
            wd_buf.at[slot], sem_d.at[slot])

    rmax = xbuf.shape[0]
    rows_per_tile = rmax // j2
    last_tok = tok_ref.shape[0] - 1

    def row_gather(base, r):
        tok = tok_ref[jnp.minimum(base + r, last_tok)]
        return pltpu.make_async_copy(up_hbm.at[pl.ds(tok, 1), :], xbuf.at[pl.ds(r, 1), :], sem_x)

    def x_wait_all():
        for r in range(MOE_ITEM_BLOCKS):
            pltpu.make_async_copy(up_hbm.at[pl.ds(0, R), :], xbuf.at[pl.ds(r * R, R), :],
                                  sem_x).wait()

    def y_copy(slot, r, jd):
        return pltpu.make_async_copy(
            ybuf.at[slot, pl.ds(r * R, R), :],
            ys_hbm.at[pl.ds((sb + r) * R, R), pl.ds(pl.multiple_of(jd * tn2, tn2), tn2)],
            sem_y.at[slot])

    def for_blocks(fn):
        for r in range(MOE_ITEM_BLOCKS):
            @pl.when(r < nb)
            def _():
                fn(r)

    @pl.when(i == 0)
    def _():
        def start(r, c):
            row_gather(sb * R, r).start()
            return c
        lax.fori_loop(0, rmax, start, 0, unroll=8)
        for cp in gu_copies(ex, 0, 0):
            cp.start()

    prev_active = inb_ref[jnp.maximum(i - 1, 0)] > 0

    @pl.when((i == 0) | prev_active)
    def _():
        x_wait_all()

    def gate_up_block(jt, r0, rows, wg, wu):
        x32 = xbuf[pl.ds(r0, rows), :]
        x_lo = pltpu.bitcast(x32 << 16, F32).astype(BF16)
        x_hi = pltpu.bitcast(x32 & jnp.uint32(0xFFFF0000), F32).astype(BF16)
        x = jnp.concatenate([x_lo, x_hi], axis=1)
        g = jnp.dot(x, wg, preferred_element_type=F32) + bgu_ref[pl.ds(jt, 1), :]
        u = jnp.dot(x, wu, preferred_element_type=F32) + bgu_ref[pl.ds(j1 + jt, 1), :]
        gt = jnp.minimum(g, SWIGLU_LIMIT)
        up = jnp.clip(u, -SWIGLU_LIMIT, SWIGLU_LIMIT)
        a = gt * jax.nn.sigmoid(SWIGLU_ALPHA * gt) * (up + 1.0)
        act[jt, pl.ds(r0, rows), :] = a.astype(BF16)

    def down_block(jd, slot, r0, rows, wd):
        lhs = jnp.concatenate([act[c, pl.ds(r0, rows), :] for c in range(j1)], axis=1)
        ybuf[slot, pl.ds(r0, rows), :] = (jnp.dot(lhs, wd, preferred_element_type=F32)
                                          + bd_ref[pl.ds(jd, 1), :])

    def rest_blocks(block_fn):
        rest = nb - 1

        def pair(p, c):
            block_fn(pl.multiple_of(R + p * 2 * R, R), 2 * R)
            return c

        lax.fori_loop(0, rest // 2, pair, 0)

        @pl.when(rest % 2 == 1)
        def _():
            block_fn(pl.multiple_of((nb - 1) * R, R), R)

    nxt = jnp.minimum(i + 1, n_items - 1)
    next_base = isb_ref[nxt] * R

    @pl.when(nb > 0)
    def _():
        d_copy(ex, 0, 0).start()

        def gate_up_tile(jt, carry):
            slot = jt % 2

            @pl.when(jt + 1 < j1)
            def _():
                for cp in gu_copies(ex, jt + 1, 1 - slot):
                    cp.start()

            for cp in gu_copies(ex, jt, slot):
                cp.wait()
            wg = wgu_buf[slot, 0].astype(BF16)
            wu = wgu_buf[slot, 1].astype(BF16)
            wg_s[...] = wg
            wu_s[...] = wu
            gate_up_block(jt, 0, R, wg, wu)
            rest_blocks(lambda r0, rows: gate_up_block(jt, r0, rows, wg_s[...], wu_s[...]))
            return carry

        lax.fori_loop(0, j1, gate_up_tile, 0)

        @pl.when(i + 1 < n_items)
        def _():
            @pl.when(inb_ref[i + 1] > 0)
            def _():
                for cp in gu_copies(ie_ref[i + 1], 0, 0):
                    cp.start()

        def down_tile(jd, carry):
            slot = jd % 2

            @pl.when(jd + 1 < j2)
            def _():
                d_copy(ex, jd + 1, 1 - slot).start()

            @pl.when(jd >= 2)
            def _():
                for_blocks(lambda r: y_copy(slot, r, jd).wait())

            d_copy(ex, jd, slot).wait()
            wd = wd_buf[slot].astype(BF16)
            wd_s[...] = wd
            down_block(jd, slot, 0, R, wd)
            for rr in range(rows_per_tile):
                row_gather(next_base, jd * rows_per_tile + rr).start()
            rest_blocks(lambda r0, rows: down_block(jd, slot, r0, rows, wd_s[...]))
            for_blocks(lambda r: y_copy(slot, r, jd).start())
            return carry

        lax.fori_loop(0, j2, down_tile, 0)
        for_blocks(lambda r: y_copy(0, r, 0).wait())
        for_blocks(lambda r: y_copy(1, r, 0).wait())

        @pl.when(i == n_items - 1)
        def _():
            x_wait_all()


def expert_mlp(u_packed, row_tok, item_e, item_sb, item_nb, w_gate_up, b_gate_up, w_down, b_down,
               tn1=256, tn2=512):
    p = row_tok.shape[0]
    d = 2 * u_packed.shape[1]
    n_items = item_e.shape[0]
    de = w_down.shape[1]
    j1 = de // tn1
    j2 = d // tn2
    rmax = MOE_ITEM_BLOCKS * MOE_ROWS

    assert rmax % j2 == 0

    def bias_map(i, ie, isb, inb, tok):
        return (ie[i], 0, 0)

    return pl.pallas_call(
        functools.partial(_expert_kernel, j1=j1, j2=j2, tn1=tn1, tn2=tn2, n_items=n_items),
        grid_spec=pltpu.PrefetchScalarGridSpec(
            num_scalar_prefetch=4,
            grid=(n_items,),
            in_specs=[
                pl.BlockSpec(memory_space=pl.ANY),
                pl.BlockSpec(memory_space=pl.ANY),
                pl.BlockSpec(memory_space=pl.ANY),
                pl.BlockSpec((None, 2 * j1, tn1), bias_map),
                pl.BlockSpec((None, j2, tn2), bias_map),
            ],
            out_specs=pl.BlockSpec(memory_space=pl.ANY),
            scratch_shapes=[
                pltpu.VMEM((rmax, d // 2), jnp.uint32),
                pltpu.VMEM((j1, rmax, tn1), BF16),
                pltpu.VMEM((2, rmax, tn2), F32),
                pltpu.VMEM((2, 2, d, tn1), F32),
                pltpu.VMEM((2, de, tn2), F32),
                pltpu.VMEM((d, tn1), BF16),
                pltpu.VMEM((d, tn1), BF16),
                pltpu.VMEM((de, tn2), BF16),
                pltpu.SemaphoreType.DMA(()),
                pltpu.SemaphoreType.DMA((2,)),
                pltpu.SemaphoreType.DMA((2,)),
                pltpu.SemaphoreType.DMA((2,)),
            ],
        ),
        out_shape=jax.ShapeDtypeStruct((p, d), F32),
        compiler_params=_params("arbitrary", vmem=VMEM_LIMIT),
        name="expert_mlp",
    )(item_e, item_sb, item_nb, row_tok, u_packed, w_gate_up, w_down,
      b_gate_up.reshape(N_EXPERTS, 2 * j1, tn1), b_down.reshape(N_EXPERTS, j2, tn2))


def _combine_kernel(dest_ref, h_ref, g_ref, ys_hbm, o_ref, buf, sem, *, tt, n_tok, n_steps):
    i = pl.program_id(0)

    def issue(blk, slot):
        for k in range(TOP_K):
            def start(t, c):
                src = dest_ref[k * n_tok + blk * tt + t]
                pltpu.make_async_copy(ys_hbm.at[pl.ds(src, 1), :], buf.at[slot, k, pl.ds(t, 1), :],
                                      sem.at[slot]).start()
                return c
            lax.fori_loop(0, tt, start, 0, unroll=8)

    @pl.when(i == 0)
    def _():
        issue(0, 0)

    slot = i % 2

    @pl.when(i + 1 < n_steps)
    def _():
        issue(i + 1, 1 - slot)

    for k in range(TOP_K):
        pltpu.make_async_copy(ys_hbm.at[pl.ds(0, tt), :], buf.at[slot, k], sem.at[slot]).wait()
    acc = h_ref[...]
    g = g_ref[...]
    for k in range(TOP_K):
        acc = acc + g[:, k:k + 1] * buf[slot, k]
    o_ref[...] = acc


def combine(h2, gates, ys, dest_km, tt=64):
    n, d = h2.shape
    return pl.pallas_call(
        functools.partial(_combine_kernel, tt=tt, n_tok=n, n_steps=n // tt),
        grid_spec=pltpu.PrefetchScalarGridSpec(
            num_scalar_prefetch=1,
            grid=(n // tt,),
            in_specs=[pl.BlockSpec((tt, d), lambda i, dst: (i, 0)),
                      pl.BlockSpec((tt, LANES), lambda i, dst: (i, 0)),
                      pl.BlockSpec(memory_space=pl.ANY)],
            out_specs=pl.BlockSpec((tt, d), lambda i, dst: (i, 0)),
            scratch_shapes=[pltpu.VMEM((2, TOP_K, tt, d), F32), pltpu.SemaphoreType.DMA((2,))],
        ),
        out_shape=jax.ShapeDtypeStruct((n, d), F32),
        compiler_params=_params("arbitrary", vmem=VMEM_LIMIT),
        name="moe_combine",
    )(dest_km, h2, gates, ys)


def routing_tables(top_idx, n_tok):
    a = n_tok * TOP_K
    e_flat = top_idx.reshape(-1)
    onehot = (e_flat[:, None] == jnp.arange(N_EXPERTS, dtype=jnp.int32)[None, :]).astype(jnp.int32)
    csum = jnp.cumsum(onehot, axis=0)
    rank = jnp.sum(csum * onehot, axis=1) - 1
    counts = csum[-1]
    nblk = (counts + MOE_ROWS - 1) // MOE_ROWS
    blk_end = jnp.cumsum(nblk)
    blk_start = blk_end - nblk
    dest = blk_start[e_flat] * MOE_ROWS + rank
    n_blocks = -(-(a + N_EXPERTS * (MOE_ROWS - 1)) // MOE_ROWS)
    p = n_blocks * MOE_ROWS
    t_flat = jnp.arange(a, dtype=jnp.int32) // TOP_K
    row_tok = jnp.zeros((p,), jnp.int32).at[dest].set(t_flat)

    n_items = N_EXPERTS + n_blocks // MOE_ITEM_BLOCKS
    items_per_e = (nblk + MOE_ITEM_BLOCKS - 1) // MOE_ITEM_BLOCKS
    item_end = jnp.cumsum(items_per_e)
    item_start = item_end - items_per_e
    total_items = item_end[-1]
    ii = jnp.arange(n_items, dtype=jnp.int32)
    ic = jnp.minimum(ii, total_items - 1)
    e_i = jnp.minimum(jnp.searchsorted(item_end, ic, side='right'), N_EXPERTS - 1).astype(jnp.int32)
    local = ic - item_start[e_i]
    item_sb = (blk_start[e_i] + local * MOE_ITEM_BLOCKS).astype(jnp.int32)
    item_nb = jnp.clip(nblk[e_i] - local * MOE_ITEM_BLOCKS, 0, MOE_ITEM_BLOCKS)
    item_nb = jnp.where(ii < total_items, item_nb, 0).astype(jnp.int32)
    dest_km = dest.astype(jnp.int32).reshape(n_tok, TOP_K).T.reshape(-1)
    return dest_km, row_tok, e_i, item_sb, item_nb


def kernel(x, meta_tokens, mix_norm, w_in, dn_conv, dn_a_log, dn_dt_bias, dn_out_norm, da_q_norm, da_k_norm, da_lam_q1, da_lam_k1, da_lam_q2, da_lam_k2, da_sub_norm, w_branch_a, w_branch_b, w_out, ffn_norm, router_w, router_b, w_gate_up, b_gate_up, w_down, b_down):
    bsz, seq, d = x.shape
    n_tok = bsz * seq
    x2 = x.reshape(n_tok, d)
    layer = 0

    w_bf = w_in[layer].astype(BF16)
    w_da = w_bf[:, COL_BA + 2 * DN_HEADS:]
    tn = 1024

    u_x = rms_cast(x2, mix_norm[layer], 256)
    u_m = rms_cast(meta_tokens.astype(F32), mix_norm[layer], N_META)
    proj_dn = matmul_fullk(u_x, w_bf, BF16, 1024, tn, name="in_proj_dn", n_blocks=COL_BA // tn)
    proj_da = matmul_fullk(u_x, w_da, BF16, 1024, tn, name="in_proj_da")
    ba_x = matmul_fullk(u_x, w_bf, F32, 1024, LANES, name="in_proj_ba",
                        n_offset=COL_BA // LANES, n_blocks=1)
    proj_dn_m = matmul_fullk(u_m, w_bf, BF16, N_META, tn, name="in_proj_dn_meta", n_blocks=COL_DZ // tn)
    proj_kv_m = matmul_fullk(u_m, w_da, BF16, N_META, tn, name="in_proj_kv_meta",
                             n_offset=COL_AK // tn, n_blocks=(COL_GA - COL_AK) // tn)
    ba_m = matmul_fullk(u_m, w_bf, F32, N_META, LANES, name="in_proj_ba_meta",
                        n_offset=COL_BA // LANES, n_blocks=1)

    y_a = gated_deltanet(proj_dn, ba_x, proj_dn_m, ba_m, dn_conv[layer], dn_a_log[layer],
                         dn_dt_bias[layer], dn_out_norm[layer], bsz, seq)
    lam4 = jnp.stack([da_lam_q1[layer], da_lam_k1[layer], da_lam_q2[layer], da_lam_k2[layer]])
    y_b = diff_attention(proj_da, proj_kv_m, da_q_norm[layer], da_k_norm[layer], lam4,
                         da_sub_norm[layer], bsz, seq)

    merged = branch_merge(y_a, y_b, w_branch_a[layer], w_branch_b[layer], proj_da, d)
    h2 = matmul_fullk(merged, w_out[layer], F32, 1024, 512, res=x2, name="out_proj")

    top_idx, gates, u_packed = router(h2, ffn_norm[layer], router_w[layer], router_b[layer])
    dest, row_tok, item_e, item_sb, item_nb = routing_tables(top_idx[:, :TOP_K], n_tok)
    ys = expert_mlp(u_packed, row_tok, item_e, item_sb, item_nb, w_gate_up[layer], b_gate_up[layer],
                    w_down[layer], b_down[layer])
    out = combine(h2, gates, ys, dest)
    return out.reshape(bsz, seq, d)
```

```python
import functools
import math

import jax
import jax.numpy as jnp
from jax import lax
from jax.experimental import pallas as pl
from jax.experimental.pallas import tpu as pltpu

F32 = jnp.float32
BF16 = jnp.bfloat16

N_META = 16
RMS_EPS = 1e-6
L2_EPS = 1e-6

DN_HEADS = 16
DN_DK = 128
DN_DV = 128
DN_CONV = 4
DN_CHUNK = 64
DN_W = DN_HEADS * DN_DK
DN_HIST = 16
DN_CONV_COLS = 1024

DA_HEADS = 8
DA_DK = 128
DA_DV = 256
LAMBDA_INIT = 0.8 - 0.6 * math.exp(-0.3 * 0)

N_EXPERTS = 32
TOP_K = 4
D_EXPERT = 1536
SWIGLU_LIMIT = 7.0
SWIGLU_ALPHA = 1.702

LANES = 128
MOE_ROWS = 256
MOE_ITEM_BLOCKS = 5
NEG_BIG = -1e30
VMEM_LIMIT = 56 * 1024 * 1024

COL_DZ = 3 * DN_W
COL_BA = 4 * DN_W
COL_AQ = 0
COL_AK = COL_AQ + DA_HEADS * 2 * DA_DK
COL_AV = COL_AK + DA_HEADS * 2 * DA_DK
COL_GA = COL_AV + DA_HEADS * DA_DV


def _params(*sem, vmem=None):
    return pltpu.CompilerParams(dimension_semantics=sem, vmem_limit_bytes=vmem)


def _rms_cast_kernel(x_ref, w_ref, o_ref):
    x = x_ref[...]
    y = x * lax.rsqrt(jnp.mean(x * x, axis=-1, keepdims=True) + RMS_EPS) * w_ref[...]
    o_ref[...] = y.astype(o_ref.dtype)


def rms_cast(x, w, tm):
    m, d = x.shape
    return pl.pallas_call(
        _rms_cast_kernel,
        grid=(m // tm,),
        in_specs=[pl.BlockSpec((tm, d), lambda i: (i, 0)),
                  pl.BlockSpec((1, d), lambda i: (0, 0))],
        out_specs=pl.BlockSpec((tm, d), lambda i: (i, 0)),
        out_shape=jax.ShapeDtypeStruct((m, d), BF16),
        compiler_params=_params("parallel"),
        name="rms_cast",
    )(x, w.reshape(1, d))


def _mm_fullk_kernel(a_ref, b_ref, *rest, has_res):
    acc = jnp.dot(a_ref[...], b_ref[...].astype(BF16), preferred_element_type=F32)
    if has_res:
        r_ref, o_ref = rest
        acc = acc + r_ref[...]
    else:
        o_ref, = rest
    o_ref[...] = acc.astype(o_ref.dtype)


def matmul_fullk(a, b, out_dtype, tm, tn, res=None, name="matmul", n_offset=0, n_blocks=None):
    m, kd = a.shape
    if n_blocks is None:
        n_blocks = b.shape[1] // tn
    in_specs = [pl.BlockSpec((tm, kd), lambda i, j: (i, 0)),
                pl.BlockSpec((kd, tn), lambda i, j: (0, n_offset + j))]
    args = [a, b]
    if res is not None:
        in_specs.append(pl.BlockSpec((tm, tn), lambda i, j: (i, j)))
        args.append(res)
    return pl.pallas_call(
        functools.partial(_mm_fullk_kernel, has_res=res is not None),
        grid=(m // tm, n_blocks),
        in_specs=in_specs,
        out_specs=pl.BlockSpec((tm, tn), lambda i, j: (i, j)),
        out_shape=jax.ShapeDtypeStruct((m, n_blocks * tn), out_dtype),
        compiler_params=_params("parallel", "parallel", vmem=VMEM_LIMIT),
        name=name,
    )(*args)


def _deltanet_kernel(qkv_ref, z_ref, ba_ref, qkvm_ref, bam_ref, cw_ref, hp_ref, nw_ref,
                     o_ref, full_ref, act_ref, ba_s, s_ref):
    t = pl.program_id(1)
    C = DN_CHUNK
    HIST = DN_HIST
    n_pad = C - N_META

    @pl.when(t == 0)
    def _():
        s_ref[...] = jnp.zeros_like(s_ref)
        full_ref[0:HIST + n_pad, :] = jnp.zeros((HIST + n_pad, 3 * DN_W), BF16)
        full_ref[HIST + n_pad:HIST + C, :] = qkvm_ref[...]
        ba_s[0:n_pad, :] = jnp.zeros((n_pad, LANES), F32)
        ba_s[n_pad:C, :] = bam_ref[...]

    @pl.when(t > 0)
    def _():
        full_ref[HIST:HIST + C, :] = qkv_ref[...]
        ba_s[...] = ba_ref[...]

    n_shift = DN_CONV - 1
    sr = lax.broadcasted_iota(jnp.int32, (n_shift * C, HIST + C), 0)
    sc = lax.broadcasted_iota(jnp.int32, (n_shift * C, HIST + C), 1)
    shift = jnp.where(sc == HIST - n_shift + (sr % C) + (sr // C), 1.0, 0.0).astype(BF16)
    for c0 in range(0, 3 * DN_W, DN_CONV_COLS):
        cs = slice(c0, c0 + DN_CONV_COLS)
        delayed = jnp.dot(shift, full_ref[:, cs], preferred_element_type=F32)
        conv = cw_ref[n_shift:n_shift + 1, cs] * full_ref[HIST:HIST + C, cs].astype(F32)
        for j in range(n_shift):
            conv = conv + cw_ref[j:j + 1, cs] * delayed[j * C:(j + 1) * C, :]
        act_ref[:, cs] = conv * jax.nn.sigmoid(conv)
    full_ref[0:HIST, :] = full_ref[C:C + HIST, :]

    row1 = lax.broadcasted_iota(jnp.int32, (C, 1), 0)
    valid = jnp.where((t > 0) | (row1 >= n_pad), 1.0, 0.0).astype(F32)
    ba = ba_s[...]
    beta_all = jax.nn.sigmoid(ba) * valid
    xg = ba + hp_ref[1:2, :]
    softplus = jnp.maximum(xg, 0.0) + jnp.log1p(jnp.exp(-jnp.abs(xg)))
    g_all = -jnp.exp(hp_ref[0:1, :]) * softplus * valid

    row = lax.broadcasted_iota(jnp.int32, (C, C), 0)
    col = lax.broadcasted_iota(jnp.int32, (C, C), 1)
    causal = row >= col
    strict = row > col
    eye = jnp.where(row == col, 1.0, 0.0).astype(F32)
    tril = jnp.where(causal, 1.0, 0.0).astype(BF16)

    g_hi = g_all.astype(BF16)
    r1 = g_all - g_hi.astype(F32)
    g_mid = r1.astype(BF16)
    g_lo = (r1 - g_mid.astype(F32)).astype(BF16)
    g3 = jnp.dot(tril, jnp.concatenate([g_hi, g_mid, g_lo], axis=1), preferred_element_type=F32)
    gcum = g3[:, 0:LANES] + g3[:, LANES:2 * LANES] + g3[:, 2 * LANES:3 * LANES]
    gcum_t = gcum.T

    scale = DN_DK ** -0.5
    nw = nw_ref[...]
    heads = range(DN_HEADS)

    def mm(a, b):
        return jnp.dot(a.astype(BF16), b.astype(BF16), preferred_element_type=F32)

    kn_b, lhs1, decay, rhs, qe_b, kdec_t, e_last = [], [], [], [], [], [], []
    for h in heads:
        qh = act_ref[:, h * DN_DK:(h + 1) * DN_DK]
        kh = act_ref[:, DN_W + h * DN_DK:DN_W + (h + 1) * DN_DK]
        vh = act_ref[:, 2 * DN_W + h * DN_DV:2 * DN_W + (h + 1) * DN_DV]
        qn = qh * (lax.rsqrt(jnp.sum(qh * qh, axis=-1, keepdims=True) + L2_EPS) * scale)
        kn = kh * lax.rsqrt(jnp.sum(kh * kh, axis=-1, keepdims=True) + L2_EPS)
        beta = beta_all[:, h:h + 1]
        gc = gcum[:, DN_HEADS + h:DN_HEADS + h + 1]
        gr = gcum_t[DN_HEADS + h:DN_HEADS + h + 1, :]
        g_last = gc[C - 1:C, :]
        e_g = jnp.exp(gc)
        kb = kn * beta
        kn_b.append(kn.astype(BF16))
        lhs1.append(jnp.concatenate([kb, qn], axis=0).astype(BF16))
        decay.append(jnp.exp(jnp.where(causal, gc - gr, NEG_BIG)))
        rhs.append(jnp.concatenate([vh * beta, kb * e_g], axis=1).astype(BF16))
        qe_b.append((qn * e_g).astype(BF16))
        kdec_t.append((kn * jnp.exp(g_last - gc)).T.astype(BF16))
        e_last.append(jnp.exp(g_last))

    kq = [lax.dot_general(lhs1[h], kn_b[h], (((1,), (1,)), ((), ())), preferred_element_type=F32)
          for h in heads]
    qk_b = [jnp.where(causal, kq[h][C:2 * C, :] * decay[h], 0.0).astype(BF16) for h in heads]
    b1 = [-jnp.where(strict, kq[h][0:C, :] * decay[h], 0.0) for h in heads]
    b2 = [mm(b1[h], b1[h]) for h in heads]
    b4 = [mm(b2[h], b2[h]) for h in heads]
    p1 = [(eye + b1[h]) + mm(eye + b1[h], b2[h]) for h in heads]
    b8 = [mm(b4[h], b4[h]) for h in heads]
    b16 = [mm(b8[h], b8[h]) for h in heads]
    p2 = [(eye + b4[h]) + mm(eye + b4[h], b8[h]) for h in heads]
    b32 = [mm(b16[h], b16[h]) for h in heads]
    p12 = [mm(p1[h], p2[h]) for h in heads]
    p3 = [(eye + b16[h]) + mm(eye + b16[h], b32[h]) for h in heads]
    inv = [mm(p12[h], p3[h]) for h in heads]
    sol = [mm(inv[h], rhs[h]) for h in heads]

    s_old = [s_ref[h] for h in heads]
    ws_lhs = [jnp.concatenate([sol[h][:, DN_DV:2 * DN_DV].astype(BF16), qe_b[h]], axis=0) for h in heads]
    ws = [mm(ws_lhs[h], s_old[h]) for h in heads]
    v_new = [(sol[h][:, 0:DN_DV] - ws[h][0:C, :]).astype(BF16) for h in heads]
    ov_lhs = [jnp.concatenate([qk_b[h], kdec_t[h]], axis=0) for h in heads]
    ov = [mm(ov_lhs[h], v_new[h]) for h in heads]
    for h in heads:
        s_ref[h] = s_old[h] * e_last[h] + ov[h][C:C + DN_DK, :]
        o = ws[h][C:2 * C, :] + ov[h][0:C, :]
        zh = z_ref[:, h * DN_DV:(h + 1) * DN_DV].astype(F32)
        o = o * lax.rsqrt(jnp.mean(o * o, axis=-1, keepdims=True) + RMS_EPS) * nw
        o_ref[:, h * DN_DV:(h + 1) * DN_DV] = (o * (zh * jax.nn.sigmoid(zh))).astype(o_ref.dtype)


def gated_deltanet(proj_x, ba_x, proj_m, ba_m, conv_w, a_log, dt_bias, norm_w, bsz, seq):
    C = DN_CHUNK
    n_chunks = seq // C
    w3 = 3 * DN_W
    hp = jnp.zeros((2, LANES), F32)
    hp = hp.at[0, DN_HEADS:2 * DN_HEADS].set(a_log.astype(F32))
    hp = hp.at[1, DN_HEADS:2 * DN_HEADS].set(dt_bias.astype(F32))

    def xrow(b, t):
        return b * n_chunks + jnp.maximum(t - 1, 0)

    return pl.pallas_call(
        _deltanet_kernel,
        grid=(bsz, n_chunks + 1),
        in_specs=[
            pl.BlockSpec((C, w3), lambda b, t: (xrow(b, t), 0)),
            pl.BlockSpec((C, DN_W), lambda b, t: (xrow(b, t), COL_DZ // DN_W)),
            pl.BlockSpec((C, LANES), lambda b, t: (xrow(b, t), 0)),
            pl.BlockSpec((N_META, w3), lambda b, t: (0, 0)),
            pl.BlockSpec((N_META, LANES), lambda b, t: (0, 0)),
            pl.BlockSpec((DN_CONV, w3), lambda b, t: (0, 0)),
            pl.BlockSpec((2, LANES), lambda b, t: (0, 0)),
            pl.BlockSpec((1, DN_DV), lambda b, t: (0, 0)),
        ],
        out_specs=pl.BlockSpec((C, DN_W), lambda b, t: (xrow(b, t), 0)),
        out_shape=jax.ShapeDtypeStruct((bsz * seq, DN_W), BF16),
        scratch_shapes=[
            pltpu.VMEM((DN_HIST + C, w3), BF16),
            pltpu.VMEM((C, w3), F32),
            pltpu.VMEM((C, LANES), F32),
            pltpu.VMEM((DN_HEADS, DN_DK, DN_DV), F32),
        ],
        compiler_params=_params("parallel", "arbitrary", vmem=VMEM_LIMIT),
        name="gated_deltanet",
    )(proj_x, proj_x, ba_x, proj_m, ba_m, conv_w.astype(F32), hp, norm_w.reshape(1, DN_DV).astype(F32))


def _diff_attn_kernel(q_ref, k_ref, v_ref, km_ref, vm_ref, qn_ref, kn_ref, lam_ref, sn_ref,
                      o_ref, kx_s, kmeta_s, *, tq, seq):
    qi = pl.program_id(2)

    def rms(x, w):
        return x * lax.rsqrt(jnp.mean(x * x, axis=-1, keepdims=True) + RMS_EPS) * w

    @pl.when(qi == 0)
    def _():
        for m in range(2):
            kx = k_ref[:, m * DA_DK:(m + 1) * DA_DK].astype(F32)
            kx_s[m] = rms(kx, kn_ref[...]).astype(BF16)
            kmx = km_ref[:, m * DA_DK:(m + 1) * DA_DK].astype(F32)
            kmeta_s[m] = rms(kmx, kn_ref[...]).astype(BF16)

    lam = (jnp.exp(jnp.sum(lam_ref[0:1, :] * lam_ref[1:2, :], axis=-1, keepdims=True))
           - jnp.exp(jnp.sum(lam_ref[2:3, :] * lam_ref[3:4, :], axis=-1, keepdims=True))
           + LAMBDA_INIT)
    row = lax.broadcasted_iota(jnp.int32, (tq, tq), 0)
    col = lax.broadcasted_iota(jnp.int32, (tq, tq), 1)
    diag_visible = col <= row
    v_m = vm_ref[...]
    scale = DA_DK ** -0.5 * math.log2(math.e)
    nt = (((1,), (1,)), ((), ()))

    def attend(n_prev):
        maps = range(2)
        qn = [(rms(q_ref[:, m * DA_DK:(m + 1) * DA_DK].astype(F32), qn_ref[...]) * scale).astype(BF16)
              for m in maps]
        sd = [jnp.where(diag_visible,
                        lax.dot_general(qn[m], kx_s[m, n_prev:n_prev + tq, :], nt,
                                        preferred_element_type=F32), NEG_BIG) for m in maps]
        sm = [lax.dot_general(qn[m], kmeta_s[m], nt, preferred_element_type=F32) for m in maps]
        mx = [jnp.maximum(jnp.max(sd[m], axis=-1, keepdims=True),
                          jnp.max(sm[m], axis=-1, keepdims=True)) for m in maps]
        if n_prev:
            sp = [lax.dot_general(qn[m], kx_s[m, 0:n_prev, :], nt, preferred_element_type=F32)
                  for m in maps]
            mx = [jnp.maximum(mx[m], jnp.max(sp[m], axis=-1, keepdims=True)) for m in maps]
        pd = [jnp.exp2(sd[m] - mx[m]) for m in maps]
        pm = [jnp.exp2(sm[m] - mx[m]) for m in maps]
        den = [jnp.sum(pd[m], axis=-1, keepdims=True) + jnp.sum(pm[m], axis=-1, keepdims=True)
               for m in maps]
        acc = [jnp.dot(pd[m].astype(BF16), v_ref[n_prev:n_prev + tq, :], preferred_element_type=F32)
               + jnp.dot(pm[m].astype(BF16), v_m, preferred_element_type=F32) for m in maps]
        if n_prev:
            pp = [jnp.exp2(sp[m] - mx[m]) for m in maps]
            den = [den[m] + jnp.sum(pp[m], axis=-1, keepdims=True) for m in maps]
            acc = [acc[m] + jnp.dot(pp[m].astype(BF16), v_ref[0:n_prev, :], preferred_element_type=F32)
                   for m in maps]
        out = acc[0] / den[0] - lam * (acc[1] / den[1])
        out = rms(out, sn_ref[...]) * (1.0 - LAMBDA_INIT)
        o_ref[...] = out.astype(o_ref.dtype)

    for blk in range(seq // tq):
        @pl.when(qi == blk)
        def _():
            attend(blk * tq)


def diff_attention(proj_x, proj_m, q_norm, k_norm, lam4, sub_norm, bsz, seq, tq=512):
    nq = seq // tq
    cq = COL_AQ // DA_DV
    ck = COL_AK // DA_DV
    cv = COL_AV // DA_DV
    return pl.pallas_call(
        functools.partial(_diff_attn_kernel, tq=tq, seq=seq),
        grid=(bsz, DA_HEADS, nq),
        in_specs=[
            pl.BlockSpec((tq, DA_DV), lambda b, h, i: (b * nq + i, cq + h)),
            pl.BlockSpec((seq, DA_DV), lambda b, h, i: (b, ck + h)),
            pl.BlockSpec((seq, DA_DV), lambda b, h, i: (b, cv + h)),
            pl.BlockSpec((N_META, DA_DV), lambda b, h, i: (0, h)),
            pl.BlockSpec((N_META, DA_DV), lambda b, h, i: (0, DA_HEADS + h)),
            pl.BlockSpec((1, DA_DK), lambda b, h, i: (0, 0)),
            pl.BlockSpec((1, DA_DK), lambda b, h, i: (0, 0)),
            pl.BlockSpec((4, DA_DK), lambda b, h, i: (0, 0)),
            pl.BlockSpec((1, DA_DV), lambda b, h, i: (0, 0)),
        ],
        out_specs=pl.BlockSpec((tq, DA_DV), lambda b, h, i: (b * nq + i, h)),
        out_shape=jax.ShapeDtypeStruct((bsz * seq, DA_HEADS * DA_DV), BF16),
        scratch_shapes=[pltpu.VMEM((2, seq, DA_DK), BF16),
                        pltpu.VMEM((2, N_META, DA_DK), BF16)],
        compiler_params=_params("parallel", "parallel", "arbitrary", vmem=VMEM_LIMIT),
        name="diff_attention",
    )(proj_x, proj_x, proj_x, proj_m, proj_m,
      q_norm.reshape(1, DA_DK).astype(F32), k_norm.reshape(1, DA_DK).astype(F32),
      lam4.astype(F32), sub_norm.reshape(1, DA_DV).astype(F32))


def _merge_kernel(ya_ref, yb_ref, wa_ref, wb_ref, ga_ref, gb_ref, o_ref):
    pa = jnp.dot(ya_ref[...], wa_ref[...].astype(BF16), preferred_element_type=F32)
    pb = jnp.dot(yb_ref[...], wb_ref[...].astype(BF16), preferred_element_type=F32)
    ga = jax.nn.sigmoid(ga_ref[...].astype(F32))
    gb = jax.nn.sigmoid(gb_ref[...].astype(F32))
    o_ref[...] = (ga * pa + gb * pb).astype(o_ref.dtype)


def branch_merge(y_a, y_b, w_a, w_b, proj_x, d_model, tm=1024, tn=512):
    m, ka = y_a.shape
    kb = y_b.shape[1]
    cga = COL_GA // tn
    cgb = (COL_GA + d_model) // tn
    return pl.pallas_call(
        _merge_kernel,
        grid=(m // tm, d_model // tn),
        in_specs=[
            pl.BlockSpec((tm, ka), lambda i, j: (i, 0)),
            pl.BlockSpec((tm, kb), lambda i, j: (i, 0)),
            pl.BlockSpec((ka, tn), lambda i, j: (0, j)),
            pl.BlockSpec((kb, tn), lambda i, j: (0, j)),
            pl.BlockSpec((tm, tn), lambda i, j: (i, cga + j)),
            pl.BlockSpec((tm, tn), lambda i, j: (i, cgb + j)),
        ],
        out_specs=pl.BlockSpec((tm, tn), lambda i, j: (i, j)),
        out_shape=jax.ShapeDtypeStruct((m, d_model), BF16),
        compiler_params=_params("parallel", "parallel", vmem=VMEM_LIMIT),
        name="branch_merge",
    )(y_a, y_b, w_a, w_b, proj_x, proj_x)


def _router_kernel(h_ref, nw_ref, whi_ref, wlo_ref, b_ref, idx_ref, gate_ref, up_ref):
    x = h_ref[...]
    u = x * lax.rsqrt(jnp.mean(x * x, axis=-1, keepdims=True) + RMS_EPS) * nw_ref[...]
    u_hi = u.astype(BF16)
    u_lo = (u - u_hi.astype(F32)).astype(BF16)
    half = u.shape[1] // 2
    bits = pltpu.bitcast(u_hi.astype(F32), jnp.uint32)
    up_ref[...] = bits[:, half:] | (bits[:, :half] >> 16)
    logits = (jnp.dot(u_hi, whi_ref[...], preferred_element_type=F32)
              + jnp.dot(u_lo, whi_ref[...], preferred_element_type=F32)
              + jnp.dot(u_hi, wlo_ref[...], preferred_element_type=F32)
              + b_ref[...])
    lane = lax.broadcasted_iota(jnp.int32, logits.shape, 1)
    lane_f = lane.astype(F32)
    vals = logits
    tops, idxs = [], []
    for _ in range(TOP_K):
        mx = jnp.max(vals, axis=-1, keepdims=True)
        ix = jnp.min(jnp.where(vals == mx, lane_f, float(LANES)), axis=-1, keepdims=True).astype(jnp.int32)
        tops.append(mx)
        idxs.append(ix)
        vals = jnp.where(lane == ix, -3.0e38, vals)
    exps = [jnp.exp(tv - tops[0]) for tv in tops]
    den = exps[0] + exps[1] + exps[2] + exps[3]
    idx_out = jnp.zeros(logits.shape, jnp.int32)
    gate_out = jnp.zeros(logits.shape, F32)
    for k in range(TOP_K):
        idx_out = jnp.where(lane == k, idxs[k], idx_out)
        gate_out = jnp.where(lane == k, exps[k] / den, gate_out)
    idx_ref[...] = idx_out
    gate_ref[...] = gate_out


def router(h2, ffn_norm, router_w, router_b, tm=256):
    n, d = h2.shape
    wpad = jnp.zeros((d, LANES), F32).at[:, :N_EXPERTS].set(router_w.astype(F32))
    w_hi = wpad.astype(BF16)
    w_lo = (wpad - w_hi.astype(F32)).astype(BF16)
    bias = jnp.full((1, LANES), NEG_BIG, F32).at[0, :N_EXPERTS].set(router_b.astype(F32))
    return pl.pallas_call(
        _router_kernel,
        grid=(n // tm,),
        in_specs=[
            pl.BlockSpec((tm, d), lambda i: (i, 0)),
            pl.BlockSpec((1, d), lambda i: (0, 0)),
            pl.BlockSpec((d, LANES), lambda i: (0, 0)),
            pl.BlockSpec((d, LANES), lambda i: (0, 0)),
            pl.BlockSpec((1, LANES), lambda i: (0, 0)),
        ],
        out_specs=[pl.BlockSpec((tm, LANES), lambda i: (i, 0)),
                   pl.BlockSpec((tm, LANES), lambda i: (i, 0)),
                   pl.BlockSpec((tm, d // 2), lambda i: (i, 0))],
        out_shape=[jax.ShapeDtypeStruct((n, LANES), jnp.int32),
                   jax.ShapeDtypeStruct((n, LANES), F32),
                   jax.ShapeDtypeStruct((n, d // 2), jnp.uint32)],
        compiler_params=_params("parallel", vmem=VMEM_LIMIT),
        name="router",
    )(h2, ffn_norm.reshape(1, d).astype(F32), w_hi, w_lo, bias)


def _expert_kernel(ie_ref, isb_ref, inb_ref, tok_ref, up_hbm, wgu_hbm, wd_hbm, bgu_ref, bd_ref,
                   ys_hbm, xbuf, act, ybuf, wgu_buf, wd_buf, wg_s, wu_s, wd_s,
                   sem_x, sem_y, sem_gu, sem_d, *, j1, j2, tn1, tn2, n_items):
    i = pl.program_id(0)
    nb = inb_ref[i]
    sb = isb_ref[i]
    ex = ie_ref[i]
    R = MOE_ROWS

    def gu_copies(expert, jt, slot):
        return [pltpu.make_async_copy(
            wgu_hbm.at[expert, :, pl.ds(pl.multiple_of(which * j1 * tn1 + jt * tn1, tn1), tn1)],
            wgu_buf.at[slot, which], sem_gu.at[slot]) for which in range(2)]

    def d_copy(expert, jd, slot):
        return pltpu.make_async_copy(
            wd_hbm.at[expert, :, pl.ds(pl.multiple_of(jd * tn2, tn2), tn2)],
            wd_buf.at[slot], sem_d.at[slot])

    rmax = xbuf.shape[0]
    rows_per_tile = rmax // j2
    g_rest = 2 * rows_per_tile // MOE_ITEM_BLOCKS
    g_first = rows_per_tile - g_rest * (MOE_ITEM_BLOCKS // 2)
    last_tok = tok_ref.shape[0] - 1

    def row_gather(base, r):
        tok = tok_ref[jnp.minimum(base + r, last_tok)]
        return pltpu.make_async_copy(up_hbm.at[pl.ds(tok, 1), :], xbuf.at[pl.ds(r, 1), :], sem_x)

    def x_wait_all():
        for r in range(MOE_ITEM_BLOCKS):
            pltpu.make_async_copy(up_hbm.at[pl.ds(0, R), :], xbuf.at[pl.ds(r * R, R), :],
                                  sem_x).wait()

    def y_copy(slot, r, jd):
        return pltpu.make_async_copy(
            ybuf.at[slot, pl.ds(r * R, R), :],
            ys_hbm.at[pl.ds((sb + r) * R, R), pl.ds(pl.multiple_of(jd * tn2, tn2), tn2)],
            sem_y.at[slot])

    def for_blocks(fn):
        for r in range(MOE_ITEM_BLOCKS):
            @pl.when(r < nb)
            def _():
                fn(r)

    @pl.when(i == 0)
    def _():
        def start(r, c):
            row_gather(sb * R, r).start()
            return c
        lax.fori_loop(0, rmax, start, 0, unroll=8)
        for cp in gu_copies(ex, 0, 0):
            cp.start()

    prev_active = inb_ref[jnp.maximum(i - 1, 0)] > 0

    @pl.when((i == 0) | prev_active)
    def _():
        x_wait_all()

    def gate_up_block(jt, r0, rows, wg, wu):
        x32 = xbuf[pl.ds(r0, rows), :]
        x_lo = pltpu.bitcast(x32 << 16, F32).astype(BF16)
        x_hi = pltpu.bitcast(x32 & jnp.uint32(0xFFFF0000), F32).astype(BF16)
        x = jnp.concatenate([x_lo, x_hi], axis=1)
        g = jnp.dot(x, wg, preferred_element_type=F32) + bgu_ref[pl.ds(jt, 1), :]
        u = jnp.dot(x, wu, preferred_element_type=F32) + bgu_ref[pl.ds(j1 + jt, 1), :]
        gt = jnp.minimum(g, SWIGLU_LIMIT)
        up = jnp.clip(u, -SWIGLU_LIMIT, SWIGLU_LIMIT)
        a = gt * jax.nn.sigmoid(SWIGLU_ALPHA * gt) * (up + 1.0)
        act[jt, pl.ds(r0, rows), :] = a.astype(BF16)

    def down_block(jd, slot, r0, rows, wd):
        lhs = jnp.concatenate([act[c, pl.ds(r0, rows), :] for c in range(j1)], axis=1)
        ybuf[slot, pl.ds(r0, rows), :] = (jnp.dot(lhs, wd, preferred_element_type=F32)
                                          + bd_ref[pl.ds(jd, 1), :])

    def rest_blocks(block_fn):
        rest = nb - 1

        def pair(p, c):
            block_fn(pl.multiple_of(R + p * 2 * R, R), 2 * R, p)
            return c

        lax.fori_loop(0, rest // 2, pair, 0)

        @pl.when(rest % 2 == 1)
        def _():
            block_fn(pl.multiple_of((nb - 1) * R, R), R, rest // 2)

    nxt = jnp.minimum(i + 1, n_items - 1)
    next_base = isb_ref[nxt] * R

    @pl.when(nb > 0)
    def _():
        d_copy(ex, 0, 0).start()

        def gate_up_tile(jt, carry):
            slot = jt % 2

            @pl.when(jt + 1 < j1)
            def _():
                for cp in gu_copies(ex, jt + 1, 1 - slot):
                    cp.start()

            for cp in gu_copies(ex, jt, slot):
                cp.wait()
            wg = wgu_buf[slot, 0].astype(BF16)
            wu = wgu_buf[slot, 1].astype(BF16)
            wg_s[...] = wg
            wu_s[...] = wu
            gate_up_block(jt, 0, R, wg, wu)
            rest_blocks(lambda r0, rows, k: gate_up_block(jt, r0, rows, wg_s[...], wu_s[...]))
            return carry

        lax.fori_loop(0, j1, gate_up_tile, 0)

        @pl.when(i + 1 < n_items)
        def _():
            @pl.when(inb_ref[i + 1] > 0)
            def _():
                for cp in gu_copies(ie_ref[i + 1], 0, 0):
                    cp.start()

        def down_tile(jd, carry):
            slot = jd % 2

            @pl.when(jd + 1 < j2)
            def _():
                d_copy(ex, jd + 1, 1 - slot).start()

            @pl.when(jd >= 2)
            def _():
                for_blocks(lambda r: y_copy(slot, r, jd).wait())

            d_copy(ex, jd, slot).wait()
            wd = wd_buf[slot].astype(BF16)
            wd_s[...] = wd
            down_block(jd, slot, 0, R, wd)
            tile_row0 = jd * rows_per_tile

            def gather_batch(first, count):
                for rr in range(count):
                    row_gather(next_base, tile_row0 + first + rr).start()

            gather_batch(0, g_first)

            def later_block(r0, rows, k):
                down_block(jd, slot, r0, rows, wd_s[...])
                gather_batch(g_first + k * g_rest, g_rest)

            rest_blocks(later_block)
            issued = g_first + g_rest * (nb // 2)

            def make_up(r, c):
                row_gather(next_base, tile_row0 + r).start()
                return c

            lax.fori_loop(jnp.minimum(issued, rows_per_tile), rows_per_tile, make_up, 0)
            for_blocks(lambda r: y_copy(slot, r, jd).start())
            return carry

        lax.fori_loop(0, j2, down_tile, 0)
        for_blocks(lambda r: y_copy(0, r, 0).wait())
        for_blocks(lambda r: y_copy(1, r, 0).wait())

        @pl.when(i == n_items - 1)
        def _():
            x_wait_all()


def expert_mlp(u_packed, row_tok, item_e, item_sb, item_nb, w_gate_up, b_gate_up, w_down, b_down,
               tn1=256, tn2=512):
    p = row_tok.shape[0]
    d = 2 * u_packed.shape[1]
    n_items = item_e.shape[0]
    de = w_down.shape[1]
    j1 = de // tn1
    j2 = d // tn2
    rmax = MOE_ITEM_BLOCKS * MOE_ROWS

    assert rmax % j2 == 0

    def bias_map(i, ie, isb, inb, tok):
        return (ie[i], 0, 0)

    return pl.pallas_call(
        functools.partial(_expert_kernel, j1=j1, j2=j2, tn1=tn1, tn2=tn2, n_items=n_items),
        grid_spec=pltpu.PrefetchScalarGridSpec(
            num_scalar_prefetch=4,
            grid=(n_items,),
            in_specs=[
                pl.BlockSpec(memory_space=pl.ANY),
                pl.BlockSpec(memory_space=pl.ANY),
                pl.BlockSpec(memory_space=pl.ANY),
                pl.BlockSpec((None, 2 * j1, tn1), bias_map),
                pl.BlockSpec((None, j2, tn2), bias_map),
            ],
            out_specs=pl.BlockSpec(memory_space=pl.ANY),
            scratch_shapes=[
                pltpu.VMEM((rmax, d // 2), jnp.uint32),
                pltpu.VMEM((j1, rmax, tn1), BF16),
                pltpu.VMEM((2, rmax, tn2), F32),
                pltpu.VMEM((2, 2, d, tn1), F32),
                pltpu.VMEM((2, de, tn2), F32),
                pltpu.VMEM((d, tn1), BF16),
                pltpu.VMEM((d, tn1), BF16),
                pltpu.VMEM((de, tn2), BF16),
                pltpu.SemaphoreType.DMA(()),
                pltpu.SemaphoreType.DMA((2,)),
                pltpu.SemaphoreType.DMA((2,)),
                pltpu.SemaphoreType.DMA((2,)),
            ],
        ),
        out_shape=jax.ShapeDtypeStruct((p, d), F32),
        compiler_params=_params("arbitrary", vmem=VMEM_LIMIT),
        name="expert_mlp",
    )(item_e, item_sb, item_nb, row_tok, u_packed, w_gate_up, w_down,
      b_gate_up.reshape(N_EXPERTS, 2 * j1, tn1), b_down.reshape(N_EXPERTS, j2, tn2))


def _combine_kernel(dest_ref, h_ref, g_ref, ys_hbm, o_ref, buf, sem, *, tt, n_tok, n_steps):
    i = pl.program_id(0)

    def issue(blk, slot):
        for k in range(TOP_K):
            def start(t, c):
                src = dest_ref[k * n_tok + blk * tt + t]
                pltpu.make_async_copy(ys_hbm.at[pl.ds(src, 1), :], buf.at[slot, k, pl.ds(t, 1), :],
                                      sem.at[slot]).start()
                return c
            lax.fori_loop(0, tt, start, 0, unroll=8)

    @pl.when(i == 0)
    def _():
        issue(0, 0)

    slot = i % 2

    @pl.when(i + 1 < n_steps)
    def _():
        issue(i + 1, 1 - slot)

    for k in range(TOP_K):
        pltpu.make_async_copy(ys_hbm.at[pl.ds(0, tt), :], buf.at[slot, k], sem.at[slot]).wait()
    acc = h_ref[...]
    g = g_ref[...]
    for k in range(TOP_K):
        acc = acc + g[:, k:k + 1] * buf[slot, k]
    o_ref[...] = acc


def combine(h2, gates, ys, dest_km, tt=64):
    n, d = h2.shape
    return pl.pallas_call(
        functools.partial(_combine_kernel, tt=tt, n_tok=n, n_steps=n // tt),
        grid_spec=pltpu.PrefetchScalarGridSpec(
            num_scalar_prefetch=1,
            grid=(n // tt,),
            in_specs=[pl.BlockSpec((tt, d), lambda i, dst: (i, 0)),
                      pl.BlockSpec((tt, LANES), lambda i, dst: (i, 0)),
                      pl.BlockSpec(memory_space=pl.ANY)],
            out_specs=pl.BlockSpec((tt, d), lambda i, dst: (i, 0)),
            scratch_shapes=[pltpu.VMEM((2, TOP_K, tt, d), F32), pltpu.SemaphoreType.DMA((2,))],
        ),
        out_shape=jax.ShapeDtypeStruct((n, d), F32),
        compiler_params=_params("arbitrary", vmem=VMEM_LIMIT),
        name="moe_combine",
    )(dest_km, h2, gates, ys)


def routing_tables(top_idx, n_tok):
    a = n_tok * TOP_K
    e_flat = top_idx.reshape(-1)
    onehot = (e_flat[:, None] == jnp.arange(N_EXPERTS, dtype=jnp.int32)[None, :]).astype(jnp.int32)
    csum = jnp.cumsum(onehot, axis=0)
    rank = jnp.sum(csum * onehot, axis=1) - 1
    counts = csum[-1]
    nblk = (counts + MOE_ROWS - 1) // MOE_ROWS
    blk_end = jnp.cumsum(nblk)
    blk_start = blk_end - nblk
    dest = blk_start[e_flat] * MOE_ROWS + rank
    n_blocks = -(-(a + N_EXPERTS * (MOE_ROWS - 1)) // MOE_ROWS)
    p = n_blocks * MOE_ROWS
    t_flat = jnp.arange(a, dtype=jnp.int32) // TOP_K
    row_tok = jnp.zeros((p,), jnp.int32).at[dest].set(t_flat)

    n_items = N_EXPERTS + n_blocks // MOE_ITEM_BLOCKS
    items_per_e = (nblk + MOE_ITEM_BLOCKS - 1) // MOE_ITEM_BLOCKS
    item_end = jnp.cumsum(items_per_e)
    item_start = item_end - items_per_e
    total_items = item_end[-1]
    ii = jnp.arange(n_items, dtype=jnp.int32)
    ic = jnp.minimum(ii, total_items - 1)
    e_i = jnp.minimum(jnp.searchsorted(item_end, ic, side='right'), N_EXPERTS - 1).astype(jnp.int32)
    local = ic - item_start[e_i]
    item_sb = (blk_start[e_i] + local * MOE_ITEM_BLOCKS).astype(jnp.int32)
    item_nb = jnp.clip(nblk[e_i] - local * MOE_ITEM_BLOCKS, 0, MOE_ITEM_BLOCKS)
    item_nb = jnp.where(ii < total_items, item_nb, 0).astype(jnp.int32)
    dest_km = dest.astype(jnp.int32).reshape(n_tok, TOP_K).T.reshape(-1)
    return dest_km, row_tok, e_i, item_sb, item_nb


def kernel(x, meta_tokens, mix_norm, w_in, dn_conv, dn_a_log, dn_dt_bias, dn_out_norm, da_q_norm, da_k_norm, da_lam_q1, da_lam_k1, da_lam_q2, da_lam_k2, da_sub_norm, w_branch_a, w_branch_b, w_out, ffn_norm, router_w, router_b, w_gate_up, b_gate_up, w_down, b_down):
    bsz, seq, d = x.shape
    n_tok = bsz * seq
    x2 = x.reshape(n_tok, d)
    layer = 0

    w_bf = w_in[layer].astype(BF16)
    w_da = w_bf[:, COL_BA + 2 * DN_HEADS:]
    tn = 1024

    u_x = rms_cast(x2, mix_norm[layer], 256)
    u_m = rms_cast(meta_tokens.astype(F32), mix_norm[layer], N_META)
    proj_dn = matmul_fullk(u_x, w_bf, BF16, 1024, tn, name="in_proj_dn", n_blocks=COL_BA // tn)
    proj_da = matmul_fullk(u_x, w_da, BF16, 1024, tn, name="in_proj_da")
    ba_x = matmul_fullk(u_x, w_bf, F32, 1024, LANES, name="in_proj_ba",
                        n_offset=COL_BA // LANES, n_blocks=1)
    proj_dn_m = matmul_fullk(u_m, w_bf, BF16, N_META, tn, name="in_proj_dn_meta", n_blocks=COL_DZ // tn)
    proj_kv_m = matmul_fullk(u_m, w_da, BF16, N_META, tn, name="in_proj_kv_meta",
                             n_offset=COL_AK // tn, n_blocks=(COL_GA - COL_AK) // tn)
    ba_m = matmul_fullk(u_m, w_bf, F32, N_META, LANES, name="in_proj_ba_meta",
                        n_offset=COL_BA // LANES, n_blocks=1)

    y_a = gated_deltanet(proj_dn, ba_x, proj_dn_m, ba_m, dn_conv[layer], dn_a_log[layer],
                         dn_dt_bias[layer], dn_out_norm[layer], bsz, seq)
    lam4 = jnp.stack([da_lam_q1[layer], da_lam_k1[layer], da_lam_q2[layer], da_lam_k2[layer]])
    y_b = diff_attention(proj_da, proj_kv_m, da_q_norm[layer], da_k_norm[layer], lam4,
                         da_sub_norm[layer], bsz, seq)

    merged = branch_merge(y_a, y_b, w_branch_a[layer], w_branch_b[layer], proj_da, d)
    h2 = matmul_fullk(merged, w_out[layer], F32, 1024, 512, res=x2, name="out_proj")

    top_idx, gates, u_packed = router(h2, ffn_norm[layer], router_w[layer], router_b[layer])
    dest, row_tok, item_e, item_sb, item_nb = routing_tables(top_idx[:, :TOP_K], n_tok)
    ys = expert_mlp(u_packed, row_tok, item_e, item_sb, item_nb, w_gate_up[layer], b_gate_up[layer],
                    w_down[layer], b_down[layer])
    out = combine(h2, gates, ys, dest)
    return out.reshape(bsz, seq, d)
```

```python
import functools
import math

import jax
import jax.numpy as jnp
from jax import lax
from jax.experimental import pallas as pl
from jax.experimental.pallas import tpu as pltpu

F32 = jnp.float32
BF16 = jnp.bfloat16

N_META = 16
RMS_EPS = 1e-6
L2_EPS = 1e-6

DN_HEADS = 16
DN_DK = 128
DN_DV = 128
DN_CONV = 4
DN_CHUNK = 64
DN_W = DN_HEADS * DN_DK
DN_HIST = 16
DN_CONV_COLS = 1024

DA_HEADS = 8
DA_DK = 128
DA_DV = 256
LAMBDA_INIT = 0.8 - 0.6 * math.exp(-0.3 * 0)

N_EXPERTS = 32
TOP_K = 4
D_EXPERT = 1536
SWIGLU_LIMIT = 7.0
SWIGLU_ALPHA = 1.702

LANES = 128
MOE_ROWS = 256
MOE_ITEM_BLOCKS = 5
MOE_DOWN_SLOTS = 3
MOE_DOWN_TILE = 512
NEG_BIG = -1e30
VMEM_LIMIT = 56 * 1024 * 1024

COL_DZ = 3 * DN_W
COL_BA = 4 * DN_W
COL_AQ = 0
COL_AK = COL_AQ + DA_HEADS * 2 * DA_DK
COL_AV = COL_AK + DA_HEADS * 2 * DA_DK
COL_GA = COL_AV + DA_HEADS * DA_DV


def _params(*sem, vmem=None):
    return pltpu.CompilerParams(dimension_semantics=sem, vmem_limit_bytes=vmem)


def _rms_cast_kernel(x_ref, w_ref, o_ref):
    x = x_ref[...]
    y = x * lax.rsqrt(jnp.mean(x * x, axis=-1, keepdims=True) + RMS_EPS) * w_ref[...]
    o_ref[...] = y.astype(o_ref.dtype)


def rms_cast(x, w, tm):
    m, d = x.shape
    return pl.pallas_call(
        _rms_cast_kernel,
        grid=(m // tm,),
        in_specs=[pl.BlockSpec((tm, d), lambda i: (i, 0)),
                  pl.BlockSpec((1, d), lambda i: (0, 0))],
        out_specs=pl.BlockSpec((tm, d), lambda i: (i, 0)),
        out_shape=jax.ShapeDtypeStruct((m, d), BF16),
        compiler_params=_params("parallel"),
        name="rms_cast",
    )(x, w.reshape(1, d))


def _mm_fullk_kernel(a_ref, b_ref, *rest, has_res):
    acc = jnp.dot(a_ref[...], b_ref[...].astype(BF16), preferred_element_type=F32)
    if has_res:
        r_ref, o_ref = rest
        acc = acc + r_ref[...]
    else:
        o_ref, = rest
    o_ref[...] = acc.astype(o_ref.dtype)


def matmul_fullk(a, b, out_dtype, tm, tn, res=None, name="matmul", n_offset=0, n_blocks=None):
    m, kd = a.shape
    if n_blocks is None:
        n_blocks = b.shape[1] // tn
    in_specs = [pl.BlockSpec((tm, kd), lambda i, j: (i, 0)),
                pl.BlockSpec((kd, tn), lambda i, j: (0, n_offset + j))]
    args = [a, b]
    if res is not None:
        in_specs.append(pl.BlockSpec((tm, tn), lambda i, j: (i, j)))
        args.append(res)
    return pl.pallas_call(
        functools.partial(_mm_fullk_kernel, has_res=res is not None),
        grid=(m // tm, n_blocks),
        in_specs=in_specs,
        out_specs=pl.BlockSpec((tm, tn), lambda i, j: (i, j)),
        out_shape=jax.ShapeDtypeStruct((m, n_blocks * tn), out_dtype),
        compiler_params=_params("parallel", "parallel", vmem=VMEM_LIMIT),
        name=name,
    )(*args)


def _deltanet_kernel(qkv_ref, z_ref, ba_ref, qkvm_ref, bam_ref, cw_ref, hp_ref, nw_ref,
                     o_ref, full_ref, act_ref, ba_s, s_ref):
    t = pl.program_id(1)
    C = DN_CHUNK
    HIST = DN_HIST
    n_pad = C - N_META

    @pl.when(t == 0)
    def _():
        s_ref[...] = jnp.zeros_like(s_ref)
        full_ref[0:HIST + n_pad, :] = jnp.zeros((HIST + n_pad, 3 * DN_W), BF16)
        full_ref[HIST + n_pad:HIST + C, :] = qkvm_ref[...]
        ba_s[0:n_pad, :] = jnp.zeros((n_pad, LANES), F32)
        ba_s[n_pad:C, :] = bam_ref[...]

    @pl.when(t > 0)
    def _():
        full_ref[HIST:HIST + C, :] = qkv_ref[...]
        ba_s[...] = ba_ref[...]

    n_shift = DN_CONV - 1
    sr = lax.broadcasted_iota(jnp.int32, (n_shift * C, HIST + C), 0)
    sc = lax.broadcasted_iota(jnp.int32, (n_shift * C, HIST + C), 1)
    shift = jnp.where(sc == HIST - n_shift + (sr % C) + (sr // C), 1.0, 0.0).astype(BF16)
    for c0 in range(0, 3 * DN_W, DN_CONV_COLS):
        cs = slice(c0, c0 + DN_CONV_COLS)
        delayed = jnp.dot(shift, full_ref[:, cs], preferred_element_type=F32)
        conv = cw_ref[n_shift:n_shift + 1, cs] * full_ref[HIST:HIST + C, cs].astype(F32)
        for j in range(n_shift):
            conv = conv + cw_ref[j:j + 1, cs] * delayed[j * C:(j + 1) * C, :]
        act_ref[:, cs] = conv * jax.nn.sigmoid(conv)
    full_ref[0:HIST, :] = full_ref[C:C + HIST, :]

    row1 = lax.broadcasted_iota(jnp.int32, (C, 1), 0)
    valid = jnp.where((t > 0) | (row1 >= n_pad), 1.0, 0.0).astype(F32)
    ba = ba_s[...]
    beta_all = jax.nn.sigmoid(ba) * valid
    xg = ba + hp_ref[1:2, :]
    softplus = jnp.maximum(xg, 0.0) + jnp.log1p(jnp.exp(-jnp.abs(xg)))
    g_all = -jnp.exp(hp_ref[0:1, :]) * softplus * valid

    row = lax.broadcasted_iota(jnp.int32, (C, C), 0)
    col = lax.broadcasted_iota(jnp.int32, (C, C), 1)
    causal = row >= col
    strict = row > col
    eye = jnp.where(row == col, 1.0, 0.0).astype(F32)
    tril = jnp.where(causal, 1.0, 0.0).astype(BF16)

    g_hi = g_all.astype(BF16)
    r1 = g_all - g_hi.astype(F32)
    g_mid = r1.astype(BF16)
    g_lo = (r1 - g_mid.astype(F32)).astype(BF16)
    g3 = jnp.dot(tril, jnp.concatenate([g_hi, g_mid, g_lo], axis=1), preferred_element_type=F32)
    gcum = g3[:, 0:LANES] + g3[:, LANES:2 * LANES] + g3[:, 2 * LANES:3 * LANES]
    gcum_t = gcum.T

    scale = DN_DK ** -0.5
    nw = nw_ref[...]
    heads = range(DN_HEADS)

    def mm(a, b):
        return jnp.dot(a.astype(BF16), b.astype(BF16), preferred_element_type=F32)

    kn_b, lhs1, decay, rhs, qe_b, kdec_t, e_last = [], [], [], [], [], [], []
    for h in heads:
        qh = act_ref[:, h * DN_DK:(h + 1) * DN_DK]
        kh = act_ref[:, DN_W + h * DN_DK:DN_W + (h + 1) * DN_DK]
        vh = act_ref[:, 2 * DN_W + h * DN_DV:2 * DN_W + (h + 1) * DN_DV]
        qn = qh * (lax.rsqrt(jnp.sum(qh * qh, axis=-1, keepdims=True) + L2_EPS) * scale)
        kn = kh * lax.rsqrt(jnp.sum(kh * kh, axis=-1, keepdims=True) + L2_EPS)
        beta = beta_all[:, h:h + 1]
        gc = gcum[:, DN_HEADS + h:DN_HEADS + h + 1]
        gr = gcum_t[DN_HEADS + h:DN_HEADS + h + 1, :]
        g_last = gc[C - 1:C, :]
        e_g = jnp.exp(gc)
        kb = kn * beta
        kn_b.append(kn.astype(BF16))
        lhs1.append(jnp.concatenate([kb, qn], axis=0).astype(BF16))
        decay.append(jnp.exp(jnp.where(causal, gc - gr, NEG_BIG)))
        rhs.append(jnp.concatenate([vh * beta, kb * e_g], axis=1).astype(BF16))
        qe_b.append((qn * e_g).astype(BF16))
        kdec_t.append((kn * jnp.exp(g_last - gc)).T.astype(BF16))
        e_last.append(jnp.exp(g_last))

    kq = [lax.dot_general(lhs1[h], kn_b[h], (((1,), (1,)), ((), ())), preferred_element_type=F32)
          for h in heads]
    qk_b = [jnp.where(causal, kq[h][C:2 * C, :] * decay[h], 0.0).astype(BF16) for h in heads]
    b1 = [-jnp.where(strict, kq[h][0:C, :] * decay[h], 0.0) for h in heads]
    b2 = [mm(b1[h], b1[h]) for h in heads]
    b4 = [mm(b2[h], b2[h]) for h in heads]
    p1 = [(eye + b1[h]) + mm(eye + b1[h], b2[h]) for h in heads]
    b8 = [mm(b4[h], b4[h]) for h in heads]
    b16 = [mm(b8[h], b8[h]) for h in heads]
    p2 = [(eye + b4[h]) + mm(eye + b4[h], b8[h]) for h in heads]
    b32 = [mm(b16[h], b16[h]) for h in heads]
    p12 = [mm(p1[h], p2[h]) for h in heads]
    p3 = [(eye + b16[h]) + mm(eye + b16[h], b32[h]) for h in heads]
    inv = [mm(p12[h], p3[h]) for h in heads]
    sol = [mm(inv[h], rhs[h]) for h in heads]

    s_old = [s_ref[h] for h in heads]
    ws_lhs = [jnp.concatenate([sol[h][:, DN_DV:2 * DN_DV].astype(BF16), qe_b[h]], axis=0) for h in heads]
    ws = [mm(ws_lhs[h], s_old[h]) for h in heads]
    v_new = [(sol[h][:, 0:DN_DV] - ws[h][0:C, :]).astype(BF16) for h in heads]
    ov_lhs = [jnp.concatenate([qk_b[h], kdec_t[h]], axis=0) for h in heads]
    ov = [mm(ov_lhs[h], v_new[h]) for h in heads]
    for h in heads:
        s_ref[h] = s_old[h] * e_last[h] + ov[h][C:C + DN_DK, :]
        o = ws[h][C:2 * C, :] + ov[h][0:C, :]
        zh = z_ref[:, h * DN_DV:(h + 1) * DN_DV].astype(F32)
        o = o * lax.rsqrt(jnp.mean(o * o, axis=-1, keepdims=True) + RMS_EPS) * nw
        o_ref[:, h * DN_DV:(h + 1) * DN_DV] = (o * (zh * jax.nn.sigmoid(zh))).astype(o_ref.dtype)


def gated_deltanet(proj_x, ba_x, proj_m, ba_m, conv_w, a_log, dt_bias, norm_w, bsz, seq):
    C = DN_CHUNK
    n_chunks = seq // C
    w3 = 3 * DN_W
    hp = jnp.zeros((2, LANES), F32)
    hp = hp.at[0, DN_HEADS:2 * DN_HEADS].set(a_log.astype(F32))
    hp = hp.at[1, DN_HEADS:2 * DN_HEADS].set(dt_bias.astype(F32))

    def xrow(b, t):
        return b * n_chunks + jnp.maximum(t - 1, 0)

    return pl.pallas_call(
        _deltanet_kernel,
        grid=(bsz, n_chunks + 1),
        in_specs=[
            pl.BlockSpec((C, w3), lambda b, t: (xrow(b, t), 0)),
            pl.BlockSpec((C, DN_W), lambda b, t: (xrow(b, t), COL_DZ // DN_W)),
            pl.BlockSpec((C, LANES), lambda b, t: (xrow(b, t), 0)),
            pl.BlockSpec((N_META, w3), lambda b, t: (0, 0)),
            pl.BlockSpec((N_META, LANES), lambda b, t: (0, 0)),
            pl.BlockSpec((DN_CONV, w3), lambda b, t: (0, 0)),
            pl.BlockSpec((2, LANES), lambda b, t: (0, 0)),
            pl.BlockSpec((1, DN_DV), lambda b, t: (0, 0)),
        ],
        out_specs=pl.BlockSpec((C, DN_W), lambda b, t: (xrow(b, t), 0)),
        out_shape=jax.ShapeDtypeStruct((bsz * seq, DN_W), BF16),
        scratch_shapes=[
            pltpu.VMEM((DN_HIST + C, w3), BF16),
            pltpu.VMEM((C, w3), F32),
            pltpu.VMEM((C, LANES), F32),
            pltpu.VMEM((DN_HEADS, DN_DK, DN_DV), F32),
        ],
        compiler_params=_params("parallel", "arbitrary", vmem=VMEM_LIMIT),
        name="gated_deltanet",
    )(proj_x, proj_x, ba_x, proj_m, ba_m, conv_w.astype(F32), hp, norm_w.reshape(1, DN_DV).astype(F32))


def _diff_attn_kernel(q_ref, k_ref, v_ref, km_ref, vm_ref, qn_ref, kn_ref, lam_ref, sn_ref,
                      o_ref, kx_s, kmeta_s, *, tq, seq):
    qi = pl.program_id(2)

    def rms(x, w):
        return x * lax.rsqrt(jnp.mean(x * x, axis=-1, keepdims=True) + RMS_EPS) * w

    @pl.when(qi == 0)
    def _():
        for m in range(2):
            kx = k_ref[:, m * DA_DK:(m + 1) * DA_DK].astype(F32)
            kx_s[m] = rms(kx, kn_ref[...]).astype(BF16)
            kmx = km_ref[:, m * DA_DK:(m + 1) * DA_DK].astype(F32)
            kmeta_s[m] = rms(kmx, kn_ref[...]).astype(BF16)

    lam = (jnp.exp(jnp.sum(lam_ref[0:1, :] * lam_ref[1:2, :], axis=-1, keepdims=True))
           - jnp.exp(jnp.sum(lam_ref[2:3, :] * lam_ref[3:4, :], axis=-1, keepdims=True))
           + LAMBDA_INIT)
    row = lax.broadcasted_iota(jnp.int32, (tq, tq), 0)
    col = lax.broadcasted_iota(jnp.int32, (tq, tq), 1)
    diag_visible = col <= row
    v_m = vm_ref[...]
    scale = DA_DK ** -0.5 * math.log2(math.e)
    nt = (((1,), (1,)), ((), ()))

    def attend(n_prev):
        maps = range(2)
        qn = [(rms(q_ref[:, m * DA_DK:(m + 1) * DA_DK].astype(F32), qn_ref[...]) * scale).astype(BF16)
              for m in maps]
        sd = [jnp.where(diag_visible,
                        lax.dot_general(qn[m], kx_s[m, n_prev:n_prev + tq, :], nt,
                                        preferred_element_type=F32), NEG_BIG) for m in maps]
        sm = [lax.dot_general(qn[m], kmeta_s[m], nt, preferred_element_type=F32) for m in maps]
        mx = [jnp.maximum(jnp.max(sd[m], axis=-1, keepdims=True),
                          jnp.max(sm[m], axis=-1, keepdims=True)) for m in maps]
        if n_prev:
            sp = [lax.dot_general(qn[m], kx_s[m, 0:n_prev, :], nt, preferred_element_type=F32)
                  for m in maps]
            mx = [jnp.maximum(mx[m], jnp.max(sp[m], axis=-1, keepdims=True)) for m in maps]
        pd = [jnp.exp2(sd[m] - mx[m]) for m in maps]
        pm = [jnp.exp2(sm[m] - mx[m]) for m in maps]
        den = [jnp.sum(pd[m], axis=-1, keepdims=True) + jnp.sum(pm[m], axis=-1, keepdims=True)
               for m in maps]
        acc = [jnp.dot(pd[m].astype(BF16), v_ref[n_prev:n_prev + tq, :], preferred_element_type=F32)
               + jnp.dot(pm[m].astype(BF16), v_m, preferred_element_type=F32) for m in maps]
        if n_prev:
            pp = [jnp.exp2(sp[m] - mx[m]) for m in maps]
            den = [den[m] + jnp.sum(pp[m], axis=-1, keepdims=True) for m in maps]
            acc = [acc[m] + jnp.dot(pp[m].astype(BF16), v_ref[0:n_prev, :], preferred_element_type=F32)
                   for m in maps]
        out = acc[0] / den[0] - lam * (acc[1] / den[1])
        out = rms(out, sn_ref[...]) * (1.0 - LAMBDA_INIT)
        o_ref[...] = out.astype(o_ref.dtype)

    for blk in range(seq // tq):
        @pl.when(qi == blk)
        def _():
            attend(blk * tq)


def diff_attention(proj_x, proj_m, q_norm, k_norm, lam4, sub_norm, bsz, seq, tq=512):
    nq = seq // tq
    cq = COL_AQ // DA_DV
    ck = COL_AK // DA_DV
    cv = COL_AV // DA_DV
    return pl.pallas_call(
        functools.partial(_diff_attn_kernel, tq=tq, seq=seq),
        grid=(bsz, DA_HEADS, nq),
        in_specs=[
            pl.BlockSpec((tq, DA_DV), lambda b, h, i: (b * nq + i, cq + h)),
            pl.BlockSpec((seq, DA_DV), lambda b, h, i: (b, ck + h)),
            pl.BlockSpec((seq, DA_DV), lambda b, h, i: (b, cv + h)),
            pl.BlockSpec((N_META, DA_DV), lambda b, h, i: (0, h)),
            pl.BlockSpec((N_META, DA_DV), lambda b, h, i: (0, DA_HEADS + h)),
            pl.BlockSpec((1, DA_DK), lambda b, h, i: (0, 0)),
            pl.BlockSpec((1, DA_DK), lambda b, h, i: (0, 0)),
            pl.BlockSpec((4, DA_DK), lambda b, h, i: (0, 0)),
            pl.BlockSpec((1, DA_DV), lambda b, h, i: (0, 0)),
        ],
        out_specs=pl.BlockSpec((tq, DA_DV), lambda b, h, i: (b * nq + i, h)),
        out_shape=jax.ShapeDtypeStruct((bsz * seq, DA_HEADS * DA_DV), BF16),
        scratch_shapes=[pltpu.VMEM((2, seq, DA_DK), BF16),
                        pltpu.VMEM((2, N_META, DA_DK), BF16)],
        compiler_params=_params("parallel", "parallel", "arbitrary", vmem=VMEM_LIMIT),
        name="diff_attention",
    )(proj_x, proj_x, proj_x, proj_m, proj_m,
      q_norm.reshape(1, DA_DK).astype(F32), k_norm.reshape(1, DA_DK).astype(F32),
      lam4.astype(F32), sub_norm.reshape(1, DA_DV).astype(F32))


def _merge_kernel(ya_ref, yb_ref, wa_ref, wb_ref, ga_ref, gb_ref, o_ref):
    pa = jnp.dot(ya_ref[...], wa_ref[...].astype(BF16), preferred_element_type=F32)
    pb = jnp.dot(yb_ref[...], wb_ref[...].astype(BF16), preferred_element_type=F32)
    ga = jax.nn.sigmoid(ga_ref[...].astype(F32))
    gb = jax.nn.sigmoid(gb_ref[...].astype(F32))
    o_ref[...] = (ga * pa + gb * pb).astype(o_ref.dtype)


def branch_merge(y_a, y_b, w_a, w_b, proj_x, d_model, tm=1024, tn=512):
    m, ka = y_a.shape
    kb = y_b.shape[1]
    cga = COL_GA // tn
    cgb = (COL_GA + d_model) // tn
    return pl.pallas_call(
        _merge_kernel,
        grid=(m // tm, d_model // tn),
        in_specs=[
            pl.BlockSpec((tm, ka), lambda i, j: (i, 0)),
            pl.BlockSpec((tm, kb), lambda i, j: (i, 0)),
            pl.BlockSpec((ka, tn), lambda i, j: (0, j)),
            pl.BlockSpec((kb, tn), lambda i, j: (0, j)),
            pl.BlockSpec((tm, tn), lambda i, j: (i, cga + j)),
            pl.BlockSpec((tm, tn), lambda i, j: (i, cgb + j)),
        ],
        out_specs=pl.BlockSpec((tm, tn), lambda i, j: (i, j)),
        out_shape=jax.ShapeDtypeStruct((m, d_model), BF16),
        compiler_params=_params("parallel", "parallel", vmem=VMEM_LIMIT),
        name="branch_merge",
    )(y_a, y_b, w_a, w_b, proj_x, proj_x)


def _router_kernel(h_ref, nw_ref, whi_ref, wlo_ref, b_ref, idx_ref, gate_ref, up_ref):
    x = h_ref[...]
    u = x * lax.rsqrt(jnp.mean(x * x, axis=-1, keepdims=True) + RMS_EPS) * nw_ref[...]
    u_hi = u.astype(BF16)
    u_lo = (u - u_hi.astype(F32)).astype(BF16)
    half = u.shape[1] // 2
    bits = pltpu.bitcast(u_hi.astype(F32), jnp.uint32)
    up_ref[...] = bits[:, half:] | (bits[:, :half] >> 16)
    logits = (jnp.dot(u_hi, whi_ref[...], preferred_element_type=F32)
              + jnp.dot(u_lo, whi_ref[...], preferred_element_type=F32)
              + jnp.dot(u_hi, wlo_ref[...], preferred_element_type=F32)
              + b_ref[...])
    lane = lax.broadcasted_iota(jnp.int32, logits.shape, 1)
    lane_f = lane.astype(F32)
    vals = logits
    tops, idxs = [], []
    for _ in range(TOP_K):
        mx = jnp.max(vals, axis=-1, keepdims=True)
        ix = jnp.min(jnp.where(vals == mx, lane_f, float(LANES)), axis=-1, keepdims=True).astype(jnp.int32)
        tops.append(mx)
        idxs.append(ix)
        vals = jnp.where(lane == ix, -3.0e38, vals)
    exps = [jnp.exp(tv - tops[0]) for tv in tops]
    den = exps[0] + exps[1] + exps[2] + exps[3]
    idx_out = jnp.zeros(logits.shape, jnp.int32)
    gate_out = jnp.zeros(logits.shape, F32)
    for k in range(TOP_K):
        idx_out = jnp.where(lane == k, idxs[k], idx_out)
        gate_out = jnp.where(lane == k, exps[k] / den, gate_out)
    idx_ref[...] = idx_out
    gate_ref[...] = gate_out


def router(h2, ffn_norm, router_w, router_b, tm=256):
    n, d = h2.shape
    wpad = jnp.zeros((d, LANES), F32).at[:, :N_EXPERTS].set(router_w.astype(F32))
    w_hi = wpad.astype(BF16)
    w_lo = (wpad - w_hi.astype(F32)).astype(BF16)
    bias = jnp.full((1, LANES), NEG_BIG, F32).at[0, :N_EXPERTS].set(router_b.astype(F32))
    return pl.pallas_call(
        _router_kernel,
        grid=(n // tm,),
        in_specs=[
            pl.BlockSpec((tm, d), lambda i: (i, 0)),
            pl.BlockSpec((1, d), lambda i: (0, 0)),
            pl.BlockSpec((d, LANES), lambda i: (0, 0)),
            pl.BlockSpec((d, LANES), lambda i: (0, 0)),
            pl.BlockSpec((1, LANES), lambda i: (0, 0)),
        ],
        out_specs=[pl.BlockSpec((tm, LANES), lambda i: (i, 0)),
                   pl.BlockSpec((tm, LANES), lambda i: (i, 0)),
                   pl.BlockSpec((tm, d // 2), lambda i: (i, 0))],
        out_shape=[jax.ShapeDtypeStruct((n, LANES), jnp.int32),
                   jax.ShapeDtypeStruct((n, LANES), F32),
                   jax.ShapeDtypeStruct((n, d // 2), jnp.uint32)],
        compiler_params=_params("parallel", vmem=VMEM_LIMIT),
        name="router",
    )(h2, ffn_norm.reshape(1, d).astype(F32), w_hi, w_lo, bias)


def _expert_kernel(ie_ref, isb_ref, inb_ref, tok_ref, up_hbm, wgu_hbm, wd_hbm, bgu_ref, bd_ref,
                   ys_hbm, xbuf, act, ybuf, wgu_buf, wd_buf, wg_s, wu_s, wd_s,
                   sem_x, sem_y, sem_gu, sem_d, *, j1, j2, tn1, tn2, n_items):
    i = pl.program_id(0)
    nb = inb_ref[i]
    sb = isb_ref[i]
    ex = ie_ref[i]
    R = MOE_ROWS

    def gu_copies(expert, jt, slot):
        return [pltpu.make_async_copy(
            wgu_hbm.at[expert, :, pl.ds(pl.multiple_of(which * j1 * tn1 + jt * tn1, tn1), tn1)],
            wgu_buf.at[slot, which], sem_gu.at[slot]) for which in range(2)]

    def d_copy(expert, jd, slot):
        return pltpu.make_async_copy(
            wd_hbm.at[expert, :, pl.ds(pl.multiple_of(jd * tn2, tn2), tn2)],
            wd_buf.at[slot], sem_d.at[slot])

    rmax = xbuf.shape[0]
    rows_per_tile = rmax // j2
    g_rest = 2 * rows_per_tile // MOE_ITEM_BLOCKS
    g_first = rows_per_tile - g_rest * (MOE_ITEM_BLOCKS // 2)
    last_tok = tok_ref.shape[0] - 1

    def row_gather(base, r):
        tok = tok_ref[jnp.minimum(base + r, last_tok)]
        return pltpu.make_async_copy(up_hbm.at[pl.ds(tok, 1), :], xbuf.at[pl.ds(r, 1), :], sem_x)

    def x_wait_all():
        for r in range(MOE_ITEM_BLOCKS):
            pltpu.make_async_copy(up_hbm.at[pl.ds(0, R), :], xbuf.at[pl.ds(r * R, R), :],
                                  sem_x).wait()

    yw = tn2 // 2

    def y_copy(slot, r, jd):
        return pltpu.make_async_copy(
            ybuf.at[slot, pl.ds(r * R, R), :],
            ys_hbm.at[pl.ds((sb + r) * R, R), pl.ds(pl.multiple_of(jd * yw, yw), yw)],
            sem_y.at[slot])

    def for_blocks(fn):
        for r in range(MOE_ITEM_BLOCKS):
            @pl.when(r < nb)
            def _():
                fn(r)

    @pl.when(i == 0)
    def _():
        def start(r, c):
            row_gather(sb * R, r).start()
            return c
        lax.fori_loop(0, rmax, start, 0, unroll=8)
        for cp in gu_copies(ex, 0, 0):
            cp.start()

    prev_active = inb_ref[jnp.maximum(i - 1, 0)] > 0

    @pl.when((i == 0) | prev_active)
    def _():
        x_wait_all()

    def gate_up_block(jt, r0, rows, wg, wu):
        x32 = xbuf[pl.ds(r0, rows), :]
        x_lo = pltpu.bitcast(x32 << 16, F32).astype(BF16)
        x_hi = pltpu.bitcast(x32 & jnp.uint32(0xFFFF0000), F32).astype(BF16)
        x = jnp.concatenate([x_lo, x_hi], axis=1)
        g = jnp.dot(x, wg, preferred_element_type=F32) + bgu_ref[pl.ds(jt, 1), :]
        u = jnp.dot(x, wu, preferred_element_type=F32) + bgu_ref[pl.ds(j1 + jt, 1), :]
        gt = jnp.minimum(g, SWIGLU_LIMIT)
        up = jnp.clip(u, -SWIGLU_LIMIT, SWIGLU_LIMIT)
        a = gt * jax.nn.sigmoid(SWIGLU_ALPHA * gt) * (up + 1.0)
        act[jt, pl.ds(r0, rows), :] = a.astype(BF16)

    def down_block(jd, slot, r0, rows, wd):
        lhs = jnp.concatenate([act[c, pl.ds(r0, rows), :] for c in range(j1)], axis=1)
        y = jnp.dot(lhs, wd, preferred_element_type=F32) + bd_ref[pl.ds(jd, 1), :]
        bits = pltpu.bitcast(y.astype(BF16).astype(F32), jnp.uint32)
        ybuf[slot, pl.ds(r0, rows), :] = bits[:, yw:] | (bits[:, :yw] >> 16)

    def rest_blocks(block_fn):
        rest = nb - 1

        def pair(p, c):
            block_fn(pl.multiple_of(R + p * 2 * R, R), 2 * R, p)
            return c

        lax.fori_loop(0, rest // 2, pair, 0)

        @pl.when(rest % 2 == 1)
        def _():
            block_fn(pl.multiple_of((nb - 1) * R, R), R, rest // 2)

    nxt = jnp.minimum(i + 1, n_items - 1)
    next_base = isb_ref[nxt] * R

    n_wd = wd_buf.shape[0]

    @pl.when(nb > 0)
    def _():
        for t0 in range(n_wd - 1):
            d_copy(ex, t0, t0).start()

        def gate_up_tile(jt, carry):
            slot = jt % 2

            @pl.when(jt + 1 < j1)
            def _():
                for cp in gu_copies(ex, jt + 1, 1 - slot):
                    cp.start()

            for cp in gu_copies(ex, jt, slot):
                cp.wait()
            wg = wgu_buf[slot, 0].astype(BF16)
            wu = wgu_buf[slot, 1].astype(BF16)
            wg_s[...] = wg
            wu_s[...] = wu
            gate_up_block(jt, 0, R, wg, wu)
            rest_blocks(lambda r0, rows, k: gate_up_block(jt, r0, rows, wg_s[...], wu_s[...]))
            return carry

        lax.fori_loop(0, j1, gate_up_tile, 0)

        @pl.when(i + 1 < n_items)
        def _():
            @pl.when(inb_ref[i + 1] > 0)
            def _():
                for cp in gu_copies(ie_ref[i + 1], 0, 0):
                    cp.start()

        def down_tile(jd, carry):
            slot = jd % 2
            wslot = jd % n_wd

            @pl.when(jd + n_wd - 1 < j2)
            def _():
                d_copy(ex, jd + n_wd - 1, (jd + n_wd - 1) % n_wd).start()

            @pl.when(jd >= 2)
            def _():
                for_blocks(lambda r: y_copy(slot, r, jd).wait())

            d_copy(ex, jd, wslot).wait()
            wd = wd_buf[wslot].astype(BF16)
            wd_s[...] = wd
            down_block(jd, slot, 0, R, wd)
            tile_row0 = jd * rows_per_tile

            def gather_batch(first, count):
                for rr in range(count):
                    row_gather(next_base, tile_row0 + first + rr).start()

            gather_batch(0, g_first)

            def later_block(r0, rows, k):
                down_block(jd, slot, r0, rows, wd_s[...])
                gather_batch(g_first + k * g_rest, g_rest)

            rest_blocks(later_block)
            issued = g_first + g_rest * (nb // 2)

            def make_up(r, c):
                row_gather(next_base, tile_row0 + r).start()
                return c

            lax.fori_loop(jnp.minimum(issued, rows_per_tile), rows_per_tile, make_up, 0)
            for_blocks(lambda r: y_copy(slot, r, jd).start())
            return carry

        lax.fori_loop(0, j2, down_tile, 0)
        for_blocks(lambda r: y_copy(0, r, 0).wait())
        for_blocks(lambda r: y_copy(1, r, 0).wait())

        @pl.when(i == n_items - 1)
        def _():
            x_wait_all()


def expert_mlp(u_packed, row_tok, item_e, item_sb, item_nb, w_gate_up, b_gate_up, w_down, b_down,
               tn1=256, tn2=MOE_DOWN_TILE):
    p = row_tok.shape[0]
    d = 2 * u_packed.shape[1]
    n_items = item_e.shape[0]
    de = w_down.shape[1]
    j1 = de // tn1
    j2 = d // tn2
    rmax = MOE_ITEM_BLOCKS * MOE_ROWS

    assert rmax % j2 == 0

    def bias_map(i, ie, isb, inb, tok):
        return (ie[i], 0, 0)

    return pl.pallas_call(
        functools.partial(_expert_kernel, j1=j1, j2=j2, tn1=tn1, tn2=tn2, n_items=n_items),
        grid_spec=pltpu.PrefetchScalarGridSpec(
            num_scalar_prefetch=4,
            grid=(n_items,),
            in_specs=[
                pl.BlockSpec(memory_space=pl.ANY),
                pl.BlockSpec(memory_space=pl.ANY),
                pl.BlockSpec(memory_space=pl.ANY),
                pl.BlockSpec((None, 2 * j1, tn1), bias_map),
                pl.BlockSpec((None, j2, tn2), bias_map),
            ],
            out_specs=pl.BlockSpec(memory_space=pl.ANY),
            scratch_shapes=[
                pltpu.VMEM((rmax, d // 2), jnp.uint32),
                pltpu.VMEM((j1, rmax, tn1), BF16),
                pltpu.VMEM((2, rmax, tn2 // 2), jnp.uint32),
                pltpu.VMEM((2, 2, d, tn1), F32),
                pltpu.VMEM((MOE_DOWN_SLOTS, de, tn2), F32),
                pltpu.VMEM((d, tn1), BF16),
                pltpu.VMEM((d, tn1), BF16),
                pltpu.VMEM((de, tn2), BF16),
                pltpu.SemaphoreType.DMA(()),
                pltpu.SemaphoreType.DMA((2,)),
                pltpu.SemaphoreType.DMA((2,)),
                pltpu.SemaphoreType.DMA((MOE_DOWN_SLOTS,)),
            ],
        ),
        out_shape=jax.ShapeDtypeStruct((p, d // 2), jnp.uint32),
        compiler_params=_params("arbitrary", vmem=VMEM_LIMIT),
        name="expert_mlp",
    )(item_e, item_sb, item_nb, row_tok, u_packed, w_gate_up, w_down,
      b_gate_up.reshape(N_EXPERTS, 2 * j1, tn1), b_down.reshape(N_EXPERTS, j2, tn2))


def _combine_kernel(dest_ref, h_ref, g_ref, ys_hbm, o_ref, buf, sem, *, tt, n_tok, n_steps, yw):
    i = pl.program_id(0)

    def issue(blk, slot):
        for k in range(TOP_K):
            def start(t, c):
                src = dest_ref[k * n_tok + blk * tt + t]
                pltpu.make_async_copy(ys_hbm.at[pl.ds(src, 1), :], buf.at[slot, k, pl.ds(t, 1), :],
                                      sem.at[slot]).start()
                return c
            lax.fori_loop(0, tt, start, 0, unroll=8)

    @pl.when(i == 0)
    def _():
        issue(0, 0)

    slot = i % 2

    @pl.when(i + 1 < n_steps)
    def _():
        issue(i + 1, 1 - slot)

    for k in range(TOP_K):
        pltpu.make_async_copy(ys_hbm.at[pl.ds(0, tt), :], buf.at[slot, k], sem.at[slot]).wait()
    g = g_ref[...]
    lo = None
    hi = None
    for k in range(TOP_K):
        word = buf[slot, k]
        gk = g[:, k:k + 1]
        lo_k = gk * pltpu.bitcast(word << 16, F32)
        hi_k = gk * pltpu.bitcast(word & jnp.uint32(0xFFFF0000), F32)
        lo = lo_k if lo is None else lo + lo_k
        hi = hi_k if hi is None else hi + hi_k
    for c in range(lo.shape[1] // yw):
        o_ref[:, 2 * c * yw:(2 * c + 1) * yw] = (h_ref[:, 2 * c * yw:(2 * c + 1) * yw]
                                                  + lo[:, c * yw:(c + 1) * yw])
        o_ref[:, (2 * c + 1) * yw:(2 * c + 2) * yw] = (h_ref[:, (2 * c + 1) * yw:(2 * c + 2) * yw]
                                                        + hi[:, c * yw:(c + 1) * yw])


def combine(h2, gates, ys, dest_km, yw, tt=128):
    n, d = h2.shape
    return pl.pallas_call(
        functools.partial(_combine_kernel, tt=tt, n_tok=n, n_steps=n // tt, yw=yw),
        grid_spec=pltpu.PrefetchScalarGridSpec(
            num_scalar_prefetch=1,
            grid=(n // tt,),
            in_specs=[pl.BlockSpec((tt, d), lambda i, dst: (i, 0)),
                      pl.BlockSpec((tt, LANES), lambda i, dst: (i, 0)),
                      pl.BlockSpec(memory_space=pl.ANY)],
            out_specs=pl.BlockSpec((tt, d), lambda i, dst: (i, 0)),
            scratch_shapes=[pltpu.VMEM((2, TOP_K, tt, d // 2), jnp.uint32),
                            pltpu.SemaphoreType.DMA((2,))],
        ),
        out_shape=jax.ShapeDtypeStruct((n, d), F32),
        compiler_params=_params("arbitrary", vmem=VMEM_LIMIT),
        name="moe_combine",
    )(dest_km, h2, gates, ys)


def routing_tables(top_idx, n_tok):
    a = n_tok * TOP_K
    e_flat = top_idx.reshape(-1)
    onehot = (e_flat[:, None] == jnp.arange(N_EXPERTS, dtype=jnp.int32)[None, :]).astype(jnp.int32)
    csum = jnp.cumsum(onehot, axis=0)
    rank = jnp.sum(csum * onehot, axis=1) - 1
    counts = csum[-1]
    nblk = (counts + MOE_ROWS - 1) // MOE_ROWS
    blk_end = jnp.cumsum(nblk)
    blk_start = blk_end - nblk
    dest = blk_start[e_flat] * MOE_ROWS + rank
    n_blocks = -(-(a + N_EXPERTS * (MOE_ROWS - 1)) // MOE_ROWS)
    p = n_blocks * MOE_ROWS
    t_flat = jnp.arange(a, dtype=jnp.int32) // TOP_K
    row_tok = jnp.zeros((p,), jnp.int32).at[dest].set(t_flat)

    n_items = N_EXPERTS + n_blocks // MOE_ITEM_BLOCKS
    items_per_e = (nblk + MOE_ITEM_BLOCKS - 1) // MOE_ITEM_BLOCKS
    item_end = jnp.cumsum(items_per_e)
    item_start = item_end - items_per_e
    total_items = item_end[-1]
    ii = jnp.arange(n_items, dtype=jnp.int32)
    ic = jnp.minimum(ii, total_items - 1)
    e_i = jnp.minimum(jnp.searchsorted(item_end, ic, side='right'), N_EXPERTS - 1).astype(jnp.int32)
    local = ic - item_start[e_i]
    item_sb = (blk_start[e_i] + local * MOE_ITEM_BLOCKS).astype(jnp.int32)
    item_nb = jnp.clip(nblk[e_i] - local * MOE_ITEM_BLOCKS, 0, MOE_ITEM_BLOCKS)
    item_nb = jnp.where(ii < total_items, item_nb, 0).astype(jnp.int32)
    dest_km = dest.astype(jnp.int32).reshape(n_tok, TOP_K).T.reshape(-1)
    return dest_km, row_tok, e_i, item_sb, item_nb


def kernel(x, meta_tokens, mix_norm, w_in, dn_conv, dn_a_log, dn_dt_bias, dn_out_norm, da_q_norm, da_k_norm, da_lam_q1, da_lam_k1, da_lam_q2, da_lam_k2, da_sub_norm, w_branch_a, w_branch_b, w_out, ffn_norm, router_w, router_b, w_gate_up, b_gate_up, w_down, b_down):
    bsz, seq, d = x.shape
    n_tok = bsz * seq
    x2 = x.reshape(n_tok, d)
    layer = 0

    w_bf = w_in[layer].astype(BF16)
    w_da = w_bf[:, COL_BA + 2 * DN_HEADS:]
    tn = 1024

    u_x = rms_cast(x2, mix_norm[layer], 256)
    u_m = rms_cast(meta_tokens.astype(F32), mix_norm[layer], N_META)
    proj_dn = matmul_fullk(u_x, w_bf, BF16, 1024, tn, name="in_proj_dn", n_blocks=COL_BA // tn)
    proj_da = matmul_fullk(u_x, w_da, BF16, 1024, tn, name="in_proj_da")
    ba_x = matmul_fullk(u_x, w_bf, F32, 1024, LANES, name="in_proj_ba",
                        n_offset=COL_BA // LANES, n_blocks=1)
    proj_dn_m = matmul_fullk(u_m, w_bf, BF16, N_META, tn, name="in_proj_dn_meta", n_blocks=COL_DZ // tn)
    proj_kv_m = matmul_fullk(u_m, w_da, BF16, N_META, tn, name="in_proj_kv_meta",
                             n_offset=COL_AK // tn, n_blocks=(COL_GA - COL_AK) // tn)
    ba_m = matmul_fullk(u_m, w_bf, F32, N_META, LANES, name="in_proj_ba_meta",
                        n_offset=COL_BA // LANES, n_blocks=1)

    y_a = gated_deltanet(proj_dn, ba_x, proj_dn_m, ba_m, dn_conv[layer], dn_a_log[layer],
                         dn_dt_bias[layer], dn_out_norm[layer], bsz, seq)
    lam4 = jnp.stack([da_lam_q1[layer], da_lam_k1[layer], da_lam_q2[layer], da_lam_k2[layer]])
    y_b = diff_attention(proj_da, proj_kv_m, da_q_norm[layer], da_k_norm[layer], lam4,
                         da_sub_norm[layer], bsz, seq)

    merged = branch_merge(y_a, y_b, w_branch_a[layer], w_branch_b[layer], proj_da, d)
    h2 = matmul_fullk(merged, w_out[layer], F32, 1024, 512, res=x2, name="out_proj")

    top_idx, gates, u_packed = router(h2, ffn_norm[layer], router_w[layer], router_b[layer])
    dest, row_tok, item_e, item_sb, item_nb = routing_tables(top_idx[:, :TOP_K], n_tok)
    ys = expert_mlp(u_packed, row_tok, item_e, item_sb, item_nb, w_gate_up[layer], b_gate_up[layer],
                    w_down[layer], b_down[layer])
    out = combine(h2, gates, ys, dest, MOE_DOWN_TILE // 2)
    return out.reshape(bsz, seq, d)
```

```python
import functools
import math

import jax
import jax.numpy as jnp
from jax import lax
from jax.experimental import pallas as pl
from jax.experimental.pallas import tpu as pltpu

F32 = jnp.float32
BF16 = jnp.bfloat16

N_META = 16
RMS_EPS = 1e-6
L2_EPS = 1e-6

DN_HEADS = 16
DN_DK = 128
DN_DV = 128
DN_CONV = 4
DN_CHUNK = 64
DN_W = DN_HEADS * DN_DK
DN_HIST = 16
DN_CONV_COLS = 1024

DA_HEADS = 8
DA_DK = 128
DA_DV = 256
LAMBDA_INIT = 0.8 - 0.6 * math.exp(-0.3 * 0)

N_EXPERTS = 32
TOP_K = 4
D_EXPERT = 1536
SWIGLU_LIMIT = 7.0
SWIGLU_ALPHA = 1.702

LANES = 128
MOE_ROWS = 128
MOE_ITEM_BLOCKS = 10
MOE_HEAD_UNITS = 2
MOE_BODY_UNITS = 4
MOE_DOWN_SLOTS = 3
MOE_DOWN_TILE = 512
NEG_BIG = -1e30
VMEM_LIMIT = 56 * 1024 * 1024

COL_DZ = 3 * DN_W
COL_BA = 4 * DN_W
COL_AQ = 0
COL_AK = COL_AQ + DA_HEADS * 2 * DA_DK
COL_AV = COL_AK + DA_HEADS * 2 * DA_DK
COL_GA = COL_AV + DA_HEADS * DA_DV


def _params(*sem, vmem=None):
    return pltpu.CompilerParams(dimension_semantics=sem, vmem_limit_bytes=vmem)


def _rms_cast_kernel(x_ref, w_ref, o_ref):
    x = x_ref[...]
    y = x * lax.rsqrt(jnp.mean(x * x, axis=-1, keepdims=True) + RMS_EPS) * w_ref[...]
    o_ref[...] = y.astype(o_ref.dtype)


def rms_cast(x, w, tm):
    m, d = x.shape
    return pl.pallas_call(
        _rms_cast_kernel,
        grid=(m // tm,),
        in_specs=[pl.BlockSpec((tm, d), lambda i: (i, 0)),
                  pl.BlockSpec((1, d), lambda i: (0, 0))],
        out_specs=pl.BlockSpec((tm, d), lambda i: (i, 0)),
        out_shape=jax.ShapeDtypeStruct((m, d), BF16),
        compiler_params=_params("parallel"),
        name="rms_cast",
    )(x, w.reshape(1, d))


def _mm_fullk_kernel(a_ref, b_ref, *rest, has_res):
    acc = jnp.dot(a_ref[...], b_ref[...].astype(BF16), preferred_element_type=F32)
    if has_res:
        r_ref, o_ref = rest
        acc = acc + r_ref[...]
    else:
        o_ref, = rest
    o_ref[...] = acc.astype(o_ref.dtype)


def matmul_fullk(a, b, out_dtype, tm, tn, res=None, name="matmul", n_offset=0, n_blocks=None):
    m, kd = a.shape
    if n_blocks is None:
        n_blocks = b.shape[1] // tn
    in_specs = [pl.BlockSpec((tm, kd), lambda i, j: (i, 0)),
                pl.BlockSpec((kd, tn), lambda i, j: (0, n_offset + j))]
    args = [a, b]
    if res is not None:
        in_specs.append(pl.BlockSpec((tm, tn), lambda i, j: (i, j)))
        args.append(res)
    return pl.pallas_call(
        functools.partial(_mm_fullk_kernel, has_res=res is not None),
        grid=(m // tm, n_blocks),
        in_specs=in_specs,
        out_specs=pl.BlockSpec((tm, tn), lambda i, j: (i, j)),
        out_shape=jax.ShapeDtypeStruct((m, n_blocks * tn), out_dtype),
        compiler_params=_params("parallel", "parallel", vmem=VMEM_LIMIT),
        name=name,
    )(*args)


def _deltanet_kernel(qkv_ref, z_ref, ba_ref, qkvm_ref, bam_ref, cw_ref, hp_ref, nw_ref,
                     o_ref, full_ref, act_ref, ba_s, s_ref):
    t = pl.program_id(1)
    C = DN_CHUNK
    HIST = DN_HIST
    n_pad = C - N_META

    @pl.when(t == 0)
    def _():
        s_ref[...] = jnp.zeros_like(s_ref)
        full_ref[0:HIST + n_pad, :] = jnp.zeros((HIST + n_pad, 3 * DN_W), BF16)
        full_ref[HIST + n_pad:HIST + C, :] = qkvm_ref[...]
        ba_s[0:n_pad, :] = jnp.zeros((n_pad, LANES), F32)
        ba_s[n_pad:C, :] = bam_ref[...]

    @pl.when(t > 0)
    def _():
        full_ref[HIST:HIST + C, :] = qkv_ref[...]
        ba_s[...] = ba_ref[...]

    n_shift = DN_CONV - 1
    sr = lax.broadcasted_iota(jnp.int32, (n_shift * C, HIST + C), 0)
    sc = lax.broadcasted_iota(jnp.int32, (n_shift * C, HIST + C), 1)
    shift = jnp.where(sc == HIST - n_shift + (sr % C) + (sr // C), 1.0, 0.0).astype(BF16)
    for c0 in range(0, 3 * DN_W, DN_CONV_COLS):
        cs = slice(c0, c0 + DN_CONV_COLS)
        delayed = jnp.dot(shift, full_ref[:, cs], preferred_element_type=F32)
        conv = cw_ref[n_shift:n_shift + 1, cs] * full_ref[HIST:HIST + C, cs].astype(F32)
        for j in range(n_shift):
            conv = conv + cw_ref[j:j + 1, cs] * delayed[j * C:(j + 1) * C, :]
        act_ref[:, cs] = conv * jax.nn.sigmoid(conv)
    full_ref[0:HIST, :] = full_ref[C:C + HIST, :]

    row1 = lax.broadcasted_iota(jnp.int32, (C, 1), 0)
    valid = jnp.where((t > 0) | (row1 >= n_pad), 1.0, 0.0).astype(F32)
    ba = ba_s[...]
    beta_all = jax.nn.sigmoid(ba) * valid
    xg = ba + hp_ref[1:2, :]
    softplus = jnp.maximum(xg, 0.0) + jnp.log1p(jnp.exp(-jnp.abs(xg)))
    g_all = -jnp.exp(hp_ref[0:1, :]) * softplus * valid

    row = lax.broadcasted_iota(jnp.int32, (C, C), 0)
    col = lax.broadcasted_iota(jnp.int32, (C, C), 1)
    causal = row >= col
    strict = row > col
    eye = jnp.where(row == col, 1.0, 0.0).astype(F32)
    tril = jnp.where(causal, 1.0, 0.0).astype(BF16)

    g_hi = g_all.astype(BF16)
    r1 = g_all - g_hi.astype(F32)
    g_mid = r1.astype(BF16)
    g_lo = (r1 - g_mid.astype(F32)).astype(BF16)
    g3 = jnp.dot(tril, jnp.concatenate([g_hi, g_mid, g_lo], axis=1), preferred_element_type=F32)
    gcum = g3[:, 0:LANES] + g3[:, LANES:2 * LANES] + g3[:, 2 * LANES:3 * LANES]
    gcum_t = gcum.T

    scale = DN_DK ** -0.5
    nw = nw_ref[...]
    heads = range(DN_HEADS)

    def mm(a, b):
        return jnp.dot(a.astype(BF16), b.astype(BF16), preferred_element_type=F32)

    kn_b, lhs1, decay, rhs, qe_b, kdec_t, e_last = [], [], [], [], [], [], []
    for h in heads:
        qh = act_ref[:, h * DN_DK:(h + 1) * DN_DK]
        kh = act_ref[:, DN_W + h * DN_DK:DN_W + (h + 1) * DN_DK]
        vh = act_ref[:, 2 * DN_W + h * DN_DV:2 * DN_W + (h + 1) * DN_DV]
        qn = qh * (lax.rsqrt(jnp.sum(qh * qh, axis=-1, keepdims=True) + L2_EPS) * scale)
        kn = kh * lax.rsqrt(jnp.sum(kh * kh, axis=-1, keepdims=True) + L2_EPS)
        beta = beta_all[:, h:h + 1]
        gc = gcum[:, DN_HEADS + h:DN_HEADS + h + 1]
        gr = gcum_t[DN_HEADS + h:DN_HEADS + h + 1, :]
        g_last = gc[C - 1:C, :]
        e_g = jnp.exp(gc)
        kb = kn * beta
        kn_b.append(kn.astype(BF16))
        lhs1.append(jnp.concatenate([kb, qn], axis=0).astype(BF16))
        decay.append(jnp.exp(jnp.where(causal, gc - gr, NEG_BIG)))
        rhs.append(jnp.concatenate([vh * beta, kb * e_g], axis=1).astype(BF16))
        qe_b.append((qn * e_g).astype(BF16))
        kdec_t.append((kn * jnp.exp(g_last - gc)).T.astype(BF16))
        e_last.append(jnp.exp(g_last))

    kq = [lax.dot_general(lhs1[h], kn_b[h], (((1,), (1,)), ((), ())), preferred_element_type=F32)
          for h in heads]
    qk_b = [jnp.where(causal, kq[h][C:2 * C, :] * decay[h], 0.0).astype(BF16) for h in heads]
    b1 = [-jnp.where(strict, kq[h][0:C, :] * decay[h], 0.0) for h in heads]
    b2 = [mm(b1[h], b1[h]) for h in heads]
    b4 = [mm(b2[h], b2[h]) for h in heads]
    p1 = [(eye + b1[h]) + mm(eye + b1[h], b2[h]) for h in heads]
    b8 = [mm(b4[h], b4[h]) for h in heads]
    b16 = [mm(b8[h], b8[h]) for h in heads]
    p2 = [(eye + b4[h]) + mm(eye + b4[h], b8[h]) for h in heads]
    b32 = [mm(b16[h], b16[h]) for h in heads]
    p12 = [mm(p1[h], p2[h]) for h in heads]
    p3 = [(eye + b16[h]) + mm(eye + b16[h], b32[h]) for h in heads]
    inv = [mm(p12[h], p3[h]) for h in heads]
    sol = [mm(inv[h], rhs[h]) for h in heads]

    s_old = [s_ref[h] for h in heads]
    ws_lhs = [jnp.concatenate([sol[h][:, DN_DV:2 * DN_DV].astype(BF16), qe_b[h]], axis=0) for h in heads]
    ws = [mm(ws_lhs[h], s_old[h]) for h in heads]
    v_new = [(sol[h][:, 0:DN_DV] - ws[h][0:C, :]).astype(BF16) for h in heads]
    ov_lhs = [jnp.concatenate([qk_b[h], kdec_t[h]], axis=0) for h in heads]
    ov = [mm(ov_lhs[h], v_new[h]) for h in heads]
    for h in heads:
        s_ref[h] = s_old[h] * e_last[h] + ov[h][C:C + DN_DK, :]
        o = ws[h][C:2 * C, :] + ov[h][0:C, :]
        zh = z_ref[:, h * DN_DV:(h + 1) * DN_DV].astype(F32)
        o = o * lax.rsqrt(jnp.mean(o * o, axis=-1, keepdims=True) + RMS_EPS) * nw
        o_ref[:, h * DN_DV:(h + 1) * DN_DV] = (o * (zh * jax.nn.sigmoid(zh))).astype(o_ref.dtype)


def gated_deltanet(proj_x, ba_x, proj_m, ba_m, conv_w, a_log, dt_bias, norm_w, bsz, seq):
    C = DN_CHUNK
    n_chunks = seq // C
    w3 = 3 * DN_W
    hp = jnp.zeros((2, LANES), F32)
    hp = hp.at[0, DN_HEADS:2 * DN_HEADS].set(a_log.astype(F32))
    hp = hp.at[1, DN_HEADS:2 * DN_HEADS].set(dt_bias.astype(F32))

    def xrow(b, t):
        return b * n_chunks + jnp.maximum(t - 1, 0)

    return pl.pallas_call(
        _deltanet_kernel,
        grid=(bsz, n_chunks + 1),
        in_specs=[
            pl.BlockSpec((C, w3), lambda b, t: (xrow(b, t), 0)),
            pl.BlockSpec((C, DN_W), lambda b, t: (xrow(b, t), COL_DZ // DN_W)),
            pl.BlockSpec((C, LANES), lambda b, t: (xrow(b, t), 0)),
            pl.BlockSpec((N_META, w3), lambda b, t: (0, 0)),
            pl.BlockSpec((N_META, LANES), lambda b, t: (0, 0)),
            pl.BlockSpec((DN_CONV, w3), lambda b, t: (0, 0)),
            pl.BlockSpec((2, LANES), lambda b, t: (0, 0)),
            pl.BlockSpec((1, DN_DV), lambda b, t: (0, 0)),
        ],
        out_specs=pl.BlockSpec((C, DN_W), lambda b, t: (xrow(b, t), 0)),
        out_shape=jax.ShapeDtypeStruct((bsz * seq, DN_W), BF16),
        scratch_shapes=[
            pltpu.VMEM((DN_HIST + C, w3), BF16),
            pltpu.VMEM((C, w3), F32),
            pltpu.VMEM((C, LANES), F32),
            pltpu.VMEM((DN_HEADS, DN_DK, DN_DV), F32),
        ],
        compiler_params=_params("parallel", "arbitrary", vmem=VMEM_LIMIT),
        name="gated_deltanet",
    )(proj_x, proj_x, ba_x, proj_m, ba_m, conv_w.astype(F32), hp, norm_w.reshape(1, DN_DV).astype(F32))


def _diff_attn_kernel(q_ref, k_ref, v_ref, km_ref, vm_ref, qn_ref, kn_ref, lam_ref, sn_ref,
                      o_ref, kx_s, kmeta_s, *, tq, seq):
    qi = pl.program_id(2)

    def rms(x, w):
        return x * lax.rsqrt(jnp.mean(x * x, axis=-1, keepdims=True) + RMS_EPS) * w

    @pl.when(qi == 0)
    def _():
        for m in range(2):
            kx = k_ref[:, m * DA_DK:(m + 1) * DA_DK].astype(F32)
            kx_s[m] = rms(kx, kn_ref[...]).astype(BF16)
            kmx = km_ref[:, m * DA_DK:(m + 1) * DA_DK].astype(F32)
            kmeta_s[m] = rms(kmx, kn_ref[...]).astype(BF16)

    lam = (jnp.exp(jnp.sum(lam_ref[0:1, :] * lam_ref[1:2, :], axis=-1, keepdims=True))
           - jnp.exp(jnp.sum(lam_ref[2:3, :] * lam_ref[3:4, :], axis=-1, keepdims=True))
           + LAMBDA_INIT)
    row = lax.broadcasted_iota(jnp.int32, (tq, tq), 0)
    col = lax.broadcasted_iota(jnp.int32, (tq, tq), 1)
    diag_visible = col <= row
    v_m = vm_ref[...]
    scale = DA_DK ** -0.5 * math.log2(math.e)
    nt = (((1,), (1,)), ((), ()))

    def attend(n_prev):
        maps = range(2)
        qn = [(rms(q_ref[:, m * DA_DK:(m + 1) * DA_DK].astype(F32), qn_ref[...]) * scale).astype(BF16)
              for m in maps]
        sd = [jnp.where(diag_visible,
                        lax.dot_general(qn[m], kx_s[m, n_prev:n_prev + tq, :], nt,
                                        preferred_element_type=F32), NEG_BIG) for m in maps]
        sm = [lax.dot_general(qn[m], kmeta_s[m], nt, preferred_element_type=F32) for m in maps]
        mx = [jnp.maximum(jnp.max(sd[m], axis=-1, keepdims=True),
                          jnp.max(sm[m], axis=-1, keepdims=True)) for m in maps]
        if n_prev:
            sp = [lax.dot_general(qn[m], kx_s[m, 0:n_prev, :], nt, preferred_element_type=F32)
                  for m in maps]
            mx = [jnp.maximum(mx[m], jnp.max(sp[m], axis=-1, keepdims=True)) for m in maps]
        pd = [jnp.exp2(sd[m] - mx[m]) for m in maps]
        pm = [jnp.exp2(sm[m] - mx[m]) for m in maps]
        den = [jnp.sum(pd[m], axis=-1, keepdims=True) + jnp.sum(pm[m], axis=-1, keepdims=True)
               for m in maps]
        acc = [jnp.dot(pd[m].astype(BF16), v_ref[n_prev:n_prev + tq, :], preferred_element_type=F32)
               + jnp.dot(pm[m].astype(BF16), v_m, preferred_element_type=F32) for m in maps]
        if n_prev:
            pp = [jnp.exp2(sp[m] - mx[m]) for m in maps]
            den = [den[m] + jnp.sum(pp[m], axis=-1, keepdims=True) for m in maps]
            acc = [acc[m] + jnp.dot(pp[m].astype(BF16), v_ref[0:n_prev, :], preferred_element_type=F32)
                   for m in maps]
        out = acc[0] / den[0] - lam * (acc[1] / den[1])
        out = rms(out, sn_ref[...]) * (1.0 - LAMBDA_INIT)
        o_ref[...] = out.astype(o_ref.dtype)

    for blk in range(seq // tq):
        @pl.when(qi == blk)
        def _():
            attend(blk * tq)


def diff_attention(proj_x, proj_m, q_norm, k_norm, lam4, sub_norm, bsz, seq, tq=512):
    nq = seq // tq
    cq = COL_AQ // DA_DV
    ck = COL_AK // DA_DV
    cv = COL_AV // DA_DV
    return pl.pallas_call(
        functools.partial(_diff_attn_kernel, tq=tq, seq=seq),
        grid=(bsz, DA_HEADS, nq),
        in_specs=[
            pl.BlockSpec((tq, DA_DV), lambda b, h, i: (b * nq + i, cq + h)),
            pl.BlockSpec((seq, DA_DV), lambda b, h, i: (b, ck + h)),
            pl.BlockSpec((seq, DA_DV), lambda b, h, i: (b, cv + h)),
            pl.BlockSpec((N_META, DA_DV), lambda b, h, i: (0, h)),
            pl.BlockSpec((N_META, DA_DV), lambda b, h, i: (0, DA_HEADS + h)),
            pl.BlockSpec((1, DA_DK), lambda b, h, i: (0, 0)),
            pl.BlockSpec((1, DA_DK), lambda b, h, i: (0, 0)),
            pl.BlockSpec((4, DA_DK), lambda b, h, i: (0, 0)),
            pl.BlockSpec((1, DA_DV), lambda b, h, i: (0, 0)),
        ],
        out_specs=pl.BlockSpec((tq, DA_DV), lambda b, h, i: (b * nq + i, h)),
        out_shape=jax.ShapeDtypeStruct((bsz * seq, DA_HEADS * DA_DV), BF16),
        scratch_shapes=[pltpu.VMEM((2, seq, DA_DK), BF16),
                        pltpu.VMEM((2, N_META, DA_DK), BF16)],
        compiler_params=_params("parallel", "parallel", "arbitrary", vmem=VMEM_LIMIT),
        name="diff_attention",
    )(proj_x, proj_x, proj_x, proj_m, proj_m,
      q_norm.reshape(1, DA_DK).astype(F32), k_norm.reshape(1, DA_DK).astype(F32),
      lam4.astype(F32), sub_norm.reshape(1, DA_DV).astype(F32))


def _merge_kernel(ya_ref, yb_ref, wa_ref, wb_ref, ga_ref, gb_ref, o_ref):
    pa = jnp.dot(ya_ref[...], wa_ref[...].astype(BF16), preferred_element_type=F32)
    pb = jnp.dot(yb_ref[...], wb_ref[...].astype(BF16), preferred_element_type=F32)
    ga = jax.nn.sigmoid(ga_ref[...].astype(F32))
    gb = jax.nn.sigmoid(gb_ref[...].astype(F32))
    o_ref[...] = (ga * pa + gb * pb).astype(o_ref.dtype)


def branch_merge(y_a, y_b, w_a, w_b, proj_x, d_model, tm=1024, tn=512):
    m, ka = y_a.shape
    kb = y_b.shape[1]
    cga = COL_GA // tn
    cgb = (COL_GA + d_model) // tn
    return pl.pallas_call(
        _merge_kernel,
        grid=(m // tm, d_model // tn),
        in_specs=[
            pl.BlockSpec((tm, ka), lambda i, j: (i, 0)),
            pl.BlockSpec((tm, kb), lambda i, j: (i, 0)),
            pl.BlockSpec((ka, tn), lambda i, j: (0, j)),
            pl.BlockSpec((kb, tn), lambda i, j: (0, j)),
            pl.BlockSpec((tm, tn), lambda i, j: (i, cga + j)),
            pl.BlockSpec((tm, tn), lambda i, j: (i, cgb + j)),
        ],
        out_specs=pl.BlockSpec((tm, tn), lambda i, j: (i, j)),
        out_shape=jax.ShapeDtypeStruct((m, d_model), BF16),
        compiler_params=_params("parallel", "parallel", vmem=VMEM_LIMIT),
        name="branch_merge",
    )(y_a, y_b, w_a, w_b, proj_x, proj_x)


def _router_kernel(h_ref, nw_ref, whi_ref, wlo_ref, b_ref, idx_ref, gate_ref, up_ref):
    x = h_ref[...]
    u = x * lax.rsqrt(jnp.mean(x * x, axis=-1, keepdims=True) + RMS_EPS) * nw_ref[...]
    u_hi = u.astype(BF16)
    u_lo = (u - u_hi.astype(F32)).astype(BF16)
    half = u.shape[1] // 2
    bits = pltpu.bitcast(u_hi.astype(F32), jnp.uint32)
    up_ref[...] = bits[:, half:] | (bits[:, :half] >> 16)
    logits = (jnp.dot(u_hi, whi_ref[...], preferred_element_type=F32)
              + jnp.dot(u_lo, whi_ref[...], preferred_element_type=F32)
              + jnp.dot(u_hi, wlo_ref[...], preferred_element_type=F32)
              + b_ref[...])
    lane = lax.broadcasted_iota(jnp.int32, logits.shape, 1)
    lane_f = lane.astype(F32)
    vals = logits
    tops, idxs = [], []
    for _ in range(TOP_K):
        mx = jnp.max(vals, axis=-1, keepdims=True)
        ix = jnp.min(jnp.where(vals == mx, lane_f, float(LANES)), axis=-1, keepdims=True).astype(jnp.int32)
        tops.append(mx)
        idxs.append(ix)
        vals = jnp.where(lane == ix, -3.0e38, vals)
    exps = [jnp.exp(tv - tops[0]) for tv in tops]
    den = exps[0] + exps[1] + exps[2] + exps[3]
    idx_out = jnp.zeros(logits.shape, jnp.int32)
    gate_out = jnp.zeros(logits.shape, F32)
    for k in range(TOP_K):
        idx_out = jnp.where(lane == k, idxs[k], idx_out)
        gate_out = jnp.where(lane == k, exps[k] / den, gate_out)
    idx_ref[...] = idx_out
    gate_ref[...] = gate_out


def router(h2, ffn_norm, router_w, router_b, tm=256):
    n, d = h2.shape
    wpad = jnp.zeros((d, LANES), F32).at[:, :N_EXPERTS].set(router_w.astype(F32))
    w_hi = wpad.astype(BF16)
    w_lo = (wpad - w_hi.astype(F32)).astype(BF16)
    bias = jnp.full((1, LANES), NEG_BIG, F32).at[0, :N_EXPERTS].set(router_b.astype(F32))
    return pl.pallas_call(
        _router_kernel,
        grid=(n // tm,),
        in_specs=[
            pl.BlockSpec((tm, d), lambda i: (i, 0)),
            pl.BlockSpec((1, d), lambda i: (0, 0)),
            pl.BlockSpec((d, LANES), lambda i: (0, 0)),
            pl.BlockSpec((d, LANES), lambda i: (0, 0)),
            pl.BlockSpec((1, LANES), lambda i: (0, 0)),
        ],
        out_specs=[pl.BlockSpec((tm, LANES), lambda i: (i, 0)),
                   pl.BlockSpec((tm, LANES), lambda i: (i, 0)),
                   pl.BlockSpec((tm, d // 2), lambda i: (i, 0))],
        out_shape=[jax.ShapeDtypeStruct((n, LANES), jnp.int32),
                   jax.ShapeDtypeStruct((n, LANES), F32),
                   jax.ShapeDtypeStruct((n, d // 2), jnp.uint32)],
        compiler_params=_params("parallel", vmem=VMEM_LIMIT),
        name="router",
    )(h2, ffn_norm.reshape(1, d).astype(F32), w_hi, w_lo, bias)


def _expert_kernel(ie_ref, isb_ref, inb_ref, tok_ref, up_hbm, wgu_hbm, wd_hbm, bgu_ref, bd_ref,
                   ys_hbm, xbuf, act, ybuf, wgu_buf, wd_buf, wg_s, wu_s, wd_s,
                   sem_x, sem_y, sem_gu, sem_d, *, j1, j2, tn1, tn2, n_items):
    i = pl.program_id(0)
    nb = inb_ref[i]
    sb = isb_ref[i]
    ex = ie_ref[i]
    R = MOE_ROWS

    def gu_copies(expert, jt, slot):
        return [pltpu.make_async_copy(
            wgu_hbm.at[expert, :, pl.ds(pl.multiple_of(which * j1 * tn1 + jt * tn1, tn1), tn1)],
            wgu_buf.at[slot, which], sem_gu.at[slot]) for which in range(2)]

    def d_copy(expert, jd, slot):
        return pltpu.make_async_copy(
            wd_hbm.at[expert, :, pl.ds(pl.multiple_of(jd * tn2, tn2), tn2)],
            wd_buf.at[slot], sem_d.at[slot])

    rmax = xbuf.shape[0]
    HEAD = MOE_HEAD_UNITS * R
    BODY = MOE_BODY_UNITS * R
    last_tok = tok_ref.shape[0] - 1

    def batch(rows):
        return rows // j2

    own_units = jnp.maximum(nb, MOE_HEAD_UNITS)

    def row_gather(base, r):
        tok = tok_ref[jnp.minimum(base + r, last_tok)]
        return pltpu.make_async_copy(up_hbm.at[pl.ds(tok, 1), :], xbuf.at[pl.ds(r, 1), :], sem_x)

    def x_wait_units(n):
        for u in range(MOE_ITEM_BLOCKS):
            @pl.when(u < n)
            def _():
                pltpu.make_async_copy(up_hbm.at[pl.ds(0, R), :], xbuf.at[pl.ds(u * R, R), :],
                                      sem_x).wait()

    yw = tn2 // 2

    def y_copy(slot, r, jd):
        return pltpu.make_async_copy(
            ybuf.at[slot, pl.ds(r * R, R), :],
            ys_hbm.at[pl.ds((sb + r) * R, R), pl.ds(pl.multiple_of(jd * yw, yw), yw)],
            sem_y.at[slot])

    def for_blocks(fn):
        for r in range(MOE_ITEM_BLOCKS):
            @pl.when(r < nb)
            def _():
                fn(r)

    @pl.when(i == 0)
    def _():
        def start(r, c):
            row_gather(sb * R, r).start()
            return c
        lax.fori_loop(0, rmax, start, 0, unroll=8)
        for cp in gu_copies(ex, 0, 0):
            cp.start()

    nb_prev = inb_ref[jnp.maximum(i - 1, 0)]
    x_wait_units(jnp.where(
        i == 0, MOE_ITEM_BLOCKS,
        jnp.where(nb_prev > 0, jnp.maximum(jnp.maximum(nb_prev, MOE_HEAD_UNITS), nb), 0)))

    def gate_up_block(jt, r0, rows, wg, wu):
        x32 = xbuf[pl.ds(r0, rows), :]
        x_lo = pltpu.bitcast(x32 << 16, F32).astype(BF16)
        x_hi = pltpu.bitcast(x32 & jnp.uint32(0xFFFF0000), F32).astype(BF16)
        x = jnp.concatenate([x_lo, x_hi], axis=1)
        g = jnp.dot(x, wg, preferred_element_type=F32) + bgu_ref[pl.ds(jt, 1), :]
        u = jnp.dot(x, wu, preferred_element_type=F32) + bgu_ref[pl.ds(j1 + jt, 1), :]
        gt = jnp.minimum(g, SWIGLU_LIMIT)
        up = jnp.clip(u, -SWIGLU_LIMIT, SWIGLU_LIMIT)
        a = gt * jax.nn.sigmoid(SWIGLU_ALPHA * gt) * (up + 1.0)
        act[jt, pl.ds(r0, rows), :] = a.astype(BF16)

    def down_block(jd, slot, r0, rows, wd):
        lhs = jnp.concatenate([act[c, pl.ds(r0, rows), :] for c in range(j1)], axis=1)
        y = jnp.dot(lhs, wd, preferred_element_type=F32) + bd_ref[pl.ds(jd, 1), :]
        bits = pltpu.bitcast(y.astype(BF16).astype(F32), jnp.uint32)
        ybuf[slot, pl.ds(r0, rows), :] = bits[:, yw:] | (bits[:, :yw] >> 16)

    def rest_blocks(block_fn):
        rest = jnp.maximum(nb - MOE_HEAD_UNITS, 0)
        n_body = rest // MOE_BODY_UNITS
        rem = rest % MOE_BODY_UNITS
        r_tail = HEAD + n_body * BODY
        g_tail = batch(HEAD) + n_body * batch(BODY)

        def body(p, c):
            block_fn(pl.multiple_of(HEAD + p * BODY, R), BODY, batch(HEAD) + p * batch(BODY))
            return c

        lax.fori_loop(0, n_body, body, 0)

        @pl.when(rem >= 2)
        def _():
            block_fn(pl.multiple_of(r_tail, R), 2 * R, g_tail)

        @pl.when(rem % 2 == 1)
        def _():
            two = (rem // 2) * 2
            block_fn(pl.multiple_of(r_tail + two * R, R), R, g_tail + (rem // 2) * batch(2 * R))

    nxt = jnp.minimum(i + 1, n_items - 1)
    next_base = isb_ref[nxt] * R
    fetch_units = jnp.maximum(own_units, inb_ref[nxt])
    rows_per_tile = fetch_units * batch(R)

    n_wd = wd_buf.shape[0]

    @pl.when(nb > 0)
    def _():
        for t0 in range(n_wd - 1):
            d_copy(ex, t0, t0).start()

        def gate_up_tile(jt, carry):
            slot = jt % 2

            @pl.when(jt + 1 < j1)
            def _():
                for cp in gu_copies(ex, jt + 1, 1 - slot):
                    cp.start()

            for cp in gu_copies(ex, jt, slot):
                cp.wait()
            wg = wgu_buf[slot, 0].astype(BF16)
            wu = wgu_buf[slot, 1].astype(BF16)
            wg_s[...] = wg
            wu_s[...] = wu
            gate_up_block(jt, 0, HEAD, wg, wu)
            rest_blocks(lambda r0, rows, g0: gate_up_block(jt, r0, rows, wg_s[...], wu_s[...]))
            return carry

        lax.fori_loop(0, j1, gate_up_tile, 0)

        @pl.when(i + 1 < n_items)
        def _():
            @pl.when(inb_ref[i + 1] > 0)
            def _():
                for cp in gu_copies(ie_ref[i + 1], 0, 0):
                    cp.start()

        def down_tile(jd, carry):
            slot = jd % 2
            wslot = jd % n_wd

            @pl.when(jd + n_wd - 1 < j2)
            def _():
                d_copy(ex, jd + n_wd - 1, (jd + n_wd - 1) % n_wd).start()

            @pl.when(jd >= 2)
            def _():
                for_blocks(lambda r: y_copy(slot, r, jd).wait())

            d_copy(ex, jd, wslot).wait()
            wd = wd_buf[wslot].astype(BF16)
            wd_s[...] = wd
            down_block(jd, slot, 0, HEAD, wd)
            tile_row0 = jd * rows_per_tile

            def gather_batch(first, count):
                for rr in range(count):
                    row_gather(next_base, tile_row0 + first + rr).start()

            gather_batch(0, batch(HEAD))

            def later_block(r0, rows, g0):
                down_block(jd, slot, r0, rows, wd_s[...])
                gather_batch(g0, batch(rows))

            rest_blocks(later_block)

            def make_up(r, c):
                row_gather(next_base, tile_row0 + r).start()
                return c

            lax.fori_loop(own_units * batch(R), rows_per_tile, make_up, 0)
            for_blocks(lambda r: y_copy(slot, r, jd).start())
            return carry

        lax.fori_loop(0, j2, down_tile, 0)
        for_blocks(lambda r: y_copy(0, r, 0).wait())
        for_blocks(lambda r: y_copy(1, r, 0).wait())

        @pl.when(i == n_items - 1)
        def _():
            x_wait_units(fetch_units)


def expert_mlp(u_packed, row_tok, item_e, item_sb, item_nb, w_gate_up, b_gate_up, w_down, b_down,
               tn1=256, tn2=MOE_DOWN_TILE):
    p = row_tok.shape[0]
    d = 2 * u_packed.shape[1]
    n_items = item_e.shape[0]
    de = w_down.shape[1]
    j1 = de // tn1
    j2 = d // tn2
    rmax = MOE_ITEM_BLOCKS * MOE_ROWS

    assert rmax % j2 == 0

    def bias_map(i, ie, isb, inb, tok):
        return (ie[i], 0, 0)

    return pl.pallas_call(
        functools.partial(_expert_kernel, j1=j1, j2=j2, tn1=tn1, tn2=tn2, n_items=n_items),
        grid_spec=pltpu.PrefetchScalarGridSpec(
            num_scalar_prefetch=4,
            grid=(n_items,),
            in_specs=[
                pl.BlockSpec(memory_space=pl.ANY),
                pl.BlockSpec(memory_space=pl.ANY),
                pl.BlockSpec(memory_space=pl.ANY),
                pl.BlockSpec((None, 2 * j1, tn1), bias_map),
                pl.BlockSpec((None, j2, tn2), bias_map),
            ],
            out_specs=pl.BlockSpec(memory_space=pl.ANY),
            scratch_shapes=[
                pltpu.VMEM((rmax, d // 2), jnp.uint32),
                pltpu.VMEM((j1, rmax, tn1), BF16),
                pltpu.VMEM((2, rmax, tn2 // 2), jnp.uint32),
                pltpu.VMEM((2, 2, d, tn1), F32),
                pltpu.VMEM((MOE_DOWN_SLOTS, de, tn2), F32),
                pltpu.VMEM((d, tn1), BF16),
                pltpu.VMEM((d, tn1), BF16),
                pltpu.VMEM((de, tn2), BF16),
                pltpu.SemaphoreType.DMA(()),
                pltpu.SemaphoreType.DMA((2,)),
                pltpu.SemaphoreType.DMA((2,)),
                pltpu.SemaphoreType.DMA((MOE_DOWN_SLOTS,)),
            ],
        ),
        out_shape=jax.ShapeDtypeStruct((p, d // 2), jnp.uint32),
        compiler_params=_params("arbitrary", vmem=VMEM_LIMIT),
        name="expert_mlp",
    )(item_e, item_sb, item_nb, row_tok, u_packed, w_gate_up, w_down,
      b_gate_up.reshape(N_EXPERTS, 2 * j1, tn1), b_down.reshape(N_EXPERTS, j2, tn2))


def _combine_kernel(dest_ref, h_ref, g_ref, ys_hbm, o_ref, buf, sem, *, tt, n_tok, n_steps, yw):
    i = pl.program_id(0)

    def issue(blk, slot):
        for k in range(TOP_K):
            def start(t, c):
                src = dest_ref[k * n_tok + blk * tt + t]
                pltpu.make_async_copy(ys_hbm.at[pl.ds(src, 1), :], buf.at[slot, k, pl.ds(t, 1), :],
                                      sem.at[slot]).start()
                return c
            lax.fori_loop(0, tt, start, 0, unroll=8)

    @pl.when(i == 0)
    def _():
        issue(0, 0)

    slot = i % 2

    @pl.when(i + 1 < n_steps)
    def _():
        issue(i + 1, 1 - slot)

    for k in range(TOP_K):
        pltpu.make_async_copy(ys_hbm.at[pl.ds(0, tt), :], buf.at[slot, k], sem.at[slot]).wait()
    g = g_ref[...]
    lo = None
    hi = None
    for k in range(TOP_K):
        word = buf[slot, k]
        gk = g[:, k:k + 1]
        lo_k = gk * pltpu.bitcast(word << 16, F32)
        hi_k = gk * pltpu.bitcast(word & jnp.uint32(0xFFFF0000), F32)
        lo = lo_k if lo is None else lo + lo_k
        hi = hi_k if hi is None else hi + hi_k
    for c in range(lo.shape[1] // yw):
        o_ref[:, 2 * c * yw:(2 * c + 1) * yw] = (h_ref[:, 2 * c * yw:(2 * c + 1) * yw]
                                                  + lo[:, c * yw:(c + 1) * yw])
        o_ref[:, (2 * c + 1) * yw:(2 * c + 2) * yw] = (h_ref[:, (2 * c + 1) * yw:(2 * c + 2) * yw]
                                                        + hi[:, c * yw:(c + 1) * yw])


def combine(h2, gates, ys, dest_km, yw, tt=128):
    n, d = h2.shape
    return pl.pallas_call(
        functools.partial(_combine_kernel, tt=tt, n_tok=n, n_steps=n // tt, yw=yw),
        grid_spec=pltpu.PrefetchScalarGridSpec(
            num_scalar_prefetch=1,
            grid=(n // tt,),
            in_specs=[pl.BlockSpec((tt, d), lambda i, dst: (i, 0)),
                      pl.BlockSpec((tt, LANES), lambda i, dst: (i, 0)),
                      pl.BlockSpec(memory_space=pl.ANY)],
            out_specs=pl.BlockSpec((tt, d), lambda i, dst: (i, 0)),
            scratch_shapes=[pltpu.VMEM((2, TOP_K, tt, d // 2), jnp.uint32),
                            pltpu.SemaphoreType.DMA((2,))],
        ),
        out_shape=jax.ShapeDtypeStruct((n, d), F32),
        compiler_params=_params("arbitrary", vmem=VMEM_LIMIT),
        name="moe_combine",
    )(dest_km, h2, gates, ys)


def routing_tables(top_idx, n_tok):
    a = n_tok * TOP_K
    e_flat = top_idx.reshape(-1)
    onehot = (e_flat[:, None] == jnp.arange(N_EXPERTS, dtype=jnp.int32)[None, :]).astype(jnp.int32)
    csum = jnp.cumsum(onehot, axis=0)
    rank = jnp.sum(csum * onehot, axis=1) - 1
    counts = csum[-1]
    nblk = (counts + MOE_ROWS - 1) // MOE_ROWS
    blk_end = jnp.cumsum(nblk)
    blk_start = blk_end - nblk
    dest = blk_start[e_flat] * MOE_ROWS + rank
    n_blocks = -(-(a + N_EXPERTS * (MOE_ROWS - 1)) // MOE_ROWS)
    p = n_blocks * MOE_ROWS
    t_flat = jnp.arange(a, dtype=jnp.int32) // TOP_K
    row_tok = jnp.zeros((p,), jnp.int32).at[dest].set(t_flat)

    n_items = N_EXPERTS + n_blocks // MOE_ITEM_BLOCKS
    items_per_e = (nblk + MOE_ITEM_BLOCKS - 1) // MOE_ITEM_BLOCKS
    item_end = jnp.cumsum(items_per_e)
    item_start = item_end - items_per_e
    total_items = item_end[-1]
    ii = jnp.arange(n_items, dtype=jnp.int32)
    ic = jnp.minimum(ii, total_items - 1)
    e_i = jnp.minimum(jnp.searchsorted(item_end, ic, side='right'), N_EXPERTS - 1).astype(jnp.int32)
    local = ic - item_start[e_i]
    item_sb = (blk_start[e_i] + local * MOE_ITEM_BLOCKS).astype(jnp.int32)
    item_nb = jnp.clip(nblk[e_i] - local * MOE_ITEM_BLOCKS, 0, MOE_ITEM_BLOCKS)
    item_nb = jnp.where(ii < total_items, item_nb, 0).astype(jnp.int32)
    dest_km = dest.astype(jnp.int32).reshape(n_tok, TOP_K).T.reshape(-1)
    return dest_km, row_tok, e_i, item_sb, item_nb


def kernel(x, meta_tokens, mix_norm, w_in, dn_conv, dn_a_log, dn_dt_bias, dn_out_norm, da_q_norm, da_k_norm, da_lam_q1, da_lam_k1, da_lam_q2, da_lam_k2, da_sub_norm, w_branch_a, w_branch_b, w_out, ffn_norm, router_w, router_b, w_gate_up, b_gate_up, w_down, b_down):
    bsz, seq, d = x.shape
    n_tok = bsz * seq
    x2 = x.reshape(n_tok, d)
    layer = 0

    w_bf = w_in[layer].astype(BF16)
    w_da = w_bf[:, COL_BA + 2 * DN_HEADS:]
    tn = 1024

    u_x = rms_cast(x2, mix_norm[layer], 256)
    u_m = rms_cast(meta_tokens.astype(F32), mix_norm[layer], N_META)
    proj_dn = matmul_fullk(u_x, w_bf, BF16, 1024, tn, name="in_proj_dn", n_blocks=COL_BA // tn)
    proj_da = matmul_fullk(u_x, w_da, BF16, 1024, tn, name="in_proj_da")
    ba_x = matmul_fullk(u_x, w_bf, F32, 1024, LANES, name="in_proj_ba",
                        n_offset=COL_BA // LANES, n_blocks=1)
    proj_dn_m = matmul_fullk(u_m, w_bf, BF16, N_META, tn, name="in_proj_dn_meta", n_blocks=COL_DZ // tn)
    proj_kv_m = matmul_fullk(u_m, w_da, BF16, N_META, tn, name="in_proj_kv_meta",
                             n_offset=COL_AK // tn, n_blocks=(COL_GA - COL_AK) // tn)
    ba_m = matmul_fullk(u_m, w_bf, F32, N_META, LANES, name="in_proj_ba_meta",
                        n_offset=COL_BA // LANES, n_blocks=1)

    y_a = gated_deltanet(proj_dn, ba_x, proj_dn_m, ba_m, dn_conv[layer], dn_a_log[layer],
                         dn_dt_bias[layer], dn_out_norm[layer], bsz, seq)
    lam4 = jnp.stack([da_lam_q1[layer], da_lam_k1[layer], da_lam_q2[layer], da_lam_k2[layer]])
    y_b = diff_attention(proj_da, proj_kv_m, da_q_norm[layer], da_k_norm[layer], lam4,
                         da_sub_norm[layer], bsz, seq)

    merged = branch_merge(y_a, y_b, w_branch_a[layer], w_branch_b[layer], proj_da, d)
    h2 = matmul_fullk(merged, w_out[layer], F32, 1024, 512, res=x2, name="out_proj")

    top_idx, gates, u_packed = router(h2, ffn_norm[layer], router_w[layer], router_b[layer])
    dest, row_tok, item_e, item_sb, item_nb = routing_tables(top_idx[:, :TOP_K], n_tok)
    ys = expert_mlp(u_packed, row_tok, item_e, item_sb, item_nb, w_gate_up[layer], b_gate_up[layer],
                    w_down[layer], b_down[layer])
    out = combine(h2, gates, ys, dest, MOE_DOWN_TILE // 2)
    return out.reshape(bsz, seq, d)
```

```python
import functools
import math

import jax
import jax.numpy as jnp
from jax import lax
from jax.experimental import pallas as pl
from jax.experimental.pallas import tpu as pltpu

F32 = jnp.float32
BF16 = jnp.bfloat16

N_META = 16
RMS_EPS = 1e-6
L2_EPS = 1e-6

DN_HEADS = 16
DN_DK = 128
DN_DV = 128
DN_CONV = 4
DN_CHUNK = 64
DN_W = DN_HEADS * DN_DK
DN_HIST = 16
DN_CONV_COLS = 1024

DA_HEADS = 8
DA_DK = 128
DA_DV = 256
LAMBDA_INIT = 0.8 - 0.6 * math.exp(-0.3 * 0)

N_EXPERTS = 32
TOP_K = 4
D_EXPERT = 1536
SWIGLU_LIMIT = 7.0
SWIGLU_ALPHA = 1.702

LANES = 128
MOE_ROWS = 128
MOE_ITEM_BLOCKS = 10
MOE_HEAD_UNITS = 2
MOE_BODY_UNITS = 4
MOE_DOWN_SLOTS = 3
MOE_DOWN_TILE = 512
NEG_BIG = -1e30
VMEM_LIMIT = 56 * 1024 * 1024

COL_DZ = 3 * DN_W
COL_BA = 4 * DN_W
COL_AQ = 0
COL_AK = COL_AQ + DA_HEADS * 2 * DA_DK
COL_AV = COL_AK + DA_HEADS * 2 * DA_DK
COL_GA = COL_AV + DA_HEADS * DA_DV


def _params(*sem, vmem=None):
    return pltpu.CompilerParams(dimension_semantics=sem, vmem_limit_bytes=vmem)


def _rms_cast_kernel(x_ref, w_ref, o_ref):
    x = x_ref[...]
    y = x * lax.rsqrt(jnp.mean(x * x, axis=-1, keepdims=True) + RMS_EPS) * w_ref[...]
    o_ref[...] = y.astype(o_ref.dtype)


def rms_cast(x, w, tm):
    m, d = x.shape
    return pl.pallas_call(
        _rms_cast_kernel,
        grid=(m // tm,),
        in_specs=[pl.BlockSpec((tm, d), lambda i: (i, 0)),
                  pl.BlockSpec((1, d), lambda i: (0, 0))],
        out_specs=pl.BlockSpec((tm, d), lambda i: (i, 0)),
        out_shape=jax.ShapeDtypeStruct((m, d), BF16),
        compiler_params=_params("parallel"),
        name="rms_cast",
    )(x, w.reshape(1, d))


def _mm_fullk_kernel(a_ref, b_ref, *rest, has_res):
    acc = jnp.dot(a_ref[...], b_ref[...].astype(BF16), preferred_element_type=F32)
    if has_res:
        r_ref, o_ref = rest
        acc = acc + r_ref[...]
    else:
        o_ref, = rest
    o_ref[...] = acc.astype(o_ref.dtype)


def matmul_fullk(a, b, out_dtype, tm, tn, res=None, name="matmul", n_offset=0, n_blocks=None):
    m, kd = a.shape
    if n_blocks is None:
        n_blocks = b.shape[1] // tn
    in_specs = [pl.BlockSpec((tm, kd), lambda i, j: (i, 0)),
                pl.BlockSpec((kd, tn), lambda i, j: (0, n_offset + j))]
    args = [a, b]
    if res is not None:
        in_specs.append(pl.BlockSpec((tm, tn), lambda i, j: (i, j)))
        args.append(res)
    return pl.pallas_call(
        functools.partial(_mm_fullk_kernel, has_res=res is not None),
        grid=(m // tm, n_blocks),
        in_specs=in_specs,
        out_specs=pl.BlockSpec((tm, tn), lambda i, j: (i, j)),
        out_shape=jax.ShapeDtypeStruct((m, n_blocks * tn), out_dtype),
        compiler_params=_params("parallel", "parallel", vmem=VMEM_LIMIT),
        name=name,
    )(*args)


def _mm_shift_kernel(a_ref, b0_ref, b1_ref, o_ref, *, shift):
    b = jnp.concatenate([b0_ref[:, shift:], b1_ref[:, :shift]], axis=1)
    o_ref[...] = jnp.dot(a_ref[...], b, preferred_element_type=F32).astype(o_ref.dtype)


def matmul_fullk_shift(a, b, out_dtype, tm, tn, col0, shift, n_blocks, name, n_offset=0):
    m, kd = a.shape
    first = col0 // tn + n_offset
    return pl.pallas_call(
        functools.partial(_mm_shift_kernel, shift=shift),
        grid=(m // tm, n_blocks),
        in_specs=[pl.BlockSpec((tm, kd), lambda i, j: (i, 0)),
                  pl.BlockSpec((kd, tn), lambda i, j: (0, first + j)),
                  pl.BlockSpec((kd, LANES), lambda i, j: (0, (first + j + 1) * (tn // LANES)))],
        out_specs=pl.BlockSpec((tm, tn), lambda i, j: (i, j)),
        out_shape=jax.ShapeDtypeStruct((m, n_blocks * tn), out_dtype),
        compiler_params=_params("parallel", "parallel", vmem=VMEM_LIMIT),
        name=name,
    )(a, b, b)


def _deltanet_kernel(qkv_ref, z_ref, ba_ref, qkvm_ref, bam_ref, cw_ref, hp_ref, nw_ref,
                     o_ref, full_ref, act_ref, ba_s, s_ref):
    t = pl.program_id(1)
    C = DN_CHUNK
    HIST = DN_HIST
    n_pad = C - N_META

    @pl.when(t == 0)
    def _():
        s_ref[...] = jnp.zeros_like(s_ref)
        full_ref[0:HIST + n_pad, :] = jnp.zeros((HIST + n_pad, 3 * DN_W), BF16)
        full_ref[HIST + n_pad:HIST + C, :] = qkvm_ref[...]
        ba_s[0:n_pad, :] = jnp.zeros((n_pad, LANES), F32)
        ba_s[n_pad:C, :] = bam_ref[...]

    @pl.when(t > 0)
    def _():
        full_ref[HIST:HIST + C, :] = qkv_ref[...]
        ba_s[...] = ba_ref[...]

    n_shift = DN_CONV - 1
    sr = lax.broadcasted_iota(jnp.int32, (n_shift * C, HIST + C), 0)
    sc = lax.broadcasted_iota(jnp.int32, (n_shift * C, HIST + C), 1)
    shift = jnp.where(sc == HIST - n_shift + (sr % C) + (sr // C), 1.0, 0.0).astype(BF16)
    for c0 in range(0, 3 * DN_W, DN_CONV_COLS):
        cs = slice(c0, c0 + DN_CONV_COLS)
        delayed = jnp.dot(shift, full_ref[:, cs], preferred_element_type=F32)
        conv = cw_ref[n_shift:n_shift + 1, cs] * full_ref[HIST:HIST + C, cs].astype(F32)
        for j in range(n_shift):
            conv = conv + cw_ref[j:j + 1, cs] * delayed[j * C:(j + 1) * C, :]
        act_ref[:, cs] = conv * jax.nn.sigmoid(conv)
    full_ref[0:HIST, :] = full_ref[C:C + HIST, :]

    row1 = lax.broadcasted_iota(jnp.int32, (C, 1), 0)
    valid = jnp.where((t > 0) | (row1 >= n_pad), 1.0, 0.0).astype(F32)
    ba = ba_s[...]
    beta_all = jax.nn.sigmoid(ba) * valid
    xg = ba + hp_ref[1:2, :]
    softplus = jnp.maximum(xg, 0.0) + jnp.log1p(jnp.exp(-jnp.abs(xg)))
    g_all = -jnp.exp(hp_ref[0:1, :]) * softplus * valid

    row = lax.broadcasted_iota(jnp.int32, (C, C), 0)
    col = lax.broadcasted_iota(jnp.int32, (C, C), 1)
    causal = row >= col
    strict = row > col
    eye = jnp.where(row == col, 1.0, 0.0).astype(F32)
    tril = jnp.where(causal, 1.0, 0.0).astype(BF16)

    g_hi = g_all.astype(BF16)
    r1 = g_all - g_hi.astype(F32)
    g_mid = r1.astype(BF16)
    g_lo = (r1 - g_mid.astype(F32)).astype(BF16)
    g3 = jnp.dot(tril, jnp.concatenate([g_hi, g_mid, g_lo], axis=1), preferred_element_type=F32)
    gcum = g3[:, 0:LANES] + g3[:, LANES:2 * LANES] + g3[:, 2 * LANES:3 * LANES]
    gcum_t = gcum.T

    scale = DN_DK ** -0.5
    nw = nw_ref[...]
    heads = range(DN_HEADS)

    def mm(a, b):
        return jnp.dot(a.astype(BF16), b.astype(BF16), preferred_element_type=F32)

    kn_b, lhs1, decay, rhs, qe_b, kdec_t, e_last = [], [], [], [], [], [], []
    for h in heads:
        qh = act_ref[:, h * DN_DK:(h + 1) * DN_DK]
        kh = act_ref[:, DN_W + h * DN_DK:DN_W + (h + 1) * DN_DK]
        vh = act_ref[:, 2 * DN_W + h * DN_DV:2 * DN_W + (h + 1) * DN_DV]
        qn = qh * (lax.rsqrt(jnp.sum(qh * qh, axis=-1, keepdims=True) + L2_EPS) * scale)
        kn = kh * lax.rsqrt(jnp.sum(kh * kh, axis=-1, keepdims=True) + L2_EPS)
        beta = beta_all[:, h:h + 1]
        gc = gcum[:, DN_HEADS + h:DN_HEADS + h + 1]
        gr = gcum_t[DN_HEADS + h:DN_HEADS + h + 1, :]
        g_last = gc[C - 1:C, :]
        e_g = jnp.exp(gc)
        kb = kn * beta
        kn_b.append(kn.astype(BF16))
        lhs1.append(jnp.concatenate([kb, qn], axis=0).astype(BF16))
        decay.append(jnp.exp(jnp.where(causal, gc - gr, NEG_BIG)))
        rhs.append(jnp.concatenate([vh * beta, kb * e_g], axis=1).astype(BF16))
        qe_b.append((qn * e_g).astype(BF16))
        kdec_t.append((kn * jnp.exp(g_last - gc)).T.astype(BF16))
        e_last.append(jnp.exp(g_last))

    kq = [lax.dot_general(lhs1[h], kn_b[h], (((1,), (1,)), ((), ())), preferred_element_type=F32)
          for h in heads]
    qk_b = [jnp.where(causal, kq[h][C:2 * C, :] * decay[h], 0.0).astype(BF16) for h in heads]
    b1 = [-jnp.where(strict, kq[h][0:C, :] * decay[h], 0.0) for h in heads]
    b2 = [mm(b1[h], b1[h]) for h in heads]
    b4 = [mm(b2[h], b2[h]) for h in heads]
    p1 = [(eye + b1[h]) + mm(eye + b1[h], b2[h]) for h in heads]
    b8 = [mm(b4[h], b4[h]) for h in heads]
    b16 = [mm(b8[h], b8[h]) for h in heads]
    p2 = [(eye + b4[h]) + mm(eye + b4[h], b8[h]) for h in heads]
    b32 = [mm(b16[h], b16[h]) for h in heads]
    p12 = [mm(p1[h], p2[h]) for h in heads]
    p3 = [(eye + b16[h]) + mm(eye + b16[h], b32[h]) for h in heads]
    inv = [mm(p12[h], p3[h]) for h in heads]
    sol = [mm(inv[h], rhs[h]) for h in heads]

    s_old = [s_ref[h] for h in heads]
    ws_lhs = [jnp.concatenate([sol[h][:, DN_DV:2 * DN_DV].astype(BF16), qe_b[h]], axis=0) for h in heads]
    ws = [mm(ws_lhs[h], s_old[h]) for h in heads]
    v_new = [(sol[h][:, 0:DN_DV] - ws[h][0:C, :]).astype(BF16) for h in heads]
    ov_lhs = [jnp.concatenate([qk_b[h], kdec_t[h]], axis=0) for h in heads]
    ov = [mm(ov_lhs[h], v_new[h]) for h in heads]
    for h in heads:
        s_ref[h] = s_old[h] * e_last[h] + ov[h][C:C + DN_DK, :]
        o = ws[h][C:2 * C, :] + ov[h][0:C, :]
        zh = z_ref[:, h * DN_DV:(h + 1) * DN_DV].astype(F32)
        o = o * lax.rsqrt(jnp.mean(o * o, axis=-1, keepdims=True) + RMS_EPS) * nw
        o_ref[:, h * DN_DV:(h + 1) * DN_DV] = (o * (zh * jax.nn.sigmoid(zh))).astype(o_ref.dtype)


def gated_deltanet(proj_x, ba_x, proj_m, ba_m, conv_w, a_log, dt_bias, norm_w, bsz, seq):
    C = DN_CHUNK
    n_chunks = seq // C
    w3 = 3 * DN_W
    hp = jnp.zeros((2, LANES), F32)
    hp = hp.at[0, DN_HEADS:2 * DN_HEADS].set(a_log.astype(F32))
    hp = hp.at[1, DN_HEADS:2 * DN_HEADS].set(dt_bias.astype(F32))

    def xrow(b, t):
        return b * n_chunks + jnp.maximum(t - 1, 0)

    return pl.pallas_call(
        _deltanet_kernel,
        grid=(bsz, n_chunks + 1),
        in_specs=[
            pl.BlockSpec((C, w3), lambda b, t: (xrow(b, t), 0)),
            pl.BlockSpec((C, DN_W), lambda b, t: (xrow(b, t), COL_DZ // DN_W)),
            pl.BlockSpec((C, LANES), lambda b, t: (xrow(b, t), 0)),
            pl.BlockSpec((N_META, w3), lambda b, t: (0, 0)),
            pl.BlockSpec((N_META, LANES), lambda b, t: (0, 0)),
            pl.BlockSpec((DN_CONV, w3), lambda b, t: (0, 0)),
            pl.BlockSpec((2, LANES), lambda b, t: (0, 0)),
            pl.BlockSpec((1, DN_DV), lambda b, t: (0, 0)),
        ],
        out_specs=pl.BlockSpec((C, DN_W), lambda b, t: (xrow(b, t), 0)),
        out_shape=jax.ShapeDtypeStruct((bsz * seq, DN_W), BF16),
        scratch_shapes=[
            pltpu.VMEM((DN_HIST + C, w3), BF16),
            pltpu.VMEM((C, w3), F32),
            pltpu.VMEM((C, LANES), F32),
            pltpu.VMEM((DN_HEADS, DN_DK, DN_DV), F32),
        ],
        compiler_params=_params("parallel", "arbitrary", vmem=VMEM_LIMIT),
        name="gated_deltanet",
    )(proj_x, proj_x, ba_x, proj_m, ba_m, conv_w.astype(F32), hp, norm_w.reshape(1, DN_DV).astype(F32))


def _diff_attn_kernel(q_ref, k_ref, v_ref, km_ref, vm_ref, qn_ref, kn_ref, lam_ref, sn_ref,
                      o_ref, kx_s, kmeta_s, *, tq, seq):
    qi = pl.program_id(2)

    def rms(x, w):
        return x * lax.rsqrt(jnp.mean(x * x, axis=-1, keepdims=True) + RMS_EPS) * w

    @pl.when(qi == 0)
    def _():
        for m in range(2):
            kx = k_ref[:, m * DA_DK:(m + 1) * DA_DK].astype(F32)
            kx_s[m] = rms(kx, kn_ref[...]).astype(BF16)
            kmx = km_ref[:, m * DA_DK:(m + 1) * DA_DK].astype(F32)
            kmeta_s[m] = rms(kmx, kn_ref[...]).astype(BF16)

    lam = (jnp.exp(jnp.sum(lam_ref[0:1, :] * lam_ref[1:2, :], axis=-1, keepdims=True))
           - jnp.exp(jnp.sum(lam_ref[2:3, :] * lam_ref[3:4, :], axis=-1, keepdims=True))
           + LAMBDA_INIT)
    row = lax.broadcasted_iota(jnp.int32, (tq, tq), 0)
    col = lax.broadcasted_iota(jnp.int32, (tq, tq), 1)
    diag_visible = col <= row
    v_m = vm_ref[...]
    scale = DA_DK ** -0.5 * math.log2(math.e)
    nt = (((1,), (1,)), ((), ()))

    def attend(n_prev):
        maps = range(2)
        qn = [(rms(q_ref[:, m * DA_DK:(m + 1) * DA_DK].astype(F32), qn_ref[...]) * scale).astype(BF16)
              for m in maps]
        sd = [jnp.where(diag_visible,
                        lax.dot_general(qn[m], kx_s[m, n_prev:n_prev + tq, :], nt,
                                        preferred_element_type=F32), NEG_BIG) for m in maps]
        sm = [lax.dot_general(qn[m], kmeta_s[m], nt, preferred_element_type=F32) for m in maps]
        mx = [jnp.maximum(jnp.max(sd[m], axis=-1, keepdims=True),
                          jnp.max(sm[m], axis=-1, keepdims=True)) for m in maps]
        if n_prev:
            sp = [lax.dot_general(qn[m], kx_s[m, 0:n_prev, :], nt, preferred_element_type=F32)
                  for m in maps]
            mx = [jnp.maximum(mx[m], jnp.max(sp[m], axis=-1, keepdims=True)) for m in maps]
        pd = [jnp.exp2(sd[m] - mx[m]) for m in maps]
        pm = [jnp.exp2(sm[m] - mx[m]) for m in maps]
        den = [jnp.sum(pd[m], axis=-1, keepdims=True) + jnp.sum(pm[m], axis=-1, keepdims=True)
               for m in maps]
        acc = [jnp.dot(pd[m].astype(BF16), v_ref[n_prev:n_prev + tq, :], preferred_element_type=F32)
               + jnp.dot(pm[m].astype(BF16), v_m, preferred_element_type=F32) for m in maps]
        if n_prev:
            pp = [jnp.exp2(sp[m] - mx[m]) for m in maps]
            den = [den[m] + jnp.sum(pp[m], axis=-1, keepdims=True) for m in maps]
            acc = [acc[m] + jnp.dot(pp[m].astype(BF16), v_ref[0:n_prev, :], preferred_element_type=F32)
                   for m in maps]
        out = acc[0] / den[0] - lam * (acc[1] / den[1])
        out = rms(out, sn_ref[...]) * (1.0 - LAMBDA_INIT)
        o_ref[...] = out.astype(o_ref.dtype)

    for blk in range(seq // tq):
        @pl.when(qi == blk)
        def _():
            attend(blk * tq)


def diff_attention(proj_x, proj_m, q_norm, k_norm, lam4, sub_norm, bsz, seq, tq=512):
    nq = seq // tq
    cq = COL_AQ // DA_DV
    ck = COL_AK // DA_DV
    cv = COL_AV // DA_DV
    return pl.pallas_call(
        functools.partial(_diff_attn_kernel, tq=tq, seq=seq),
        grid=(bsz, DA_HEADS, nq),
        in_specs=[
            pl.BlockSpec((tq, DA_DV), lambda b, h, i: (b * nq + i, cq + h)),
            pl.BlockSpec((seq, DA_DV), lambda b, h, i: (b, ck + h)),
            pl.BlockSpec((seq, DA_DV), lambda b, h, i: (b, cv + h)),
            pl.BlockSpec((N_META, DA_DV), lambda b, h, i: (0, h)),
            pl.BlockSpec((N_META, DA_DV), lambda b, h, i: (0, DA_HEADS + h)),
            pl.BlockSpec((1, DA_DK), lambda b, h, i: (0, 0)),
            pl.BlockSpec((1, DA_DK), lambda b, h, i: (0, 0)),
            pl.BlockSpec((4, DA_DK), lambda b, h, i: (0, 0)),
            pl.BlockSpec((1, DA_DV), lambda b, h, i: (0, 0)),
        ],
        out_specs=pl.BlockSpec((tq, DA_DV), lambda b, h, i: (b * nq + i, h)),
        out_shape=jax.ShapeDtypeStruct((bsz * seq, DA_HEADS * DA_DV), BF16),
        scratch_shapes=[pltpu.VMEM((2, seq, DA_DK), BF16),
                        pltpu.VMEM((2, N_META, DA_DK), BF16)],
        compiler_params=_params("parallel", "parallel", "arbitrary", vmem=VMEM_LIMIT),
        name="diff_attention",
    )(proj_x, proj_x, proj_x, proj_m, proj_m,
      q_norm.reshape(1, DA_DK).astype(F32), k_norm.reshape(1, DA_DK).astype(F32),
      lam4.astype(F32), sub_norm.reshape(1, DA_DV).astype(F32))


def _merge_kernel(ya_ref, yb_ref, wa_ref, wb_ref, ga_ref, gb_ref, o_ref):
    pa = jnp.dot(ya_ref[...], wa_ref[...].astype(BF16), preferred_element_type=F32)
    pb = jnp.dot(yb_ref[...], wb_ref[...].astype(BF16), preferred_element_type=F32)
    ga = jax.nn.sigmoid(ga_ref[...].astype(F32))
    gb = jax.nn.sigmoid(gb_ref[...].astype(F32))
    o_ref[...] = (ga * pa + gb * pb).astype(o_ref.dtype)


def branch_merge(y_a, y_b, w_a, w_b, proj_x, d_model, tm=1024, tn=512):
    m, ka = y_a.shape
    kb = y_b.shape[1]
    cga = COL_GA // tn
    cgb = (COL_GA + d_model) // tn
    return pl.pallas_call(
        _merge_kernel,
        grid=(m // tm, d_model // tn),
        in_specs=[
            pl.BlockSpec((tm, ka), lambda i, j: (i, 0)),
            pl.BlockSpec((tm, kb), lambda i, j: (i, 0)),
            pl.BlockSpec((ka, tn), lambda i, j: (0, j)),
            pl.BlockSpec((kb, tn), lambda i, j: (0, j)),
            pl.BlockSpec((tm, tn), lambda i, j: (i, cga + j)),
            pl.BlockSpec((tm, tn), lambda i, j: (i, cgb + j)),
        ],
        out_specs=pl.BlockSpec((tm, tn), lambda i, j: (i, j)),
        out_shape=jax.ShapeDtypeStruct((m, d_model), BF16),
        compiler_params=_params("parallel", "parallel", vmem=VMEM_LIMIT),
        name="branch_merge",
    )(y_a, y_b, w_a, w_b, proj_x, proj_x)


def _router_kernel(h_ref, nw_ref, whi_ref, wlo_ref, b_ref, idx_ref, gate_ref, up_ref):
    x = h_ref[...]
    u = x * lax.rsqrt(jnp.mean(x * x, axis=-1, keepdims=True) + RMS_EPS) * nw_ref[...]
    u_hi = u.astype(BF16)
    u_lo = (u - u_hi.astype(F32)).astype(BF16)
    half = u.shape[1] // 2
    bits = pltpu.bitcast(u_hi.astype(F32), jnp.uint32)
    up_ref[...] = bits[:, half:] | (bits[:, :half] >> 16)
    logits = (jnp.dot(u_hi, whi_ref[...], preferred_element_type=F32)
              + jnp.dot(u_lo, whi_ref[...], preferred_element_type=F32)
              + jnp.dot(u_hi, wlo_ref[...], preferred_element_type=F32)
              + b_ref[...])
    lane = lax.broadcasted_iota(jnp.int32, logits.shape, 1)
    lane_f = lane.astype(F32)
    vals = logits
    tops, idxs = [], []
    for _ in range(TOP_K):
        mx = jnp.max(vals, axis=-1, keepdims=True)
        ix = jnp.min(jnp.where(vals == mx, lane_f, float(LANES)), axis=-1, keepdims=True).astype(jnp.int32)
        tops.append(mx)
        idxs.append(ix)
        vals = jnp.where(lane == ix, -3.0e38, vals)
    exps = [jnp.exp(tv - tops[0]) for tv in tops]
    den = exps[0] + exps[1] + exps[2] + exps[3]
    idx_out = jnp.zeros(logits.shape, jnp.int32)
    gate_out = jnp.zeros(logits.shape, F32)
    for k in range(TOP_K):
        idx_out = jnp.where(lane == k, idxs[k], idx_out)
        gate_out = jnp.where(lane == k, exps[k] / den, gate_out)
    idx_ref[...] = idx_out
    gate_ref[...] = gate_out


def router(h2, ffn_norm, router_w, router_b, tm=256):
    n, d = h2.shape
    wpad = jnp.zeros((d, LANES), F32).at[:, :N_EXPERTS].set(router_w.astype(F32))
    w_hi = wpad.astype(BF16)
    w_lo = (wpad - w_hi.astype(F32)).astype(BF16)
    bias = jnp.full((1, LANES), NEG_BIG, F32).at[0, :N_EXPERTS].set(router_b.astype(F32))
    return pl.pallas_call(
        _router_kernel,
        grid=(n // tm,),
        in_specs=[
            pl.BlockSpec((tm, d), lambda i: (i, 0)),
            pl.BlockSpec((1, d), lambda i: (0, 0)),
            pl.BlockSpec((d, LANES), lambda i: (0, 0)),
            pl.BlockSpec((d, LANES), lambda i: (0, 0)),
            pl.BlockSpec((1, LANES), lambda i: (0, 0)),
        ],
        out_specs=[pl.BlockSpec((tm, LANES), lambda i: (i, 0)),
                   pl.BlockSpec((tm, LANES), lambda i: (i, 0)),
                   pl.BlockSpec((tm, d // 2), lambda i: (i, 0))],
        out_shape=[jax.ShapeDtypeStruct((n, LANES), jnp.int32),
                   jax.ShapeDtypeStruct((n, LANES), F32),
                   jax.ShapeDtypeStruct((n, d // 2), jnp.uint32)],
        compiler_params=_params("parallel", vmem=VMEM_LIMIT),
        name="router",
    )(h2, ffn_norm.reshape(1, d).astype(F32), w_hi, w_lo, bias)


def _expert_kernel(ie_ref, isb_ref, inb_ref, ifl_ref, tok_ref, up_hbm, wgu_hbm, wd_hbm, bgu_ref,
                   bd_ref, ys_hbm, xbuf, act, ybuf, wgu_buf, wd_buf, wg_s, wu_s, wd_s, zbuf,
                   sem_x, sem_y, sem_gu, sem_d, sem_z, *, j1, j2, tn1, tn2, n_items):
    i = pl.program_id(0)
    nb = inb_ref[i]
    sb = isb_ref[i]
    ex = ie_ref[i]
    R = MOE_ROWS

    n_fill = ifl_ref[i]

    @pl.when(n_fill > 0)
    def _():
        zbuf[...] = jnp.zeros_like(zbuf)

        def z_copy(u):
            return pltpu.make_async_copy(zbuf, ys_hbm.at[pl.ds((sb + u) * R, R), :], sem_z)

        for u in range(MOE_ITEM_BLOCKS):
            @pl.when(u < n_fill)
            def _():
                z_copy(u).start()
        for u in range(MOE_ITEM_BLOCKS):
            @pl.when(u < n_fill)
            def _():
                z_copy(u).wait()

    def gu_copies(expert, jt, slot):
        return [pltpu.make_async_copy(
            wgu_hbm.at[expert, :, pl.ds(pl.multiple_of(which * j1 * tn1 + jt * tn1, tn1), tn1)],
            wgu_buf.at[slot, which], sem_gu.at[slot]) for which in range(2)]

    def d_copy(expert, jd, slot):
        return pltpu.make_async_copy(
            wd_hbm.at[expert, :, pl.ds(pl.multiple_of(jd * tn2, tn2), tn2)],
            wd_buf.at[slot], sem_d.at[slot])

    rmax = xbuf.shape[0]
    HEAD = MOE_HEAD_UNITS * R
    BODY = MOE_BODY_UNITS * R
    last_tok = tok_ref.shape[0] - 1

    def batch(rows):
        return rows // j2

    own_units = jnp.maximum(nb, MOE_HEAD_UNITS)

    def row_gather(base, r):
        tok = tok_ref[jnp.minimum(base + r, last_tok)]
        return pltpu.make_async_copy(up_hbm.at[pl.ds(tok, 1), :], xbuf.at[pl.ds(r, 1), :], sem_x)

    def x_wait_units(n):
        for u in range(MOE_ITEM_BLOCKS):
            @pl.when(u < n)
            def _():
                pltpu.make_async_copy(up_hbm.at[pl.ds(0, R), :], xbuf.at[pl.ds(u * R, R), :],
                                      sem_x).wait()

    yw = tn2 // 2

    def y_copy(slot, r, jd):
        return pltpu.make_async_copy(
            ybuf.at[slot, pl.ds(r * R, R), :],
            ys_hbm.at[pl.ds((sb + r) * R, R), pl.ds(pl.multiple_of(jd * yw, yw), yw)],
            sem_y.at[slot])

    def for_blocks(fn):
        for r in range(MOE_ITEM_BLOCKS):
            @pl.when(r < nb)
            def _():
                fn(r)

    @pl.when(i == 0)
    def _():
        def start(r, c):
            row_gather(sb * R, r).start()
            return c
        lax.fori_loop(0, rmax, start, 0, unroll=8)
        for cp in gu_copies(ex, 0, 0):
            cp.start()

    nb_prev = inb_ref[jnp.maximum(i - 1, 0)]
    x_wait_units(jnp.where(
        i == 0, MOE_ITEM_BLOCKS,
        jnp.where(nb_prev > 0, jnp.maximum(jnp.maximum(nb_prev, MOE_HEAD_UNITS), nb), 0)))

    def gate_up_block(jt, r0, rows, wg, wu):
        x32 = xbuf[pl.ds(r0, rows), :]
        x_lo = pltpu.bitcast(x32 << 16, F32).astype(BF16)
        x_hi = pltpu.bitcast(x32 & jnp.uint32(0xFFFF0000), F32).astype(BF16)
        x = jnp.concatenate([x_lo, x_hi], axis=1)
        g = jnp.dot(x, wg, preferred_element_type=F32) + bgu_ref[pl.ds(jt, 1), :]
        u = jnp.dot(x, wu, preferred_element_type=F32) + bgu_ref[pl.ds(j1 + jt, 1), :]
        gt = jnp.minimum(g, SWIGLU_LIMIT)
        up = jnp.clip(u, -SWIGLU_LIMIT, SWIGLU_LIMIT)
        a = gt * jax.nn.sigmoid(SWIGLU_ALPHA * gt) * (up + 1.0)
        act[jt, pl.ds(r0, rows), :] = a.astype(BF16)

    def down_block(jd, slot, r0, rows, wd):
        lhs = jnp.concatenate([act[c, pl.ds(r0, rows), :] for c in range(j1)], axis=1)
        y = jnp.dot(lhs, wd, preferred_element_type=F32) + bd_ref[pl.ds(jd, 1), :]
        bits = pltpu.bitcast(y.astype(BF16).astype(F32), jnp.uint32)
        ybuf[slot, pl.ds(r0, rows), :] = bits[:, yw:] | (bits[:, :yw] >> 16)

    def rest_blocks(block_fn):
        rest = jnp.maximum(nb - MOE_HEAD_UNITS, 0)
        n_body = rest // MOE_BODY_UNITS
        rem = rest % MOE_BODY_UNITS
        r_tail = HEAD + n_body * BODY
        g_tail = batch(HEAD) + n_body * batch(BODY)

        def body(p, c):
            block_fn(pl.multiple_of(HEAD + p * BODY, R), BODY, batch(HEAD) + p * batch(BODY))
            return c

        lax.fori_loop(0, n_body, body, 0)

        @pl.when(rem >= 2)
        def _():
            block_fn(pl.multiple_of(r_tail, R), 2 * R, g_tail)

        @pl.when(rem % 2 == 1)
        def _():
            two = (rem // 2) * 2
            block_fn(pl.multiple_of(r_tail + two * R, R), R, g_tail + (rem // 2) * batch(2 * R))

    nxt = jnp.minimum(i + 1, n_items - 1)
    next_base = isb_ref[nxt] * R
    fetch_units = jnp.maximum(own_units, inb_ref[nxt])
    rows_per_tile = fetch_units * batch(R)

    n_wd = wd_buf.shape[0]

    @pl.when(nb > 0)
    def _():
        for t0 in range(n_wd - 1):
            d_copy(ex, t0, t0).start()

        def gate_up_tile(jt, carry):
            slot = jt % 2

            @pl.when(jt + 1 < j1)
            def _():
                for cp in gu_copies(ex, jt + 1, 1 - slot):
                    cp.start()

            for cp in gu_copies(ex, jt, slot):
                cp.wait()
            wg = wgu_buf[slot, 0].astype(BF16)
            wu = wgu_buf[slot, 1].astype(BF16)
            wg_s[...] = wg
            wu_s[...] = wu
            gate_up_block(jt, 0, HEAD, wg, wu)
            rest_blocks(lambda r0, rows, g0: gate_up_block(jt, r0, rows, wg_s[...], wu_s[...]))
            return carry

        lax.fori_loop(0, j1, gate_up_tile, 0)

        @pl.when(i + 1 < n_items)
        def _():
            @pl.when(inb_ref[i + 1] > 0)
            def _():
                for cp in gu_copies(ie_ref[i + 1], 0, 0):
                    cp.start()

        def down_tile(jd, carry):
            slot = jd % 2
            wslot = jd % n_wd

            @pl.when(jd + n_wd - 1 < j2)
            def _():
                d_copy(ex, jd + n_wd - 1, (jd + n_wd - 1) % n_wd).start()

            @pl.when(jd >= 2)
            def _():
                for_blocks(lambda r: y_copy(slot, r, jd).wait())

            d_copy(ex, jd, wslot).wait()
            wd = wd_buf[wslot].astype(BF16)
            wd_s[...] = wd
            down_block(jd, slot, 0, HEAD, wd)
            tile_row0 = jd * rows_per_tile

            def gather_batch(first, count):
                for rr in range(count):
                    row_gather(next_base, tile_row0 + first + rr).start()

            gather_batch(0, batch(HEAD))

            def later_block(r0, rows, g0):
                down_block(jd, slot, r0, rows, wd_s[...])
                gather_batch(g0, batch(rows))

            rest_blocks(later_block)

            def make_up(r, c):
                row_gather(next_base, tile_row0 + r).start()
                return c

            lax.fori_loop(own_units * batch(R), rows_per_tile, make_up, 0)
            for_blocks(lambda r: y_copy(slot, r, jd).start())
            return carry

        lax.fori_loop(0, j2, down_tile, 0)
        for_blocks(lambda r: y_copy(0, r, 0).wait())
        for_blocks(lambda r: y_copy(1, r, 0).wait())

        @pl.when(i == n_items - 1)
        def _():
            x_wait_units(fetch_units)


def expert_mlp(u_packed, row_tok, item_e, item_sb, item_nb, item_fill, w_gate_up, b_gate_up, w_down,
               b_down, tn1=256, tn2=MOE_DOWN_TILE):
    p = row_tok.shape[0]
    d = 2 * u_packed.shape[1]
    n_items = item_e.shape[0]
    de = w_down.shape[1]
    j1 = de // tn1
    j2 = d // tn2
    rmax = MOE_ITEM_BLOCKS * MOE_ROWS

    assert rmax % j2 == 0

    def bias_map(i, ie, isb, inb, ifl, tok):
        return (ie[i], 0, 0)

    return pl.pallas_call(
        functools.partial(_expert_kernel, j1=j1, j2=j2, tn1=tn1, tn2=tn2, n_items=n_items),
        grid_spec=pltpu.PrefetchScalarGridSpec(
            num_scalar_prefetch=5,
            grid=(n_items,),
            in_specs=[
                pl.BlockSpec(memory_space=pl.ANY),
                pl.BlockSpec(memory_space=pl.ANY),
                pl.BlockSpec(memory_space=pl.ANY),
                pl.BlockSpec((None, 2 * j1, tn1), bias_map),
                pl.BlockSpec((None, j2, tn2), bias_map),
            ],
            out_specs=pl.BlockSpec(memory_space=pl.ANY),
            scratch_shapes=[
                pltpu.VMEM((rmax, d // 2), jnp.uint32),
                pltpu.VMEM((j1, rmax, tn1), BF16),
                pltpu.VMEM((2, rmax, tn2 // 2), jnp.uint32),
                pltpu.VMEM((2, 2, d, tn1), F32),
                pltpu.VMEM((MOE_DOWN_SLOTS, de, tn2), F32),
                pltpu.VMEM((d, tn1), BF16),
                pltpu.VMEM((d, tn1), BF16),
                pltpu.VMEM((de, tn2), BF16),
                pltpu.VMEM((MOE_ROWS, d // 2), jnp.uint32),
                pltpu.SemaphoreType.DMA(()),
                pltpu.SemaphoreType.DMA((2,)),
                pltpu.SemaphoreType.DMA((2,)),
                pltpu.SemaphoreType.DMA((MOE_DOWN_SLOTS,)),
                pltpu.SemaphoreType.DMA(()),
            ],
        ),
        out_shape=jax.ShapeDtypeStruct((p, d // 2), jnp.uint32),
        compiler_params=_params("arbitrary", vmem=VMEM_LIMIT),
        name="expert_mlp",
    )(item_e, item_sb, item_nb, item_fill, row_tok, u_packed, w_gate_up, w_down,
      b_gate_up.reshape(N_EXPERTS, 2 * j1, tn1), b_down.reshape(N_EXPERTS, j2, tn2))


def _combine_kernel(dest_ref, h_ref, g_ref, ys_hbm, o_ref, buf, sem, *, tt, n_tok, n_steps, yw):
    i = pl.program_id(0)

    def issue(blk, slot):
        for k in range(TOP_K):
            def start(t, c):
                src = dest_ref[k * n_tok + blk * tt + t]
                pltpu.make_async_copy(ys_hbm.at[pl.ds(src, 1), :], buf.at[slot, k, pl.ds(t, 1), :],
                                      sem.at[slot]).start()
                return c
            lax.fori_loop(0, tt, start, 0, unroll=8)

    @pl.when(i == 0)
    def _():
        issue(0, 0)

    slot = i % 2

    @pl.when(i + 1 < n_steps)
    def _():
        issue(i + 1, 1 - slot)

    for k in range(TOP_K):
        pltpu.make_async_copy(ys_hbm.at[pl.ds(0, tt), :], buf.at[slot, k], sem.at[slot]).wait()
    g = g_ref[...]
    lo = None
    hi = None
    for k in range(TOP_K):
        word = buf[slot, k]
        gk = g[:, k:k + 1]
        lo_k = gk * pltpu.bitcast(word << 16, F32)
        hi_k = gk * pltpu.bitcast(word & jnp.uint32(0xFFFF0000), F32)
        lo = lo_k if lo is None else lo + lo_k
        hi = hi_k if hi is None else hi + hi_k
    for c in range(lo.shape[1] // yw):
        o_ref[:, 2 * c * yw:(2 * c + 1) * yw] = (h_ref[:, 2 * c * yw:(2 * c + 1) * yw]
                                                  + lo[:, c * yw:(c + 1) * yw])
        o_ref[:, (2 * c + 1) * yw:(2 * c + 2) * yw] = (h_ref[:, (2 * c + 1) * yw:(2 * c + 2) * yw]
                                                        + hi[:, c * yw:(c + 1) * yw])


def combine(h2, gates, ys, dest_km, yw, tt=128):
    n, d = h2.shape
    return pl.pallas_call(
        functools.partial(_combine_kernel, tt=tt, n_tok=n, n_steps=n // tt, yw=yw),
        grid_spec=pltpu.PrefetchScalarGridSpec(
            num_scalar_prefetch=1,
            grid=(n // tt,),
            in_specs=[pl.BlockSpec((tt, d), lambda i, dst: (i, 0)),
                      pl.BlockSpec((tt, LANES), lambda i, dst: (i, 0)),
                      pl.BlockSpec(memory_space=pl.ANY)],
            out_specs=pl.BlockSpec((tt, d), lambda i, dst: (i, 0)),
            scratch_shapes=[pltpu.VMEM((2, TOP_K, tt, d // 2), jnp.uint32),
                            pltpu.SemaphoreType.DMA((2,))],
        ),
        out_shape=jax.ShapeDtypeStruct((n, d), F32),
        compiler_params=_params("arbitrary", vmem=VMEM_LIMIT),
        name="moe_combine",
    )(dest_km, h2, gates, ys)


def routing_tables(top_idx, n_tok):
    a = n_tok * TOP_K
    e_flat = top_idx.reshape(-1)
    onehot = (e_flat[:, None] == jnp.arange(N_EXPERTS, dtype=jnp.int32)[None, :]).astype(jnp.int32)
    csum = jnp.cumsum(onehot, axis=0)
    rank = jnp.sum(csum * onehot, axis=1) - 1
    counts = csum[-1]
    nblk = (counts + MOE_ROWS - 1) // MOE_ROWS
    blk_end = jnp.cumsum(nblk)
    blk_start = blk_end - nblk
    dest = blk_start[e_flat] * MOE_ROWS + rank
    n_blocks = -(-(a + N_EXPERTS * (MOE_ROWS - 1)) // MOE_ROWS)
    p = n_blocks * MOE_ROWS
    t_flat = jnp.arange(a, dtype=jnp.int32) // TOP_K
    row_tok = jnp.zeros((p,), jnp.int32).at[dest].set(t_flat, unique_indices=True,
                                                      mode='promise_in_bounds')

    n_items = N_EXPERTS + n_blocks // MOE_ITEM_BLOCKS
    items_per_e = (nblk + MOE_ITEM_BLOCKS - 1) // MOE_ITEM_BLOCKS
    item_end = jnp.cumsum(items_per_e)
    item_start = item_end - items_per_e
    total_items = item_end[-1]
    ii = jnp.arange(n_items, dtype=jnp.int32)
    ic = jnp.minimum(ii, total_items - 1)
    e_i = jnp.minimum(jnp.searchsorted(item_end, ic, side='right'), N_EXPERTS - 1).astype(jnp.int32)
    local = ic - item_start[e_i]
    item_sb = (blk_start[e_i] + local * MOE_ITEM_BLOCKS).astype(jnp.int32)
    item_nb = jnp.clip(nblk[e_i] - local * MOE_ITEM_BLOCKS, 0, MOE_ITEM_BLOCKS)
    active = ii < total_items
    item_nb = jnp.where(active, item_nb, 0).astype(jnp.int32)
    fill_start = blk_end[-1] + (ii - total_items) * MOE_ITEM_BLOCKS
    item_fill = jnp.where(active, 0, jnp.clip(n_blocks - fill_start, 0, MOE_ITEM_BLOCKS)).astype(jnp.int32)
    item_sb = jnp.where(active, item_sb, jnp.minimum(fill_start, n_blocks - 1)).astype(jnp.int32)
    dest_km = dest.astype(jnp.int32).reshape(n_tok, TOP_K).T.reshape(-1)
    return dest_km, row_tok, e_i, item_sb, item_nb, item_fill


def kernel(x, meta_tokens, mix_norm, w_in, dn_conv, dn_a_log, dn_dt_bias, dn_out_norm, da_q_norm, da_k_norm, da_lam_q1, da_lam_k1, da_lam_q2, da_lam_k2, da_sub_norm, w_branch_a, w_branch_b, w_out, ffn_norm, router_w, router_b, w_gate_up, b_gate_up, w_down, b_down):
    bsz, seq, d = x.shape
    n_tok = bsz * seq
    x2 = x.reshape(n_tok, d)
    layer = 0

    w_bf = w_in[layer].astype(BF16)
    da_shift = 2 * DN_HEADS
    da_cols = w_bf.shape[1] - COL_BA - da_shift
    tn = 1024

    u_x = rms_cast(x2, mix_norm[layer], 256)
    u_m = rms_cast(meta_tokens.astype(F32), mix_norm[layer], N_META)
    proj_dn = matmul_fullk(u_x, w_bf, BF16, 1024, tn, name="in_proj_dn", n_blocks=COL_BA // tn)
    proj_da = matmul_fullk_shift(u_x, w_bf, BF16, 1024, tn, COL_BA, da_shift, da_cols // tn,
                                 name="in_proj_da")
    ba_x = matmul_fullk(u_x, w_bf, F32, 1024, LANES, name="in_proj_ba",
                        n_offset=COL_BA // LANES, n_blocks=1)
    proj_dn_m = matmul_fullk(u_m, w_bf, BF16, N_META, tn, name="in_proj_dn_meta", n_blocks=COL_DZ // tn)
    proj_kv_m = matmul_fullk_shift(u_m, w_bf, BF16, N_META, tn, COL_BA, da_shift,
                                   (COL_GA - COL_AK) // tn, name="in_proj_kv_meta",
                                   n_offset=COL_AK // tn)
    ba_m = matmul_fullk(u_m, w_bf, F32, N_META, LANES, name="in_proj_ba_meta",
                        n_offset=COL_BA // LANES, n_blocks=1)

    y_a = gated_deltanet(proj_dn, ba_x, proj_dn_m, ba_m, dn_conv[layer], dn_a_log[layer],
                         dn_dt_bias[layer], dn_out_norm[layer], bsz, seq)
    lam4 = jnp.stack([da_lam_q1[layer], da_lam_k1[layer], da_lam_q2[layer], da_lam_k2[layer]])
    y_b = diff_attention(proj_da, proj_kv_m, da_q_norm[layer], da_k_norm[layer], lam4,
                         da_sub_norm[layer], bsz, seq)

    merged = branch_merge(y_a, y_b, w_branch_a[layer], w_branch_b[layer], proj_da, d)
    h2 = matmul_fullk(merged, w_out[layer], F32, 1024, 512, res=x2, name="out_proj")

    top_idx, gates, u_packed = router(h2, ffn_norm[layer], router_w[layer], router_b[layer])
    dest, row_tok, item_e, item_sb, item_nb, item_fill = routing_tables(top_idx[:, :TOP_K], n_tok)
    ys = expert_mlp(u_packed, row_tok, item_e, item_sb, item_nb, item_fill, w_gate_up[layer],
                    b_gate_up[layer], w_down[layer], b_down[layer])
    out = combine(h2, gates, ys, dest, MOE_DOWN_TILE // 2)
    return out.reshape(bsz, seq, d)
```

```python
import functools
import math

import jax
import jax.numpy as jnp
from jax import lax
from jax.experimental import pallas as pl
from jax.experimental.pallas import tpu as pltpu

F32 = jnp.float32
BF16 = jnp.bfloat16

N_META = 16
RMS_EPS = 1e-6
L2_EPS = 1e-6

DN_HEADS = 16
DN_DK = 128
DN_DV = 128
DN_CONV = 4
DN_CHUNK = 64
DN_W = DN_HEADS * DN_DK
DN_HIST = 16
DN_CONV_COLS = 1024

DA_HEADS = 8
DA_DK = 128
DA_DV = 256
LAMBDA_INIT = 0.8 - 0.6 * math.exp(-0.3 * 0)

N_EXPERTS = 32
TOP_K = 4
D_EXPERT = 1536
SWIGLU_LIMIT = 7.0
SWIGLU_ALPHA = 1.702

LANES = 128
MOE_ROWS = 128
MOE_ITEM_BLOCKS = 10
MOE_HEAD_UNITS = 4
MOE_BODY_UNITS = 4
MOE_DOWN_SLOTS = 3
MOE_DOWN_TILE = 512
NEG_BIG = -1e30
VMEM_LIMIT = 56 * 1024 * 1024

COL_DZ = 3 * DN_W
COL_BA = 4 * DN_W
COL_AQ = 0
COL_AK = COL_AQ + DA_HEADS * 2 * DA_DK
COL_AV = COL_AK + DA_HEADS * 2 * DA_DK
COL_GA = COL_AV + DA_HEADS * DA_DV


def _params(*sem, vmem=None):
    return pltpu.CompilerParams(dimension_semantics=sem, vmem_limit_bytes=vmem)


def _rms_cast_kernel(x_ref, w_ref, o_ref):
    x = x_ref[...]
    y = x * lax.rsqrt(jnp.mean(x * x, axis=-1, keepdims=True) + RMS_EPS) * w_ref[...]
    o_ref[...] = y.astype(o_ref.dtype)


def rms_cast(x, w, tm):
    m, d = x.shape
    return pl.pallas_call(
        _rms_cast_kernel,
        grid=(m // tm,),
        in_specs=[pl.BlockSpec((tm, d), lambda i: (i, 0)),
                  pl.BlockSpec((1, d), lambda i: (0, 0))],
        out_specs=pl.BlockSpec((tm, d), lambda i: (i, 0)),
        out_shape=jax.ShapeDtypeStruct((m, d), BF16),
        compiler_params=_params("parallel"),
        name="rms_cast",
    )(x, w.reshape(1, d))


def _mm_fullk_kernel(a_ref, b_ref, *rest, has_res):
    acc = jnp.dot(a_ref[...], b_ref[...].astype(BF16), preferred_element_type=F32)
    if has_res:
        r_ref, o_ref = rest
        acc = acc + r_ref[...]
    else:
        o_ref, = rest
    o_ref[...] = acc.astype(o_ref.dtype)


def matmul_fullk(a, b, out_dtype, tm, tn, res=None, name="matmul", n_offset=0, n_blocks=None):
    m, kd = a.shape
    if n_blocks is None:
        n_blocks = b.shape[1] // tn
    in_specs = [pl.BlockSpec((tm, kd), lambda i, j: (i, 0)),
                pl.BlockSpec((kd, tn), lambda i, j: (0, n_offset + j))]
    args = [a, b]
    if res is not None:
        in_specs.append(pl.BlockSpec((tm, tn), lambda i, j: (i, j)))
        args.append(res)
    return pl.pallas_call(
        functools.partial(_mm_fullk_kernel, has_res=res is not None),
        grid=(m // tm, n_blocks),
        in_specs=in_specs,
        out_specs=pl.BlockSpec((tm, tn), lambda i, j: (i, j)),
        out_shape=jax.ShapeDtypeStruct((m, n_blocks * tn), out_dtype),
        compiler_params=_params("parallel", "parallel", vmem=VMEM_LIMIT),
        name=name,
    )(*args)


def _mm_shift_kernel(a_ref, b0_ref, b1_ref, o_ref, *, shift):
    b = jnp.concatenate([b0_ref[:, shift:], b1_ref[:, :shift]], axis=1)
    o_ref[...] = jnp.dot(a_ref[...], b, preferred_element_type=F32).astype(o_ref.dtype)


def matmul_fullk_shift(a, b, out_dtype, tm, tn, col0, shift, n_blocks, name, n_offset=0):
    m, kd = a.shape
    first = col0 // tn + n_offset
    return pl.pallas_call(
        functools.partial(_mm_shift_kernel, shift=shift),
        grid=(m // tm, n_blocks),
        in_specs=[pl.BlockSpec((tm, kd), lambda i, j: (i, 0)),
                  pl.BlockSpec((kd, tn), lambda i, j: (0, first + j)),
                  pl.BlockSpec((kd, LANES), lambda i, j: (0, (first + j + 1) * (tn // LANES)))],
        out_specs=pl.BlockSpec((tm, tn), lambda i, j: (i, j)),
        out_shape=jax.ShapeDtypeStruct((m, n_blocks * tn), out_dtype),
        compiler_params=_params("parallel", "parallel", vmem=VMEM_LIMIT),
        name=name,
    )(a, b, b)


def _deltanet_kernel(qkv_ref, z_ref, ba_ref, qkvm_ref, bam_ref, cw_ref, hp_ref, nw_ref,
                     o_ref, full_ref, act_ref, ba_s, s_ref):
    t = pl.program_id(1)
    C = DN_CHUNK
    HIST = DN_HIST
    n_pad = C - N_META

    @pl.when(t == 0)
    def _():
        s_ref[...] = jnp.zeros_like(s_ref)
        full_ref[0:HIST + n_pad, :] = jnp.zeros((HIST + n_pad, 3 * DN_W), BF16)
        full_ref[HIST + n_pad:HIST + C, :] = qkvm_ref[...]
        ba_s[0:n_pad, :] = jnp.zeros((n_pad, LANES), F32)
        ba_s[n_pad:C, :] = bam_ref[...]

    @pl.when(t > 0)
    def _():
        full_ref[HIST:HIST + C, :] = qkv_ref[...]
        ba_s[...] = ba_ref[...]

    n_shift = DN_CONV - 1
    sr = lax.broadcasted_iota(jnp.int32, (n_shift * C, HIST + C), 0)
    sc = lax.broadcasted_iota(jnp.int32, (n_shift * C, HIST + C), 1)
    shift = jnp.where(sc == HIST - n_shift + (sr % C) + (sr // C), 1.0, 0.0).astype(BF16)
    for c0 in range(0, 3 * DN_W, DN_CONV_COLS):
        cs = slice(c0, c0 + DN_CONV_COLS)
        delayed = jnp.dot(shift, full_ref[:, cs], preferred_element_type=F32)
        conv = cw_ref[n_shift:n_shift + 1, cs] * full_ref[HIST:HIST + C, cs].astype(F32)
        for j in range(n_shift):
            conv = conv + cw_ref[j:j + 1, cs] * delayed[j * C:(j + 1) * C, :]
        act_ref[:, cs] = conv * jax.nn.sigmoid(conv)
    full_ref[0:HIST, :] = full_ref[C:C + HIST, :]

    row1 = lax.broadcasted_iota(jnp.int32, (C, 1), 0)
    valid = jnp.where((t > 0) | (row1 >= n_pad), 1.0, 0.0).astype(F32)
    ba = ba_s[...]
    beta_all = jax.nn.sigmoid(ba) * valid
    xg = ba + hp_ref[1:2, :]
    softplus = jnp.maximum(xg, 0.0) + jnp.log1p(jnp.exp(-jnp.abs(xg)))
    g_all = -jnp.exp(hp_ref[0:1, :]) * softplus * valid

    row = lax.broadcasted_iota(jnp.int32, (C, C), 0)
    col = lax.broadcasted_iota(jnp.int32, (C, C), 1)
    causal = row >= col
    strict = row > col
    eye = jnp.where(row == col, 1.0, 0.0).astype(F32)
    tril = jnp.where(causal, 1.0, 0.0).astype(BF16)

    g_hi = g_all.astype(BF16)
    r1 = g_all - g_hi.astype(F32)
    g_mid = r1.astype(BF16)
    g_lo = (r1 - g_mid.astype(F32)).astype(BF16)
    g3 = jnp.dot(tril, jnp.concatenate([g_hi, g_mid, g_lo], axis=1), preferred_element_type=F32)
    gcum = g3[:, 0:LANES] + g3[:, LANES:2 * LANES] + g3[:, 2 * LANES:3 * LANES]
    gcum_t = gcum.T

    scale = DN_DK ** -0.5
    nw = nw_ref[...]
    heads = range(DN_HEADS)

    def mm(a, b):
        return jnp.dot(a.astype(BF16), b.astype(BF16), preferred_element_type=F32)

    kn_b, lhs1, decay, rhs, qe_b, kdec_t, e_last = [], [], [], [], [], [], []
    for h in heads:
        qh = act_ref[:, h * DN_DK:(h + 1) * DN_DK]
        kh = act_ref[:, DN_W + h * DN_DK:DN_W + (h + 1) * DN_DK]
        vh = act_ref[:, 2 * DN_W + h * DN_DV:2 * DN_W + (h + 1) * DN_DV]
        qn = qh * (lax.rsqrt(jnp.sum(qh * qh, axis=-1, keepdims=True) + L2_EPS) * scale)
        kn = kh * lax.rsqrt(jnp.sum(kh * kh, axis=-1, keepdims=True) + L2_EPS)
        beta = beta_all[:, h:h + 1]
        gc = gcum[:, DN_HEADS + h:DN_HEADS + h + 1]
        gr = gcum_t[DN_HEADS + h:DN_HEADS + h + 1, :]
        g_last = gc[C - 1:C, :]
        e_g = jnp.exp(gc)
        kb = kn * beta
        kn_b.append(kn.astype(BF16))
        lhs1.append(jnp.concatenate([kb, qn], axis=0).astype(BF16))
        decay.append(jnp.exp(jnp.where(causal, gc - gr, NEG_BIG)))
        rhs.append(jnp.concatenate([vh * beta, kb * e_g], axis=1).astype(BF16))
        qe_b.append((qn * e_g).astype(BF16))
        kdec_t.append((kn * jnp.exp(g_last - gc)).T.astype(BF16))
        e_last.append(jnp.exp(g_last))

    kq = [lax.dot_general(lhs1[h], kn_b[h], (((1,), (1,)), ((), ())), preferred_element_type=F32)
          for h in heads]
    qk_b = [jnp.where(causal, kq[h][C:2 * C, :] * decay[h], 0.0).astype(BF16) for h in heads]
    b1 = [-jnp.where(strict, kq[h][0:C, :] * decay[h], 0.0) for h in heads]
    b2 = [mm(b1[h], b1[h]) for h in heads]
    b4 = [mm(b2[h], b2[h]) for h in heads]
    p1 = [(eye + b1[h]) + mm(eye + b1[h], b2[h]) for h in heads]
    b8 = [mm(b4[h], b4[h]) for h in heads]
    b16 = [mm(b8[h], b8[h]) for h in heads]
    p2 = [(eye + b4[h]) + mm(eye + b4[h], b8[h]) for h in heads]
    b32 = [mm(b16[h], b16[h]) for h in heads]
    p12 = [mm(p1[h], p2[h]) for h in heads]
    p3 = [(eye + b16[h]) + mm(eye + b16[h], b32[h]) for h in heads]
    inv = [mm(p12[h], p3[h]) for h in heads]
    sol = [mm(inv[h], rhs[h]) for h in heads]

    s_old = [s_ref[h] for h in heads]
    ws_lhs = [jnp.concatenate([sol[h][:, DN_DV:2 * DN_DV].astype(BF16), qe_b[h]], axis=0) for h in heads]
    ws = [mm(ws_lhs[h], s_old[h]) for h in heads]
    v_new = [(sol[h][:, 0:DN_DV] - ws[h][0:C, :]).astype(BF16) for h in heads]
    ov_lhs = [jnp.concatenate([qk_b[h], kdec_t[h]], axis=0) for h in heads]
    ov = [mm(ov_lhs[h], v_new[h]) for h in heads]
    for h in heads:
        s_ref[h] = s_old[h] * e_last[h] + ov[h][C:C + DN_DK, :]
        o = ws[h][C:2 * C, :] + ov[h][0:C, :]
        zh = z_ref[:, h * DN_DV:(h + 1) * DN_DV].astype(F32)
        o = o * lax.rsqrt(jnp.mean(o * o, axis=-1, keepdims=True) + RMS_EPS) * nw
        o_ref[:, h * DN_DV:(h + 1) * DN_DV] = (o * (zh * jax.nn.sigmoid(zh))).astype(o_ref.dtype)


def gated_deltanet(proj_x, ba_x, proj_m, ba_m, conv_w, a_log, dt_bias, norm_w, bsz, seq):
    C = DN_CHUNK
    n_chunks = seq // C
    w3 = 3 * DN_W
    hp = jnp.zeros((2, LANES), F32)
    hp = hp.at[0, DN_HEADS:2 * DN_HEADS].set(a_log.astype(F32))
    hp = hp.at[1, DN_HEADS:2 * DN_HEADS].set(dt_bias.astype(F32))

    def xrow(b, t):
        return b * n_chunks + jnp.maximum(t - 1, 0)

    return pl.pallas_call(
        _deltanet_kernel,
        grid=(bsz, n_chunks + 1),
        in_specs=[
            pl.BlockSpec((C, w3), lambda b, t: (xrow(b, t), 0)),
            pl.BlockSpec((C, DN_W), lambda b, t: (xrow(b, t), COL_DZ // DN_W)),
            pl.BlockSpec((C, LANES), lambda b, t: (xrow(b, t), 0)),
            pl.BlockSpec((N_META, w3), lambda b, t: (0, 0)),
            pl.BlockSpec((N_META, LANES), lambda b, t: (0, 0)),
            pl.BlockSpec((DN_CONV, w3), lambda b, t: (0, 0)),
            pl.BlockSpec((2, LANES), lambda b, t: (0, 0)),
            pl.BlockSpec((1, DN_DV), lambda b, t: (0, 0)),
        ],
        out_specs=pl.BlockSpec((C, DN_W), lambda b, t: (xrow(b, t), 0)),
        out_shape=jax.ShapeDtypeStruct((bsz * seq, DN_W), BF16),
        scratch_shapes=[
            pltpu.VMEM((DN_HIST + C, w3), BF16),
            pltpu.VMEM((C, w3), F32),
            pltpu.VMEM((C, LANES), F32),
            pltpu.VMEM((DN_HEADS, DN_DK, DN_DV), F32),
        ],
        compiler_params=_params("parallel", "arbitrary", vmem=VMEM_LIMIT),
        name="gated_deltanet",
    )(proj_x, proj_x, ba_x, proj_m, ba_m, conv_w.astype(F32), hp, norm_w.reshape(1, DN_DV).astype(F32))


def _diff_attn_kernel(q_ref, k_ref, v_ref, km_ref, vm_ref, qn_ref, kn_ref, lam_ref, sn_ref,
                      o_ref, kx_s, kmeta_s, *, tq, seq):
    qi = pl.program_id(2)

    def rms(x, w):
        return x * lax.rsqrt(jnp.mean(x * x, axis=-1, keepdims=True) + RMS_EPS) * w

    @pl.when(qi == 0)
    def _():
        for m in range(2):
            kx = k_ref[:, m * DA_DK:(m + 1) * DA_DK].astype(F32)
            kx_s[m] = rms(kx, kn_ref[...]).astype(BF16)
            kmx = km_ref[:, m * DA_DK:(m + 1) * DA_DK].astype(F32)
            kmeta_s[m] = rms(kmx, kn_ref[...]).astype(BF16)

    lam = (jnp.exp(jnp.sum(lam_ref[0:1, :] * lam_ref[1:2, :], axis=-1, keepdims=True))
           - jnp.exp(jnp.sum(lam_ref[2:3, :] * lam_ref[3:4, :], axis=-1, keepdims=True))
           + LAMBDA_INIT)
    row = lax.broadcasted_iota(jnp.int32, (tq, tq), 0)
    col = lax.broadcasted_iota(jnp.int32, (tq, tq), 1)
    diag_visible = col <= row
    v_m = vm_ref[...]
    scale = DA_DK ** -0.5 * math.log2(math.e)
    nt = (((1,), (1,)), ((), ()))

    def attend(n_prev):
        maps = range(2)
        qn = [(rms(q_ref[:, m * DA_DK:(m + 1) * DA_DK].astype(F32), qn_ref[...]) * scale).astype(BF16)
              for m in maps]
        sd = [jnp.where(diag_visible,
                        lax.dot_general(qn[m], kx_s[m, n_prev:n_prev + tq, :], nt,
                                        preferred_element_type=F32), NEG_BIG) for m in maps]
        sm = [lax.dot_general(qn[m], kmeta_s[m], nt, preferred_element_type=F32) for m in maps]
        mx = [jnp.maximum(jnp.max(sd[m], axis=-1, keepdims=True),
                          jnp.max(sm[m], axis=-1, keepdims=True)) for m in maps]
        if n_prev:
            sp = [lax.dot_general(qn[m], kx_s[m, 0:n_prev, :], nt, preferred_element_type=F32)
                  for m in maps]
            mx = [jnp.maximum(mx[m], jnp.max(sp[m], axis=-1, keepdims=True)) for m in maps]
        pd = [jnp.exp2(sd[m] - mx[m]) for m in maps]
        pm = [jnp.exp2(sm[m] - mx[m]) for m in maps]
        den = [jnp.sum(pd[m], axis=-1, keepdims=True) + jnp.sum(pm[m], axis=-1, keepdims=True)
               for m in maps]
        acc = [jnp.dot(pd[m].astype(BF16), v_ref[n_prev:n_prev + tq, :], preferred_element_type=F32)
               + jnp.dot(pm[m].astype(BF16), v_m, preferred_element_type=F32) for m in maps]
        if n_prev:
            pp = [jnp.exp2(sp[m] - mx[m]) for m in maps]
            den = [den[m] + jnp.sum(pp[m], axis=-1, keepdims=True) for m in maps]
            acc = [acc[m] + jnp.dot(pp[m].astype(BF16), v_ref[0:n_prev, :], preferred_element_type=F32)
                   for m in maps]
        out = acc[0] / den[0] - lam * (acc[1] / den[1])
        out = rms(out, sn_ref[...]) * (1.0 - LAMBDA_INIT)
        o_ref[...] = out.astype(o_ref.dtype)

    for blk in range(seq // tq):
        @pl.when(qi == blk)
        def _():
            attend(blk * tq)


def diff_attention(proj_x, proj_m, q_norm, k_norm, lam4, sub_norm, bsz, seq, tq=512):
    nq = seq // tq
    cq = COL_AQ // DA_DV
    ck = COL_AK // DA_DV
    cv = COL_AV // DA_DV
    return pl.pallas_call(
        functools.partial(_diff_attn_kernel, tq=tq, seq=seq),
        grid=(bsz, DA_HEADS, nq),
        in_specs=[
            pl.BlockSpec((tq, DA_DV), lambda b, h, i: (b * nq + i, cq + h)),
            pl.BlockSpec((seq, DA_DV), lambda b, h, i: (b, ck + h)),
            pl.BlockSpec((seq, DA_DV), lambda b, h, i: (b, cv + h)),
            pl.BlockSpec((N_META, DA_DV), lambda b, h, i: (0, h)),
            pl.BlockSpec((N_META, DA_DV), lambda b, h, i: (0, DA_HEADS + h)),
            pl.BlockSpec((1, DA_DK), lambda b, h, i: (0, 0)),
            pl.BlockSpec((1, DA_DK), lambda b, h, i: (0, 0)),
            pl.BlockSpec((4, DA_DK), lambda b, h, i: (0, 0)),
            pl.BlockSpec((1, DA_DV), lambda b, h, i: (0, 0)),
        ],
        out_specs=pl.BlockSpec((tq, DA_DV), lambda b, h, i: (b * nq + i, h)),
        out_shape=jax.ShapeDtypeStruct((bsz * seq, DA_HEADS * DA_DV), BF16),
        scratch_shapes=[pltpu.VMEM((2, seq, DA_DK), BF16),
                        pltpu.VMEM((2, N_META, DA_DK), BF16)],
        compiler_params=_params("parallel", "parallel", "arbitrary", vmem=VMEM_LIMIT),
        name="diff_attention",
    )(proj_x, proj_x, proj_x, proj_m, proj_m,
      q_norm.reshape(1, DA_DK).astype(F32), k_norm.reshape(1, DA_DK).astype(F32),
      lam4.astype(F32), sub_norm.reshape(1, DA_DV).astype(F32))


def _merge_kernel(ya_ref, yb_ref, wa_ref, wb_ref, ga_ref, gb_ref, o_ref):
    pa = jnp.dot(ya_ref[...], wa_ref[...].astype(BF16), preferred_element_type=F32)
    pb = jnp.dot(yb_ref[...], wb_ref[...].astype(BF16), preferred_element_type=F32)
    ga = jax.nn.sigmoid(ga_ref[...].astype(F32))
    gb = jax.nn.sigmoid(gb_ref[...].astype(F32))
    o_ref[...] = (ga * pa + gb * pb).astype(o_ref.dtype)


def branch_merge(y_a, y_b, w_a, w_b, proj_x, d_model, tm=1024, tn=512):
    m, ka = y_a.shape
    kb = y_b.shape[1]
    cga = COL_GA // tn
    cgb = (COL_GA + d_model) // tn
    return pl.pallas_call(
        _merge_kernel,
        grid=(m // tm, d_model // tn),
        in_specs=[
            pl.BlockSpec((tm, ka), lambda i, j: (i, 0)),
            pl.BlockSpec((tm, kb), lambda i, j: (i, 0)),
            pl.BlockSpec((ka, tn), lambda i, j: (0, j)),
            pl.BlockSpec((kb, tn), lambda i, j: (0, j)),
            pl.BlockSpec((tm, tn), lambda i, j: (i, cga + j)),
            pl.BlockSpec((tm, tn), lambda i, j: (i, cgb + j)),
        ],
        out_specs=pl.BlockSpec((tm, tn), lambda i, j: (i, j)),
        out_shape=jax.ShapeDtypeStruct((m, d_model), BF16),
        compiler_params=_params("parallel", "parallel", vmem=VMEM_LIMIT),
        name="branch_merge",
    )(y_a, y_b, w_a, w_b, proj_x, proj_x)


def _router_kernel(h_ref, nw_ref, whi_ref, wlo_ref, b_ref, idx_ref, gate_ref, up_ref):
    x = h_ref[...]
    u = x * lax.rsqrt(jnp.mean(x * x, axis=-1, keepdims=True) + RMS_EPS) * nw_ref[...]
    u_hi = u.astype(BF16)
    u_lo = (u - u_hi.astype(F32)).astype(BF16)
    half = u.shape[1] // 2
    bits = pltpu.bitcast(u_hi.astype(F32), jnp.uint32)
    up_ref[...] = bits[:, half:] | (bits[:, :half] >> 16)
    logits = (jnp.dot(u_hi, whi_ref[...], preferred_element_type=F32)
              + jnp.dot(u_lo, whi_ref[...], preferred_element_type=F32)
              + jnp.dot(u_hi, wlo_ref[...], preferred_element_type=F32)
              + b_ref[...])
    lane = lax.broadcasted_iota(jnp.int32, logits.shape, 1)
    lane_f = lane.astype(F32)
    vals = logits
    tops, idxs = [], []
    for _ in range(TOP_K):
        mx = jnp.max(vals, axis=-1, keepdims=True)
        ix = jnp.min(jnp.where(vals == mx, lane_f, float(LANES)), axis=-1, keepdims=True).astype(jnp.int32)
        tops.append(mx)
        idxs.append(ix)
        vals = jnp.where(lane == ix, -3.0e38, vals)
    exps = [jnp.exp(tv - tops[0]) for tv in tops]
    den = exps[0] + exps[1] + exps[2] + exps[3]
    idx_out = jnp.zeros(logits.shape, jnp.int32)
    gate_out = jnp.zeros(logits.shape, F32)
    for k in range(TOP_K):
        idx_out = jnp.where(lane == k, idxs[k], idx_out)
        gate_out = jnp.where(lane == k, exps[k] / den, gate_out)
    idx_ref[...] = idx_out
    gate_ref[...] = gate_out


def router(h2, ffn_norm, router_w, router_b, tm=256):
    n, d = h2.shape
    wpad = jnp.zeros((d, LANES), F32).at[:, :N_EXPERTS].set(router_w.astype(F32))
    w_hi = wpad.astype(BF16)
    w_lo = (wpad - w_hi.astype(F32)).astype(BF16)
    bias = jnp.full((1, LANES), NEG_BIG, F32).at[0, :N_EXPERTS].set(router_b.astype(F32))
    return pl.pallas_call(
        _router_kernel,
        grid=(n // tm,),
        in_specs=[
            pl.BlockSpec((tm, d), lambda i: (i, 0)),
            pl.BlockSpec((1, d), lambda i: (0, 0)),
            pl.BlockSpec((d, LANES), lambda i: (0, 0)),
            pl.BlockSpec((d, LANES), lambda i: (0, 0)),
            pl.BlockSpec((1, LANES), lambda i: (0, 0)),
        ],
        out_specs=[pl.BlockSpec((tm, LANES), lambda i: (i, 0)),
                   pl.BlockSpec((tm, LANES), lambda i: (i, 0)),
                   pl.BlockSpec((tm, d // 2), lambda i: (i, 0))],
        out_shape=[jax.ShapeDtypeStruct((n, LANES), jnp.int32),
                   jax.ShapeDtypeStruct((n, LANES), F32),
                   jax.ShapeDtypeStruct((n, d // 2), jnp.uint32)],
        compiler_params=_params("parallel", vmem=VMEM_LIMIT),
        name="router",
    )(h2, ffn_norm.reshape(1, d).astype(F32), w_hi, w_lo, bias)


def _expert_kernel(ie_ref, isb_ref, inb_ref, ifl_ref, tok_ref, up_hbm, wgu_hbm, wd_hbm, bgu_ref,
                   bd_ref, ys_hbm, xbuf, act, ybuf, wgu_buf, wd_buf, wg_s, wu_s, wd_s, zbuf,
                   sem_x, sem_y, sem_gu, sem_d, sem_z, *, j1, j2, tn1, tn2, n_items):
    i = pl.program_id(0)
    nb = inb_ref[i]
    sb = isb_ref[i]
    ex = ie_ref[i]
    R = MOE_ROWS

    n_fill = ifl_ref[i]

    @pl.when(n_fill > 0)
    def _():
        zbuf[...] = jnp.zeros_like(zbuf)

        def z_copy(u):
            return pltpu.make_async_copy(zbuf, ys_hbm.at[pl.ds((sb + u) * R, R), :], sem_z)

        for u in range(MOE_ITEM_BLOCKS):
            @pl.when(u < n_fill)
            def _():
                z_copy(u).start()
        for u in range(MOE_ITEM_BLOCKS):
            @pl.when(u < n_fill)
            def _():
                z_copy(u).wait()

    def gu_copies(expert, jt, slot):
        return [pltpu.make_async_copy(
            wgu_hbm.at[expert, :, pl.ds(pl.multiple_of(which * j1 * tn1 + jt * tn1, tn1), tn1)],
            wgu_buf.at[slot, which], sem_gu.at[slot]) for which in range(2)]

    def d_copy(expert, jd, slot):
        return pltpu.make_async_copy(
            wd_hbm.at[expert, :, pl.ds(pl.multiple_of(jd * tn2, tn2), tn2)],
            wd_buf.at[slot], sem_d.at[slot])

    rmax = xbuf.shape[0]
    HEAD = MOE_HEAD_UNITS * R
    BODY = MOE_BODY_UNITS * R
    last_tok = tok_ref.shape[0] - 1

    def batch(rows):
        return rows // j2

    own_units = jnp.maximum(nb, MOE_HEAD_UNITS)

    def row_gather(base, r):
        tok = tok_ref[jnp.minimum(base + r, last_tok)]
        return pltpu.make_async_copy(up_hbm.at[pl.ds(tok, 1), :], xbuf.at[pl.ds(r, 1), :], sem_x)

    def x_wait_units(n):
        for u in range(MOE_ITEM_BLOCKS):
            @pl.when(u < n)
            def _():
                pltpu.make_async_copy(up_hbm.at[pl.ds(0, R), :], xbuf.at[pl.ds(u * R, R), :],
                                      sem_x).wait()

    yw = tn2 // 2

    def y_copy(slot, r, jd):
        return pltpu.make_async_copy(
            ybuf.at[slot, pl.ds(r * R, R), :],
            ys_hbm.at[pl.ds((sb + r) * R, R), pl.ds(pl.multiple_of(jd * yw, yw), yw)],
            sem_y.at[slot])

    def for_blocks(fn):
        for r in range(MOE_ITEM_BLOCKS):
            @pl.when(r < nb)
            def _():
                fn(r)

    @pl.when(i == 0)
    def _():
        def start(r, c):
            row_gather(sb * R, r).start()
            return c
        lax.fori_loop(0, rmax, start, 0, unroll=8)
        for cp in gu_copies(ex, 0, 0):
            cp.start()

    nb_prev = inb_ref[jnp.maximum(i - 1, 0)]
    x_wait_units(jnp.where(
        i == 0, MOE_ITEM_BLOCKS,
        jnp.where(nb_prev > 0, jnp.maximum(jnp.maximum(nb_prev, MOE_HEAD_UNITS), nb), 0)))

    def gate_up_block(jt, r0, rows, wg, wu):
        x32 = xbuf[pl.ds(r0, rows), :]
        x_lo = pltpu.bitcast(x32 << 16, F32).astype(BF16)
        x_hi = pltpu.bitcast(x32 & jnp.uint32(0xFFFF0000), F32).astype(BF16)
        x = jnp.concatenate([x_lo, x_hi], axis=1)
        g = jnp.dot(x, wg, preferred_element_type=F32) + bgu_ref[pl.ds(jt, 1), :]
        u = jnp.dot(x, wu, preferred_element_type=F32) + bgu_ref[pl.ds(j1 + jt, 1), :]
        gt = jnp.minimum(g, SWIGLU_LIMIT)
        up = jnp.clip(u, -SWIGLU_LIMIT, SWIGLU_LIMIT)
        a = gt * jax.nn.sigmoid(SWIGLU_ALPHA * gt) * (up + 1.0)
        act[jt, pl.ds(r0, rows), :] = a.astype(BF16)

    def down_block(jd, slot, r0, rows, wd):
        lhs = jnp.concatenate([act[c, pl.ds(r0, rows), :] for c in range(j1)], axis=1)
        y = jnp.dot(lhs, wd, preferred_element_type=F32) + bd_ref[pl.ds(jd, 1), :]
        bits = pltpu.bitcast(y.astype(BF16).astype(F32), jnp.uint32)
        ybuf[slot, pl.ds(r0, rows), :] = bits[:, yw:] | (bits[:, :yw] >> 16)

    def rest_blocks(block_fn):
        rest = jnp.maximum(nb - MOE_HEAD_UNITS, 0)
        n_body = rest // MOE_BODY_UNITS
        rem = rest % MOE_BODY_UNITS
        r_tail = HEAD + n_body * BODY
        g_tail = batch(HEAD) + n_body * batch(BODY)

        def body(p, c):
            block_fn(pl.multiple_of(HEAD + p * BODY, R), BODY, batch(HEAD) + p * batch(BODY))
            return c

        lax.fori_loop(0, n_body, body, 0)

        @pl.when(rem >= 2)
        def _():
            block_fn(pl.multiple_of(r_tail, R), 2 * R, g_tail)

        @pl.when(rem % 2 == 1)
        def _():
            two = (rem // 2) * 2
            block_fn(pl.multiple_of(r_tail + two * R, R), R, g_tail + (rem // 2) * batch(2 * R))

    nxt = jnp.minimum(i + 1, n_items - 1)
    next_base = isb_ref[nxt] * R
    fetch_units = jnp.maximum(own_units, inb_ref[nxt])
    rows_per_tile = fetch_units * batch(R)

    n_wd = wd_buf.shape[0]

    @pl.when(nb > 0)
    def _():
        for t0 in range(n_wd - 1):
            d_copy(ex, t0, t0).start()

        def gate_up_tile(jt, carry):
            slot = jt % 2

            @pl.when(jt + 1 < j1)
            def _():
                for cp in gu_copies(ex, jt + 1, 1 - slot):
                    cp.start()

            for cp in gu_copies(ex, jt, slot):
                cp.wait()
            wg = wgu_buf[slot, 0].astype(BF16)
            wu = wgu_buf[slot, 1].astype(BF16)
            wg_s[...] = wg
            wu_s[...] = wu
            gate_up_block(jt, 0, HEAD, wg, wu)
            rest_blocks(lambda r0, rows, g0: gate_up_block(jt, r0, rows, wg_s[...], wu_s[...]))
            return carry

        lax.fori_loop(0, j1, gate_up_tile, 0)

        @pl.when(i + 1 < n_items)
        def _():
            @pl.when(inb_ref[i + 1] > 0)
            def _():
                for cp in gu_copies(ie_ref[i + 1], 0, 0):
                    cp.start()

        def down_tile(jd, carry):
            slot = jd % 2
            wslot = jd % n_wd

            @pl.when(jd + n_wd - 1 < j2)
            def _():
                d_copy(ex, jd + n_wd - 1, (jd + n_wd - 1) % n_wd).start()

            @pl.when(jd >= 2)
            def _():
                for_blocks(lambda r: y_copy(slot, r, jd).wait())

            d_copy(ex, jd, wslot).wait()
            wd = wd_buf[wslot].astype(BF16)
            wd_s[...] = wd
            down_block(jd, slot, 0, HEAD, wd)
            tile_row0 = jd * rows_per_tile

            def gather_batch(first, count):
                for rr in range(count):
                    row_gather(next_base, tile_row0 + first + rr).start()

            gather_batch(0, batch(HEAD))

            def later_block(r0, rows, g0):
                down_block(jd, slot, r0, rows, wd_s[...])
                gather_batch(g0, batch(rows))

            rest_blocks(later_block)

            def make_up(r, c):
                row_gather(next_base, tile_row0 + r).start()
                return c

            lax.fori_loop(own_units * batch(R), rows_per_tile, make_up, 0)
            for_blocks(lambda r: y_copy(slot, r, jd).start())
            return carry

        lax.fori_loop(0, j2, down_tile, 0)
        for_blocks(lambda r: y_copy(0, r, 0).wait())
        for_blocks(lambda r: y_copy(1, r, 0).wait())

        @pl.when(i == n_items - 1)
        def _():
            x_wait_units(fetch_units)


def expert_mlp(u_packed, row_tok, item_e, item_sb, item_nb, item_fill, w_gate_up, b_gate_up, w_down,
               b_down, tn1=256, tn2=MOE_DOWN_TILE):
    p = row_tok.shape[0]
    d = 2 * u_packed.shape[1]
    n_items = item_e.shape[0]
    de = w_down.shape[1]
    j1 = de // tn1
    j2 = d // tn2
    rmax = MOE_ITEM_BLOCKS * MOE_ROWS

    assert rmax % j2 == 0

    def bias_map(i, ie, isb, inb, ifl, tok):
        return (ie[i], 0, 0)

    return pl.pallas_call(
        functools.partial(_expert_kernel, j1=j1, j2=j2, tn1=tn1, tn2=tn2, n_items=n_items),
        grid_spec=pltpu.PrefetchScalarGridSpec(
            num_scalar_prefetch=5,
            grid=(n_items,),
            in_specs=[
                pl.BlockSpec(memory_space=pl.ANY),
                pl.BlockSpec(memory_space=pl.ANY),
                pl.BlockSpec(memory_space=pl.ANY),
                pl.BlockSpec((None, 2 * j1, tn1), bias_map),
                pl.BlockSpec((None, j2, tn2), bias_map),
            ],
            out_specs=pl.BlockSpec(memory_space=pl.ANY),
            scratch_shapes=[
                pltpu.VMEM((rmax, d // 2), jnp.uint32),
                pltpu.VMEM((j1, rmax, tn1), BF16),
                pltpu.VMEM((2, rmax, tn2 // 2), jnp.uint32),
                pltpu.VMEM((2, 2, d, tn1), F32),
                pltpu.VMEM((MOE_DOWN_SLOTS, de, tn2), F32),
                pltpu.VMEM((d, tn1), BF16),
                pltpu.VMEM((d, tn1), BF16),
                pltpu.VMEM((de, tn2), BF16),
                pltpu.VMEM((MOE_ROWS, d // 2), jnp.uint32),
                pltpu.SemaphoreType.DMA(()),
                pltpu.SemaphoreType.DMA((2,)),
                pltpu.SemaphoreType.DMA((2,)),
                pltpu.SemaphoreType.DMA((MOE_DOWN_SLOTS,)),
                pltpu.SemaphoreType.DMA(()),
            ],
        ),
        out_shape=jax.ShapeDtypeStruct((p, d // 2), jnp.uint32),
        compiler_params=_params("arbitrary", vmem=VMEM_LIMIT),
        name="expert_mlp",
    )(item_e, item_sb, item_nb, item_fill, row_tok, u_packed, w_gate_up, w_down,
      b_gate_up.reshape(N_EXPERTS, 2 * j1, tn1), b_down.reshape(N_EXPERTS, j2, tn2))


def _combine_kernel(dest_ref, h_ref, g_ref, ys_hbm, o_ref, buf, sem, *, tt, n_tok, n_steps, yw):
    i = pl.program_id(0)

    def issue(blk, slot):
        for k in range(TOP_K):
            def start(t, c):
                src = dest_ref[k * n_tok + blk * tt + t]
                pltpu.make_async_copy(ys_hbm.at[pl.ds(src, 1), :], buf.at[slot, k, pl.ds(t, 1), :],
                                      sem.at[slot]).start()
                return c
            lax.fori_loop(0, tt, start, 0, unroll=8)

    @pl.when(i == 0)
    def _():
        issue(0, 0)

    slot = i % 2

    @pl.when(i + 1 < n_steps)
    def _():
        issue(i + 1, 1 - slot)

    for k in range(TOP_K):
        pltpu.make_async_copy(ys_hbm.at[pl.ds(0, tt), :], buf.at[slot, k], sem.at[slot]).wait()
    g = g_ref[...]
    lo = None
    hi = None
    for k in range(TOP_K):
        word = buf[slot, k]
        gk = g[:, k:k + 1]
        lo_k = gk * pltpu.bitcast(word << 16, F32)
        hi_k = gk * pltpu.bitcast(word & jnp.uint32(0xFFFF0000), F32)
        lo = lo_k if lo is None else lo + lo_k
        hi = hi_k if hi is None else hi + hi_k
    for c in range(lo.shape[1] // yw):
        o_ref[:, 2 * c * yw:(2 * c + 1) * yw] = (h_ref[:, 2 * c * yw:(2 * c + 1) * yw]
                                                  + lo[:, c * yw:(c + 1) * yw])
        o_ref[:, (2 * c + 1) * yw:(2 * c + 2) * yw] = (h_ref[:, (2 * c + 1) * yw:(2 * c + 2) * yw]
                                                        + hi[:, c * yw:(c + 1) * yw])


def combine(h2, gates, ys, dest_km, yw, tt=128):
    n, d = h2.shape
    return pl.pallas_call(
        functools.partial(_combine_kernel, tt=tt, n_tok=n, n_steps=n // tt, yw=yw),
        grid_spec=pltpu.PrefetchScalarGridSpec(
            num_scalar_prefetch=1,
            grid=(n // tt,),
            in_specs=[pl.BlockSpec((tt, d), lambda i, dst: (i, 0)),
                      pl.BlockSpec((tt, LANES), lambda i, dst: (i, 0)),
                      pl.BlockSpec(memory_space=pl.ANY)],
            out_specs=pl.BlockSpec((tt, d), lambda i, dst: (i, 0)),
            scratch_shapes=[pltpu.VMEM((2, TOP_K, tt, d // 2), jnp.uint32),
                            pltpu.SemaphoreType.DMA((2,))],
        ),
        out_shape=jax.ShapeDtypeStruct((n, d), F32),
        compiler_params=_params("arbitrary", vmem=VMEM_LIMIT),
        name="moe_combine",
    )(dest_km, h2, gates, ys)


def routing_tables(top_idx, n_tok):
    a = n_tok * TOP_K
    e_flat = top_idx.reshape(-1)
    onehot = (e_flat[:, None] == jnp.arange(N_EXPERTS, dtype=jnp.int32)[None, :]).astype(jnp.int32)
    csum = jnp.cumsum(onehot, axis=0)
    rank = jnp.sum(csum * onehot, axis=1) - 1
    counts = csum[-1]
    nblk = (counts + MOE_ROWS - 1) // MOE_ROWS
    blk_end = jnp.cumsum(nblk)
    blk_start = blk_end - nblk
    dest = blk_start[e_flat] * MOE_ROWS + rank
    n_blocks = -(-(a + N_EXPERTS * (MOE_ROWS - 1)) // MOE_ROWS)
    p = n_blocks * MOE_ROWS
    t_flat = jnp.arange(a, dtype=jnp.int32) // TOP_K
    row_tok = jnp.zeros((p,), jnp.int32).at[dest].set(t_flat, unique_indices=True,
                                                      mode='promise_in_bounds')

    n_items = N_EXPERTS + n_blocks // MOE_ITEM_BLOCKS
    items_per_e = (nblk + MOE_ITEM_BLOCKS - 1) // MOE_ITEM_BLOCKS
    item_end = jnp.cumsum(items_per_e)
    item_start = item_end - items_per_e
    total_items = item_end[-1]
    ii = jnp.arange(n_items, dtype=jnp.int32)
    ic = jnp.minimum(ii, total_items - 1)
    e_i = jnp.minimum(jnp.searchsorted(item_end, ic, side='right'), N_EXPERTS - 1).astype(jnp.int32)
    local = ic - item_start[e_i]
    item_sb = (blk_start[e_i] + local * MOE_ITEM_BLOCKS).astype(jnp.int32)
    item_nb = jnp.clip(nblk[e_i] - local * MOE_ITEM_BLOCKS, 0, MOE_ITEM_BLOCKS)
    active = ii < total_items
    item_nb = jnp.where(active, item_nb, 0).astype(jnp.int32)
    fill_start = blk_end[-1] + (ii - total_items) * MOE_ITEM_BLOCKS
    item_fill = jnp.where(active, 0, jnp.clip(n_blocks - fill_start, 0, MOE_ITEM_BLOCKS)).astype(jnp.int32)
    item_sb = jnp.where(active, item_sb, jnp.minimum(fill_start, n_blocks - 1)).astype(jnp.int32)
    dest_km = dest.astype(jnp.int32).reshape(n_tok, TOP_K).T.reshape(-1)
    return dest_km, row_tok, e_i, item_sb, item_nb, item_fill


def kernel(x, meta_tokens, mix_norm, w_in, dn_conv, dn_a_log, dn_dt_bias, dn_out_norm, da_q_norm, da_k_norm, da_lam_q1, da_lam_k1, da_lam_q2, da_lam_k2, da_sub_norm, w_branch_a, w_branch_b, w_out, ffn_norm, router_w, router_b, w_gate_up, b_gate_up, w_down, b_down):
    bsz, seq, d = x.shape
    n_tok = bsz * seq
    x2 = x.reshape(n_tok, d)
    layer = 0

    w_bf = w_in[layer].astype(BF16)
    da_shift = 2 * DN_HEADS
    da_cols = w_bf.shape[1] - COL_BA - da_shift
    tn = 1024

    u_x = rms_cast(x2, mix_norm[layer], 256)
    u_m = rms_cast(meta_tokens.astype(F32), mix_norm[layer], N_META)
    proj_dn = matmul_fullk(u_x, w_bf, BF16, 1024, tn, name="in_proj_dn", n_blocks=COL_BA // tn)
    proj_da = matmul_fullk_shift(u_x, w_bf, BF16, 1024, tn, COL_BA, da_shift, da_cols // tn,
                                 name="in_proj_da")
    ba_x = matmul_fullk(u_x, w_bf, F32, 1024, LANES, name="in_proj_ba",
                        n_offset=COL_BA // LANES, n_blocks=1)
    proj_dn_m = matmul_fullk(u_m, w_bf, BF16, N_META, tn, name="in_proj_dn_meta", n_blocks=COL_DZ // tn)
    proj_kv_m = matmul_fullk_shift(u_m, w_bf, BF16, N_META, tn, COL_BA, da_shift,
                                   (COL_GA - COL_AK) // tn, name="in_proj_kv_meta",
                                   n_offset=COL_AK // tn)
    ba_m = matmul_fullk(u_m, w_bf, F32, N_META, LANES, name="in_proj_ba_meta",
                        n_offset=COL_BA // LANES, n_blocks=1)

    y_a = gated_deltanet(proj_dn, ba_x, proj_dn_m, ba_m, dn_conv[layer], dn_a_log[layer],
                         dn_dt_bias[layer], dn_out_norm[layer], bsz, seq)
    lam4 = jnp.stack([da_lam_q1[layer], da_lam_k1[layer], da_lam_q2[layer], da_lam_k2[layer]])
    y_b = diff_attention(proj_da, proj_kv_m, da_q_norm[layer], da_k_norm[layer], lam4,
                         da_sub_norm[layer], bsz, seq)

    merged = branch_merge(y_a, y_b, w_branch_a[layer], w_branch_b[layer], proj_da, d)
    h2 = matmul_fullk(merged, w_out[layer], F32, 1024, 512, res=x2, name="out_proj")

    top_idx, gates, u_packed = router(h2, ffn_norm[layer], router_w[layer], router_b[layer])
    dest, row_tok, item_e, item_sb, item_nb, item_fill = routing_tables(top_idx[:, :TOP_K], n_tok)
    ys = expert_mlp(u_packed, row_tok, item_e, item_sb, item_nb, item_fill, w_gate_up[layer],
                    b_gate_up[layer], w_down[layer], b_down[layer])
    out = combine(h2, gates, ys, dest, MOE_DOWN_TILE // 2)
    return out.reshape(bsz, seq, d)
```

```python
import functools
import math

import jax
import jax.numpy as jnp
from jax import lax
from jax.experimental import pallas as pl
from jax.experimental.pallas import tpu as pltpu

F32 = jnp.float32
BF16 = jnp.bfloat16

N_META = 16
RMS_EPS = 1e-6
L2_EPS = 1e-6

DN_HEADS = 16
DN_DK = 128
DN_DV = 128
DN_CONV = 4
DN_CHUNK = 64
DN_W = DN_HEADS * DN_DK
DN_HIST = 16
DN_CONV_COLS = 1024

DA_HEADS = 8
DA_DK = 128
DA_DV = 256
LAMBDA_INIT = 0.8 - 0.6 * math.exp(-0.3 * 0)

N_EXPERTS = 32
TOP_K = 4
D_EXPERT = 1536
SWIGLU_LIMIT = 7.0
SWIGLU_ALPHA = 1.702

LANES = 128
MOE_ROWS = 128
MOE_ITEM_BLOCKS = 10
MOE_HEAD_UNITS = 4
MOE_BODY_UNITS = 4
MOE_DOWN_SLOTS = 4
MOE_DOWN_TILE = 512
NEG_BIG = -1e30
VMEM_LIMIT = 56 * 1024 * 1024

COL_DZ = 3 * DN_W
COL_BA = 4 * DN_W
COL_AQ = 0
COL_AK = COL_AQ + DA_HEADS * 2 * DA_DK
COL_AV = COL_AK + DA_HEADS * 2 * DA_DK
COL_GA = COL_AV + DA_HEADS * DA_DV


def _params(*sem, vmem=None):
    return pltpu.CompilerParams(dimension_semantics=sem, vmem_limit_bytes=vmem)


def _rms_cast_kernel(x_ref, w_ref, o_ref):
    x = x_ref[...]
    y = x * lax.rsqrt(jnp.mean(x * x, axis=-1, keepdims=True) + RMS_EPS) * w_ref[...]
    o_ref[...] = y.astype(o_ref.dtype)


def rms_cast(x, w, tm):
    m, d = x.shape
    return pl.pallas_call(
        _rms_cast_kernel,
        grid=(m // tm,),
        in_specs=[pl.BlockSpec((tm, d), lambda i: (i, 0)),
                  pl.BlockSpec((1, d), lambda i: (0, 0))],
        out_specs=pl.BlockSpec((tm, d), lambda i: (i, 0)),
        out_shape=jax.ShapeDtypeStruct((m, d), BF16),
        compiler_params=_params("parallel"),
        name="rms_cast",
    )(x, w.reshape(1, d))


def _mm_fullk_kernel(a_ref, b_ref, *rest, has_res):
    acc = jnp.dot(a_ref[...], b_ref[...].astype(BF16), preferred_element_type=F32)
    if has_res:
        r_ref, o_ref = rest
        acc = acc + r_ref[...]
    else:
        o_ref, = rest
    o_ref[...] = acc.astype(o_ref.dtype)


def matmul_fullk(a, b, out_dtype, tm, tn, res=None, name="matmul", n_offset=0, n_blocks=None):
    m, kd = a.shape
    if n_blocks is None:
        n_blocks = b.shape[1] // tn
    in_specs = [pl.BlockSpec((tm, kd), lambda i, j: (i, 0)),
                pl.BlockSpec((kd, tn), lambda i, j: (0, n_offset + j))]
    args = [a, b]
    if res is not None:
        in_specs.append(pl.BlockSpec((tm, tn), lambda i, j: (i, j)))
        args.append(res)
    return pl.pallas_call(
        functools.partial(_mm_fullk_kernel, has_res=res is not None),
        grid=(m // tm, n_blocks),
        in_specs=in_specs,
        out_specs=pl.BlockSpec((tm, tn), lambda i, j: (i, j)),
        out_shape=jax.ShapeDtypeStruct((m, n_blocks * tn), out_dtype),
        compiler_params=_params("parallel", "parallel", vmem=VMEM_LIMIT),
        name=name,
    )(*args)


def _mm_shift_kernel(a_ref, b0_ref, b1_ref, o_ref, *, shift):
    b = jnp.concatenate([b0_ref[:, shift:], b1_ref[:, :shift]], axis=1)
    o_ref[...] = jnp.dot(a_ref[...], b, preferred_element_type=F32).astype(o_ref.dtype)


def matmul_fullk_shift(a, b, out_dtype, tm, tn, col0, shift, n_blocks, name, n_offset=0):
    m, kd = a.shape
    first = col0 // tn + n_offset
    return pl.pallas_call(
        functools.partial(_mm_shift_kernel, shift=shift),
        grid=(m // tm, n_blocks),
        in_specs=[pl.BlockSpec((tm, kd), lambda i, j: (i, 0)),
                  pl.BlockSpec((kd, tn), lambda i, j: (0, first + j)),
                  pl.BlockSpec((kd, LANES), lambda i, j: (0, (first + j + 1) * (tn // LANES)))],
        out_specs=pl.BlockSpec((tm, tn), lambda i, j: (i, j)),
        out_shape=jax.ShapeDtypeStruct((m, n_blocks * tn), out_dtype),
        compiler_params=_params("parallel", "parallel", vmem=VMEM_LIMIT),
        name=name,
    )(a, b, b)


def _deltanet_kernel(qkv_ref, z_ref, ba_ref, qkvm_ref, bam_ref, cw_ref, hp_ref, nw_ref,
                     o_ref, full_ref, act_ref, ba_s, s_ref):
    t = pl.program_id(1)
    C = DN_CHUNK
    HIST = DN_HIST
    n_pad = C - N_META

    @pl.when(t == 0)
    def _():
        s_ref[...] = jnp.zeros_like(s_ref)
        full_ref[0:HIST + n_pad, :] = jnp.zeros((HIST + n_pad, 3 * DN_W), BF16)
        full_ref[HIST + n_pad:HIST + C, :] = qkvm_ref[...]
        ba_s[0:n_pad, :] = jnp.zeros((n_pad, LANES), F32)
        ba_s[n_pad:C, :] = bam_ref[...]

    @pl.when(t > 0)
    def _():
        full_ref[HIST:HIST + C, :] = qkv_ref[...]
        ba_s[...] = ba_ref[...]

    n_shift = DN_CONV - 1
    sr = lax.broadcasted_iota(jnp.int32, (n_shift * C, HIST + C), 0)
    sc = lax.broadcasted_iota(jnp.int32, (n_shift * C, HIST + C), 1)
    shift = jnp.where(sc == HIST - n_shift + (sr % C) + (sr // C), 1.0, 0.0).astype(BF16)
    for c0 in range(0, 3 * DN_W, DN_CONV_COLS):
        cs = slice(c0, c0 + DN_CONV_COLS)
        delayed = jnp.dot(shift, full_ref[:, cs], preferred_element_type=F32)
        conv = cw_ref[n_shift:n_shift + 1, cs] * full_ref[HIST:HIST + C, cs].astype(F32)
        for j in range(n_shift):
            conv = conv + cw_ref[j:j + 1, cs] * delayed[j * C:(j + 1) * C, :]
        act_ref[:, cs] = conv * jax.nn.sigmoid(conv)
    full_ref[0:HIST, :] = full_ref[C:C + HIST, :]

    row1 = lax.broadcasted_iota(jnp.int32, (C, 1), 0)
    valid = jnp.where((t > 0) | (row1 >= n_pad), 1.0, 0.0).astype(F32)
    ba = ba_s[...]
    beta_all = jax.nn.sigmoid(ba) * valid
    xg = ba + hp_ref[1:2, :]
    softplus = jnp.maximum(xg, 0.0) + jnp.log1p(jnp.exp(-jnp.abs(xg)))
    g_all = -jnp.exp(hp_ref[0:1, :]) * softplus * valid

    row = lax.broadcasted_iota(jnp.int32, (C, C), 0)
    col = lax.broadcasted_iota(jnp.int32, (C, C), 1)
    causal = row >= col
    strict = row > col
    eye = jnp.where(row == col, 1.0, 0.0).astype(F32)
    tril = jnp.where(causal, 1.0, 0.0).astype(BF16)

    g_hi = g_all.astype(BF16)
    r1 = g_all - g_hi.astype(F32)
    g_mid = r1.astype(BF16)
    g_lo = (r1 - g_mid.astype(F32)).astype(BF16)
    g3 = jnp.dot(tril, jnp.concatenate([g_hi, g_mid, g_lo], axis=1), preferred_element_type=F32)
    gcum = g3[:, 0:LANES] + g3[:, LANES:2 * LANES] + g3[:, 2 * LANES:3 * LANES]
    gcum_t = gcum.T

    scale = DN_DK ** -0.5
    nw = nw_ref[...]
    heads = range(DN_HEADS)

    def mm(a, b):
        return jnp.dot(a.astype(BF16), b.astype(BF16), preferred_element_type=F32)

    kn_b, lhs1, decay, rhs, qe_b, kdec_t, e_last = [], [], [], [], [], [], []
    for h in heads:
        qh = act_ref[:, h * DN_DK:(h + 1) * DN_DK]
        kh = act_ref[:, DN_W + h * DN_DK:DN_W + (h + 1) * DN_DK]
        vh = act_ref[:, 2 * DN_W + h * DN_DV:2 * DN_W + (h + 1) * DN_DV]
        qn = qh * (lax.rsqrt(jnp.sum(qh * qh, axis=-1, keepdims=True) + L2_EPS) * scale)
        kn = kh * lax.rsqrt(jnp.sum(kh * kh, axis=-1, keepdims=True) + L2_EPS)
        beta = beta_all[:, h:h + 1]
        gc = gcum[:, DN_HEADS + h:DN_HEADS + h + 1]
        gr = gcum_t[DN_HEADS + h:DN_HEADS + h + 1, :]
        g_last = gc[C - 1:C, :]
        e_g = jnp.exp(gc)
        kb = kn * beta
        kn_b.append(kn.astype(BF16))
        lhs1.append(jnp.concatenate([kb, qn], axis=0).astype(BF16))
        decay.append(jnp.exp(jnp.where(causal, gc - gr, NEG_BIG)))
        rhs.append(jnp.concatenate([vh * beta, kb * e_g], axis=1).astype(BF16))
        qe_b.append((qn * e_g).astype(BF16))
        kdec_t.append((kn * jnp.exp(g_last - gc)).T.astype(BF16))
        e_last.append(jnp.exp(g_last))

    kq = [lax.dot_general(lhs1[h], kn_b[h], (((1,), (1,)), ((), ())), preferred_element_type=F32)
          for h in heads]
    qk_b = [jnp.where(causal, kq[h][C:2 * C, :] * decay[h], 0.0).astype(BF16) for h in heads]
    b1 = [-jnp.where(strict, kq[h][0:C, :] * decay[h], 0.0) for h in heads]
    b2 = [mm(b1[h], b1[h]) for h in heads]
    b4 = [mm(b2[h], b2[h]) for h in heads]
    p1 = [(eye + b1[h]) + mm(eye + b1[h], b2[h]) for h in heads]
    b8 = [mm(b4[h], b4[h]) for h in heads]
    b16 = [mm(b8[h], b8[h]) for h in heads]
    p2 = [(eye + b4[h]) + mm(eye + b4[h], b8[h]) for h in heads]
    b32 = [mm(b16[h], b16[h]) for h in heads]
    p12 = [mm(p1[h], p2[h]) for h in heads]
    p3 = [(eye + b16[h]) + mm(eye + b16[h], b32[h]) for h in heads]
    inv = [mm(p12[h], p3[h]) for h in heads]
    sol = [mm(inv[h], rhs[h]) for h in heads]

    s_old = [s_ref[h] for h in heads]
    ws_lhs = [jnp.concatenate([sol[h][:, DN_DV:2 * DN_DV].astype(BF16), qe_b[h]], axis=0) for h in heads]
    ws = [mm(ws_lhs[h], s_old[h]) for h in heads]
    v_new = [(sol[h][:, 0:DN_DV] - ws[h][0:C, :]).astype(BF16) for h in heads]
    ov_lhs = [jnp.concatenate([qk_b[h], kdec_t[h]], axis=0) for h in heads]
    ov = [mm(ov_lhs[h], v_new[h]) for h in heads]
    for h in heads:
        s_ref[h] = s_old[h] * e_last[h] + ov[h][C:C + DN_DK, :]
        o = ws[h][C:2 * C, :] + ov[h][0:C, :]
        zh = z_ref[:, h * DN_DV:(h + 1) * DN_DV].astype(F32)
        o = o * lax.rsqrt(jnp.mean(o * o, axis=-1, keepdims=True) + RMS_EPS) * nw
        o_ref[:, h * DN_DV:(h + 1) * DN_DV] = (o * (zh * jax.nn.sigmoid(zh))).astype(o_ref.dtype)


def gated_deltanet(proj_x, ba_x, proj_m, ba_m, conv_w, a_log, dt_bias, norm_w, bsz, seq):
    C = DN_CHUNK
    n_chunks = seq // C
    w3 = 3 * DN_W
    hp = jnp.zeros((2, LANES), F32)
    hp = hp.at[0, DN_HEADS:2 * DN_HEADS].set(a_log.astype(F32))
    hp = hp.at[1, DN_HEADS:2 * DN_HEADS].set(dt_bias.astype(F32))

    def xrow(b, t):
        return b * n_chunks + jnp.maximum(t - 1, 0)

    return pl.pallas_call(
        _deltanet_kernel,
        grid=(bsz, n_chunks + 1),
        in_specs=[
            pl.BlockSpec((C, w3), lambda b, t: (xrow(b, t), 0)),
            pl.BlockSpec((C, DN_W), lambda b, t: (xrow(b, t), COL_DZ // DN_W)),
            pl.BlockSpec((C, LANES), lambda b, t: (xrow(b, t), 0)),
            pl.BlockSpec((N_META, w3), lambda b, t: (0, 0)),
            pl.BlockSpec((N_META, LANES), lambda b, t: (0, 0)),
            pl.BlockSpec((DN_CONV, w3), lambda b, t: (0, 0)),
            pl.BlockSpec((2, LANES), lambda b, t: (0, 0)),
            pl.BlockSpec((1, DN_DV), lambda b, t: (0, 0)),
        ],
        out_specs=pl.BlockSpec((C, DN_W), lambda b, t: (xrow(b, t), 0)),
        out_shape=jax.ShapeDtypeStruct((bsz * seq, DN_W), BF16),
        scratch_shapes=[
            pltpu.VMEM((DN_HIST + C, w3), BF16),
            pltpu.VMEM((C, w3), F32),
            pltpu.VMEM((C, LANES), F32),
            pltpu.VMEM((DN_HEADS, DN_DK, DN_DV), F32),
        ],
        compiler_params=_params("parallel", "arbitrary", vmem=VMEM_LIMIT),
        name="gated_deltanet",
    )(proj_x, proj_x, ba_x, proj_m, ba_m, conv_w.astype(F32), hp, norm_w.reshape(1, DN_DV).astype(F32))


def _diff_attn_kernel(q_ref, k_ref, v_ref, km_ref, vm_ref, qn_ref, kn_ref, lam_ref, sn_ref,
                      o_ref, kx_s, kmeta_s, *, tq, seq):
    qi = pl.program_id(2)

    def rms(x, w):
        return x * lax.rsqrt(jnp.mean(x * x, axis=-1, keepdims=True) + RMS_EPS) * w

    @pl.when(qi == 0)
    def _():
        for m in range(2):
            kx = k_ref[:, m * DA_DK:(m + 1) * DA_DK].astype(F32)
            kx_s[m] = rms(kx, kn_ref[...]).astype(BF16)
            kmx = km_ref[:, m * DA_DK:(m + 1) * DA_DK].astype(F32)
            kmeta_s[m] = rms(kmx, kn_ref[...]).astype(BF16)

    lam = (jnp.exp(jnp.sum(lam_ref[0:1, :] * lam_ref[1:2, :], axis=-1, keepdims=True))
           - jnp.exp(jnp.sum(lam_ref[2:3, :] * lam_ref[3:4, :], axis=-1, keepdims=True))
           + LAMBDA_INIT)
    row = lax.broadcasted_iota(jnp.int32, (tq, tq), 0)
    col = lax.broadcasted_iota(jnp.int32, (tq, tq), 1)
    diag_visible = col <= row
    v_m = vm_ref[...]
    scale = DA_DK ** -0.5 * math.log2(math.e)
    nt = (((1,), (1,)), ((), ()))

    def attend(n_prev):
        maps = range(2)
        qn = [(rms(q_ref[:, m * DA_DK:(m + 1) * DA_DK].astype(F32), qn_ref[...]) * scale).astype(BF16)
              for m in maps]
        sd = [jnp.where(diag_visible,
                        lax.dot_general(qn[m], kx_s[m, n_prev:n_prev + tq, :], nt,
                                        preferred_element_type=F32), NEG_BIG) for m in maps]
        sm = [lax.dot_general(qn[m], kmeta_s[m], nt, preferred_element_type=F32) for m in maps]
        mx = [jnp.maximum(jnp.max(sd[m], axis=-1, keepdims=True),
                          jnp.max(sm[m], axis=-1, keepdims=True)) for m in maps]
        if n_prev:
            sp = [lax.dot_general(qn[m], kx_s[m, 0:n_prev, :], nt, preferred_element_type=F32)
                  for m in maps]
            mx = [jnp.maximum(mx[m], jnp.max(sp[m], axis=-1, keepdims=True)) for m in maps]
        pd = [jnp.exp2(sd[m] - mx[m]) for m in maps]
        pm = [jnp.exp2(sm[m] - mx[m]) for m in maps]
        den = [jnp.sum(pd[m], axis=-1, keepdims=True) + jnp.sum(pm[m], axis=-1, keepdims=True)
               for m in maps]
        acc = [jnp.dot(pd[m].astype(BF16), v_ref[n_prev:n_prev + tq, :], preferred_element_type=F32)
               + jnp.dot(pm[m].astype(BF16), v_m, preferred_element_type=F32) for m in maps]
        if n_prev:
            pp = [jnp.exp2(sp[m] - mx[m]) for m in maps]
            den = [den[m] + jnp.sum(pp[m], axis=-1, keepdims=True) for m in maps]
            acc = [acc[m] + jnp.dot(pp[m].astype(BF16), v_ref[0:n_prev, :], preferred_element_type=F32)
                   for m in maps]
        out = acc[0] / den[0] - lam * (acc[1] / den[1])
        out = rms(out, sn_ref[...]) * (1.0 - LAMBDA_INIT)
        o_ref[...] = out.astype(o_ref.dtype)

    for blk in range(seq // tq):
        @pl.when(qi == blk)
        def _():
            attend(blk * tq)


def diff_attention(proj_x, proj_m, q_norm, k_norm, lam4, sub_norm, bsz, seq, tq=512):
    nq = seq // tq
    cq = COL_AQ // DA_DV
    ck = COL_AK // DA_DV
    cv = COL_AV // DA_DV
    return pl.pallas_call(
        functools.partial(_diff_attn_kernel, tq=tq, seq=seq),
        grid=(bsz, DA_HEADS, nq),
        in_specs=[
            pl.BlockSpec((tq, DA_DV), lambda b, h, i: (b * nq + i, cq + h)),
            pl.BlockSpec((seq, DA_DV), lambda b, h, i: (b, ck + h)),
            pl.BlockSpec((seq, DA_DV), lambda b, h, i: (b, cv + h)),
            pl.BlockSpec((N_META, DA_DV), lambda b, h, i: (0, h)),
            pl.BlockSpec((N_META, DA_DV), lambda b, h, i: (0, DA_HEADS + h)),
            pl.BlockSpec((1, DA_DK), lambda b, h, i: (0, 0)),
            pl.BlockSpec((1, DA_DK), lambda b, h, i: (0, 0)),
            pl.BlockSpec((4, DA_DK), lambda b, h, i: (0, 0)),
            pl.BlockSpec((1, DA_DV), lambda b, h, i: (0, 0)),
        ],
        out_specs=pl.BlockSpec((tq, DA_DV), lambda b, h, i: (b * nq + i, h)),
        out_shape=jax.ShapeDtypeStruct((bsz * seq, DA_HEADS * DA_DV), BF16),
        scratch_shapes=[pltpu.VMEM((2, seq, DA_DK), BF16),
                        pltpu.VMEM((2, N_META, DA_DK), BF16)],
        compiler_params=_params("parallel", "parallel", "arbitrary", vmem=VMEM_LIMIT),
        name="diff_attention",
    )(proj_x, proj_x, proj_x, proj_m, proj_m,
      q_norm.reshape(1, DA_DK).astype(F32), k_norm.reshape(1, DA_DK).astype(F32),
      lam4.astype(F32), sub_norm.reshape(1, DA_DV).astype(F32))


def _merge_kernel(ya_ref, yb_ref, wa_ref, wb_ref, ga_ref, gb_ref, o_ref):
    pa = jnp.dot(ya_ref[...], wa_ref[...].astype(BF16), preferred_element_type=F32)
    pb = jnp.dot(yb_ref[...], wb_ref[...].astype(BF16), preferred_element_type=F32)
    ga = jax.nn.sigmoid(ga_ref[...].astype(F32))
    gb = jax.nn.sigmoid(gb_ref[...].astype(F32))
    o_ref[...] = (ga * pa + gb * pb).astype(o_ref.dtype)


def branch_merge(y_a, y_b, w_a, w_b, proj_x, d_model, tm=1024, tn=512):
    m, ka = y_a.shape
    kb = y_b.shape[1]
    cga = COL_GA // tn
    cgb = (COL_GA + d_model) // tn
    return pl.pallas_call(
        _merge_kernel,
        grid=(m // tm, d_model // tn),
        in_specs=[
            pl.BlockSpec((tm, ka), lambda i, j: (i, 0)),
            pl.BlockSpec((tm, kb), lambda i, j: (i, 0)),
            pl.BlockSpec((ka, tn), lambda i, j: (0, j)),
            pl.BlockSpec((kb, tn), lambda i, j: (0, j)),
            pl.BlockSpec((tm, tn), lambda i, j: (i, cga + j)),
            pl.BlockSpec((tm, tn), lambda i, j: (i, cgb + j)),
        ],
        out_specs=pl.BlockSpec((tm, tn), lambda i, j: (i, j)),
        out_shape=jax.ShapeDtypeStruct((m, d_model), BF16),
        compiler_params=_params("parallel", "parallel", vmem=VMEM_LIMIT),
        name="branch_merge",
    )(y_a, y_b, w_a, w_b, proj_x, proj_x)


def _router_kernel(h_ref, nw_ref, whi_ref, wlo_ref, b_ref, idx_ref, gate_ref, up_ref):
    x = h_ref[...]
    u = x * lax.rsqrt(jnp.mean(x * x, axis=-1, keepdims=True) + RMS_EPS) * nw_ref[...]
    u_hi = u.astype(BF16)
    u_lo = (u - u_hi.astype(F32)).astype(BF16)
    half = u.shape[1] // 2
    bits = pltpu.bitcast(u_hi.astype(F32), jnp.uint32)
    up_ref[...] = bits[:, half:] | (bits[:, :half] >> 16)
    logits = (jnp.dot(u_hi, whi_ref[...], preferred_element_type=F32)
              + jnp.dot(u_lo, whi_ref[...], preferred_element_type=F32)
              + jnp.dot(u_hi, wlo_ref[...], preferred_element_type=F32)
              + b_ref[...])
    lane = lax.broadcasted_iota(jnp.int32, logits.shape, 1)
    lane_f = lane.astype(F32)
    vals = logits
    tops, idxs = [], []
    for _ in range(TOP_K):
        mx = jnp.max(vals, axis=-1, keepdims=True)
        ix = jnp.min(jnp.where(vals == mx, lane_f, float(LANES)), axis=-1, keepdims=True).astype(jnp.int32)
        tops.append(mx)
        idxs.append(ix)
        vals = jnp.where(lane == ix, -3.0e38, vals)
    exps = [jnp.exp(tv - tops[0]) for tv in tops]
    den = exps[0] + exps[1] + exps[2] + exps[3]
    idx_out = jnp.zeros(logits.shape, jnp.int32)
    gate_out = jnp.zeros(logits.shape, F32)
    for k in range(TOP_K):
        idx_out = jnp.where(lane == k, idxs[k], idx_out)
        gate_out = jnp.where(lane == k, exps[k] / den, gate_out)
    idx_ref[...] = idx_out
    gate_ref[...] = gate_out


def router(h2, ffn_norm, router_w, router_b, tm=512):
    n, d = h2.shape
    wpad = jnp.zeros((d, LANES), F32).at[:, :N_EXPERTS].set(router_w.astype(F32))
    w_hi = wpad.astype(BF16)
    w_lo = (wpad - w_hi.astype(F32)).astype(BF16)
    bias = jnp.full((1, LANES), NEG_BIG, F32).at[0, :N_EXPERTS].set(router_b.astype(F32))
    return pl.pallas_call(
        _router_kernel,
        grid=(n // tm,),
        in_specs=[
            pl.BlockSpec((tm, d), lambda i: (i, 0)),
            pl.BlockSpec((1, d), lambda i: (0, 0)),
            pl.BlockSpec((d, LANES), lambda i: (0, 0)),
            pl.BlockSpec((d, LANES), lambda i: (0, 0)),
            pl.BlockSpec((1, LANES), lambda i: (0, 0)),
        ],
        out_specs=[pl.BlockSpec((tm, LANES), lambda i: (i, 0)),
                   pl.BlockSpec((tm, LANES), lambda i: (i, 0)),
                   pl.BlockSpec((tm, d // 2), lambda i: (i, 0))],
        out_shape=[jax.ShapeDtypeStruct((n, LANES), jnp.int32),
                   jax.ShapeDtypeStruct((n, LANES), F32),
                   jax.ShapeDtypeStruct((n, d // 2), jnp.uint32)],
        compiler_params=_params("parallel", vmem=VMEM_LIMIT),
        name="router",
    )(h2, ffn_norm.reshape(1, d).astype(F32), w_hi, w_lo, bias)


def _expert_kernel(ie_ref, isb_ref, inb_ref, ifl_ref, tok_ref, up_hbm, wgu_hbm, wd_hbm, bgu_ref,
                   bd_ref, ys_hbm, xbuf, act, ybuf, wgu_buf, wd_buf, wg_s, wu_s, wd_s, zbuf,
                   sem_x, sem_y, sem_gu, sem_d, sem_z, *, j1, j2, tn1, tn2, n_items):
    i = pl.program_id(0)
    nb = inb_ref[i]
    sb = isb_ref[i]
    ex = ie_ref[i]
    R = MOE_ROWS

    n_fill = ifl_ref[i]

    @pl.when(n_fill > 0)
    def _():
        zbuf[...] = jnp.zeros_like(zbuf)

        def z_copy(u):
            return pltpu.make_async_copy(zbuf, ys_hbm.at[pl.ds((sb + u) * R, R), :], sem_z)

        for u in range(MOE_ITEM_BLOCKS):
            @pl.when(u < n_fill)
            def _():
                z_copy(u).start()
        for u in range(MOE_ITEM_BLOCKS):
            @pl.when(u < n_fill)
            def _():
                z_copy(u).wait()

    def gu_copies(expert, jt, slot):
        return [pltpu.make_async_copy(
            wgu_hbm.at[expert, :, pl.ds(pl.multiple_of(which * j1 * tn1 + jt * tn1, tn1), tn1)],
            wgu_buf.at[slot, which], sem_gu.at[slot]) for which in range(2)]

    def d_copy(expert, jd, slot):
        return pltpu.make_async_copy(
            wd_hbm.at[expert, :, pl.ds(pl.multiple_of(jd * tn2, tn2), tn2)],
            wd_buf.at[slot], sem_d.at[slot])

    rmax = xbuf.shape[0]
    HEAD = MOE_HEAD_UNITS * R
    BODY = MOE_BODY_UNITS * R
    last_tok = tok_ref.shape[0] - 1

    def batch(rows):
        return rows // j2

    own_units = jnp.maximum(nb, MOE_HEAD_UNITS)

    def row_gather(base, r):
        tok = tok_ref[jnp.minimum(base + r, last_tok)]
        return pltpu.make_async_copy(up_hbm.at[pl.ds(tok, 1), :], xbuf.at[pl.ds(r, 1), :], sem_x)

    def x_wait_units(n):
        for u in range(MOE_ITEM_BLOCKS):
            @pl.when(u < n)
            def _():
                pltpu.make_async_copy(up_hbm.at[pl.ds(0, R), :], xbuf.at[pl.ds(u * R, R), :],
                                      sem_x).wait()

    yw = tn2 // 2

    def y_copy(slot, r, jd):
        return pltpu.make_async_copy(
            ybuf.at[slot, pl.ds(r * R, R), :],
            ys_hbm.at[pl.ds((sb + r) * R, R), pl.ds(pl.multiple_of(jd * yw, yw), yw)],
            sem_y.at[slot])

    def for_blocks(fn):
        for r in range(MOE_ITEM_BLOCKS):
            @pl.when(r < nb)
            def _():
                fn(r)

    @pl.when(i == 0)
    def _():
        def start(r, c):
            row_gather(sb * R, r).start()
            return c
        lax.fori_loop(0, rmax, start, 0, unroll=8)
        for cp in gu_copies(ex, 0, 0):
            cp.start()

    nb_prev = inb_ref[jnp.maximum(i - 1, 0)]
    x_wait_units(jnp.where(
        i == 0, MOE_ITEM_BLOCKS,
        jnp.where(nb_prev > 0, jnp.maximum(jnp.maximum(nb_prev, MOE_HEAD_UNITS), nb), 0)))

    def gate_up_block(jt, r0, rows, wg, wu):
        x32 = xbuf[pl.ds(r0, rows), :]
        x_lo = pltpu.bitcast(x32 << 16, F32).astype(BF16)
        x_hi = pltpu.bitcast(x32 & jnp.uint32(0xFFFF0000), F32).astype(BF16)
        x = jnp.concatenate([x_lo, x_hi], axis=1)
        g = jnp.dot(x, wg, preferred_element_type=F32) + bgu_ref[pl.ds(jt, 1), :]
        u = jnp.dot(x, wu, preferred_element_type=F32) + bgu_ref[pl.ds(j1 + jt, 1), :]
        gt = jnp.minimum(g, SWIGLU_LIMIT)
        up = jnp.clip(u, -SWIGLU_LIMIT, SWIGLU_LIMIT)
        a = gt * jax.nn.sigmoid(SWIGLU_ALPHA * gt) * (up + 1.0)
        act[jt, pl.ds(r0, rows), :] = a.astype(BF16)

    def down_block(jd, slot, r0, rows, wd):
        lhs = jnp.concatenate([act[c, pl.ds(r0, rows), :] for c in range(j1)], axis=1)
        y = jnp.dot(lhs, wd, preferred_element_type=F32) + bd_ref[pl.ds(jd, 1), :]
        bits = pltpu.bitcast(y.astype(BF16).astype(F32), jnp.uint32)
        ybuf[slot, pl.ds(r0, rows), :] = bits[:, yw:] | (bits[:, :yw] >> 16)

    def rest_blocks(block_fn):
        rest = jnp.maximum(nb - MOE_HEAD_UNITS, 0)
        n_body = rest // MOE_BODY_UNITS
        rem = rest % MOE_BODY_UNITS
        r_tail = HEAD + n_body * BODY
        g_tail = batch(HEAD) + n_body * batch(BODY)

        def body(p, c):
            block_fn(pl.multiple_of(HEAD + p * BODY, R), BODY, batch(HEAD) + p * batch(BODY))
            return c

        lax.fori_loop(0, n_body, body, 0)

        @pl.when(rem >= 2)
        def _():
            block_fn(pl.multiple_of(r_tail, R), 2 * R, g_tail)

        @pl.when(rem % 2 == 1)
        def _():
            two = (rem // 2) * 2
            block_fn(pl.multiple_of(r_tail + two * R, R), R, g_tail + (rem // 2) * batch(2 * R))

    nxt = jnp.minimum(i + 1, n_items - 1)
    next_base = isb_ref[nxt] * R
    fetch_units = jnp.maximum(own_units, inb_ref[nxt])
    rows_per_tile = fetch_units * batch(R)

    n_wd = wd_buf.shape[0]

    @pl.when(nb > 0)
    def _():
        for t0 in range(n_wd - 1):
            d_copy(ex, t0, t0).start()

        def gate_up_tile(jt, carry):
            slot = jt % 2

            @pl.when(jt + 1 < j1)
            def _():
                for cp in gu_copies(ex, jt + 1, 1 - slot):
                    cp.start()

            for cp in gu_copies(ex, jt, slot):
                cp.wait()
            wg = wgu_buf[slot, 0].astype(BF16)
            wu = wgu_buf[slot, 1].astype(BF16)
            wg_s[...] = wg
            wu_s[...] = wu
            gate_up_block(jt, 0, HEAD, wg, wu)
            rest_blocks(lambda r0, rows, g0: gate_up_block(jt, r0, rows, wg_s[...], wu_s[...]))
            return carry

        lax.fori_loop(0, j1, gate_up_tile, 0)

        @pl.when(i + 1 < n_items)
        def _():
            @pl.when(inb_ref[i + 1] > 0)
            def _():
                for cp in gu_copies(ie_ref[i + 1], 0, 0):
                    cp.start()

        def down_tile(jd, carry):
            slot = jd % 2
            wslot = jd % n_wd

            @pl.when(jd + n_wd - 1 < j2)
            def _():
                d_copy(ex, jd + n_wd - 1, (jd + n_wd - 1) % n_wd).start()

            @pl.when(jd >= 2)
            def _():
                for_blocks(lambda r: y_copy(slot, r, jd).wait())

            d_copy(ex, jd, wslot).wait()
            wd = wd_buf[wslot].astype(BF16)
            wd_s[...] = wd
            down_block(jd, slot, 0, HEAD, wd)
            tile_row0 = jd * rows_per_tile

            def gather_batch(first, count):
                for rr in range(count):
                    row_gather(next_base, tile_row0 + first + rr).start()

            gather_batch(0, batch(HEAD))

            def later_block(r0, rows, g0):
                down_block(jd, slot, r0, rows, wd_s[...])
                gather_batch(g0, batch(rows))

            rest_blocks(later_block)

            def make_up(r, c):
                row_gather(next_base, tile_row0 + r).start()
                return c

            lax.fori_loop(own_units * batch(R), rows_per_tile, make_up, 0)
            for_blocks(lambda r: y_copy(slot, r, jd).start())
            return carry

        lax.fori_loop(0, j2, down_tile, 0)
        for_blocks(lambda r: y_copy(0, r, 0).wait())
        for_blocks(lambda r: y_copy(1, r, 0).wait())

        @pl.when(i == n_items - 1)
        def _():
            x_wait_units(fetch_units)


def expert_mlp(u_packed, row_tok, item_e, item_sb, item_nb, item_fill, w_gate_up, b_gate_up, w_down,
               b_down, tn1=256, tn2=MOE_DOWN_TILE):
    p = row_tok.shape[0]
    d = 2 * u_packed.shape[1]
    n_items = item_e.shape[0]
    de = w_down.shape[1]
    j1 = de // tn1
    j2 = d // tn2
    rmax = MOE_ITEM_BLOCKS * MOE_ROWS

    assert MOE_ROWS % j2 == 0 and MOE_BODY_UNITS == 4 and MOE_DOWN_SLOTS <= j2

    def bias_map(i, ie, isb, inb, ifl, tok):
        return (ie[i], 0, 0)

    return pl.pallas_call(
        functools.partial(_expert_kernel, j1=j1, j2=j2, tn1=tn1, tn2=tn2, n_items=n_items),
        grid_spec=pltpu.PrefetchScalarGridSpec(
            num_scalar_prefetch=5,
            grid=(n_items,),
            in_specs=[
                pl.BlockSpec(memory_space=pl.ANY),
                pl.BlockSpec(memory_space=pl.ANY),
                pl.BlockSpec(memory_space=pl.ANY),
                pl.BlockSpec((None, 2 * j1, tn1), bias_map),
                pl.BlockSpec((None, j2, tn2), bias_map),
            ],
            out_specs=pl.BlockSpec(memory_space=pl.ANY),
            scratch_shapes=[
                pltpu.VMEM((rmax, d // 2), jnp.uint32),
                pltpu.VMEM((j1, rmax, tn1), BF16),
                pltpu.VMEM((2, rmax, tn2 // 2), jnp.uint32),
                pltpu.VMEM((2, 2, d, tn1), F32),
                pltpu.VMEM((MOE_DOWN_SLOTS, de, tn2), F32),
                pltpu.VMEM((d, tn1), BF16),
                pltpu.VMEM((d, tn1), BF16),
                pltpu.VMEM((de, tn2), BF16),
                pltpu.VMEM((MOE_ROWS, d // 2), jnp.uint32),
                pltpu.SemaphoreType.DMA(()),
                pltpu.SemaphoreType.DMA((2,)),
                pltpu.SemaphoreType.DMA((2,)),
                pltpu.SemaphoreType.DMA((MOE_DOWN_SLOTS,)),
                pltpu.SemaphoreType.DMA(()),
            ],
        ),
        out_shape=jax.ShapeDtypeStruct((p, d // 2), jnp.uint32),
        compiler_params=_params("arbitrary", vmem=VMEM_LIMIT),
        name="expert_mlp",
    )(item_e, item_sb, item_nb, item_fill, row_tok, u_packed, w_gate_up, w_down,
      b_gate_up.reshape(N_EXPERTS, 2 * j1, tn1), b_down.reshape(N_EXPERTS, j2, tn2))


def _combine_kernel(dest_ref, h_ref, g_ref, ys_hbm, o_ref, buf, sem, *, tt, n_tok, n_steps, yw):
    i = pl.program_id(0)

    def issue(blk, slot):
        for k in range(TOP_K):
            def start(t, c):
                src = dest_ref[k * n_tok + blk * tt + t]
                pltpu.make_async_copy(ys_hbm.at[pl.ds(src, 1), :], buf.at[slot, k, pl.ds(t, 1), :],
                                      sem.at[slot]).start()
                return c
            lax.fori_loop(0, tt, start, 0, unroll=8)

    @pl.when(i == 0)
    def _():
        issue(0, 0)

    slot = i % 2

    @pl.when(i + 1 < n_steps)
    def _():
        issue(i + 1, 1 - slot)

    for k in range(TOP_K):
        pltpu.make_async_copy(ys_hbm.at[pl.ds(0, tt), :], buf.at[slot, k], sem.at[slot]).wait()
    g = g_ref[...]
    lo = None
    hi = None
    for k in range(TOP_K):
        word = buf[slot, k]
        gk = g[:, k:k + 1]
        lo_k = gk * pltpu.bitcast(word << 16, F32)
        hi_k = gk * pltpu.bitcast(word & jnp.uint32(0xFFFF0000), F32)
        lo = lo_k if lo is None else lo + lo_k
        hi = hi_k if hi is None else hi + hi_k
    for c in range(lo.shape[1] // yw):
        o_ref[:, 2 * c * yw:(2 * c + 1) * yw] = (h_ref[:, 2 * c * yw:(2 * c + 1) * yw]
                                                  + lo[:, c * yw:(c + 1) * yw])
        o_ref[:, (2 * c + 1) * yw:(2 * c + 2) * yw] = (h_ref[:, (2 * c + 1) * yw:(2 * c + 2) * yw]
                                                        + hi[:, c * yw:(c + 1) * yw])


def combine(h2, gates, ys, dest_km, yw, tt=256):
    n, d = h2.shape
    return pl.pallas_call(
        functools.partial(_combine_kernel, tt=tt, n_tok=n, n_steps=n // tt, yw=yw),
        grid_spec=pltpu.PrefetchScalarGridSpec(
            num_scalar_prefetch=1,
            grid=(n // tt,),
            in_specs=[pl.BlockSpec((tt, d), lambda i, dst: (i, 0)),
                      pl.BlockSpec((tt, LANES), lambda i, dst: (i, 0)),
                      pl.BlockSpec(memory_space=pl.ANY)],
            out_specs=pl.BlockSpec((tt, d), lambda i, dst: (i, 0)),
            scratch_shapes=[pltpu.VMEM((2, TOP_K, tt, d // 2), jnp.uint32),
                            pltpu.SemaphoreType.DMA((2,))],
        ),
        out_shape=jax.ShapeDtypeStruct((n, d), F32),
        compiler_params=_params("arbitrary", vmem=VMEM_LIMIT),
        name="moe_combine",
    )(dest_km, h2, gates, ys)


def routing_tables(top_idx, n_tok):
    a = n_tok * TOP_K
    e_flat = top_idx.reshape(-1)
    onehot = (e_flat[:, None] == jnp.arange(N_EXPERTS, dtype=jnp.int32)[None, :]).astype(jnp.int32)
    csum = jnp.cumsum(onehot, axis=0)
    rank = jnp.sum(csum * onehot, axis=1) - 1
    counts = csum[-1]
    nblk = (counts + MOE_ROWS - 1) // MOE_ROWS
    blk_end = jnp.cumsum(nblk)
    blk_start = blk_end - nblk
    dest = blk_start[e_flat] * MOE_ROWS + rank
    n_blocks = -(-(a + N_EXPERTS * (MOE_ROWS - 1)) // MOE_ROWS)
    p = n_blocks * MOE_ROWS
    t_flat = jnp.arange(a, dtype=jnp.int32) // TOP_K
    row_tok = jnp.zeros((p,), jnp.int32).at[dest].set(t_flat, unique_indices=True,
                                                      mode='promise_in_bounds')

    n_items = N_EXPERTS + n_blocks // MOE_ITEM_BLOCKS
    items_per_e = (nblk + MOE_ITEM_BLOCKS - 1) // MOE_ITEM_BLOCKS
    item_end = jnp.cumsum(items_per_e)
    item_start = item_end - items_per_e
    total_items = item_end[-1]
    ii = jnp.arange(n_items, dtype=jnp.int32)
    ic = jnp.minimum(ii, total_items - 1)
    e_i = jnp.minimum(jnp.searchsorted(item_end, ic, side='right'), N_EXPERTS - 1).astype(jnp.int32)
    local = ic - item_start[e_i]
    item_sb = (blk_start[e_i] + local * MOE_ITEM_BLOCKS).astype(jnp.int32)
    item_nb = jnp.clip(nblk[e_i] - local * MOE_ITEM_BLOCKS, 0, MOE_ITEM_BLOCKS)
    active = ii < total_items
    item_nb = jnp.where(active, item_nb, 0).astype(jnp.int32)
    fill_start = blk_end[-1] + (ii - total_items) * MOE_ITEM_BLOCKS
    item_fill = jnp.where(active, 0, jnp.clip(n_blocks - fill_start, 0, MOE_ITEM_BLOCKS)).astype(jnp.int32)
    item_sb = jnp.where(active, item_sb, jnp.minimum(fill_start, n_blocks - 1)).astype(jnp.int32)
    dest_km = dest.astype(jnp.int32).reshape(n_tok, TOP_K).T.reshape(-1)
    return dest_km, row_tok, e_i, item_sb, item_nb, item_fill


def kernel(x, meta_tokens, mix_norm, w_in, dn_conv, dn_a_log, dn_dt_bias, dn_out_norm, da_q_norm, da_k_norm, da_lam_q1, da_lam_k1, da_lam_q2, da_lam_k2, da_sub_norm, w_branch_a, w_branch_b, w_out, ffn_norm, router_w, router_b, w_gate_up, b_gate_up, w_down, b_down):
    bsz, seq, d = x.shape
    n_tok = bsz * seq
    x2 = x.reshape(n_tok, d)
    layer = 0

    w_bf = w_in[layer].astype(BF16)
    da_shift = 2 * DN_HEADS
    da_cols = w_bf.shape[1] - COL_BA - da_shift
    tn = 1024

    u_x = rms_cast(x2, mix_norm[layer], 512)
    u_m = rms_cast(meta_tokens.astype(F32), mix_norm[layer], N_META)
    proj_dn = matmul_fullk(u_x, w_bf, BF16, 1024, tn, name="in_proj_dn", n_blocks=COL_BA // tn)
    proj_da = matmul_fullk_shift(u_x, w_bf, BF16, 1024, tn, COL_BA, da_shift, da_cols // tn,
                                 name="in_proj_da")
    ba_x = matmul_fullk(u_x, w_bf, F32, 1024, LANES, name="in_proj_ba",
                        n_offset=COL_BA // LANES, n_blocks=1)
    proj_dn_m = matmul_fullk(u_m, w_bf, BF16, N_META, tn, name="in_proj_dn_meta", n_blocks=COL_DZ // tn)
    proj_kv_m = matmul_fullk_shift(u_m, w_bf, BF16, N_META, tn, COL_BA, da_shift,
                                   (COL_GA - COL_AK) // tn, name="in_proj_kv_meta",
                                   n_offset=COL_AK // tn)
    ba_m = matmul_fullk(u_m, w_bf, F32, N_META, LANES, name="in_proj_ba_meta",
                        n_offset=COL_BA // LANES, n_blocks=1)

    y_a = gated_deltanet(proj_dn, ba_x, proj_dn_m, ba_m, dn_conv[layer], dn_a_log[layer],
                         dn_dt_bias[layer], dn_out_norm[layer], bsz, seq)
    lam4 = jnp.stack([da_lam_q1[layer], da_lam_k1[layer], da_lam_q2[layer], da_lam_k2[layer]])
    y_b = diff_attention(proj_da, proj_kv_m, da_q_norm[layer], da_k_norm[layer], lam4,
                         da_sub_norm[layer], bsz, seq)

    merged = branch_merge(y_a, y_b, w_branch_a[layer], w_branch_b[layer], proj_da, d)
    h2 = matmul_fullk(merged, w_out[layer], F32, 1024, 512, res=x2, name="out_proj")

    top_idx, gates, u_packed = router(h2, ffn_norm[layer], router_w[layer], router_b[layer])
    dest, row_tok, item_e, item_sb, item_nb, item_fill = routing_tables(top_idx[:, :TOP_K], n_tok)
    ys = expert_mlp(u_packed, row_tok, item_e, item_sb, item_nb, item_fill, w_gate_up[layer],
                    b_gate_up[layer], w_down[layer], b_down[layer])
    out = combine(h2, gates, ys, dest, MOE_DOWN_TILE // 2)
    return out.reshape(bsz, seq, d)
```

```python
import functools
import math

import jax
import jax.numpy as jnp
from jax import lax
from jax.experimental import pallas as pl
from jax.experimental.pallas import tpu as pltpu

F32 = jnp.float32
BF16 = jnp.bfloat16

N_META = 16
RMS_EPS = 1e-6
L2_EPS = 1e-6

DN_HEADS = 16
DN_DK = 128
DN_DV = 128
DN_CONV = 4
DN_CHUNK = 64
DN_W = DN_HEADS * DN_DK
DN_HIST = 16
DN_CONV_COLS = 1024

DA_HEADS = 8
DA_DK = 128
DA_DV = 256
LAMBDA_INIT = 0.8 - 0.6 * math.exp(-0.3 * 0)

N_EXPERTS = 32
TOP_K = 4
D_EXPERT = 1536
SWIGLU_LIMIT = 7.0
SWIGLU_ALPHA = 1.702

LANES = 128
MOE_ROWS = 128
MOE_ITEM_BLOCKS = 10
MOE_HEAD_UNITS = 4
MOE_BODY_UNITS = 4
MOE_DOWN_SLOTS = 3
MOE_DOWN_TILE = 512
NEG_BIG = -1e30
VMEM_LIMIT = 56 * 1024 * 1024

COL_DZ = 3 * DN_W
COL_BA = 4 * DN_W
COL_AQ = 0
COL_AK = COL_AQ + DA_HEADS * 2 * DA_DK
COL_AV = COL_AK + DA_HEADS * 2 * DA_DK
COL_GA = COL_AV + DA_HEADS * DA_DV


def _params(*sem, vmem=None):
    return pltpu.CompilerParams(dimension_semantics=sem, vmem_limit_bytes=vmem)


def _rms_cast_kernel(x_ref, w_ref, o_ref):
    x = x_ref[...]
    y = x * lax.rsqrt(jnp.mean(x * x, axis=-1, keepdims=True) + RMS_EPS) * w_ref[...]
    o_ref[...] = y.astype(o_ref.dtype)


def rms_cast(x, w, tm):
    m, d = x.shape
    return pl.pallas_call(
        _rms_cast_kernel,
        grid=(m // tm,),
        in_specs=[pl.BlockSpec((tm, d), lambda i: (i, 0)),
                  pl.BlockSpec((1, d), lambda i: (0, 0))],
        out_specs=pl.BlockSpec((tm, d), lambda i: (i, 0)),
        out_shape=jax.ShapeDtypeStruct((m, d), BF16),
        compiler_params=_params("parallel"),
        name="rms_cast",
    )(x, w.reshape(1, d))


def _mm_fullk_kernel(a_ref, b_ref, *rest, has_res):
    acc = jnp.dot(a_ref[...], b_ref[...].astype(BF16), preferred_element_type=F32)
    if has_res:
        r_ref, o_ref = rest
        acc = acc + r_ref[...]
    else:
        o_ref, = rest
    o_ref[...] = acc.astype(o_ref.dtype)


def matmul_fullk(a, b, out_dtype, tm, tn, res=None, name="matmul", n_offset=0, n_blocks=None):
    m, kd = a.shape
    if n_blocks is None:
        n_blocks = b.shape[1] // tn
    in_specs = [pl.BlockSpec((tm, kd), lambda i, j: (i, 0)),
                pl.BlockSpec((kd, tn), lambda i, j: (0, n_offset + j))]
    args = [a, b]
    if res is not None:
        in_specs.append(pl.BlockSpec((tm, tn), lambda i, j: (i, j)))
        args.append(res)
    return pl.pallas_call(
        functools.partial(_mm_fullk_kernel, has_res=res is not None),
        grid=(m // tm, n_blocks),
        in_specs=in_specs,
        out_specs=pl.BlockSpec((tm, tn), lambda i, j: (i, j)),
        out_shape=jax.ShapeDtypeStruct((m, n_blocks * tn), out_dtype),
        compiler_params=_params("parallel", "parallel", vmem=VMEM_LIMIT),
        name=name,
    )(*args)


def _mm_shift_kernel(a_ref, b0_ref, b1_ref, o_ref, *, shift):
    b = jnp.concatenate([b0_ref[:, shift:], b1_ref[:, :shift]], axis=1)
    o_ref[...] = jnp.dot(a_ref[...], b, preferred_element_type=F32).astype(o_ref.dtype)


def matmul_fullk_shift(a, b, out_dtype, tm, tn, col0, shift, n_blocks, name, n_offset=0):
    m, kd = a.shape
    first = col0 // tn + n_offset
    return pl.pallas_call(
        functools.partial(_mm_shift_kernel, shift=shift),
        grid=(m // tm, n_blocks),
        in_specs=[pl.BlockSpec((tm, kd), lambda i, j: (i, 0)),
                  pl.BlockSpec((kd, tn), lambda i, j: (0, first + j)),
                  pl.BlockSpec((kd, LANES), lambda i, j: (0, (first + j + 1) * (tn // LANES)))],
        out_specs=pl.BlockSpec((tm, tn), lambda i, j: (i, j)),
        out_shape=jax.ShapeDtypeStruct((m, n_blocks * tn), out_dtype),
        compiler_params=_params("parallel", "parallel", vmem=VMEM_LIMIT),
        name=name,
    )(a, b, b)


def _deltanet_kernel(qkv_ref, z_ref, ba_ref, qkvm_ref, bam_ref, cw_ref, hp_ref, nw_ref,
                     o_ref, full_ref, act_ref, ba_s, s_ref):
    t = pl.program_id(1)
    C = DN_CHUNK
    HIST = DN_HIST
    n_pad = C - N_META

    @pl.when(t == 0)
    def _():
        s_ref[...] = jnp.zeros_like(s_ref)
        full_ref[0:HIST + n_pad, :] = jnp.zeros((HIST + n_pad, 3 * DN_W), BF16)
        full_ref[HIST + n_pad:HIST + C, :] = qkvm_ref[...]
        ba_s[0:n_pad, :] = jnp.zeros((n_pad, LANES), F32)
        ba_s[n_pad:C, :] = bam_ref[...]

    @pl.when(t > 0)
    def _():
        full_ref[HIST:HIST + C, :] = qkv_ref[...]
        ba_s[...] = ba_ref[...]

    n_shift = DN_CONV - 1
    sr = lax.broadcasted_iota(jnp.int32, (n_shift * C, HIST + C), 0)
    sc = lax.broadcasted_iota(jnp.int32, (n_shift * C, HIST + C), 1)
    shift = jnp.where(sc == HIST - n_shift + (sr % C) + (sr // C), 1.0, 0.0).astype(BF16)
    for c0 in range(0, 3 * DN_W, DN_CONV_COLS):
        cs = slice(c0, c0 + DN_CONV_COLS)
        delayed = jnp.dot(shift, full_ref[:, cs], preferred_element_type=F32)
        conv = cw_ref[n_shift:n_shift + 1, cs] * full_ref[HIST:HIST + C, cs].astype(F32)
        for j in range(n_shift):
            conv = conv + cw_ref[j:j + 1, cs] * delayed[j * C:(j + 1) * C, :]
        act_ref[:, cs] = conv * jax.nn.sigmoid(conv)
    full_ref[0:HIST, :] = full_ref[C:C + HIST, :]

    row1 = lax.broadcasted_iota(jnp.int32, (C, 1), 0)
    valid = jnp.where((t > 0) | (row1 >= n_pad), 1.0, 0.0).astype(F32)
    ba = ba_s[...]
    beta_all = jax.nn.sigmoid(ba) * valid
    xg = ba + hp_ref[1:2, :]
    softplus = jnp.maximum(xg, 0.0) + jnp.log1p(jnp.exp(-jnp.abs(xg)))
    g_all = -jnp.exp(hp_ref[0:1, :]) * softplus * valid

    row = lax.broadcasted_iota(jnp.int32, (C, C), 0)
    col = lax.broadcasted_iota(jnp.int32, (C, C), 1)
    causal = row >= col
    strict = row > col
    eye = jnp.where(row == col, 1.0, 0.0).astype(F32)
    tril = jnp.where(causal, 1.0, 0.0).astype(BF16)

    g_hi = g_all.astype(BF16)
    r1 = g_all - g_hi.astype(F32)
    g_mid = r1.astype(BF16)
    g_lo = (r1 - g_mid.astype(F32)).astype(BF16)
    g3 = jnp.dot(tril, jnp.concatenate([g_hi, g_mid, g_lo], axis=1), preferred_element_type=F32)
    gcum = g3[:, 0:LANES] + g3[:, LANES:2 * LANES] + g3[:, 2 * LANES:3 * LANES]
    gcum_t = gcum.T

    scale = DN_DK ** -0.5
    nw = nw_ref[...]
    heads = range(DN_HEADS)

    def mm(a, b):
        return jnp.dot(a.astype(BF16), b.astype(BF16), preferred_element_type=F32)

    kn_b, lhs1, decay, rhs, qe_b, kdec_t, e_last = [], [], [], [], [], [], []
    for h in heads:
        qh = act_ref[:, h * DN_DK:(h + 1) * DN_DK]
        kh = act_ref[:, DN_W + h * DN_DK:DN_W + (h + 1) * DN_DK]
        vh = act_ref[:, 2 * DN_W + h * DN_DV:2 * DN_W + (h + 1) * DN_DV]
        qn = qh * (lax.rsqrt(jnp.sum(qh * qh, axis=-1, keepdims=True) + L2_EPS) * scale)
        kn = kh * lax.rsqrt(jnp.sum(kh * kh, axis=-1, keepdims=True) + L2_EPS)
        beta = beta_all[:, h:h + 1]
        gc = gcum[:, DN_HEADS + h:DN_HEADS + h + 1]
        gr = gcum_t[DN_HEADS + h:DN_HEADS + h + 1, :]
        g_last = gc[C - 1:C, :]
        e_g = jnp.exp(gc)
        kb = kn * beta
        kn_b.append(kn.astype(BF16))
        lhs1.append(jnp.concatenate([kb, qn], axis=0).astype(BF16))
        decay.append(jnp.exp(jnp.where(causal, gc - gr, NEG_BIG)))
        rhs.append(jnp.concatenate([vh * beta, kb * e_g], axis=1).astype(BF16))
        qe_b.append((qn * e_g).astype(BF16))
        kdec_t.append((kn * jnp.exp(g_last - gc)).T.astype(BF16))
        e_last.append(jnp.exp(g_last))

    kq = [lax.dot_general(lhs1[h], kn_b[h], (((1,), (1,)), ((), ())), preferred_element_type=F32)
          for h in heads]
    qk_b = [jnp.where(causal, kq[h][C:2 * C, :] * decay[h], 0.0).astype(BF16) for h in heads]
    b1 = [-jnp.where(strict, kq[h][0:C, :] * decay[h], 0.0) for h in heads]
    b2 = [mm(b1[h], b1[h]) for h in heads]
    b4 = [mm(b2[h], b2[h]) for h in heads]
    p1 = [(eye + b1[h]) + mm(eye + b1[h], b2[h]) for h in heads]
    b8 = [mm(b4[h], b4[h]) for h in heads]
    b16 = [mm(b8[h], b8[h]) for h in heads]
    p2 = [(eye + b4[h]) + mm(eye + b4[h], b8[h]) for h in heads]
    b32 = [mm(b16[h], b16[h]) for h in heads]
    p12 = [mm(p1[h], p2[h]) for h in heads]
    p3 = [(eye + b16[h]) + mm(eye + b16[h], b32[h]) for h in heads]
    inv = [mm(p12[h], p3[h]) for h in heads]
    sol = [mm(inv[h], rhs[h]) for h in heads]

    s_old = [s_ref[h] for h in heads]
    ws_lhs = [jnp.concatenate([sol[h][:, DN_DV:2 * DN_DV].astype(BF16), qe_b[h]], axis=0) for h in heads]
    ws = [mm(ws_lhs[h], s_old[h]) for h in heads]
    v_new = [(sol[h][:, 0:DN_DV] - ws[h][0:C, :]).astype(BF16) for h in heads]
    ov_lhs = [jnp.concatenate([qk_b[h], kdec_t[h]], axis=0) for h in heads]
    ov = [mm(ov_lhs[h], v_new[h]) for h in heads]
    for h in heads:
        s_ref[h] = s_old[h] * e_last[h] + ov[h][C:C + DN_DK, :]
        o = ws[h][C:2 * C, :] + ov[h][0:C, :]
        zh = z_ref[:, h * DN_DV:(h + 1) * DN_DV].astype(F32)
        o = o * lax.rsqrt(jnp.mean(o * o, axis=-1, keepdims=True) + RMS_EPS) * nw
        o_ref[:, h * DN_DV:(h + 1) * DN_DV] = (o * (zh * jax.nn.sigmoid(zh))).astype(o_ref.dtype)


def gated_deltanet(proj_x, ba_x, proj_m, ba_m, conv_w, a_log, dt_bias, norm_w, bsz, seq):
    C = DN_CHUNK
    n_chunks = seq // C
    w3 = 3 * DN_W
    hp = jnp.zeros((2, LANES), F32)
    hp = hp.at[0, DN_HEADS:2 * DN_HEADS].set(a_log.astype(F32))
    hp = hp.at[1, DN_HEADS:2 * DN_HEADS].set(dt_bias.astype(F32))

    def xrow(b, t):
        return b * n_chunks + jnp.maximum(t - 1, 0)

    return pl.pallas_call(
        _deltanet_kernel,
        grid=(bsz, n_chunks + 1),
        in_specs=[
            pl.BlockSpec((C, w3), lambda b, t: (xrow(b, t), 0)),
            pl.BlockSpec((C, DN_W), lambda b, t: (xrow(b, t), COL_DZ // DN_W)),
            pl.BlockSpec((C, LANES), lambda b, t: (xrow(b, t), 0)),
            pl.BlockSpec((N_META, w3), lambda b, t: (0, 0)),
            pl.BlockSpec((N_META, LANES), lambda b, t: (0, 0)),
            pl.BlockSpec((DN_CONV, w3), lambda b, t: (0, 0)),
            pl.BlockSpec((2, LANES), lambda b, t: (0, 0)),
            pl.BlockSpec((1, DN_DV), lambda b, t: (0, 0)),
        ],
        out_specs=pl.BlockSpec((C, DN_W), lambda b, t: (xrow(b, t), 0)),
        out_shape=jax.ShapeDtypeStruct((bsz * seq, DN_W), BF16),
        scratch_shapes=[
            pltpu.VMEM((DN_HIST + C, w3), BF16),
            pltpu.VMEM((C, w3), F32),
            pltpu.VMEM((C, LANES), F32),
            pltpu.VMEM((DN_HEADS, DN_DK, DN_DV), F32),
        ],
        compiler_params=_params("parallel", "arbitrary", vmem=VMEM_LIMIT),
        name="gated_deltanet",
    )(proj_x, proj_x, ba_x, proj_m, ba_m, conv_w.astype(F32), hp, norm_w.reshape(1, DN_DV).astype(F32))


def _diff_attn_kernel(q_ref, k_ref, v_ref, km_ref, vm_ref, qn_ref, kn_ref, lam_ref, sn_ref,
                      o_ref, kx_s, kmeta_s, *, tq, seq):
    qi = pl.program_id(2)

    def rms(x, w):
        return x * lax.rsqrt(jnp.mean(x * x, axis=-1, keepdims=True) + RMS_EPS) * w

    @pl.when(qi == 0)
    def _():
        for m in range(2):
            kx = k_ref[:, m * DA_DK:(m + 1) * DA_DK].astype(F32)
            kx_s[m] = rms(kx, kn_ref[...]).astype(BF16)
            kmx = km_ref[:, m * DA_DK:(m + 1) * DA_DK].astype(F32)
            kmeta_s[m] = rms(kmx, kn_ref[...]).astype(BF16)

    lam = (jnp.exp(jnp.sum(lam_ref[0:1, :] * lam_ref[1:2, :], axis=-1, keepdims=True))
           - jnp.exp(jnp.sum(lam_ref[2:3, :] * lam_ref[3:4, :], axis=-1, keepdims=True))
           + LAMBDA_INIT)
    row = lax.broadcasted_iota(jnp.int32, (tq, tq), 0)
    col = lax.broadcasted_iota(jnp.int32, (tq, tq), 1)
    diag_visible = col <= row
    v_m = vm_ref[...]
    scale = DA_DK ** -0.5 * math.log2(math.e)
    nt = (((1,), (1,)), ((), ()))

    def attend(n_prev):
        maps = range(2)
        qn = [(rms(q_ref[:, m * DA_DK:(m + 1) * DA_DK].astype(F32), qn_ref[...]) * scale).astype(BF16)
              for m in maps]
        sd = [jnp.where(diag_visible,
                        lax.dot_general(qn[m], kx_s[m, n_prev:n_prev + tq, :], nt,
                                        preferred_element_type=F32), NEG_BIG) for m in maps]
        sm = [lax.dot_general(qn[m], kmeta_s[m], nt, preferred_element_type=F32) for m in maps]
        mx = [jnp.maximum(jnp.max(sd[m], axis=-1, keepdims=True),
                          jnp.max(sm[m], axis=-1, keepdims=True)) for m in maps]
        if n_prev:
            sp = [lax.dot_general(qn[m], kx_s[m, 0:n_prev, :], nt, preferred_element_type=F32)
                  for m in maps]
            mx = [jnp.maximum(mx[m], jnp.max(sp[m], axis=-1, keepdims=True)) for m in maps]
        pd = [jnp.exp2(sd[m] - mx[m]) for m in maps]
        pm = [jnp.exp2(sm[m] - mx[m]) for m in maps]
        den = [jnp.sum(pd[m], axis=-1, keepdims=True) + jnp.sum(pm[m], axis=-1, keepdims=True)
               for m in maps]
        acc = [jnp.dot(pd[m].astype(BF16), v_ref[n_prev:n_prev + tq, :], preferred_element_type=F32)
               + jnp.dot(pm[m].astype(BF16), v_m, preferred_element_type=F32) for m in maps]
        if n_prev:
            pp = [jnp.exp2(sp[m] - mx[m]) for m in maps]
            den = [den[m] + jnp.sum(pp[m], axis=-1, keepdims=True) for m in maps]
            acc = [acc[m] + jnp.dot(pp[m].astype(BF16), v_ref[0:n_prev, :], preferred_element_type=F32)
                   for m in maps]
        out = acc[0] / den[0] - lam * (acc[1] / den[1])
        out = rms(out, sn_ref[...]) * (1.0 - LAMBDA_INIT)
        o_ref[...] = out.astype(o_ref.dtype)

    for blk in range(seq // tq):
        @pl.when(qi == blk)
        def _():
            attend(blk * tq)


def diff_attention(proj_x, proj_m, q_norm, k_norm, lam4, sub_norm, bsz, seq, tq=512):
    nq = seq // tq
    cq = COL_AQ // DA_DV
    ck = COL_AK // DA_DV
    cv = COL_AV // DA_DV
    return pl.pallas_call(
        functools.partial(_diff_attn_kernel, tq=tq, seq=seq),
        grid=(bsz, DA_HEADS, nq),
        in_specs=[
            pl.BlockSpec((tq, DA_DV), lambda b, h, i: (b * nq + i, cq + h)),
            pl.BlockSpec((seq, DA_DV), lambda b, h, i: (b, ck + h)),
            pl.BlockSpec((seq, DA_DV), lambda b, h, i: (b, cv + h)),
            pl.BlockSpec((N_META, DA_DV), lambda b, h, i: (0, h)),
            pl.BlockSpec((N_META, DA_DV), lambda b, h, i: (0, DA_HEADS + h)),
            pl.BlockSpec((1, DA_DK), lambda b, h, i: (0, 0)),
            pl.BlockSpec((1, DA_DK), lambda b, h, i: (0, 0)),
            pl.BlockSpec((4, DA_DK), lambda b, h, i: (0, 0)),
            pl.BlockSpec((1, DA_DV), lambda b, h, i: (0, 0)),
        ],
        out_specs=pl.BlockSpec((tq, DA_DV), lambda b, h, i: (b * nq + i, h)),
        out_shape=jax.ShapeDtypeStruct((bsz * seq, DA_HEADS * DA_DV), BF16),
        scratch_shapes=[pltpu.VMEM((2, seq, DA_DK), BF16),
                        pltpu.VMEM((2, N_META, DA_DK), BF16)],
        compiler_params=_params("parallel", "parallel", "arbitrary", vmem=VMEM_LIMIT),
        name="diff_attention",
    )(proj_x, proj_x, proj_x, proj_m, proj_m,
      q_norm.reshape(1, DA_DK).astype(F32), k_norm.reshape(1, DA_DK).astype(F32),
      lam4.astype(F32), sub_norm.reshape(1, DA_DV).astype(F32))


def _merge_kernel(ya_ref, yb_ref, wa_ref, wb_ref, ga_ref, gb_ref, o_ref):
    pa = jnp.dot(ya_ref[...], wa_ref[...].astype(BF16), preferred_element_type=F32)
    pb = jnp.dot(yb_ref[...], wb_ref[...].astype(BF16), preferred_element_type=F32)
    ga = jax.nn.sigmoid(ga_ref[...].astype(F32))
    gb = jax.nn.sigmoid(gb_ref[...].astype(F32))
    o_ref[...] = (ga * pa + gb * pb).astype(o_ref.dtype)


def branch_merge(y_a, y_b, w_a, w_b, proj_x, d_model, tm=1024, tn=512):
    m, ka = y_a.shape
    kb = y_b.shape[1]
    cga = COL_GA // tn
    cgb = (COL_GA + d_model) // tn
    return pl.pallas_call(
        _merge_kernel,
        grid=(m // tm, d_model // tn),
        in_specs=[
            pl.BlockSpec((tm, ka), lambda i, j: (i, 0)),
            pl.BlockSpec((tm, kb), lambda i, j: (i, 0)),
            pl.BlockSpec((ka, tn), lambda i, j: (0, j)),
            pl.BlockSpec((kb, tn), lambda i, j: (0, j)),
            pl.BlockSpec((tm, tn), lambda i, j: (i, cga + j)),
            pl.BlockSpec((tm, tn), lambda i, j: (i, cgb + j)),
        ],
        out_specs=pl.BlockSpec((tm, tn), lambda i, j: (i, j)),
        out_shape=jax.ShapeDtypeStruct((m, d_model), BF16),
        compiler_params=_params("parallel", "parallel", vmem=VMEM_LIMIT),
        name="branch_merge",
    )(y_a, y_b, w_a, w_b, proj_x, proj_x)


def _router_kernel(h_ref, nw_ref, whi_ref, wlo_ref, b_ref, idx_ref, gate_ref, up_ref):
    x = h_ref[...]
    u = x * lax.rsqrt(jnp.mean(x * x, axis=-1, keepdims=True) + RMS_EPS) * nw_ref[...]
    u_hi = u.astype(BF16)
    u_lo = (u - u_hi.astype(F32)).astype(BF16)
    half = u.shape[1] // 2
    bits = pltpu.bitcast(u_hi.astype(F32), jnp.uint32)
    up_ref[...] = bits[:, half:] | (bits[:, :half] >> 16)
    logits = (jnp.dot(u_hi, whi_ref[...], preferred_element_type=F32)
              + jnp.dot(u_lo, whi_ref[...], preferred_element_type=F32)
              + jnp.dot(u_hi, wlo_ref[...], preferred_element_type=F32)
              + b_ref[...])
    lane = lax.broadcasted_iota(jnp.int32, logits.shape, 1)
    lane_f = lane.astype(F32)
    vals = logits
    tops, idxs = [], []
    for _ in range(TOP_K):
        mx = jnp.max(vals, axis=-1, keepdims=True)
        ix = jnp.min(jnp.where(vals == mx, lane_f, float(LANES)), axis=-1, keepdims=True).astype(jnp.int32)
        tops.append(mx)
        idxs.append(ix)
        vals = jnp.where(lane == ix, -3.0e38, vals)
    exps = [jnp.exp(tv - tops[0]) for tv in tops]
    den = exps[0] + exps[1] + exps[2] + exps[3]
    idx_out = jnp.zeros(logits.shape, jnp.int32)
    gate_out = jnp.zeros(logits.shape, F32)
    for k in range(TOP_K):
        idx_out = jnp.where(lane == k, idxs[k], idx_out)
        gate_out = jnp.where(lane == k, exps[k] / den, gate_out)
    idx_ref[...] = idx_out
    gate_ref[...] = gate_out


def router(h2, ffn_norm, router_w, router_b, tm=512):
    n, d = h2.shape
    wpad = jnp.zeros((d, LANES), F32).at[:, :N_EXPERTS].set(router_w.astype(F32))
    w_hi = wpad.astype(BF16)
    w_lo = (wpad - w_hi.astype(F32)).astype(BF16)
    bias = jnp.full((1, LANES), NEG_BIG, F32).at[0, :N_EXPERTS].set(router_b.astype(F32))
    return pl.pallas_call(
        _router_kernel,
        grid=(n // tm,),
        in_specs=[
            pl.BlockSpec((tm, d), lambda i: (i, 0)),
            pl.BlockSpec((1, d), lambda i: (0, 0)),
            pl.BlockSpec((d, LANES), lambda i: (0, 0)),
            pl.BlockSpec((d, LANES), lambda i: (0, 0)),
            pl.BlockSpec((1, LANES), lambda i: (0, 0)),
        ],
        out_specs=[pl.BlockSpec((tm, LANES), lambda i: (i, 0)),
                   pl.BlockSpec((tm, LANES), lambda i: (i, 0)),
                   pl.BlockSpec((tm, d // 2), lambda i: (i, 0))],
        out_shape=[jax.ShapeDtypeStruct((n, LANES), jnp.int32),
                   jax.ShapeDtypeStruct((n, LANES), F32),
                   jax.ShapeDtypeStruct((n, d // 2), jnp.uint32)],
        compiler_params=_params("parallel", vmem=VMEM_LIMIT),
        name="router",
    )(h2, ffn_norm.reshape(1, d).astype(F32), w_hi, w_lo, bias)


def _expert_kernel(ie_ref, isb_ref, inb_ref, ifl_ref, tok_ref, up_hbm, wgu_hbm, wd_hbm, bgu_ref,
                   bd_ref, ys_hbm, xbuf, act, ybuf, wgu_buf, wd_buf, wg_s, wu_s, wd_s, zbuf,
                   sem_x, sem_y, sem_gu, sem_d, sem_z, *, j1, j2, tn1, tn2, n_items):
    i = pl.program_id(0)
    nb = inb_ref[i]
    sb = isb_ref[i]
    ex = ie_ref[i]
    R = MOE_ROWS

    n_fill = ifl_ref[i]

    @pl.when(n_fill > 0)
    def _():
        zbuf[...] = jnp.zeros_like(zbuf)

        def z_copy(u):
            return pltpu.make_async_copy(zbuf, ys_hbm.at[pl.ds((sb + u) * R, R), :], sem_z)

        for u in range(MOE_ITEM_BLOCKS):
            @pl.when(u < n_fill)
            def _():
                z_copy(u).start()
        for u in range(MOE_ITEM_BLOCKS):
            @pl.when(u < n_fill)
            def _():
                z_copy(u).wait()

    def gu_copies(expert, jt, slot):
        return [pltpu.make_async_copy(
            wgu_hbm.at[expert, :, pl.ds(pl.multiple_of(which * j1 * tn1 + jt * tn1, tn1), tn1)],
            wgu_buf.at[slot, which], sem_gu.at[slot]) for which in range(2)]

    def d_copy(expert, jd, slot):
        return pltpu.make_async_copy(
            wd_hbm.at[expert, :, pl.ds(pl.multiple_of(jd * tn2, tn2), tn2)],
            wd_buf.at[slot], sem_d.at[slot])

    rmax = xbuf.shape[0]
    HEAD = MOE_HEAD_UNITS * R
    BODY = MOE_BODY_UNITS * R
    last_tok = tok_ref.shape[0] - 1

    def batch(rows):
        return rows // j2

    own_units = jnp.maximum(nb, MOE_HEAD_UNITS)

    def row_gather(base, r):
        tok = tok_ref[jnp.minimum(base + r, last_tok)]
        return pltpu.make_async_copy(up_hbm.at[pl.ds(tok, 1), :], xbuf.at[pl.ds(r, 1), :], sem_x)

    def x_wait_units(n):
        for u in range(MOE_ITEM_BLOCKS):
            @pl.when(u < n)
            def _():
                pltpu.make_async_copy(up_hbm.at[pl.ds(0, R), :], xbuf.at[pl.ds(u * R, R), :],
                                      sem_x).wait()

    yw = tn2 // 2

    def y_copy(slot, r, jd):
        return pltpu.make_async_copy(
            ybuf.at[slot, pl.ds(r * R, R), :],
            ys_hbm.at[pl.ds((sb + r) * R, R), pl.ds(pl.multiple_of(jd * yw, yw), yw)],
            sem_y.at[slot])

    def for_blocks(fn):
        for r in range(MOE_ITEM_BLOCKS):
            @pl.when(r < nb)
            def _():
                fn(r)

    @pl.when(i == 0)
    def _():
        def start(r, c):
            row_gather(sb * R, r).start()
            return c
        lax.fori_loop(0, rmax, start, 0, unroll=8)
        for cp in gu_copies(ex, 0, 0):
            cp.start()

    nb_prev = inb_ref[jnp.maximum(i - 1, 0)]
    x_wait_units(jnp.where(
        i == 0, MOE_ITEM_BLOCKS,
        jnp.where(nb_prev > 0, jnp.maximum(jnp.maximum(nb_prev, MOE_HEAD_UNITS), nb), 0)))

    def gate_up_block(jt, r0, rows, wg, wu):
        x32 = xbuf[pl.ds(r0, rows), :]
        x_lo = pltpu.bitcast(x32 << 16, F32).astype(BF16)
        x_hi = pltpu.bitcast(x32 & jnp.uint32(0xFFFF0000), F32).astype(BF16)
        x = jnp.concatenate([x_lo, x_hi], axis=1)
        g = jnp.dot(x, wg, preferred_element_type=F32) + bgu_ref[pl.ds(jt, 1), :]
        u = jnp.dot(x, wu, preferred_element_type=F32) + bgu_ref[pl.ds(j1 + jt, 1), :]
        gt = jnp.minimum(g, SWIGLU_LIMIT)
        up = jnp.clip(u, -SWIGLU_LIMIT, SWIGLU_LIMIT)
        a = gt * jax.nn.sigmoid(SWIGLU_ALPHA * gt) * (up + 1.0)
        act[jt, pl.ds(r0, rows), :] = a.astype(BF16)

    def down_block(jd, slot, r0, rows, wd):
        lhs = jnp.concatenate([act[c, pl.ds(r0, rows), :] for c in range(j1)], axis=1)
        y = jnp.dot(lhs, wd, preferred_element_type=F32) + bd_ref[pl.ds(jd, 1), :]
        bits = pltpu.bitcast(y.astype(BF16).astype(F32), jnp.uint32)
        ybuf[slot, pl.ds(r0, rows), :] = bits[:, yw:] | (bits[:, :yw] >> 16)

    def rest_blocks(block_fn):
        rest = jnp.maximum(nb - MOE_HEAD_UNITS, 0)
        n_body = rest // MOE_BODY_UNITS
        rem = rest % MOE_BODY_UNITS
        r_tail = HEAD + n_body * BODY
        g_tail = batch(HEAD) + n_body * batch(BODY)

        def body(p, c):
            block_fn(pl.multiple_of(HEAD + p * BODY, R), BODY, batch(HEAD) + p * batch(BODY))
            return c

        lax.fori_loop(0, n_body, body, 0)

        @pl.when(rem >= 2)
        def _():
            block_fn(pl.multiple_of(r_tail, R), 2 * R, g_tail)

        @pl.when(rem % 2 == 1)
        def _():
            two = (rem // 2) * 2
            block_fn(pl.multiple_of(r_tail + two * R, R), R, g_tail + (rem // 2) * batch(2 * R))

    nxt = jnp.minimum(i + 1, n_items - 1)
    next_base = isb_ref[nxt] * R
    fetch_units = jnp.maximum(own_units, inb_ref[nxt])
    rows_per_tile = fetch_units * batch(R)

    n_wd = wd_buf.shape[0]

    @pl.when(nb > 0)
    def _():
        for t0 in range(n_wd - 1):
            d_copy(ex, t0, t0).start()

        def gate_up_tile(jt, carry):
            slot = jt % 2

            @pl.when(jt + 1 < j1)
            def _():
                for cp in gu_copies(ex, jt + 1, 1 - slot):
                    cp.start()

            for cp in gu_copies(ex, jt, slot):
                cp.wait()
            wg = wgu_buf[slot, 0].astype(BF16)
            wu = wgu_buf[slot, 1].astype(BF16)
            wg_s[...] = wg
            wu_s[...] = wu
            gate_up_block(jt, 0, HEAD, wg, wu)
            rest_blocks(lambda r0, rows, g0: gate_up_block(jt, r0, rows, wg_s[...], wu_s[...]))
            return carry

        lax.fori_loop(0, j1, gate_up_tile, 0)

        @pl.when(i + 1 < n_items)
        def _():
            @pl.when(inb_ref[i + 1] > 0)
            def _():
                for cp in gu_copies(ie_ref[i + 1], 0, 0):
                    cp.start()

        def down_tile(jd, carry):
            slot = jd % 2
            wslot = jd % n_wd

            @pl.when(jd + n_wd - 1 < j2)
            def _():
                d_copy(ex, jd + n_wd - 1, (jd + n_wd - 1) % n_wd).start()

            @pl.when(jd >= 2)
            def _():
                for_blocks(lambda r: y_copy(slot, r, jd).wait())

            d_copy(ex, jd, wslot).wait()
            wd = wd_buf[wslot].astype(BF16)
            wd_s[...] = wd
            down_block(jd, slot, 0, HEAD, wd)
            tile_row0 = jd * rows_per_tile

            def gather_batch(first, count):
                for rr in range(count):
                    row_gather(next_base, tile_row0 + first + rr).start()

            gather_batch(0, batch(HEAD))

            def later_block(r0, rows, g0):
                down_block(jd, slot, r0, rows, wd_s[...])
                gather_batch(g0, batch(rows))

            rest_blocks(later_block)

            def make_up(r, c):
                row_gather(next_base, tile_row0 + r).start()
                return c

            lax.fori_loop(own_units * batch(R), rows_per_tile, make_up, 0)
            for_blocks(lambda r: y_copy(slot, r, jd).start())
            return carry

        lax.fori_loop(0, j2, down_tile, 0)
        for_blocks(lambda r: y_copy(0, r, 0).wait())
        for_blocks(lambda r: y_copy(1, r, 0).wait())

        @pl.when(i == n_items - 1)
        def _():
            x_wait_units(fetch_units)


def expert_mlp(u_packed, row_tok, item_e, item_sb, item_nb, item_fill, w_gate_up, b_gate_up, w_down,
               b_down, tn1=256, tn2=MOE_DOWN_TILE):
    p = row_tok.shape[0]
    d = 2 * u_packed.shape[1]
    n_items = item_e.shape[0]
    de = w_down.shape[1]
    j1 = de // tn1
    j2 = d // tn2
    rmax = MOE_ITEM_BLOCKS * MOE_ROWS

    assert MOE_ROWS % j2 == 0 and MOE_BODY_UNITS == 4 and MOE_DOWN_SLOTS <= j2

    def bias_map(i, ie, isb, inb, ifl, tok):
        return (ie[i], 0, 0)

    return pl.pallas_call(
        functools.partial(_expert_kernel, j1=j1, j2=j2, tn1=tn1, tn2=tn2, n_items=n_items),
        grid_spec=pltpu.PrefetchScalarGridSpec(
            num_scalar_prefetch=5,
            grid=(n_items,),
            in_specs=[
                pl.BlockSpec(memory_space=pl.ANY),
                pl.BlockSpec(memory_space=pl.ANY),
                pl.BlockSpec(memory_space=pl.ANY),
                pl.BlockSpec((None, 2 * j1, tn1), bias_map),
                pl.BlockSpec((None, j2, tn2), bias_map),
            ],
            out_specs=pl.BlockSpec(memory_space=pl.ANY),
            scratch_shapes=[
                pltpu.VMEM((rmax, d // 2), jnp.uint32),
                pltpu.VMEM((j1, rmax, tn1), BF16),
                pltpu.VMEM((2, rmax, tn2 // 2), jnp.uint32),
                pltpu.VMEM((2, 2, d, tn1), F32),
                pltpu.VMEM((MOE_DOWN_SLOTS, de, tn2), F32),
                pltpu.VMEM((d, tn1), BF16),
                pltpu.VMEM((d, tn1), BF16),
                pltpu.VMEM((de, tn2), BF16),
                pltpu.VMEM((MOE_ROWS, d // 2), jnp.uint32),
                pltpu.SemaphoreType.DMA(()),
                pltpu.SemaphoreType.DMA((2,)),
                pltpu.SemaphoreType.DMA((2,)),
                pltpu.SemaphoreType.DMA((MOE_DOWN_SLOTS,)),
                pltpu.SemaphoreType.DMA(()),
            ],
        ),
        out_shape=jax.ShapeDtypeStruct((p, d // 2), jnp.uint32),
        compiler_params=_params("arbitrary", vmem=VMEM_LIMIT),
        name="expert_mlp",
    )(item_e, item_sb, item_nb, item_fill, row_tok, u_packed, w_gate_up, w_down,
      b_gate_up.reshape(N_EXPERTS, 2 * j1, tn1), b_down.reshape(N_EXPERTS, j2, tn2))


def _combine_kernel(dest_ref, h_ref, g_ref, ys_hbm, o_ref, buf, sem, *, tt, n_tok, n_steps, yw):
    i = pl.program_id(0)

    def issue(blk, slot):
        for k in range(TOP_K):
            def start(t, c):
                src = dest_ref[k * n_tok + blk * tt + t]
                pltpu.make_async_copy(ys_hbm.at[pl.ds(src, 1), :], buf.at[slot, k, pl.ds(t, 1), :],
                                      sem.at[slot]).start()
                return c
            lax.fori_loop(0, tt, start, 0, unroll=8)

    @pl.when(i == 0)
    def _():
        issue(0, 0)

    slot = i % 2

    @pl.when(i + 1 < n_steps)
    def _():
        issue(i + 1, 1 - slot)

    for k in range(TOP_K):
        pltpu.make_async_copy(ys_hbm.at[pl.ds(0, tt), :], buf.at[slot, k], sem.at[slot]).wait()
    g = g_ref[...]
    lo = None
    hi = None
    for k in range(TOP_K):
        word = buf[slot, k]
        gk = g[:, k:k + 1]
        lo_k = gk * pltpu.bitcast(word << 16, F32)
        hi_k = gk * pltpu.bitcast(word & jnp.uint32(0xFFFF0000), F32)
        lo = lo_k if lo is None else lo + lo_k
        hi = hi_k if hi is None else hi + hi_k
    for c in range(lo.shape[1] // yw):
        o_ref[:, 2 * c * yw:(2 * c + 1) * yw] = (h_ref[:, 2 * c * yw:(2 * c + 1) * yw]
                                                  + lo[:, c * yw:(c + 1) * yw])
        o_ref[:, (2 * c + 1) * yw:(2 * c + 2) * yw] = (h_ref[:, (2 * c + 1) * yw:(2 * c + 2) * yw]
                                                        + hi[:, c * yw:(c + 1) * yw])


def combine(h2, gates, ys, dest_km, yw, tt=256):
    n, d = h2.shape
    return pl.pallas_call(
        functools.partial(_combine_kernel, tt=tt, n_tok=n, n_steps=n // tt, yw=yw),
        grid_spec=pltpu.PrefetchScalarGridSpec(
            num_scalar_prefetch=1,
            grid=(n // tt,),
            in_specs=[pl.BlockSpec((tt, d), lambda i, dst: (i, 0)),
                      pl.BlockSpec((tt, LANES), lambda i, dst: (i, 0)),
                      pl.BlockSpec(memory_space=pl.ANY)],
            out_specs=pl.BlockSpec((tt, d), lambda i, dst: (i, 0)),
            scratch_shapes=[pltpu.VMEM((2, TOP_K, tt, d // 2), jnp.uint32),
                            pltpu.SemaphoreType.DMA((2,))],
        ),
        out_shape=jax.ShapeDtypeStruct((n, d), F32),
        compiler_params=_params("arbitrary", vmem=VMEM_LIMIT),
        name="moe_combine",
    )(dest_km, h2, gates, ys)


def routing_tables(top_idx, n_tok):
    a = n_tok * TOP_K
    e_flat = top_idx.reshape(-1)
    onehot = (e_flat[:, None] == jnp.arange(N_EXPERTS, dtype=jnp.int32)[None, :]).astype(jnp.int32)
    csum = jnp.cumsum(onehot, axis=0)
    rank = jnp.sum(csum * onehot, axis=1) - 1
    counts = csum[-1]
    nblk = (counts + MOE_ROWS - 1) // MOE_ROWS
    blk_end = jnp.cumsum(nblk)
    blk_start = blk_end - nblk
    dest = blk_start[e_flat] * MOE_ROWS + rank
    n_blocks = -(-(a + N_EXPERTS * (MOE_ROWS - 1)) // MOE_ROWS)
    p = n_blocks * MOE_ROWS
    t_flat = jnp.arange(a, dtype=jnp.int32) // TOP_K
    row_tok = jnp.zeros((p,), jnp.int32).at[dest].set(t_flat, unique_indices=True,
                                                      mode='promise_in_bounds')

    n_items = N_EXPERTS + n_blocks // MOE_ITEM_BLOCKS
    items_per_e = (nblk + MOE_ITEM_BLOCKS - 1) // MOE_ITEM_BLOCKS
    item_end = jnp.cumsum(items_per_e)
    item_start = item_end - items_per_e
    total_items = item_end[-1]
    ii = jnp.arange(n_items, dtype=jnp.int32)
    ic = jnp.minimum(ii, total_items - 1)
    e_i = jnp.minimum(jnp.searchsorted(item_end, ic, side='right'), N_EXPERTS - 1).astype(jnp.int32)
    local = ic - item_start[e_i]
    item_sb = (blk_start[e_i] + local * MOE_ITEM_BLOCKS).astype(jnp.int32)
    item_nb = jnp.clip(nblk[e_i] - local * MOE_ITEM_BLOCKS, 0, MOE_ITEM_BLOCKS)
    active = ii < total_items
    item_nb = jnp.where(active, item_nb, 0).astype(jnp.int32)
    fill_start = blk_end[-1] + (ii - total_items) * MOE_ITEM_BLOCKS
    item_fill = jnp.where(active, 0, jnp.clip(n_blocks - fill_start, 0, MOE_ITEM_BLOCKS)).astype(jnp.int32)
    item_sb = jnp.where(active, item_sb, jnp.minimum(fill_start, n_blocks - 1)).astype(jnp.int32)
    dest_km = dest.astype(jnp.int32).reshape(n_tok, TOP_K).T.reshape(-1)
    return dest_km, row_tok, e_i, item_sb, item_nb, item_fill


def kernel(x, meta_tokens, mix_norm, w_in, dn_conv, dn_a_log, dn_dt_bias, dn_out_norm, da_q_norm, da_k_norm, da_lam_q1, da_lam_k1, da_lam_q2, da_lam_k2, da_sub_norm, w_branch_a, w_branch_b, w_out, ffn_norm, router_w, router_b, w_gate_up, b_gate_up, w_down, b_down):
    bsz, seq, d = x.shape
    n_tok = bsz * seq
    x2 = x.reshape(n_tok, d)
    layer = 0

    w_bf = w_in[layer].astype(BF16)
    da_shift = 2 * DN_HEADS
    da_cols = w_bf.shape[1] - COL_BA - da_shift
    tn = 1024

    u_x = rms_cast(x2, mix_norm[layer], 512)
    u_m = rms_cast(meta_tokens.astype(F32), mix_norm[layer], N_META)
    proj_dn = matmul_fullk(u_x, w_bf, BF16, 1024, tn, name="in_proj_dn", n_blocks=COL_BA // tn)
    proj_da = matmul_fullk_shift(u_x, w_bf, BF16, 1024, tn, COL_BA, da_shift, da_cols // tn,
                                 name="in_proj_da")
    ba_x = matmul_fullk(u_x, w_bf, F32, 1024, LANES, name="in_proj_ba",
                        n_offset=COL_BA // LANES, n_blocks=1)
    proj_dn_m = matmul_fullk(u_m, w_bf, BF16, N_META, tn, name="in_proj_dn_meta", n_blocks=COL_DZ // tn)
    proj_kv_m = matmul_fullk_shift(u_m, w_bf, BF16, N_META, tn, COL_BA, da_shift,
                                   (COL_GA - COL_AK) // tn, name="in_proj_kv_meta",
                                   n_offset=COL_AK // tn)
    ba_m = matmul_fullk(u_m, w_bf, F32, N_META, LANES, name="in_proj_ba_meta",
                        n_offset=COL_BA // LANES, n_blocks=1)

    y_a = gated_deltanet(proj_dn, ba_x, proj_dn_m, ba_m, dn_conv[layer], dn_a_log[layer],
                         dn_dt_bias[layer], dn_out_norm[layer], bsz, seq)
    lam4 = jnp.stack([da_lam_q1[layer], da_lam_k1[layer], da_lam_q2[layer], da_lam_k2[layer]])
    y_b = diff_attention(proj_da, proj_kv_m, da_q_norm[layer], da_k_norm[layer], lam4,
                         da_sub_norm[layer], bsz, seq)

    merged = branch_merge(y_a, y_b, w_branch_a[layer], w_branch_b[layer], proj_da, d)
    h2 = matmul_fullk(merged, w_out[layer], F32, 1024, 512, res=x2, name="out_proj")

    top_idx, gates, u_packed = router(h2, ffn_norm[layer], router_w[layer], router_b[layer])
    dest, row_tok, item_e, item_sb, item_nb, item_fill = routing_tables(top_idx[:, :TOP_K], n_tok)
    ys = expert_mlp(u_packed, row_tok, item_e, item_sb, item_nb, item_fill, w_gate_up[layer],
                    b_gate_up[layer], w_down[layer], b_down[layer])
    out = combine(h2, gates, ys, dest, MOE_DOWN_TILE // 2)
    return out.reshape(bsz, seq, d)
```

```python
import functools
import math

import jax
import jax.numpy as jnp
from jax import lax
from jax.experimental import pallas as pl
from jax.experimental.pallas import tpu as pltpu

F32 = jnp.float32
BF16 = jnp.bfloat16

N_META = 16
RMS_EPS = 1e-6
L2_EPS = 1e-6

DN_HEADS = 16
DN_DK = 128
DN_DV = 128
DN_CONV = 4
DN_CHUNK = 64
DN_W = DN_HEADS * DN_DK
DN_HIST = 16
DN_CONV_COLS = 1024

DA_HEADS = 8
DA_DK = 128
DA_DV = 256
LAMBDA_INIT = 0.8 - 0.6 * math.exp(-0.3 * 0)

N_EXPERTS = 32
TOP_K = 4
D_EXPERT = 1536
SWIGLU_LIMIT = 7.0
SWIGLU_ALPHA = 1.702

LANES = 128
MOE_ROWS = 128
MOE_ITEM_BLOCKS = 10
MOE_HEAD_UNITS = 4
MOE_BODY_UNITS = 4
MOE_DOWN_SLOTS = 2
MOE_DOWN_TILE = 512
NEG_BIG = -1e30
VMEM_LIMIT = 56 * 1024 * 1024

COL_DZ = 3 * DN_W
COL_BA = 4 * DN_W
COL_AQ = 0
COL_AK = COL_AQ + DA_HEADS * 2 * DA_DK
COL_AV = COL_AK + DA_HEADS * 2 * DA_DK
COL_GA = COL_AV + DA_HEADS * DA_DV


def _params(*sem, vmem=None):
    return pltpu.CompilerParams(dimension_semantics=sem, vmem_limit_bytes=vmem)


def _rms_cast_kernel(x_ref, w_ref, o_ref):
    x = x_ref[...]
    y = x * lax.rsqrt(jnp.mean(x * x, axis=-1, keepdims=True) + RMS_EPS) * w_ref[...]
    o_ref[...] = y.astype(o_ref.dtype)


def rms_cast(x, w, tm):
    m, d = x.shape
    return pl.pallas_call(
        _rms_cast_kernel,
        grid=(m // tm,),
        in_specs=[pl.BlockSpec((tm, d), lambda i: (i, 0)),
                  pl.BlockSpec((1, d), lambda i: (0, 0))],
        out_specs=pl.BlockSpec((tm, d), lambda i: (i, 0)),
        out_shape=jax.ShapeDtypeStruct((m, d), BF16),
        compiler_params=_params("parallel"),
        name="rms_cast",
    )(x, w.reshape(1, d))


def _mm_fullk_kernel(a_ref, b_ref, *rest, has_res):
    acc = jnp.dot(a_ref[...], b_ref[...].astype(BF16), preferred_element_type=F32)
    if has_res:
        r_ref, o_ref = rest
        acc = acc + r_ref[...]
    else:
        o_ref, = rest
    o_ref[...] = acc.astype(o_ref.dtype)


def matmul_fullk(a, b, out_dtype, tm, tn, res=None, name="matmul", n_offset=0, n_blocks=None):
    m, kd = a.shape
    if n_blocks is None:
        n_blocks = b.shape[1] // tn
    in_specs = [pl.BlockSpec((tm, kd), lambda i, j: (i, 0)),
                pl.BlockSpec((kd, tn), lambda i, j: (0, n_offset + j))]
    args = [a, b]
    if res is not None:
        in_specs.append(pl.BlockSpec((tm, tn), lambda i, j: (i, j)))
        args.append(res)
    return pl.pallas_call(
        functools.partial(_mm_fullk_kernel, has_res=res is not None),
        grid=(m // tm, n_blocks),
        in_specs=in_specs,
        out_specs=pl.BlockSpec((tm, tn), lambda i, j: (i, j)),
        out_shape=jax.ShapeDtypeStruct((m, n_blocks * tn), out_dtype),
        compiler_params=_params("parallel", "parallel", vmem=VMEM_LIMIT),
        name=name,
    )(*args)


def _mm_shift_kernel(a_ref, b0_ref, b1_ref, o_ref, *, shift):
    b = jnp.concatenate([b0_ref[:, shift:], b1_ref[:, :shift]], axis=1)
    o_ref[...] = jnp.dot(a_ref[...], b, preferred_element_type=F32).astype(o_ref.dtype)


def matmul_fullk_shift(a, b, out_dtype, tm, tn, col0, shift, n_blocks, name, n_offset=0):
    m, kd = a.shape
    first = col0 // tn + n_offset
    return pl.pallas_call(
        functools.partial(_mm_shift_kernel, shift=shift),
        grid=(m // tm, n_blocks),
        in_specs=[pl.BlockSpec((tm, kd), lambda i, j: (i, 0)),
                  pl.BlockSpec((kd, tn), lambda i, j: (0, first + j)),
                  pl.BlockSpec((kd, LANES), lambda i, j: (0, (first + j + 1) * (tn // LANES)))],
        out_specs=pl.BlockSpec((tm, tn), lambda i, j: (i, j)),
        out_shape=jax.ShapeDtypeStruct((m, n_blocks * tn), out_dtype),
        compiler_params=_params("parallel", "parallel", vmem=VMEM_LIMIT),
        name=name,
    )(a, b, b)


def _deltanet_kernel(qkv_ref, z_ref, ba_ref, qkvm_ref, bam_ref, cw_ref, hp_ref, nw_ref,
                     o_ref, full_ref, act_ref, ba_s, s_ref):
    t = pl.program_id(1)
    C = DN_CHUNK
    HIST = DN_HIST
    n_pad = C - N_META

    @pl.when(t == 0)
    def _():
        s_ref[...] = jnp.zeros_like(s_ref)
        full_ref[0:HIST + n_pad, :] = jnp.zeros((HIST + n_pad, 3 * DN_W), BF16)
        full_ref[HIST + n_pad:HIST + C, :] = qkvm_ref[...]
        ba_s[0:n_pad, :] = jnp.zeros((n_pad, LANES), F32)
        ba_s[n_pad:C, :] = bam_ref[...]

    @pl.when(t > 0)
    def _():
        full_ref[HIST:HIST + C, :] = qkv_ref[...]
        ba_s[...] = ba_ref[...]

    n_shift = DN_CONV - 1
    sr = lax.broadcasted_iota(jnp.int32, (n_shift * C, HIST + C), 0)
    sc = lax.broadcasted_iota(jnp.int32, (n_shift * C, HIST + C), 1)
    shift = jnp.where(sc == HIST - n_shift + (sr % C) + (sr // C), 1.0, 0.0).astype(BF16)
    for c0 in range(0, 3 * DN_W, DN_CONV_COLS):
        cs = slice(c0, c0 + DN_CONV_COLS)
        delayed = jnp.dot(shift, full_ref[:, cs], preferred_element_type=F32)
        conv = cw_ref[n_shift:n_shift + 1, cs] * full_ref[HIST:HIST + C, cs].astype(F32)
        for j in range(n_shift):
            conv = conv + cw_ref[j:j + 1, cs] * delayed[j * C:(j + 1) * C, :]
        act_ref[:, cs] = conv * jax.nn.sigmoid(conv)
    full_ref[0:HIST, :] = full_ref[C:C + HIST, :]

    row1 = lax.broadcasted_iota(jnp.int32, (C, 1), 0)
    valid = jnp.where((t > 0) | (row1 >= n_pad), 1.0, 0.0).astype(F32)
    ba = ba_s[...]
    beta_all = jax.nn.sigmoid(ba) * valid
    xg = ba + hp_ref[1:2, :]
    softplus = jnp.maximum(xg, 0.0) + jnp.log1p(jnp.exp(-jnp.abs(xg)))
    g_all = -jnp.exp(hp_ref[0:1, :]) * softplus * valid

    row = lax.broadcasted_iota(jnp.int32, (C, C), 0)
    col = lax.broadcasted_iota(jnp.int32, (C, C), 1)
    causal = row >= col
    strict = row > col
    eye = jnp.where(row == col, 1.0, 0.0).astype(F32)
    tril = jnp.where(causal, 1.0, 0.0).astype(BF16)

    g_hi = g_all.astype(BF16)
    r1 = g_all - g_hi.astype(F32)
    g_mid = r1.astype(BF16)
    g_lo = (r1 - g_mid.astype(F32)).astype(BF16)
    g3 = jnp.dot(tril, jnp.concatenate([g_hi, g_mid, g_lo], axis=1), preferred_element_type=F32)
    gcum = g3[:, 0:LANES] + g3[:, LANES:2 * LANES] + g3[:, 2 * LANES:3 * LANES]
    gcum_t = gcum.T

    scale = DN_DK ** -0.5
    nw = nw_ref[...]
    heads = range(DN_HEADS)

    def mm(a, b):
        return jnp.dot(a.astype(BF16), b.astype(BF16), preferred_element_type=F32)

    kn_b, lhs1, decay, rhs, qe_b, kdec_t, e_last = [], [], [], [], [], [], []
    for h in heads:
        qh = act_ref[:, h * DN_DK:(h + 1) * DN_DK]
        kh = act_ref[:, DN_W + h * DN_DK:DN_W + (h + 1) * DN_DK]
        vh = act_ref[:, 2 * DN_W + h * DN_DV:2 * DN_W + (h + 1) * DN_DV]
        qn = qh * (lax.rsqrt(jnp.sum(qh * qh, axis=-1, keepdims=True) + L2_EPS) * scale)
        kn = kh * lax.rsqrt(jnp.sum(kh * kh, axis=-1, keepdims=True) + L2_EPS)
        beta = beta_all[:, h:h + 1]
        gc = gcum[:, DN_HEADS + h:DN_HEADS + h + 1]
        gr = gcum_t[DN_HEADS + h:DN_HEADS + h + 1, :]
        g_last = gc[C - 1:C, :]
        e_g = jnp.exp(gc)
        kb = kn * beta
        kn_b.append(kn.astype(BF16))
        lhs1.append(jnp.concatenate([kb, qn], axis=0).astype(BF16))
        decay.append(jnp.exp(jnp.where(causal, gc - gr, NEG_BIG)))
        rhs.append(jnp.concatenate([vh * beta, kb * e_g], axis=1).astype(BF16))
        qe_b.append((qn * e_g).astype(BF16))
        kdec_t.append((kn * jnp.exp(g_last - gc)).T.astype(BF16))
        e_last.append(jnp.exp(g_last))

    kq = [lax.dot_general(lhs1[h], kn_b[h], (((1,), (1,)), ((), ())), preferred_element_type=F32)
          for h in heads]
    qk_b = [jnp.where(causal, kq[h][C:2 * C, :] * decay[h], 0.0).astype(BF16) for h in heads]
    b1 = [-jnp.where(strict, kq[h][0:C, :] * decay[h], 0.0) for h in heads]
    b2 = [mm(b1[h], b1[h]) for h in heads]
    b4 = [mm(b2[h], b2[h]) for h in heads]
    p1 = [(eye + b1[h]) + mm(eye + b1[h], b2[h]) for h in heads]
    b8 = [mm(b4[h], b4[h]) for h in heads]
    b16 = [mm(b8[h], b8[h]) for h in heads]
    p2 = [(eye + b4[h]) + mm(eye + b4[h], b8[h]) for h in heads]
    b32 = [mm(b16[h], b16[h]) for h in heads]
    p12 = [mm(p1[h], p2[h]) for h in heads]
    p3 = [(eye + b16[h]) + mm(eye + b16[h], b32[h]) for h in heads]
    inv = [mm(p12[h], p3[h]) for h in heads]
    sol = [mm(inv[h], rhs[h]) for h in heads]

    s_old = [s_ref[h] for h in heads]
    ws_lhs = [jnp.concatenate([sol[h][:, DN_DV:2 * DN_DV].astype(BF16), qe_b[h]], axis=0) for h in heads]
    ws = [mm(ws_lhs[h], s_old[h]) for h in heads]
    v_new = [(sol[h][:, 0:DN_DV] - ws[h][0:C, :]).astype(BF16) for h in heads]
    ov_lhs = [jnp.concatenate([qk_b[h], kdec_t[h]], axis=0) for h in heads]
    ov = [mm(ov_lhs[h], v_new[h]) for h in heads]
    for h in heads:
        s_ref[h] = s_old[h] * e_last[h] + ov[h][C:C + DN_DK, :]
        o = ws[h][C:2 * C, :] + ov[h][0:C, :]
        zh = z_ref[:, h * DN_DV:(h + 1) * DN_DV].astype(F32)
        o = o * lax.rsqrt(jnp.mean(o * o, axis=-1, keepdims=True) + RMS_EPS) * nw
        o_ref[:, h * DN_DV:(h + 1) * DN_DV] = (o * (zh * jax.nn.sigmoid(zh))).astype(o_ref.dtype)


def gated_deltanet(proj_x, ba_x, proj_m, ba_m, conv_w, a_log, dt_bias, norm_w, bsz, seq):
    C = DN_CHUNK
    n_chunks = seq // C
    w3 = 3 * DN_W
    hp = jnp.zeros((2, LANES), F32)
    hp = hp.at[0, DN_HEADS:2 * DN_HEADS].set(a_log.astype(F32))
    hp = hp.at[1, DN_HEADS:2 * DN_HEADS].set(dt_bias.astype(F32))

    def xrow(b, t):
        return b * n_chunks + jnp.maximum(t - 1, 0)

    return pl.pallas_call(
        _deltanet_kernel,
        grid=(bsz, n_chunks + 1),
        in_specs=[
            pl.BlockSpec((C, w3), lambda b, t: (xrow(b, t), 0)),
            pl.BlockSpec((C, DN_W), lambda b, t: (xrow(b, t), COL_DZ // DN_W)),
            pl.BlockSpec((C, LANES), lambda b, t: (xrow(b, t), 0)),
            pl.BlockSpec((N_META, w3), lambda b, t: (0, 0)),
            pl.BlockSpec((N_META, LANES), lambda b, t: (0, 0)),
            pl.BlockSpec((DN_CONV, w3), lambda b, t: (0, 0)),
            pl.BlockSpec((2, LANES), lambda b, t: (0, 0)),
            pl.BlockSpec((1, DN_DV), lambda b, t: (0, 0)),
        ],
        out_specs=pl.BlockSpec((C, DN_W), lambda b, t: (xrow(b, t), 0)),
        out_shape=jax.ShapeDtypeStruct((bsz * seq, DN_W), BF16),
        scratch_shapes=[
            pltpu.VMEM((DN_HIST + C, w3), BF16),
            pltpu.VMEM((C, w3), F32),
            pltpu.VMEM((C, LANES), F32),
            pltpu.VMEM((DN_HEADS, DN_DK, DN_DV), F32),
        ],
        compiler_params=_params("parallel", "arbitrary", vmem=VMEM_LIMIT),
        name="gated_deltanet",
    )(proj_x, proj_x, ba_x, proj_m, ba_m, conv_w.astype(F32), hp, norm_w.reshape(1, DN_DV).astype(F32))


def _diff_attn_kernel(q_ref, k_ref, v_ref, km_ref, vm_ref, qn_ref, kn_ref, lam_ref, sn_ref,
                      o_ref, kx_s, kmeta_s, *, tq, seq):
    qi = pl.program_id(2)

    def rms(x, w):
        return x * lax.rsqrt(jnp.mean(x * x, axis=-1, keepdims=True) + RMS_EPS) * w

    @pl.when(qi == 0)
    def _():
        for m in range(2):
            kx = k_ref[:, m * DA_DK:(m + 1) * DA_DK].astype(F32)
            kx_s[m] = rms(kx, kn_ref[...]).astype(BF16)
            kmx = km_ref[:, m * DA_DK:(m + 1) * DA_DK].astype(F32)
            kmeta_s[m] = rms(kmx, kn_ref[...]).astype(BF16)

    lam = (jnp.exp(jnp.sum(lam_ref[0:1, :] * lam_ref[1:2, :], axis=-1, keepdims=True))
           - jnp.exp(jnp.sum(lam_ref[2:3, :] * lam_ref[3:4, :], axis=-1, keepdims=True))
           + LAMBDA_INIT)
    row = lax.broadcasted_iota(jnp.int32, (tq, tq), 0)
    col = lax.broadcasted_iota(jnp.int32, (tq, tq), 1)
    diag_visible = col <= row
    v_m = vm_ref[...]
    scale = DA_DK ** -0.5 * math.log2(math.e)
    nt = (((1,), (1,)), ((), ()))

    def attend(n_prev):
        maps = range(2)
        qn = [(rms(q_ref[:, m * DA_DK:(m + 1) * DA_DK].astype(F32), qn_ref[...]) * scale).astype(BF16)
              for m in maps]
        sd = [jnp.where(diag_visible,
                        lax.dot_general(qn[m], kx_s[m, n_prev:n_prev + tq, :], nt,
                                        preferred_element_type=F32), NEG_BIG) for m in maps]
        sm = [lax.dot_general(qn[m], kmeta_s[m], nt, preferred_element_type=F32) for m in maps]
        mx = [jnp.maximum(jnp.max(sd[m], axis=-1, keepdims=True),
                          jnp.max(sm[m], axis=-1, keepdims=True)) for m in maps]
        if n_prev:
            sp = [lax.dot_general(qn[m], kx_s[m, 0:n_prev, :], nt, preferred_element_type=F32)
                  for m in maps]
            mx = [jnp.maximum(mx[m], jnp.max(sp[m], axis=-1, keepdims=True)) for m in maps]
        pd = [jnp.exp2(sd[m] - mx[m]) for m in maps]
        pm = [jnp.exp2(sm[m] - mx[m]) for m in maps]
        den = [jnp.sum(pd[m], axis=-1, keepdims=True) + jnp.sum(pm[m], axis=-1, keepdims=True)
               for m in maps]
        acc = [jnp.dot(pd[m].astype(BF16), v_ref[n_prev:n_prev + tq, :], preferred_element_type=F32)
               + jnp.dot(pm[m].astype(BF16), v_m, preferred_element_type=F32) for m in maps]
        if n_prev:
            pp = [jnp.exp2(sp[m] - mx[m]) for m in maps]
            den = [den[m] + jnp.sum(pp[m], axis=-1, keepdims=True) for m in maps]
            acc = [acc[m] + jnp.dot(pp[m].astype(BF16), v_ref[0:n_prev, :], preferred_element_type=F32)
                   for m in maps]
        out = acc[0] / den[0] - lam * (acc[1] / den[1])
        out = rms(out, sn_ref[...]) * (1.0 - LAMBDA_INIT)
        o_ref[...] = out.astype(o_ref.dtype)

    for blk in range(seq // tq):
        @pl.when(qi == blk)
        def _():
            attend(blk * tq)


def diff_attention(proj_x, proj_m, q_norm, k_norm, lam4, sub_norm, bsz, seq, tq=512):
    nq = seq // tq
    cq = COL_AQ // DA_DV
    ck = COL_AK // DA_DV
    cv = COL_AV // DA_DV
    return pl.pallas_call(
        functools.partial(_diff_attn_kernel, tq=tq, seq=seq),
        grid=(bsz, DA_HEADS, nq),
        in_specs=[
            pl.BlockSpec((tq, DA_DV), lambda b, h, i: (b * nq + i, cq + h)),
            pl.BlockSpec((seq, DA_DV), lambda b, h, i: (b, ck + h)),
            pl.BlockSpec((seq, DA_DV), lambda b, h, i: (b, cv + h)),
            pl.BlockSpec((N_META, DA_DV), lambda b, h, i: (0, h)),
            pl.BlockSpec((N_META, DA_DV), lambda b, h, i: (0, DA_HEADS + h)),
            pl.BlockSpec((1, DA_DK), lambda b, h, i: (0, 0)),
            pl.BlockSpec((1, DA_DK), lambda b, h, i: (0, 0)),
            pl.BlockSpec((4, DA_DK), lambda b, h, i: (0, 0)),
            pl.BlockSpec((1, DA_DV), lambda b, h, i: (0, 0)),
        ],
        out_specs=pl.BlockSpec((tq, DA_DV), lambda b, h, i: (b * nq + i, h)),
        out_shape=jax.ShapeDtypeStruct((bsz * seq, DA_HEADS * DA_DV), BF16),
        scratch_shapes=[pltpu.VMEM((2, seq, DA_DK), BF16),
                        pltpu.VMEM((2, N_META, DA_DK), BF16)],
        compiler_params=_params("parallel", "parallel", "arbitrary", vmem=VMEM_LIMIT),
        name="diff_attention",
    )(proj_x, proj_x, proj_x, proj_m, proj_m,
      q_norm.reshape(1, DA_DK).astype(F32), k_norm.reshape(1, DA_DK).astype(F32),
      lam4.astype(F32), sub_norm.reshape(1, DA_DV).astype(F32))


def _merge_kernel(ya_ref, yb_ref, wa_ref, wb_ref, ga_ref, gb_ref, o_ref):
    pa = jnp.dot(ya_ref[...], wa_ref[...].astype(BF16), preferred_element_type=F32)
    pb = jnp.dot(yb_ref[...], wb_ref[...].astype(BF16), preferred_element_type=F32)
    ga = jax.nn.sigmoid(ga_ref[...].astype(F32))
    gb = jax.nn.sigmoid(gb_ref[...].astype(F32))
    o_ref[...] = (ga * pa + gb * pb).astype(o_ref.dtype)


def branch_merge(y_a, y_b, w_a, w_b, proj_x, d_model, tm=1024, tn=512):
    m, ka = y_a.shape
    kb = y_b.shape[1]
    cga = COL_GA // tn
    cgb = (COL_GA + d_model) // tn
    return pl.pallas_call(
        _merge_kernel,
        grid=(m // tm, d_model // tn),
        in_specs=[
            pl.BlockSpec((tm, ka), lambda i, j: (i, 0)),
            pl.BlockSpec((tm, kb), lambda i, j: (i, 0)),
            pl.BlockSpec((ka, tn), lambda i, j: (0, j)),
            pl.BlockSpec((kb, tn), lambda i, j: (0, j)),
            pl.BlockSpec((tm, tn), lambda i, j: (i, cga + j)),
            pl.BlockSpec((tm, tn), lambda i, j: (i, cgb + j)),
        ],
        out_specs=pl.BlockSpec((tm, tn), lambda i, j: (i, j)),
        out_shape=jax.ShapeDtypeStruct((m, d_model), BF16),
        compiler_params=_params("parallel", "parallel", vmem=VMEM_LIMIT),
        name="branch_merge",
    )(y_a, y_b, w_a, w_b, proj_x, proj_x)


def _router_kernel(h_ref, nw_ref, whi_ref, wlo_ref, b_ref, idx_ref, gate_ref, up_ref):
    x = h_ref[...]
    u = x * lax.rsqrt(jnp.mean(x * x, axis=-1, keepdims=True) + RMS_EPS) * nw_ref[...]
    u_hi = u.astype(BF16)
    u_lo = (u - u_hi.astype(F32)).astype(BF16)
    half = u.shape[1] // 2
    bits = pltpu.bitcast(u_hi.astype(F32), jnp.uint32)
    up_ref[...] = bits[:, half:] | (bits[:, :half] >> 16)
    logits = (jnp.dot(u_hi, whi_ref[...], preferred_element_type=F32)
              + jnp.dot(u_lo, whi_ref[...], preferred_element_type=F32)
              + jnp.dot(u_hi, wlo_ref[...], preferred_element_type=F32)
              + b_ref[...])
    lane = lax.broadcasted_iota(jnp.int32, logits.shape, 1)
    lane_f = lane.astype(F32)
    vals = logits
    tops, idxs = [], []
    for _ in range(TOP_K):
        mx = jnp.max(vals, axis=-1, keepdims=True)
        ix = jnp.min(jnp.where(vals == mx, lane_f, float(LANES)), axis=-1, keepdims=True).astype(jnp.int32)
        tops.append(mx)
        idxs.append(ix)
        vals = jnp.where(lane == ix, -3.0e38, vals)
    exps = [jnp.exp(tv - tops[0]) for tv in tops]
    den = exps[0] + exps[1] + exps[2] + exps[3]
    idx_out = jnp.zeros(logits.shape, jnp.int32)
    gate_out = jnp.zeros(logits.shape, F32)
    for k in range(TOP_K):
        idx_out = jnp.where(lane == k, idxs[k], idx_out)
        gate_out = jnp.where(lane == k, exps[k] / den, gate_out)
    idx_ref[...] = idx_out
    gate_ref[...] = gate_out


def router(h2, ffn_norm, router_w, router_b, tm=512):
    n, d = h2.shape
    wpad = jnp.zeros((d, LANES), F32).at[:, :N_EXPERTS].set(router_w.astype(F32))
    w_hi = wpad.astype(BF16)
    w_lo = (wpad - w_hi.astype(F32)).astype(BF16)
    bias = jnp.full((1, LANES), NEG_BIG, F32).at[0, :N_EXPERTS].set(router_b.astype(F32))
    return pl.pallas_call(
        _router_kernel,
        grid=(n // tm,),
        in_specs=[
            pl.BlockSpec((tm, d), lambda i: (i, 0)),
            pl.BlockSpec((1, d), lambda i: (0, 0)),
            pl.BlockSpec((d, LANES), lambda i: (0, 0)),
            pl.BlockSpec((d, LANES), lambda i: (0, 0)),
            pl.BlockSpec((1, LANES), lambda i: (0, 0)),
        ],
        out_specs=[pl.BlockSpec((tm, LANES), lambda i: (i, 0)),
                   pl.BlockSpec((tm, LANES), lambda i: (i, 0)),
                   pl.BlockSpec((tm, d // 2), lambda i: (i, 0))],
        out_shape=[jax.ShapeDtypeStruct((n, LANES), jnp.int32),
                   jax.ShapeDtypeStruct((n, LANES), F32),
                   jax.ShapeDtypeStruct((n, d // 2), jnp.uint32)],
        compiler_params=_params("parallel", vmem=VMEM_LIMIT),
        name="router",
    )(h2, ffn_norm.reshape(1, d).astype(F32), w_hi, w_lo, bias)


def _expert_kernel(ie_ref, isb_ref, inb_ref, ifl_ref, tok_ref, up_hbm, wgu_hbm, wd_hbm, bgu_ref,
                   bd_ref, ys_hbm, xbuf, act, ybuf, wgu_buf, wd_buf, wg_s, wu_s, wd_s, zbuf,
                   sem_x, sem_y, sem_gu, sem_d, sem_z, *, j1, j2, tn1, tn2, n_items):
    i = pl.program_id(0)
    nb = inb_ref[i]
    sb = isb_ref[i]
    ex = ie_ref[i]
    R = MOE_ROWS

    n_fill = ifl_ref[i]

    @pl.when(n_fill > 0)
    def _():
        zbuf[...] = jnp.zeros_like(zbuf)

        def z_copy(u):
            return pltpu.make_async_copy(zbuf, ys_hbm.at[pl.ds((sb + u) * R, R), :], sem_z)

        for u in range(MOE_ITEM_BLOCKS):
            @pl.when(u < n_fill)
            def _():
                z_copy(u).start()
        for u in range(MOE_ITEM_BLOCKS):
            @pl.when(u < n_fill)
            def _():
                z_copy(u).wait()

    def gu_copies(expert, jt, slot):
        return [pltpu.make_async_copy(
            wgu_hbm.at[expert, :, pl.ds(pl.multiple_of(which * j1 * tn1 + jt * tn1, tn1), tn1)],
            wgu_buf.at[slot, which], sem_gu.at[slot]) for which in range(2)]

    def d_copy(expert, jd, slot):
        return pltpu.make_async_copy(
            wd_hbm.at[expert, :, pl.ds(pl.multiple_of(jd * tn2, tn2), tn2)],
            wd_buf.at[slot], sem_d.at[slot])

    rmax = xbuf.shape[0]
    HEAD = MOE_HEAD_UNITS * R
    BODY = MOE_BODY_UNITS * R
    last_tok = tok_ref.shape[0] - 1

    def batch(rows):
        return rows // j2

    own_units = jnp.maximum(nb, MOE_HEAD_UNITS)

    def row_gather(base, r):
        tok = tok_ref[jnp.minimum(base + r, last_tok)]
        return pltpu.make_async_copy(up_hbm.at[pl.ds(tok, 1), :], xbuf.at[pl.ds(r, 1), :], sem_x)

    def x_wait_units(n):
        for u in range(MOE_ITEM_BLOCKS):
            @pl.when(u < n)
            def _():
                pltpu.make_async_copy(up_hbm.at[pl.ds(0, R), :], xbuf.at[pl.ds(u * R, R), :],
                                      sem_x).wait()

    yw = tn2 // 2

    def y_copy(slot, r, jd):
        return pltpu.make_async_copy(
            ybuf.at[slot, pl.ds(r * R, R), :],
            ys_hbm.at[pl.ds((sb + r) * R, R), pl.ds(pl.multiple_of(jd * yw, yw), yw)],
            sem_y.at[slot])

    def for_blocks(fn):
        for r in range(MOE_ITEM_BLOCKS):
            @pl.when(r < nb)
            def _():
                fn(r)

    @pl.when(i == 0)
    def _():
        def start(r, c):
            row_gather(sb * R, r).start()
            return c
        lax.fori_loop(0, rmax, start, 0, unroll=8)
        for cp in gu_copies(ex, 0, 0):
            cp.start()

    nb_prev = inb_ref[jnp.maximum(i - 1, 0)]
    x_wait_units(jnp.where(
        i == 0, MOE_ITEM_BLOCKS,
        jnp.where(nb_prev > 0, jnp.maximum(jnp.maximum(nb_prev, MOE_HEAD_UNITS), nb), 0)))

    def gate_up_block(jt, r0, rows, wg, wu):
        x32 = xbuf[pl.ds(r0, rows), :]
        x_lo = pltpu.bitcast(x32 << 16, F32).astype(BF16)
        x_hi = pltpu.bitcast(x32 & jnp.uint32(0xFFFF0000), F32).astype(BF16)
        x = jnp.concatenate([x_lo, x_hi], axis=1)
        g = jnp.dot(x, wg, preferred_element_type=F32) + bgu_ref[pl.ds(jt, 1), :]
        u = jnp.dot(x, wu, preferred_element_type=F32) + bgu_ref[pl.ds(j1 + jt, 1), :]
        gt = jnp.minimum(g, SWIGLU_LIMIT)
        up = jnp.clip(u, -SWIGLU_LIMIT, SWIGLU_LIMIT)
        a = gt * jax.nn.sigmoid(SWIGLU_ALPHA * gt) * (up + 1.0)
        act[jt, pl.ds(r0, rows), :] = a.astype(BF16)

    def down_block(jd, slot, r0, rows, wd):
        lhs = jnp.concatenate([act[c, pl.ds(r0, rows), :] for c in range(j1)], axis=1)
        y = jnp.dot(lhs, wd, preferred_element_type=F32) + bd_ref[pl.ds(jd, 1), :]
        bits = pltpu.bitcast(y.astype(BF16).astype(F32), jnp.uint32)
        ybuf[slot, pl.ds(r0, rows), :] = bits[:, yw:] | (bits[:, :yw] >> 16)

    def rest_blocks(block_fn):
        rest = jnp.maximum(nb - MOE_HEAD_UNITS, 0)
        n_body = rest // MOE_BODY_UNITS
        rem = rest % MOE_BODY_UNITS
        r_tail = HEAD + n_body * BODY
        g_tail = batch(HEAD) + n_body * batch(BODY)

        def body(p, c):
            block_fn(pl.multiple_of(HEAD + p * BODY, R), BODY, batch(HEAD) + p * batch(BODY))
            return c

        lax.fori_loop(0, n_body, body, 0)

        @pl.when(rem >= 2)
        def _():
            block_fn(pl.multiple_of(r_tail, R), 2 * R, g_tail)

        @pl.when(rem % 2 == 1)
        def _():
            two = (rem // 2) * 2
            block_fn(pl.multiple_of(r_tail + two * R, R), R, g_tail + (rem // 2) * batch(2 * R))

    nxt = jnp.minimum(i + 1, n_items - 1)
    next_base = isb_ref[nxt] * R
    fetch_units = jnp.maximum(own_units, inb_ref[nxt])
    rows_per_tile = fetch_units * batch(R)

    n_wd = wd_buf.shape[0]

    @pl.when(nb > 0)
    def _():
        for t0 in range(n_wd - 1):
            d_copy(ex, t0, t0).start()

        def gate_up_tile(jt, carry):
            slot = jt % 2

            @pl.when(jt + 1 < j1)
            def _():
                for cp in gu_copies(ex, jt + 1, 1 - slot):
                    cp.start()

            for cp in gu_copies(ex, jt, slot):
                cp.wait()
            wg = wgu_buf[slot, 0].astype(BF16)
            wu = wgu_buf[slot, 1].astype(BF16)
            wg_s[...] = wg
            wu_s[...] = wu
            gate_up_block(jt, 0, HEAD, wg, wu)
            rest_blocks(lambda r0, rows, g0: gate_up_block(jt, r0, rows, wg_s[...], wu_s[...]))
            return carry

        lax.fori_loop(0, j1, gate_up_tile, 0)

        @pl.when(i + 1 < n_items)
        def _():
            @pl.when(inb_ref[i + 1] > 0)
            def _():
                for cp in gu_copies(ie_ref[i + 1], 0, 0):
                    cp.start()

        def down_tile(jd, carry):
            slot = jd % 2
            wslot = jd % n_wd

            @pl.when(jd + n_wd - 1 < j2)
            def _():
                d_copy(ex, jd + n_wd - 1, (jd + n_wd - 1) % n_wd).start()

            @pl.when(jd >= 2)
            def _():
                for_blocks(lambda r: y_copy(slot, r, jd).wait())

            d_copy(ex, jd, wslot).wait()
            wd = wd_buf[wslot].astype(BF16)
            wd_s[...] = wd
            down_block(jd, slot, 0, HEAD, wd)
            tile_row0 = jd * rows_per_tile

            def gather_batch(first, count):
                for rr in range(count):
                    row_gather(next_base, tile_row0 + first + rr).start()

            gather_batch(0, batch(HEAD))

            def later_block(r0, rows, g0):
                down_block(jd, slot, r0, rows, wd_s[...])
                gather_batch(g0, batch(rows))

            rest_blocks(later_block)

            def make_up(r, c):
                row_gather(next_base, tile_row0 + r).start()
                return c

            lax.fori_loop(own_units * batch(R), rows_per_tile, make_up, 0)
            for_blocks(lambda r: y_copy(slot, r, jd).start())
            return carry

        lax.fori_loop(0, j2, down_tile, 0)
        for_blocks(lambda r: y_copy(0, r, 0).wait())
        for_blocks(lambda r: y_copy(1, r, 0).wait())

        @pl.when(i == n_items - 1)
        def _():
            x_wait_units(fetch_units)


def expert_mlp(u_packed, row_tok, item_e, item_sb, item_nb, item_fill, w_gate_up, b_gate_up, w_down,
               b_down, tn1=256, tn2=MOE_DOWN_TILE):
    p = row_tok.shape[0]
    d = 2 * u_packed.shape[1]
    n_items = item_e.shape[0]
    de = w_down.shape[1]
    j1 = de // tn1
    j2 = d // tn2
    rmax = MOE_ITEM_BLOCKS * MOE_ROWS

    assert MOE_ROWS % j2 == 0 and MOE_BODY_UNITS == 4 and MOE_DOWN_SLOTS <= j2

    def bias_map(i, ie, isb, inb, ifl, tok):
        return (ie[i], 0, 0)

    return pl.pallas_call(
        functools.partial(_expert_kernel, j1=j1, j2=j2, tn1=tn1, tn2=tn2, n_items=n_items),
        grid_spec=pltpu.PrefetchScalarGridSpec(
            num_scalar_prefetch=5,
            grid=(n_items,),
            in_specs=[
                pl.BlockSpec(memory_space=pl.ANY),
                pl.BlockSpec(memory_space=pl.ANY),
                pl.BlockSpec(memory_space=pl.ANY),
                pl.BlockSpec((None, 2 * j1, tn1), bias_map),
                pl.BlockSpec((None, j2, tn2), bias_map),
            ],
            out_specs=pl.BlockSpec(memory_space=pl.ANY),
            scratch_shapes=[
                pltpu.VMEM((rmax, d // 2), jnp.uint32),
                pltpu.VMEM((j1, rmax, tn1), BF16),
                pltpu.VMEM((2, rmax, tn2 // 2), jnp.uint32),
                pltpu.VMEM((2, 2, d, tn1), F32),
                pltpu.VMEM((MOE_DOWN_SLOTS, de, tn2), F32),
                pltpu.VMEM((d, tn1), BF16),
                pltpu.VMEM((d, tn1), BF16),
                pltpu.VMEM((de, tn2), BF16),
                pltpu.VMEM((MOE_ROWS, d // 2), jnp.uint32),
                pltpu.SemaphoreType.DMA(()),
                pltpu.SemaphoreType.DMA((2,)),
                pltpu.SemaphoreType.DMA((2,)),
                pltpu.SemaphoreType.DMA((MOE_DOWN_SLOTS,)),
                pltpu.SemaphoreType.DMA(()),
            ],
        ),
        out_shape=jax.ShapeDtypeStruct((p, d // 2), jnp.uint32),
        compiler_params=_params("arbitrary", vmem=VMEM_LIMIT),
        name="expert_mlp",
    )(item_e, item_sb, item_nb, item_fill, row_tok, u_packed, w_gate_up, w_down,
      b_gate_up.reshape(N_EXPERTS, 2 * j1, tn1), b_down.reshape(N_EXPERTS, j2, tn2))


def _combine_kernel(dest_ref, h_ref, g_ref, ys_hbm, o_ref, buf, sem, *, tt, n_tok, n_steps, yw):
    i = pl.program_id(0)

    def issue(blk, slot):
        for k in range(TOP_K):
            def start(t, c):
                src = dest_ref[k * n_tok + blk * tt + t]
                pltpu.make_async_copy(ys_hbm.at[pl.ds(src, 1), :], buf.at[slot, k, pl.ds(t, 1), :],
                                      sem.at[slot]).start()
                return c
            lax.fori_loop(0, tt, start, 0, unroll=8)

    @pl.when(i == 0)
    def _():
        issue(0, 0)

    slot = i % 2

    @pl.when(i + 1 < n_steps)
    def _():
        issue(i + 1, 1 - slot)

    for k in range(TOP_K):
        pltpu.make_async_copy(ys_hbm.at[pl.ds(0, tt), :], buf.at[slot, k], sem.at[slot]).wait()
    g = g_ref[...]
    lo = None
    hi = None
    for k in range(TOP_K):
        word = buf[slot, k]
        gk = g[:, k:k + 1]
        lo_k = gk * pltpu.bitcast(word << 16, F32)
        hi_k = gk * pltpu.bitcast(word & jnp.uint32(0xFFFF0000), F32)
        lo = lo_k if lo is None else lo + lo_k
        hi = hi_k if hi is None else hi + hi_k
    for c in range(lo.shape[1] // yw):
        o_ref[:, 2 * c * yw:(2 * c + 1) * yw] = (h_ref[:, 2 * c * yw:(2 * c + 1) * yw]
                                                  + lo[:, c * yw:(c + 1) * yw])
        o_ref[:, (2 * c + 1) * yw:(2 * c + 2) * yw] = (h_ref[:, (2 * c + 1) * yw:(2 * c + 2) * yw]
                                                        + hi[:, c * yw:(c + 1) * yw])


def combine(h2, gates, ys, dest_km, yw, tt=256):
    n, d = h2.shape
    return pl.pallas_call(
        functools.partial(_combine_kernel, tt=tt, n_tok=n, n_steps=n // tt, yw=yw),
        grid_spec=pltpu.PrefetchScalarGridSpec(
            num_scalar_prefetch=1,
            grid=(n // tt,),
            in_specs=[pl.BlockSpec((tt, d), lambda i, dst: (i, 0)),
                      pl.BlockSpec((tt, LANES), lambda i, dst: (i, 0)),
                      pl.BlockSpec(memory_space=pl.ANY)],
            out_specs=pl.BlockSpec((tt, d), lambda i, dst: (i, 0)),
            scratch_shapes=[pltpu.VMEM((2, TOP_K, tt, d // 2), jnp.uint32),
                            pltpu.SemaphoreType.DMA((2,))],
        ),
        out_shape=jax.ShapeDtypeStruct((n, d), F32),
        compiler_params=_params("arbitrary", vmem=VMEM_LIMIT),
        name="moe_combine",
    )(dest_km, h2, gates, ys)


def routing_tables(top_idx, n_tok):
    a = n_tok * TOP_K
    e_flat = top_idx.reshape(-1)
    onehot = (e_flat[:, None] == jnp.arange(N_EXPERTS, dtype=jnp.int32)[None, :]).astype(jnp.int32)
    csum = jnp.cumsum(onehot, axis=0)
    rank = jnp.sum(csum * onehot, axis=1) - 1
    counts = csum[-1]
    nblk = (counts + MOE_ROWS - 1) // MOE_ROWS
    blk_end = jnp.cumsum(nblk)
    blk_start = blk_end - nblk
    dest = blk_start[e_flat] * MOE_ROWS + rank
    n_blocks = -(-(a + N_EXPERTS * (MOE_ROWS - 1)) // MOE_ROWS)
    p = n_blocks * MOE_ROWS
    t_flat = jnp.arange(a, dtype=jnp.int32) // TOP_K
    row_tok = jnp.zeros((p,), jnp.int32).at[dest].set(t_flat, unique_indices=True,
                                                      mode='promise_in_bounds')

    n_items = N_EXPERTS + n_blocks // MOE_ITEM_BLOCKS
    items_per_e = (nblk + MOE_ITEM_BLOCKS - 1) // MOE_ITEM_BLOCKS
    item_end = jnp.cumsum(items_per_e)
    item_start = item_end - items_per_e
    total_items = item_end[-1]
    ii = jnp.arange(n_items, dtype=jnp.int32)
    ic = jnp.minimum(ii, total_items - 1)
    e_i = jnp.minimum(jnp.searchsorted(item_end, ic, side='right'), N_EXPERTS - 1).astype(jnp.int32)
    local = ic - item_start[e_i]
    item_sb = (blk_start[e_i] + local * MOE_ITEM_BLOCKS).astype(jnp.int32)
    item_nb = jnp.clip(nblk[e_i] - local * MOE_ITEM_BLOCKS, 0, MOE_ITEM_BLOCKS)
    active = ii < total_items
    item_nb = jnp.where(active, item_nb, 0).astype(jnp.int32)
    fill_start = blk_end[-1] + (ii - total_items) * MOE_ITEM_BLOCKS
    item_fill = jnp.where(active, 0, jnp.clip(n_blocks - fill_start, 0, MOE_ITEM_BLOCKS)).astype(jnp.int32)
    item_sb = jnp.where(active, item_sb, jnp.minimum(fill_start, n_blocks - 1)).astype(jnp.int32)
    dest_km = dest.astype(jnp.int32).reshape(n_tok, TOP_K).T.reshape(-1)
    return dest_km, row_tok, e_i, item_sb, item_nb, item_fill


def kernel(x, meta_tokens, mix_norm, w_in, dn_conv, dn_a_log, dn_dt_bias, dn_out_norm, da_q_norm, da_k_norm, da_lam_q1, da_lam_k1, da_lam_q2, da_lam_k2, da_sub_norm, w_branch_a, w_branch_b, w_out, ffn_norm, router_w, router_b, w_gate_up, b_gate_up, w_down, b_down):
    bsz, seq, d = x.shape
    n_tok = bsz * seq
    x2 = x.reshape(n_tok, d)
    layer = 0

    w_bf = w_in[layer].astype(BF16)
    da_shift = 2 * DN_HEADS
    da_cols = w_bf.shape[1] - COL_BA - da_shift
    tn = 1024

    u_x = rms_cast(x2, mix_norm[layer], 512)
    u_m = rms_cast(meta_tokens.astype(F32), mix_norm[layer], N_META)
    proj_dn = matmul_fullk(u_x, w_bf, BF16, 1024, tn, name="in_proj_dn", n_blocks=COL_BA // tn)
    proj_da = matmul_fullk_shift(u_x, w_bf, BF16, 1024, tn, COL_BA, da_shift, da_cols // tn,
                                 name="in_proj_da")
    ba_x = matmul_fullk(u_x, w_bf, F32, 1024, LANES, name="in_proj_ba",
                        n_offset=COL_BA // LANES, n_blocks=1)
    proj_dn_m = matmul_fullk(u_m, w_bf, BF16, N_META, tn, name="in_proj_dn_meta", n_blocks=COL_DZ // tn)
    proj_kv_m = matmul_fullk_shift(u_m, w_bf, BF16, N_META, tn, COL_BA, da_shift,
                                   (COL_GA - COL_AK) // tn, name="in_proj_kv_meta",
                                   n_offset=COL_AK // tn)
    ba_m = matmul_fullk(u_m, w_bf, F32, N_META, LANES, name="in_proj_ba_meta",
                        n_offset=COL_BA // LANES, n_blocks=1)

    y_a = gated_deltanet(proj_dn, ba_x, proj_dn_m, ba_m, dn_conv[layer], dn_a_log[layer],
                         dn_dt_bias[layer], dn_out_norm[layer], bsz, seq)
    lam4 = jnp.stack([da_lam_q1[layer], da_lam_k1[layer], da_lam_q2[layer], da_lam_k2[layer]])
    y_b = diff_attention(proj_da, proj_kv_m, da_q_norm[layer], da_k_norm[layer], lam4,
                         da_sub_norm[layer], bsz, seq)

    merged = branch_merge(y_a, y_b, w_branch_a[layer], w_branch_b[layer], proj_da, d)
    h2 = matmul_fullk(merged, w_out[layer], F32, 1024, 512, res=x2, name="out_proj")

    top_idx, gates, u_packed = router(h2, ffn_norm[layer], router_w[layer], router_b[layer])
    dest, row_tok, item_e, item_sb, item_nb, item_fill = routing_tables(top_idx[:, :TOP_K], n_tok)
    ys = expert_mlp(u_packed, row_tok, item_e, item_sb, item_nb, item_fill, w_gate_up[layer],
                    b_gate_up[layer], w_down[layer], b_down[layer])
    out = combine(h2, gates, ys, dest, MOE_DOWN_TILE // 2)
    return out.reshape(bsz, seq, d)
```

```python
import functools
import math

import jax
import jax.numpy as jnp
from jax import lax
from jax.experimental import pallas as pl
from jax.experimental.pallas import tpu as pltpu

F32 = jnp.float32
BF16 = jnp.bfloat16

N_META = 16
RMS_EPS = 1e-6
L2_EPS = 1e-6

DN_HEADS = 16
DN_DK = 128
DN_DV = 128
DN_CONV = 4
DN_CHUNK = 64
DN_W = DN_HEADS * DN_DK
DN_HIST = 16
DN_CONV_COLS = 1024

DA_HEADS = 8
DA_DK = 128
DA_DV = 256
LAMBDA_INIT = 0.8 - 0.6 * math.exp(-0.3 * 0)

N_EXPERTS = 32
TOP_K = 4
D_EXPERT = 1536
SWIGLU_LIMIT = 7.0
SWIGLU_ALPHA = 1.702

LANES = 128
MOE_ROWS = 128
MOE_ITEM_BLOCKS = 10
MOE_HEAD_UNITS = 4
MOE_BODY_UNITS = 4
MOE_DOWN_SLOTS = 3
WEIGHT_DMA_PRIORITY = 1
MOE_DOWN_TILE = 512
NEG_BIG = -1e30
VMEM_LIMIT = 56 * 1024 * 1024

COL_DZ = 3 * DN_W
COL_BA = 4 * DN_W
COL_AQ = 0
COL_AK = COL_AQ + DA_HEADS * 2 * DA_DK
COL_AV = COL_AK + DA_HEADS * 2 * DA_DK
COL_GA = COL_AV + DA_HEADS * DA_DV


def _params(*sem, vmem=None):
    return pltpu.CompilerParams(dimension_semantics=sem, vmem_limit_bytes=vmem)


def _rms_cast_kernel(x_ref, w_ref, o_ref):
    x = x_ref[...]
    y = x * lax.rsqrt(jnp.mean(x * x, axis=-1, keepdims=True) + RMS_EPS) * w_ref[...]
    o_ref[...] = y.astype(o_ref.dtype)


def rms_cast(x, w, tm):
    m, d = x.shape
    return pl.pallas_call(
        _rms_cast_kernel,
        grid=(m // tm,),
        in_specs=[pl.BlockSpec((tm, d), lambda i: (i, 0)),
                  pl.BlockSpec((1, d), lambda i: (0, 0))],
        out_specs=pl.BlockSpec((tm, d), lambda i: (i, 0)),
        out_shape=jax.ShapeDtypeStruct((m, d), BF16),
        compiler_params=_params("parallel"),
        name="rms_cast",
    )(x, w.reshape(1, d))


def _mm_fullk_kernel(a_ref, b_ref, *rest, has_res):
    acc = jnp.dot(a_ref[...], b_ref[...].astype(BF16), preferred_element_type=F32)
    if has_res:
        r_ref, o_ref = rest
        acc = acc + r_ref[...]
    else:
        o_ref, = rest
    o_ref[...] = acc.astype(o_ref.dtype)


def matmul_fullk(a, b, out_dtype, tm, tn, res=None, name="matmul", n_offset=0, n_blocks=None):
    m, kd = a.shape
    if n_blocks is None:
        n_blocks = b.shape[1] // tn
    in_specs = [pl.BlockSpec((tm, kd), lambda i, j: (i, 0)),
                pl.BlockSpec((kd, tn), lambda i, j: (0, n_offset + j))]
    args = [a, b]
    if res is not None:
        in_specs.append(pl.BlockSpec((tm, tn), lambda i, j: (i, j)))
        args.append(res)
    return pl.pallas_call(
        functools.partial(_mm_fullk_kernel, has_res=res is not None),
        grid=(m // tm, n_blocks),
        in_specs=in_specs,
        out_specs=pl.BlockSpec((tm, tn), lambda i, j: (i, j)),
        out_shape=jax.ShapeDtypeStruct((m, n_blocks * tn), out_dtype),
        compiler_params=_params("parallel", "parallel", vmem=VMEM_LIMIT),
        name=name,
    )(*args)


def _mm_shift_kernel(a_ref, b0_ref, b1_ref, o_ref, *, shift):
    b = jnp.concatenate([b0_ref[:, shift:], b1_ref[:, :shift]], axis=1)
    o_ref[...] = jnp.dot(a_ref[...], b, preferred_element_type=F32).astype(o_ref.dtype)


def matmul_fullk_shift(a, b, out_dtype, tm, tn, col0, shift, n_blocks, name, n_offset=0):
    m, kd = a.shape
    first = col0 // tn + n_offset
    return pl.pallas_call(
        functools.partial(_mm_shift_kernel, shift=shift),
        grid=(m // tm, n_blocks),
        in_specs=[pl.BlockSpec((tm, kd), lambda i, j: (i, 0)),
                  pl.BlockSpec((kd, tn), lambda i, j: (0, first + j)),
                  pl.BlockSpec((kd, LANES), lambda i, j: (0, (first + j + 1) * (tn // LANES)))],
        out_specs=pl.BlockSpec((tm, tn), lambda i, j: (i, j)),
        out_shape=jax.ShapeDtypeStruct((m, n_blocks * tn), out_dtype),
        compiler_params=_params("parallel", "parallel", vmem=VMEM_LIMIT),
        name=name,
    )(a, b, b)


def _deltanet_kernel(qkv_ref, z_ref, ba_ref, qkvm_ref, bam_ref, cw_ref, hp_ref, nw_ref,
                     o_ref, full_ref, act_ref, ba_s, s_ref):
    t = pl.program_id(1)
    C = DN_CHUNK
    HIST = DN_HIST
    n_pad = C - N_META

    @pl.when(t == 0)
    def _():
        s_ref[...] = jnp.zeros_like(s_ref)
        full_ref[0:HIST + n_pad, :] = jnp.zeros((HIST + n_pad, 3 * DN_W), BF16)
        full_ref[HIST + n_pad:HIST + C, :] = qkvm_ref[...]
        ba_s[0:n_pad, :] = jnp.zeros((n_pad, LANES), F32)
        ba_s[n_pad:C, :] = bam_ref[...]

    @pl.when(t > 0)
    def _():
        full_ref[HIST:HIST + C, :] = qkv_ref[...]
        ba_s[...] = ba_ref[...]

    n_shift = DN_CONV - 1
    sr = lax.broadcasted_iota(jnp.int32, (n_shift * C, HIST + C), 0)
    sc = lax.broadcasted_iota(jnp.int32, (n_shift * C, HIST + C), 1)
    shift = jnp.where(sc == HIST - n_shift + (sr % C) + (sr // C), 1.0, 0.0).astype(BF16)
    for c0 in range(0, 3 * DN_W, DN_CONV_COLS):
        cs = slice(c0, c0 + DN_CONV_COLS)
        delayed = jnp.dot(shift, full_ref[:, cs], preferred_element_type=F32)
        conv = cw_ref[n_shift:n_shift + 1, cs] * full_ref[HIST:HIST + C, cs].astype(F32)
        for j in range(n_shift):
            conv = conv + cw_ref[j:j + 1, cs] * delayed[j * C:(j + 1) * C, :]
        act_ref[:, cs] = conv * jax.nn.sigmoid(conv)
    full_ref[0:HIST, :] = full_ref[C:C + HIST, :]

    row1 = lax.broadcasted_iota(jnp.int32, (C, 1), 0)
    valid = jnp.where((t > 0) | (row1 >= n_pad), 1.0, 0.0).astype(F32)
    ba = ba_s[...]
    beta_all = jax.nn.sigmoid(ba) * valid
    xg = ba + hp_ref[1:2, :]
    softplus = jnp.maximum(xg, 0.0) + jnp.log1p(jnp.exp(-jnp.abs(xg)))
    g_all = -jnp.exp(hp_ref[0:1, :]) * softplus * valid

    row = lax.broadcasted_iota(jnp.int32, (C, C), 0)
    col = lax.broadcasted_iota(jnp.int32, (C, C), 1)
    causal = row >= col
    strict = row > col
    eye = jnp.where(row == col, 1.0, 0.0).astype(F32)
    tril = jnp.where(causal, 1.0, 0.0).astype(BF16)

    g_hi = g_all.astype(BF16)
    r1 = g_all - g_hi.astype(F32)
    g_mid = r1.astype(BF16)
    g_lo = (r1 - g_mid.astype(F32)).astype(BF16)
    g3 = jnp.dot(tril, jnp.concatenate([g_hi, g_mid, g_lo], axis=1), preferred_element_type=F32)
    gcum = g3[:, 0:LANES] + g3[:, LANES:2 * LANES] + g3[:, 2 * LANES:3 * LANES]
    gcum_t = gcum.T

    scale = DN_DK ** -0.5
    nw = nw_ref[...]
    heads = range(DN_HEADS)

    def mm(a, b):
        return jnp.dot(a.astype(BF16), b.astype(BF16), preferred_element_type=F32)

    kn_b, lhs1, decay, rhs, qe_b, kdec_t, e_last = [], [], [], [], [], [], []
    for h in heads:
        qh = act_ref[:, h * DN_DK:(h + 1) * DN_DK]
        kh = act_ref[:, DN_W + h * DN_DK:DN_W + (h + 1) * DN_DK]
        vh = act_ref[:, 2 * DN_W + h * DN_DV:2 * DN_W + (h + 1) * DN_DV]
        qn = qh * (lax.rsqrt(jnp.sum(qh * qh, axis=-1, keepdims=True) + L2_EPS) * scale)
        kn = kh * lax.rsqrt(jnp.sum(kh * kh, axis=-1, keepdims=True) + L2_EPS)
        beta = beta_all[:, h:h + 1]
        gc = gcum[:, DN_HEADS + h:DN_HEADS + h + 1]
        gr = gcum_t[DN_HEADS + h:DN_HEADS + h + 1, :]
        g_last = gc[C - 1:C, :]
        e_g = jnp.exp(gc)
        kb = kn * beta
        kn_b.append(kn.astype(BF16))
        lhs1.append(jnp.concatenate([kb, qn], axis=0).astype(BF16))
        decay.append(jnp.exp(jnp.where(causal, gc - gr, NEG_BIG)))
        rhs.append(jnp.concatenate([vh * beta, kb * e_g], axis=1).astype(BF16))
        qe_b.append((qn * e_g).astype(BF16))
        kdec_t.append((kn * jnp.exp(g_last - gc)).T.astype(BF16))
        e_last.append(jnp.exp(g_last))

    kq = [lax.dot_general(lhs1[h], kn_b[h], (((1,), (1,)), ((), ())), preferred_element_type=F32)
          for h in heads]
    qk_b = [jnp.where(causal, kq[h][C:2 * C, :] * decay[h], 0.0).astype(BF16) for h in heads]
    b1 = [-jnp.where(strict, kq[h][0:C, :] * decay[h], 0.0) for h in heads]
    b2 = [mm(b1[h], b1[h]) for h in heads]
    b4 = [mm(b2[h], b2[h]) for h in heads]
    p1 = [(eye + b1[h]) + mm(eye + b1[h], b2[h]) for h in heads]
    b8 = [mm(b4[h], b4[h]) for h in heads]
    b16 = [mm(b8[h], b8[h]) for h in heads]
    p2 = [(eye + b4[h]) + mm(eye + b4[h], b8[h]) for h in heads]
    b32 = [mm(b16[h], b16[h]) for h in heads]
    p12 = [mm(p1[h], p2[h]) for h in heads]
    p3 = [(eye + b16[h]) + mm(eye + b16[h], b32[h]) for h in heads]
    inv = [mm(p12[h], p3[h]) for h in heads]
    sol = [mm(inv[h], rhs[h]) for h in heads]

    s_old = [s_ref[h] for h in heads]
    ws_lhs = [jnp.concatenate([sol[h][:, DN_DV:2 * DN_DV].astype(BF16), qe_b[h]], axis=0) for h in heads]
    ws = [mm(ws_lhs[h], s_old[h]) for h in heads]
    v_new = [(sol[h][:, 0:DN_DV] - ws[h][0:C, :]).astype(BF16) for h in heads]
    ov_lhs = [jnp.concatenate([qk_b[h], kdec_t[h]], axis=0) for h in heads]
    ov = [mm(ov_lhs[h], v_new[h]) for h in heads]
    for h in heads:
        s_ref[h] = s_old[h] * e_last[h] + ov[h][C:C + DN_DK, :]
        o = ws[h][C:2 * C, :] + ov[h][0:C, :]
        zh = z_ref[:, h * DN_DV:(h + 1) * DN_DV].astype(F32)
        o = o * lax.rsqrt(jnp.mean(o * o, axis=-1, keepdims=True) + RMS_EPS) * nw
        o_ref[:, h * DN_DV:(h + 1) * DN_DV] = (o * (zh * jax.nn.sigmoid(zh))).astype(o_ref.dtype)


def gated_deltanet(proj_x, ba_x, proj_m, ba_m, conv_w, a_log, dt_bias, norm_w, bsz, seq):
    C = DN_CHUNK
    n_chunks = seq // C
    w3 = 3 * DN_W
    hp = jnp.zeros((2, LANES), F32)
    hp = hp.at[0, DN_HEADS:2 * DN_HEADS].set(a_log.astype(F32))
    hp = hp.at[1, DN_HEADS:2 * DN_HEADS].set(dt_bias.astype(F32))

    def xrow(b, t):
        return b * n_chunks + jnp.maximum(t - 1, 0)

    return pl.pallas_call(
        _deltanet_kernel,
        grid=(bsz, n_chunks + 1),
        in_specs=[
            pl.BlockSpec((C, w3), lambda b, t: (xrow(b, t), 0)),
            pl.BlockSpec((C, DN_W), lambda b, t: (xrow(b, t), COL_DZ // DN_W)),
            pl.BlockSpec((C, LANES), lambda b, t: (xrow(b, t), 0)),
            pl.BlockSpec((N_META, w3), lambda b, t: (0, 0)),
            pl.BlockSpec((N_META, LANES), lambda b, t: (0, 0)),
            pl.BlockSpec((DN_CONV, w3), lambda b, t: (0, 0)),
            pl.BlockSpec((2, LANES), lambda b, t: (0, 0)),
            pl.BlockSpec((1, DN_DV), lambda b, t: (0, 0)),
        ],
        out_specs=pl.BlockSpec((C, DN_W), lambda b, t: (xrow(b, t), 0)),
        out_shape=jax.ShapeDtypeStruct((bsz * seq, DN_W), BF16),
        scratch_shapes=[
            pltpu.VMEM((DN_HIST + C, w3), BF16),
            pltpu.VMEM((C, w3), F32),
            pltpu.VMEM((C, LANES), F32),
            pltpu.VMEM((DN_HEADS, DN_DK, DN_DV), F32),
        ],
        compiler_params=_params("parallel", "arbitrary", vmem=VMEM_LIMIT),
        name="gated_deltanet",
    )(proj_x, proj_x, ba_x, proj_m, ba_m, conv_w.astype(F32), hp, norm_w.reshape(1, DN_DV).astype(F32))


def _diff_attn_kernel(q_ref, k_ref, v_ref, km_ref, vm_ref, qn_ref, kn_ref, lam_ref, sn_ref,
                      o_ref, kx_s, kmeta_s, *, tq, seq):
    qi = pl.program_id(2)

    def rms(x, w):
        return x * lax.rsqrt(jnp.mean(x * x, axis=-1, keepdims=True) + RMS_EPS) * w

    @pl.when(qi == 0)
    def _():
        for m in range(2):
            kx = k_ref[:, m * DA_DK:(m + 1) * DA_DK].astype(F32)
            kx_s[m] = rms(kx, kn_ref[...]).astype(BF16)
            kmx = km_ref[:, m * DA_DK:(m + 1) * DA_DK].astype(F32)
            kmeta_s[m] = rms(kmx, kn_ref[...]).astype(BF16)

    lam = (jnp.exp(jnp.sum(lam_ref[0:1, :] * lam_ref[1:2, :], axis=-1, keepdims=True))
           - jnp.exp(jnp.sum(lam_ref[2:3, :] * lam_ref[3:4, :], axis=-1, keepdims=True))
           + LAMBDA_INIT)
    row = lax.broadcasted_iota(jnp.int32, (tq, tq), 0)
    col = lax.broadcasted_iota(jnp.int32, (tq, tq), 1)
    diag_visible = col <= row
    v_m = vm_ref[...]
    scale = DA_DK ** -0.5 * math.log2(math.e)
    nt = (((1,), (1,)), ((), ()))

    def attend(n_prev):
        maps = range(2)
        qn = [(rms(q_ref[:, m * DA_DK:(m + 1) * DA_DK].astype(F32), qn_ref[...]) * scale).astype(BF16)
              for m in maps]
        sd = [jnp.where(diag_visible,
                        lax.dot_general(qn[m], kx_s[m, n_prev:n_prev + tq, :], nt,
                                        preferred_element_type=F32), NEG_BIG) for m in maps]
        sm = [lax.dot_general(qn[m], kmeta_s[m], nt, preferred_element_type=F32) for m in maps]
        mx = [jnp.maximum(jnp.max(sd[m], axis=-1, keepdims=True),
                          jnp.max(sm[m], axis=-1, keepdims=True)) for m in maps]
        if n_prev:
            sp = [lax.dot_general(qn[m], kx_s[m, 0:n_prev, :], nt, preferred_element_type=F32)
                  for m in maps]
            mx = [jnp.maximum(mx[m], jnp.max(sp[m], axis=-1, keepdims=True)) for m in maps]
        pd = [jnp.exp2(sd[m] - mx[m]) for m in maps]
        pm = [jnp.exp2(sm[m] - mx[m]) for m in maps]
        den = [jnp.sum(pd[m], axis=-1, keepdims=True) + jnp.sum(pm[m], axis=-1, keepdims=True)
               for m in maps]
        acc = [jnp.dot(pd[m].astype(BF16), v_ref[n_prev:n_prev + tq, :], preferred_element_type=F32)
               + jnp.dot(pm[m].astype(BF16), v_m, preferred_element_type=F32) for m in maps]
        if n_prev:
            pp = [jnp.exp2(sp[m] - mx[m]) for m in maps]
            den = [den[m] + jnp.sum(pp[m], axis=-1, keepdims=True) for m in maps]
            acc = [acc[m] + jnp.dot(pp[m].astype(BF16), v_ref[0:n_prev, :], preferred_element_type=F32)
                   for m in maps]
        out = acc[0] / den[0] - lam * (acc[1] / den[1])
        out = rms(out, sn_ref[...]) * (1.0 - LAMBDA_INIT)
        o_ref[...] = out.astype(o_ref.dtype)

    for blk in range(seq // tq):
        @pl.when(qi == blk)
        def _():
            attend(blk * tq)


def diff_attention(proj_x, proj_m, q_norm, k_norm, lam4, sub_norm, bsz, seq, tq=512):
    nq = seq // tq
    cq = COL_AQ // DA_DV
    ck = COL_AK // DA_DV
    cv = COL_AV // DA_DV
    return pl.pallas_call(
        functools.partial(_diff_attn_kernel, tq=tq, seq=seq),
        grid=(bsz, DA_HEADS, nq),
        in_specs=[
            pl.BlockSpec((tq, DA_DV), lambda b, h, i: (b * nq + i, cq + h)),
            pl.BlockSpec((seq, DA_DV), lambda b, h, i: (b, ck + h)),
            pl.BlockSpec((seq, DA_DV), lambda b, h, i: (b, cv + h)),
            pl.BlockSpec((N_META, DA_DV), lambda b, h, i: (0, h)),
            pl.BlockSpec((N_META, DA_DV), lambda b, h, i: (0, DA_HEADS + h)),
            pl.BlockSpec((1, DA_DK), lambda b, h, i: (0, 0)),
            pl.BlockSpec((1, DA_DK), lambda b, h, i: (0, 0)),
            pl.BlockSpec((4, DA_DK), lambda b, h, i: (0, 0)),
            pl.BlockSpec((1, DA_DV), lambda b, h, i: (0, 0)),
        ],
        out_specs=pl.BlockSpec((tq, DA_DV), lambda b, h, i: (b * nq + i, h)),
        out_shape=jax.ShapeDtypeStruct((bsz * seq, DA_HEADS * DA_DV), BF16),
        scratch_shapes=[pltpu.VMEM((2, seq, DA_DK), BF16),
                        pltpu.VMEM((2, N_META, DA_DK), BF16)],
        compiler_params=_params("parallel", "parallel", "arbitrary", vmem=VMEM_LIMIT),
        name="diff_attention",
    )(proj_x, proj_x, proj_x, proj_m, proj_m,
      q_norm.reshape(1, DA_DK).astype(F32), k_norm.reshape(1, DA_DK).astype(F32),
      lam4.astype(F32), sub_norm.reshape(1, DA_DV).astype(F32))


def _merge_kernel(ya_ref, yb_ref, wa_ref, wb_ref, ga_ref, gb_ref, o_ref):
    pa = jnp.dot(ya_ref[...], wa_ref[...].astype(BF16), preferred_element_type=F32)
    pb = jnp.dot(yb_ref[...], wb_ref[...].astype(BF16), preferred_element_type=F32)
    ga = jax.nn.sigmoid(ga_ref[...].astype(F32))
    gb = jax.nn.sigmoid(gb_ref[...].astype(F32))
    o_ref[...] = (ga * pa + gb * pb).astype(o_ref.dtype)


def branch_merge(y_a, y_b, w_a, w_b, proj_x, d_model, tm=1024, tn=512):
    m, ka = y_a.shape
    kb = y_b.shape[1]
    cga = COL_GA // tn
    cgb = (COL_GA + d_model) // tn
    return pl.pallas_call(
        _merge_kernel,
        grid=(m // tm, d_model // tn),
        in_specs=[
            pl.BlockSpec((tm, ka), lambda i, j: (i, 0)),
            pl.BlockSpec((tm, kb), lambda i, j: (i, 0)),
            pl.BlockSpec((ka, tn), lambda i, j: (0, j)),
            pl.BlockSpec((kb, tn), lambda i, j: (0, j)),
            pl.BlockSpec((tm, tn), lambda i, j: (i, cga + j)),
            pl.BlockSpec((tm, tn), lambda i, j: (i, cgb + j)),
        ],
        out_specs=pl.BlockSpec((tm, tn), lambda i, j: (i, j)),
        out_shape=jax.ShapeDtypeStruct((m, d_model), BF16),
        compiler_params=_params("parallel", "parallel", vmem=VMEM_LIMIT),
        name="branch_merge",
    )(y_a, y_b, w_a, w_b, proj_x, proj_x)


def _router_kernel(h_ref, nw_ref, whi_ref, wlo_ref, b_ref, idx_ref, gate_ref, up_ref):
    x = h_ref[...]
    u = x * lax.rsqrt(jnp.mean(x * x, axis=-1, keepdims=True) + RMS_EPS) * nw_ref[...]
    u_hi = u.astype(BF16)
    u_lo = (u - u_hi.astype(F32)).astype(BF16)
    half = u.shape[1] // 2
    bits = pltpu.bitcast(u_hi.astype(F32), jnp.uint32)
    up_ref[...] = bits[:, half:] | (bits[:, :half] >> 16)
    logits = (jnp.dot(u_hi, whi_ref[...], preferred_element_type=F32)
              + jnp.dot(u_lo, whi_ref[...], preferred_element_type=F32)
              + jnp.dot(u_hi, wlo_ref[...], preferred_element_type=F32)
              + b_ref[...])
    lane = lax.broadcasted_iota(jnp.int32, logits.shape, 1)
    lane_f = lane.astype(F32)
    vals = logits
    tops, idxs = [], []
    for _ in range(TOP_K):
        mx = jnp.max(vals, axis=-1, keepdims=True)
        ix = jnp.min(jnp.where(vals == mx, lane_f, float(LANES)), axis=-1, keepdims=True).astype(jnp.int32)
        tops.append(mx)
        idxs.append(ix)
        vals = jnp.where(lane == ix, -3.0e38, vals)
    exps = [jnp.exp(tv - tops[0]) for tv in tops]
    den = exps[0] + exps[1] + exps[2] + exps[3]
    idx_out = jnp.zeros(logits.shape, jnp.int32)
    gate_out = jnp.zeros(logits.shape, F32)
    for k in range(TOP_K):
        idx_out = jnp.where(lane == k, idxs[k], idx_out)
        gate_out = jnp.where(lane == k, exps[k] / den, gate_out)
    idx_ref[...] = idx_out
    gate_ref[...] = gate_out


def router(h2, ffn_norm, router_w, router_b, tm=512):
    n, d = h2.shape
    wpad = jnp.zeros((d, LANES), F32).at[:, :N_EXPERTS].set(router_w.astype(F32))
    w_hi = wpad.astype(BF16)
    w_lo = (wpad - w_hi.astype(F32)).astype(BF16)
    bias = jnp.full((1, LANES), NEG_BIG, F32).at[0, :N_EXPERTS].set(router_b.astype(F32))
    return pl.pallas_call(
        _router_kernel,
        grid=(n // tm,),
        in_specs=[
            pl.BlockSpec((tm, d), lambda i: (i, 0)),
            pl.BlockSpec((1, d), lambda i: (0, 0)),
            pl.BlockSpec((d, LANES), lambda i: (0, 0)),
            pl.BlockSpec((d, LANES), lambda i: (0, 0)),
            pl.BlockSpec((1, LANES), lambda i: (0, 0)),
        ],
        out_specs=[pl.BlockSpec((tm, LANES), lambda i: (i, 0)),
                   pl.BlockSpec((tm, LANES), lambda i: (i, 0)),
                   pl.BlockSpec((tm, d // 2), lambda i: (i, 0))],
        out_shape=[jax.ShapeDtypeStruct((n, LANES), jnp.int32),
                   jax.ShapeDtypeStruct((n, LANES), F32),
                   jax.ShapeDtypeStruct((n, d // 2), jnp.uint32)],
        compiler_params=_params("parallel", vmem=VMEM_LIMIT),
        name="router",
    )(h2, ffn_norm.reshape(1, d).astype(F32), w_hi, w_lo, bias)


def _expert_kernel(ie_ref, isb_ref, inb_ref, ifl_ref, tok_ref, up_hbm, wgu_hbm, wd_hbm, bgu_ref,
                   bd_ref, ys_hbm, xbuf, act, ybuf, wgu_buf, wd_buf, wg_s, wu_s, wd_s, zbuf,
                   sem_x, sem_y, sem_gu, sem_d, sem_z, *, j1, j2, tn1, tn2, n_items):
    i = pl.program_id(0)
    nb = inb_ref[i]
    sb = isb_ref[i]
    ex = ie_ref[i]
    R = MOE_ROWS

    n_fill = ifl_ref[i]

    @pl.when(n_fill > 0)
    def _():
        zbuf[...] = jnp.zeros_like(zbuf)

        def z_copy(u):
            return pltpu.make_async_copy(zbuf, ys_hbm.at[pl.ds((sb + u) * R, R), :], sem_z)

        for u in range(MOE_ITEM_BLOCKS):
            @pl.when(u < n_fill)
            def _():
                z_copy(u).start()
        for u in range(MOE_ITEM_BLOCKS):
            @pl.when(u < n_fill)
            def _():
                z_copy(u).wait()

    def gu_copies(expert, jt, slot):
        return [pltpu.make_async_copy(
            wgu_hbm.at[expert, :, pl.ds(pl.multiple_of(which * j1 * tn1 + jt * tn1, tn1), tn1)],
            wgu_buf.at[slot, which], sem_gu.at[slot]) for which in range(2)]

    def d_copy(expert, jd, slot):
        return pltpu.make_async_copy(
            wd_hbm.at[expert, :, pl.ds(pl.multiple_of(jd * tn2, tn2), tn2)],
            wd_buf.at[slot], sem_d.at[slot])

    rmax = xbuf.shape[0]
    HEAD = MOE_HEAD_UNITS * R
    BODY = MOE_BODY_UNITS * R
    last_tok = tok_ref.shape[0] - 1

    def batch(rows):
        return rows // j2

    own_units = jnp.maximum(nb, MOE_HEAD_UNITS)

    def row_gather(base, r):
        tok = tok_ref[jnp.minimum(base + r, last_tok)]
        return pltpu.make_async_copy(up_hbm.at[pl.ds(tok, 1), :], xbuf.at[pl.ds(r, 1), :], sem_x)

    def x_wait_units(n):
        for u in range(MOE_ITEM_BLOCKS):
            @pl.when(u < n)
            def _():
                pltpu.make_async_copy(up_hbm.at[pl.ds(0, R), :], xbuf.at[pl.ds(u * R, R), :],
                                      sem_x).wait()

    yw = tn2 // 2

    def y_copy(slot, r, jd):
        return pltpu.make_async_copy(
            ybuf.at[slot, pl.ds(r * R, R), :],
            ys_hbm.at[pl.ds((sb + r) * R, R), pl.ds(pl.multiple_of(jd * yw, yw), yw)],
            sem_y.at[slot])

    def for_blocks(fn):
        for r in range(MOE_ITEM_BLOCKS):
            @pl.when(r < nb)
            def _():
                fn(r)

    @pl.when(i == 0)
    def _():
        def start(r, c):
            row_gather(sb * R, r).start()
            return c
        lax.fori_loop(0, rmax, start, 0, unroll=8)
        for cp in gu_copies(ex, 0, 0):
            cp.start(priority=WEIGHT_DMA_PRIORITY)

    nb_prev = inb_ref[jnp.maximum(i - 1, 0)]
    x_wait_units(jnp.where(
        i == 0, MOE_ITEM_BLOCKS,
        jnp.where(nb_prev > 0, jnp.maximum(jnp.maximum(nb_prev, MOE_HEAD_UNITS), nb), 0)))

    def gate_up_block(jt, r0, rows, wg, wu):
        x32 = xbuf[pl.ds(r0, rows), :]
        x_lo = pltpu.bitcast(x32 << 16, F32).astype(BF16)
        x_hi = pltpu.bitcast(x32 & jnp.uint32(0xFFFF0000), F32).astype(BF16)
        x = jnp.concatenate([x_lo, x_hi], axis=1)
        g = jnp.dot(x, wg, preferred_element_type=F32) + bgu_ref[pl.ds(jt, 1), :]
        u = jnp.dot(x, wu, preferred_element_type=F32) + bgu_ref[pl.ds(j1 + jt, 1), :]
        gt = jnp.minimum(g, SWIGLU_LIMIT)
        up = jnp.clip(u, -SWIGLU_LIMIT, SWIGLU_LIMIT)
        a = gt * jax.nn.sigmoid(SWIGLU_ALPHA * gt) * (up + 1.0)
        act[jt, pl.ds(r0, rows), :] = a.astype(BF16)

    def down_block(jd, slot, r0, rows, wd):
        lhs = jnp.concatenate([act[c, pl.ds(r0, rows), :] for c in range(j1)], axis=1)
        y = jnp.dot(lhs, wd, preferred_element_type=F32) + bd_ref[pl.ds(jd, 1), :]
        bits = pltpu.bitcast(y.astype(BF16).astype(F32), jnp.uint32)
        ybuf[slot, pl.ds(r0, rows), :] = bits[:, yw:] | (bits[:, :yw] >> 16)

    def rest_blocks(block_fn):
        rest = jnp.maximum(nb - MOE_HEAD_UNITS, 0)
        n_body = rest // MOE_BODY_UNITS
        rem = rest % MOE_BODY_UNITS
        r_tail = HEAD + n_body * BODY
        g_tail = batch(HEAD) + n_body * batch(BODY)

        def body(p, c):
            block_fn(pl.multiple_of(HEAD + p * BODY, R), BODY, batch(HEAD) + p * batch(BODY))
            return c

        lax.fori_loop(0, n_body, body, 0)

        @pl.when(rem >= 2)
        def _():
            block_fn(pl.multiple_of(r_tail, R), 2 * R, g_tail)

        @pl.when(rem % 2 == 1)
        def _():
            two = (rem // 2) * 2
            block_fn(pl.multiple_of(r_tail + two * R, R), R, g_tail + (rem // 2) * batch(2 * R))

    nxt = jnp.minimum(i + 1, n_items - 1)
    next_base = isb_ref[nxt] * R
    fetch_units = jnp.maximum(own_units, inb_ref[nxt])
    rows_per_tile = fetch_units * batch(R)

    n_wd = wd_buf.shape[0]

    @pl.when(nb > 0)
    def _():
        for t0 in range(n_wd - 1):
            d_copy(ex, t0, t0).start(priority=WEIGHT_DMA_PRIORITY)

        def gate_up_tile(jt, carry):
            slot = jt % 2

            @pl.when(jt + 1 < j1)
            def _():
                for cp in gu_copies(ex, jt + 1, 1 - slot):
                    cp.start(priority=WEIGHT_DMA_PRIORITY)

            for cp in gu_copies(ex, jt, slot):
                cp.wait()
            wg = wgu_buf[slot, 0].astype(BF16)
            wu = wgu_buf[slot, 1].astype(BF16)
            wg_s[...] = wg
            wu_s[...] = wu
            gate_up_block(jt, 0, HEAD, wg, wu)
            rest_blocks(lambda r0, rows, g0: gate_up_block(jt, r0, rows, wg_s[...], wu_s[...]))
            return carry

        lax.fori_loop(0, j1, gate_up_tile, 0)

        @pl.when(i + 1 < n_items)
        def _():
            @pl.when(inb_ref[i + 1] > 0)
            def _():
                for cp in gu_copies(ie_ref[i + 1], 0, 0):
                    cp.start(priority=WEIGHT_DMA_PRIORITY)

        def down_tile(jd, carry):
            slot = jd % 2
            wslot = jd % n_wd

            @pl.when(jd + n_wd - 1 < j2)
            def _():
                d_copy(ex, jd + n_wd - 1, (jd + n_wd - 1) % n_wd).start(priority=WEIGHT_DMA_PRIORITY)

            @pl.when(jd >= 2)
            def _():
                for_blocks(lambda r: y_copy(slot, r, jd).wait())

            d_copy(ex, jd, wslot).wait()
            wd = wd_buf[wslot].astype(BF16)
            wd_s[...] = wd
            down_block(jd, slot, 0, HEAD, wd)
            tile_row0 = jd * rows_per_tile

            def gather_batch(first, count):
                for rr in range(count):
                    row_gather(next_base, tile_row0 + first + rr).start()

            gather_batch(0, batch(HEAD))

            def later_block(r0, rows, g0):
                down_block(jd, slot, r0, rows, wd_s[...])
                gather_batch(g0, batch(rows))

            rest_blocks(later_block)

            def make_up(r, c):
                row_gather(next_base, tile_row0 + r).start()
                return c

            lax.fori_loop(own_units * batch(R), rows_per_tile, make_up, 0)
            for_blocks(lambda r: y_copy(slot, r, jd).start())
            return carry

        lax.fori_loop(0, j2, down_tile, 0)
        for_blocks(lambda r: y_copy(0, r, 0).wait())
        for_blocks(lambda r: y_copy(1, r, 0).wait())

        @pl.when(i == n_items - 1)
        def _():
            x_wait_units(fetch_units)


def expert_mlp(u_packed, row_tok, item_e, item_sb, item_nb, item_fill, w_gate_up, b_gate_up, w_down,
               b_down, tn1=256, tn2=MOE_DOWN_TILE):
    p = row_tok.shape[0]
    d = 2 * u_packed.shape[1]
    n_items = item_e.shape[0]
    de = w_down.shape[1]
    j1 = de // tn1
    j2 = d // tn2
    rmax = MOE_ITEM_BLOCKS * MOE_ROWS

    assert MOE_ROWS % j2 == 0 and MOE_BODY_UNITS == 4 and MOE_DOWN_SLOTS <= j2

    def bias_map(i, ie, isb, inb, ifl, tok):
        return (ie[i], 0, 0)

    return pl.pallas_call(
        functools.partial(_expert_kernel, j1=j1, j2=j2, tn1=tn1, tn2=tn2, n_items=n_items),
        grid_spec=pltpu.PrefetchScalarGridSpec(
            num_scalar_prefetch=5,
            grid=(n_items,),
            in_specs=[
                pl.BlockSpec(memory_space=pl.ANY),
                pl.BlockSpec(memory_space=pl.ANY),
                pl.BlockSpec(memory_space=pl.ANY),
                pl.BlockSpec((None, 2 * j1, tn1), bias_map),
                pl.BlockSpec((None, j2, tn2), bias_map),
            ],
            out_specs=pl.BlockSpec(memory_space=pl.ANY),
            scratch_shapes=[
                pltpu.VMEM((rmax, d // 2), jnp.uint32),
                pltpu.VMEM((j1, rmax, tn1), BF16),
                pltpu.VMEM((2, rmax, tn2 // 2), jnp.uint32),
                pltpu.VMEM((2, 2, d, tn1), F32),
                pltpu.VMEM((MOE_DOWN_SLOTS, de, tn2), F32),
                pltpu.VMEM((d, tn1), BF16),
                pltpu.VMEM((d, tn1), BF16),
                pltpu.VMEM((de, tn2), BF16),
                pltpu.VMEM((MOE_ROWS, d // 2), jnp.uint32),
                pltpu.SemaphoreType.DMA(()),
                pltpu.SemaphoreType.DMA((2,)),
                pltpu.SemaphoreType.DMA((2,)),
                pltpu.SemaphoreType.DMA((MOE_DOWN_SLOTS,)),
                pltpu.SemaphoreType.DMA(()),
            ],
        ),
        out_shape=jax.ShapeDtypeStruct((p, d // 2), jnp.uint32),
        compiler_params=_params("arbitrary", vmem=VMEM_LIMIT),
        name="expert_mlp",
    )(item_e, item_sb, item_nb, item_fill, row_tok, u_packed, w_gate_up, w_down,
      b_gate_up.reshape(N_EXPERTS, 2 * j1, tn1), b_down.reshape(N_EXPERTS, j2, tn2))


def _combine_kernel(dest_ref, h_ref, g_ref, ys_hbm, o_ref, buf, sem, *, tt, n_tok, n_steps, yw):
    i = pl.program_id(0)

    def issue(blk, slot):
        for k in range(TOP_K):
            def start(t2, c):
                for q in range(2):
                    t = 2 * t2 + q
                    src = dest_ref[k * n_tok + blk * tt + t]
                    pltpu.make_async_copy(ys_hbm.at[pl.ds(src, 1), :],
                                          buf.at[slot, k, pl.ds(t, 1), :],
                                          sem.at[slot]).start(priority=q)
                return c
            lax.fori_loop(0, tt // 2, start, 0, unroll=4)

    @pl.when(i == 0)
    def _():
        issue(0, 0)

    slot = i % 2

    @pl.when(i + 1 < n_steps)
    def _():
        issue(i + 1, 1 - slot)

    for k in range(TOP_K):
        pltpu.make_async_copy(ys_hbm.at[pl.ds(0, tt), :], buf.at[slot, k], sem.at[slot]).wait()
    g = g_ref[...]
    lo = None
    hi = None
    for k in range(TOP_K):
        word = buf[slot, k]
        gk = g[:, k:k + 1]
        lo_k = gk * pltpu.bitcast(word << 16, F32)
        hi_k = gk * pltpu.bitcast(word & jnp.uint32(0xFFFF0000), F32)
        lo = lo_k if lo is None else lo + lo_k
        hi = hi_k if hi is None else hi + hi_k
    for c in range(lo.shape[1] // yw):
        o_ref[:, 2 * c * yw:(2 * c + 1) * yw] = (h_ref[:, 2 * c * yw:(2 * c + 1) * yw]
                                                  + lo[:, c * yw:(c + 1) * yw])
        o_ref[:, (2 * c + 1) * yw:(2 * c + 2) * yw] = (h_ref[:, (2 * c + 1) * yw:(2 * c + 2) * yw]
                                                        + hi[:, c * yw:(c + 1) * yw])


def combine(h2, gates, ys, dest_km, yw, tt=256):
    n, d = h2.shape
    return pl.pallas_call(
        functools.partial(_combine_kernel, tt=tt, n_tok=n, n_steps=n // tt, yw=yw),
        grid_spec=pltpu.PrefetchScalarGridSpec(
            num_scalar_prefetch=1,
            grid=(n // tt,),
            in_specs=[pl.BlockSpec((tt, d), lambda i, dst: (i, 0)),
                      pl.BlockSpec((tt, LANES), lambda i, dst: (i, 0)),
                      pl.BlockSpec(memory_space=pl.ANY)],
            out_specs=pl.BlockSpec((tt, d), lambda i, dst: (i, 0)),
            scratch_shapes=[pltpu.VMEM((2, TOP_K, tt, d // 2), jnp.uint32),
                            pltpu.SemaphoreType.DMA((2,))],
        ),
        out_shape=jax.ShapeDtypeStruct((n, d), F32),
        compiler_params=_params("arbitrary", vmem=VMEM_LIMIT),
        name="moe_combine",
    )(dest_km, h2, gates, ys)


def routing_tables(top_idx, n_tok):
    a = n_tok * TOP_K
    e_flat = top_idx.reshape(-1)
    onehot = (e_flat[:, None] == jnp.arange(N_EXPERTS, dtype=jnp.int32)[None, :]).astype(jnp.int32)
    csum = jnp.cumsum(onehot, axis=0)
    rank = jnp.sum(csum * onehot, axis=1) - 1
    counts = csum[-1]
    nblk = (counts + MOE_ROWS - 1) // MOE_ROWS
    blk_end = jnp.cumsum(nblk)
    blk_start = blk_end - nblk
    dest = blk_start[e_flat] * MOE_ROWS + rank
    n_blocks = -(-(a + N_EXPERTS * (MOE_ROWS - 1)) // MOE_ROWS)
    p = n_blocks * MOE_ROWS
    t_flat = jnp.arange(a, dtype=jnp.int32) // TOP_K
    row_tok = jnp.zeros((p,), jnp.int32).at[dest].set(t_flat, unique_indices=True,
                                                      mode='promise_in_bounds')

    n_items = N_EXPERTS + n_blocks // MOE_ITEM_BLOCKS
    items_per_e = (nblk + MOE_ITEM_BLOCKS - 1) // MOE_ITEM_BLOCKS
    item_end = jnp.cumsum(items_per_e)
    item_start = item_end - items_per_e
    total_items = item_end[-1]
    ii = jnp.arange(n_items, dtype=jnp.int32)
    ic = jnp.minimum(ii, total_items - 1)
    e_i = jnp.minimum(jnp.searchsorted(item_end, ic, side='right'), N_EXPERTS - 1).astype(jnp.int32)
    local = ic - item_start[e_i]
    item_sb = (blk_start[e_i] + local * MOE_ITEM_BLOCKS).astype(jnp.int32)
    item_nb = jnp.clip(nblk[e_i] - local * MOE_ITEM_BLOCKS, 0, MOE_ITEM_BLOCKS)
    active = ii < total_items
    item_nb = jnp.where(active, item_nb, 0).astype(jnp.int32)
    fill_start = blk_end[-1] + (ii - total_items) * MOE_ITEM_BLOCKS
    item_fill = jnp.where(active, 0, jnp.clip(n_blocks - fill_start, 0, MOE_ITEM_BLOCKS)).astype(jnp.int32)
    item_sb = jnp.where(active, item_sb, jnp.minimum(fill_start, n_blocks - 1)).astype(jnp.int32)
    dest_km = dest.astype(jnp.int32).reshape(n_tok, TOP_K).T.reshape(-1)
    return dest_km, row_tok, e_i, item_sb, item_nb, item_fill


def kernel(x, meta_tokens, mix_norm, w_in, dn_conv, dn_a_log, dn_dt_bias, dn_out_norm, da_q_norm, da_k_norm, da_lam_q1, da_lam_k1, da_lam_q2, da_lam_k2, da_sub_norm, w_branch_a, w_branch_b, w_out, ffn_norm, router_w, router_b, w_gate_up, b_gate_up, w_down, b_down):
    bsz, seq, d = x.shape
    n_tok = bsz * seq
    x2 = x.reshape(n_tok, d)
    layer = 0

    w_bf = w_in[layer].astype(BF16)
    da_shift = 2 * DN_HEADS
    da_cols = w_bf.shape[1] - COL_BA - da_shift
    tn = 1024

    u_x = rms_cast(x2, mix_norm[layer], 512)
    u_m = rms_cast(meta_tokens.astype(F32), mix_norm[layer], N_META)
    proj_dn = matmul_fullk(u_x, w_bf, BF16, 1024, tn, name="in_proj_dn", n_blocks=COL_BA // tn)
    proj_da = matmul_fullk_shift(u_x, w_bf, BF16, 1024, tn, COL_BA, da_shift, da_cols // tn,
                                 name="in_proj_da")
    ba_x = matmul_fullk(u_x, w_bf, F32, 1024, LANES, name="in_proj_ba",
                        n_offset=COL_BA // LANES, n_blocks=1)
    proj_dn_m = matmul_fullk(u_m, w_bf, BF16, N_META, tn, name="in_proj_dn_meta", n_blocks=COL_DZ // tn)
    proj_kv_m = matmul_fullk_shift(u_m, w_bf, BF16, N_META, tn, COL_BA, da_shift,
                                   (COL_GA - COL_AK) // tn, name="in_proj_kv_meta",
                                   n_offset=COL_AK // tn)
    ba_m = matmul_fullk(u_m, w_bf, F32, N_META, LANES, name="in_proj_ba_meta",
                        n_offset=COL_BA // LANES, n_blocks=1)

    y_a = gated_deltanet(proj_dn, ba_x, proj_dn_m, ba_m, dn_conv[layer], dn_a_log[layer],
                         dn_dt_bias[layer], dn_out_norm[layer], bsz, seq)
    lam4 = jnp.stack([da_lam_q1[layer], da_lam_k1[layer], da_lam_q2[layer], da_lam_k2[layer]])
    y_b = diff_attention(proj_da, proj_kv_m, da_q_norm[layer], da_k_norm[layer], lam4,
                         da_sub_norm[layer], bsz, seq)

    merged = branch_merge(y_a, y_b, w_branch_a[layer], w_branch_b[layer], proj_da, d)
    h2 = matmul_fullk(merged, w_out[layer], F32, 1024, 512, res=x2, name="out_proj")

    top_idx, gates, u_packed = router(h2, ffn_norm[layer], router_w[layer], router_b[layer])
    dest, row_tok, item_e, item_sb, item_nb, item_fill = routing_tables(top_idx[:, :TOP_K], n_tok)
    ys = expert_mlp(u_packed, row_tok, item_e, item_sb, item_nb, item_fill, w_gate_up[layer],
                    b_gate_up[layer], w_down[layer], b_down[layer])
    out = combine(h2, gates, ys, dest, MOE_DOWN_TILE // 2)
    return out.reshape(bsz, seq, d)
```
